```python
import math
import jax, jax.numpy as jnp
from jax import lax
import numpy as np


D_MODEL = 1024
BATCH = 4
SEQ = 4096
DEPTH = 2

GRID_W = 64
CTX_LEN = 256
N_DIFF_HEADS = 6
DIFF_HEAD_DIM = 64
DIFF_V_DIM = 2 * DIFF_HEAD_DIM
N_FOURIER_GROUPS = 4
FOURIER_GROUP_DIM = 64
ATTN_QK_W = N_DIFF_HEADS * 2 * DIFF_HEAD_DIM
ATTN_V_W = N_DIFF_HEADS * DIFF_V_DIM
FOURIER_W = N_FOURIER_GROUPS * FOURIER_GROUP_DIM
EVEN_IN_W = 2 * ATTN_QK_W + ATTN_V_W + FOURIER_W
EVEN_MIX_W = ATTN_V_W + FOURIER_W
CONV_K = 3
N_EXPERTS = 16
EC_CAPACITY_FACTOR = 2
D_EXPERT = 1024
ROPE_BASE = 10000.0
Q_BLOCK = 128
EPS = 1e-6
N_EVEN = (DEPTH + 1) // 2
N_ODD = DEPTH // 2

kernel_name = 'hybrid_diffattn_fourier_shortconv_ecmoe_dit'


def rmsnorm(x, g):
    xf = x.astype(jnp.float32)
    y = xf * lax.rsqrt(jnp.mean(xf * xf, axis=-1, keepdims=True) + EPS)
    return (y * g.astype(jnp.float32)).astype(x.dtype)


def modulate(x, g, shift, scale):
    return rmsnorm(x, g) * (1 + scale) + shift


def ada_params(cond, w_ada, b_ada):
    m = jax.nn.silu(cond) @ w_ada + b_ada
    return jnp.split(m[..., None, :], 6, axis=-1)


def axial_rope(n, dtype):
    rows = n // GRID_W
    r = jnp.repeat(jnp.arange(rows, dtype=jnp.float32), GRID_W)
    col = jnp.tile(jnp.arange(GRID_W, dtype=jnp.float32), rows)
    n_freq = DIFF_HEAD_DIM // 4
    inv = ROPE_BASE ** (-jnp.arange(n_freq, dtype=jnp.float32) / n_freq)
    ar = r[:, None] * inv
    ac = col[:, None] * inv
    ang = jnp.concatenate([ar, ar, ac, ac], axis=-1)
    return jnp.cos(ang).astype(dtype), jnp.sin(ang).astype(dtype)


def apply_rope(x, cos, sin):
    x1, x2, x3, x4 = jnp.split(x, 4, axis=-1)
    rot = jnp.concatenate([-x2, x1, -x4, x3], axis=-1)
    return x * cos[None, :, None, None, :] + rot * sin[None, :, None, None, :]


def qk_heads(t):
    return t.reshape(t.shape[0], t.shape[1], N_DIFF_HEADS, 2, DIFF_HEAD_DIM)


def v_heads(t):
    return t.reshape(t.shape[0], t.shape[1], N_DIFF_HEADS, DIFF_V_DIM)


def diff_attend(q, k, v, lam):
    s = jnp.einsum('bqhmd,bkhmd->bhmqk', q, k).astype(jnp.float32)
    p = jax.nn.softmax(s, axis=-1)
    a = p[:, :, 0] - lam * p[:, :, 1]
    return jnp.einsum('bhqk,bkhe->bqhe', a.astype(v.dtype), v)


def blocked_diff_attend(q, k, v, lam):
    b, n = q.shape[0], q.shape[1]
    qb = q.reshape(b, n // Q_BLOCK, Q_BLOCK, N_DIFF_HEADS, 2, DIFF_HEAD_DIM).swapaxes(0, 1)
    out = lax.map(lambda qq: diff_attend(qq, k, v, lam), qb)
    return out.swapaxes(0, 1).reshape(b, n, N_DIFF_HEADS, DIFF_V_DIM)


def fourier_mix(f):
    b, n = f.shape[0], f.shape[1]
    fg = f.reshape(b, n, N_FOURIER_GROUPS, FOURIER_GROUP_DIM).astype(jnp.float32)
    y = jnp.fft.fft2(fg, axes=(1, 3), norm='ortho').real
    return y.reshape(b, n, FOURIER_W).astype(f.dtype)


def merge_even(att, f, subln, lam_init, w_out):
    b, n = att.shape[0], att.shape[1]
    att = rmsnorm(att, subln) * (1.0 - lam_init)
    o = jnp.concatenate([att.reshape(b, n, ATTN_V_W), fourier_mix(f)], axis=-1)
    return o @ w_out


def even_mixer(h, hc, w_in, q_norm, k_norm, lam, subln, w_out, lam_init, cos, sin, ctx_out):
    scale = DIFF_HEAD_DIM ** -0.5
    cuts = [ATTN_QK_W, 2 * ATTN_QK_W, 2 * ATTN_QK_W + ATTN_V_W]
    q, k, v, f = jnp.split(h @ w_in, cuts, axis=-1)
    q = apply_rope(rmsnorm(qk_heads(q), q_norm), cos, sin) * scale
    k = apply_rope(rmsnorm(qk_heads(k), k_norm), cos, sin)
    qc = None
    fc = None
    if ctx_out:
        qc, kc, vc, fc = jnp.split(hc @ w_in, cuts, axis=-1)
    else:
        kc, vc = jnp.split(hc @ w_in[:, ATTN_QK_W:2 * ATTN_QK_W + ATTN_V_W], [ATTN_QK_W], axis=-1)
    kc = rmsnorm(qk_heads(kc), k_norm)
    vc = v_heads(vc)
    keys = jnp.concatenate([k, kc], axis=1)
    vals = jnp.concatenate([v_heads(v), vc], axis=1)
    y = merge_even(blocked_diff_attend(q, keys, vals, lam), f, subln, lam_init, w_out)
    yc = None
    if ctx_out:
        qc = rmsnorm(qk_heads(qc), q_norm) * scale
        yc = merge_even(diff_attend(qc, kc, vc, lam), fc, subln, lam_init, w_out)
    return y, yc


def shortconv(z, w):
    zp = jnp.pad(z, ((0, 0), (1, 1), (0, 0)))
    return w[0] * zp[:, :-2] + w[1] * zp[:, 1:-1] + w[2] * zp[:, 2:]


def odd_mixer(h, w_in, conv_w, w_out):
    bg, cg, u = jnp.split(h @ w_in, 3, axis=-1)
    return (bg * shortconv(cg * u, conv_w)) @ w_out


def ec_moe(h, w_router, w_gate, w_up, w_down):
    b, n, d = h.shape
    cap = EC_CAPACITY_FACTOR * n // N_EXPERTS
    aff = jax.nn.softmax(h.astype(jnp.float32) @ w_router.astype(jnp.float32), axis=-1)
    g, idx = lax.top_k(aff.swapaxes(1, 2), cap)
    xs = jax.vmap(lambda hb, ib: hb[ib])(h, idx)
    a = jax.nn.silu(jnp.einsum('becd,edf->becf', xs, w_gate)) * jnp.einsum('becd,edf->becf', xs, w_up)
    y = jnp.einsum('becf,efd->becd', a, w_down) * g[..., None].astype(h.dtype)
    return jax.vmap(lambda yb, ib: jnp.zeros((n, d), yb.dtype).at[ib.reshape(-1)].add(yb.reshape(-1, d)))(y, idx)


def setup_inputs(seed: int = 0) -> dict:
    key = jax.random.key(seed)
    ks = jax.random.split(key, 24)

    def nrm(k, shape, s):
        return jax.random.normal(k, shape, jnp.float32) * s

    D = D_MODEL
    return {
        'x': nrm(ks[0], (BATCH, SEQ, D), 1.0),
        'c': nrm(ks[1], (BATCH, D), 1.0),
        'ctx': nrm(ks[2], (BATCH, CTX_LEN, D), 1.0),
        'c_ctx': nrm(ks[3], (D,), 1.0),
        'ada_w': nrm(ks[4], (DEPTH, D, 6 * D), 0.5 * D ** -0.5),
        'ada_b': nrm(ks[5], (DEPTH, 6 * D), 0.02),
        'norm_mix': 1.0 + nrm(ks[6], (DEPTH, D), 0.02),
        'norm_ffn': 1.0 + nrm(ks[7], (DEPTH, D), 0.02),
        'attn_w_in': nrm(ks[8], (N_EVEN, D, EVEN_IN_W), D ** -0.5),
        'attn_q_norm': 1.0 + nrm(ks[9], (N_EVEN, DIFF_HEAD_DIM), 0.02),
        'attn_k_norm': 1.0 + nrm(ks[10], (N_EVEN, DIFF_HEAD_DIM), 0.02),
        'lam_q1': nrm(ks[11], (N_EVEN, DIFF_HEAD_DIM), 0.1),
        'lam_k1': nrm(ks[12], (N_EVEN, DIFF_HEAD_DIM), 0.1),
        'lam_q2': nrm(ks[13], (N_EVEN, DIFF_HEAD_DIM), 0.1),
        'lam_k2': nrm(ks[14], (N_EVEN, DIFF_HEAD_DIM), 0.1),
        'attn_subln': 1.0 + nrm(ks[15], (N_EVEN, DIFF_V_DIM), 0.02),
        'attn_w_out': nrm(ks[16], (N_EVEN, EVEN_MIX_W, D), EVEN_MIX_W ** -0.5),
        'conv_w_in': nrm(ks[17], (N_ODD, D, 3 * D), D ** -0.5),
        'conv_w': nrm(ks[18], (N_ODD, CONV_K, D), CONV_K ** -0.5),
        'conv_w_out': nrm(ks[19], (N_ODD, D, D), D ** -0.5),
        'router_w': nrm(ks[20], (DEPTH, D, N_EXPERTS), D ** -0.5),
        'moe_w_gate': nrm(ks[21], (DEPTH, N_EXPERTS, D, D_EXPERT), D ** -0.5),
        'moe_w_up': nrm(ks[22], (DEPTH, N_EXPERTS, D, D_EXPERT), D ** -0.5),
        'moe_w_down': nrm(ks[23], (DEPTH, N_EXPERTS, D_EXPERT, D), D_EXPERT ** -0.5),
    }


def reference(x, c, ctx, c_ctx, ada_w, ada_b, norm_mix, norm_ffn,
              attn_w_in, attn_q_norm, attn_k_norm, lam_q1, lam_k1, lam_q2, lam_k2,
              attn_subln, attn_w_out, conv_w_in, conv_w, conv_w_out,
              router_w, moe_w_gate, moe_w_up, moe_w_down):
    cos, sin = axial_rope(x.shape[1], x.dtype)
    xc = ctx
    for l in range(DEPTH):
        ctx_out = any(j % 2 == 0 for j in range(l + 1, DEPTH))
        ctx_in = (l % 2 == 0) or ctx_out
        sh_m, sc_m, g_m, sh_f, sc_f, g_f = ada_params(c, ada_w[l], ada_b[l])
        h = modulate(x, norm_mix[l], sh_m, sc_m)
        hc = None
        cg_m = cg_f = csh_f = csc_f = None
        if ctx_in:
            csh_m, csc_m, cg_m, csh_f, csc_f, cg_f = ada_params(c_ctx, ada_w[l], ada_b[l])
            hc = modulate(xc, norm_mix[l], csh_m, csc_m)
        if l % 2 == 0:
            e = l // 2
            lam_init = 0.8 - 0.6 * math.exp(-0.3 * l)
            lam = (jnp.exp(jnp.sum(lam_q1[e].astype(jnp.float32) * lam_k1[e].astype(jnp.float32)))
                   - jnp.exp(jnp.sum(lam_q2[e].astype(jnp.float32) * lam_k2[e].astype(jnp.float32)))
                   + lam_init)
            y, yc = even_mixer(h, hc, attn_w_in[e], attn_q_norm[e], attn_k_norm[e], lam,
                               attn_subln[e], attn_w_out[e], lam_init, cos, sin, ctx_out)
        else:
            o = l // 2
            y = odd_mixer(h, conv_w_in[o], conv_w[o], conv_w_out[o])
            yc = odd_mixer(hc, conv_w_in[o], conv_w[o], conv_w_out[o]) if ctx_out else None
        x = x + g_m * y
        x = x + g_f * ec_moe(modulate(x, norm_ffn[l], sh_f, sc_f),
                             router_w[l], moe_w_gate[l], moe_w_up[l], moe_w_down[l])
        if ctx_out:
            xc = xc + cg_m * yc
            xc = xc + cg_f * ec_moe(modulate(xc, norm_ffn[l], csh_f, csc_f),
                                    router_w[l], moe_w_gate[l], moe_w_up[l], moe_w_down[l])
    return x
```

```python
import functools
import math

import jax
import jax.numpy as jnp
from jax import lax
from jax.experimental import pallas as pl
from jax.experimental.pallas import tpu as pltpu

BF = jnp.bfloat16
F32 = jnp.float32
I32 = jnp.int32

D = 1024
GRID_W = 64
N_HEADS = 6
HEAD_DIM = 64
V_DIM = 2 * HEAD_DIM
QK_W = N_HEADS * 2 * HEAD_DIM
V_W = N_HEADS * V_DIM
FOUR_W = 256
FOUR_G = 64
IN_W = 2 * QK_W + V_W + FOUR_W
N_EXP = 16
CAP_FACTOR = 2
ROPE_BASE = 10000.0
EPS = 1e-6
LANES = 128
MIB = 1024 * 1024

NT = (((1,), (1,)), ((), ()))


def _params(sem, vmem_mib):
    return pltpu.CompilerParams(dimension_semantics=sem, vmem_limit_bytes=vmem_mib * MIB)


def _split2(x):
    hi = x.astype(BF)
    lo = (x - hi.astype(F32)).astype(BF)
    return hi, lo


def _dot3(a, b, dims=(((1,), (0,)), ((), ()))):
    ah, al = _split2(a)
    bh, bl = _split2(b)
    dg = functools.partial(lax.dot_general, dimension_numbers=dims, preferred_element_type=F32)
    return dg(ah, bh) + dg(ah, bl) + dg(al, bh)


def _modulate(x, nw, shift, scale):
    ms = jnp.mean(x * x, axis=-1, keepdims=True)
    return (x * lax.rsqrt(ms + EPS) * nw) * (1.0 + scale) + shift


def _ada_kernel(c_ref, w_ref, b_ref, o_ref):
    cv = c_ref[...]
    s = cv * (1.0 / (1.0 + jnp.exp(-cv)))
    o_ref[0] = _dot3(s, w_ref[0]) + b_ref[0]


def _ada(cond8, ada_w, ada_b):
    depth = ada_w.shape[0]
    tn = 1536
    return pl.pallas_call(
        _ada_kernel,
        grid=(depth, 6 * D // tn),
        in_specs=[
            pl.BlockSpec((8, D), lambda l, j: (0, 0)),
            pl.BlockSpec((1, D, tn), lambda l, j: (l, 0, j)),
            pl.BlockSpec((1, 1, tn), lambda l, j: (l, 0, j)),
        ],
        out_specs=pl.BlockSpec((1, 8, tn), lambda l, j: (l, 0, j)),
        out_shape=jax.ShapeDtypeStruct((depth, 8, 6 * D), F32),
        compiler_params=_params(("arbitrary", "arbitrary"), 40),
        name="ada",
    )(cond8, ada_w, ada_b.reshape(depth, 1, 6 * D))


def _norm_rope(xb, wn, cos, sin_s, out_scale):
    lane = lax.broadcasted_iota(I32, xb.shape, 1)
    lo = lane < HEAD_DIM
    ss = xb * xb
    s_lo = jnp.sum(jnp.where(lo, ss, 0.0), axis=-1, keepdims=True)
    s_hi = jnp.sum(jnp.where(lo, 0.0, ss), axis=-1, keepdims=True)
    inv = jnp.where(lo, lax.rsqrt(s_lo * (1.0 / HEAD_DIM) + EPS),
                    lax.rsqrt(s_hi * (1.0 / HEAD_DIM) + EPS))
    y = xb * inv * wn
    fwd = pltpu.roll(y, LANES - 16, axis=1)
    bwd = pltpu.roll(y, 16, axis=1)
    rot = jnp.where((lane & 31) < 16, fwd, bwd)
    out = y * cos + rot * sin_s
    if out_scale != 1.0:
        out = out * out_scale
    return out


def _proj0_kernel(x_ref, nw_ref, sh_ref, sc_ref, w_ref, cos_ref, sin_ref, qn_ref, kn_ref,
                  q_ref, k_ref, v_ref, f_ref):
    h = _modulate(x_ref[0], nw_ref[...], sh_ref[0, 0], sc_ref[0, 0]).astype(BF)
    cos = cos_ref[...]
    sin_s = sin_ref[...]
    qn = qn_ref[...]
    kn = kn_ref[...]
    q_scale = HEAD_DIM ** -0.5
    for hh in range(N_HEADS):
        c0 = hh * V_DIM
        pq = jnp.dot(h, w_ref[:, c0:c0 + V_DIM], preferred_element_type=F32)
        q_ref[0, :, c0:c0 + V_DIM] = _norm_rope(pq, qn, cos, sin_s, q_scale).astype(BF)
        pk = jnp.dot(h, w_ref[:, QK_W + c0:QK_W + c0 + V_DIM], preferred_element_type=F32)
        k_ref[0, :, c0:c0 + V_DIM] = _norm_rope(pk, kn, cos, sin_s, 1.0).astype(BF)
    v_ref[0] = jnp.dot(h, w_ref[:, 2 * QK_W:2 * QK_W + V_W], preferred_element_type=F32).astype(BF)
    f_ref[0] = jnp.dot(h, w_ref[:, 2 * QK_W + V_W:], preferred_element_type=F32).astype(BF)


def _proj0(xcat, nw, sh2, sc2, w_bf, cos, sin_s, qn, kn, n_lat):
    b, s_tot, _ = xcat.shape
    tm = 256
    lat_tiles = n_lat // tm
    mod_spec = pl.BlockSpec((1, 1, 1, D), lambda bb, i: (bb, i // lat_tiles, 0, 0))
    full = lambda shape: pl.BlockSpec(shape, lambda bb, i: tuple(0 for _ in shape))
    row = lambda w: pl.BlockSpec((1, tm, w), lambda bb, i: (bb, i, 0))
    return pl.pallas_call(
        _proj0_kernel,
        grid=(b, s_tot // tm),
        in_specs=[
            row(D), full((1, D)), mod_spec, mod_spec, full((D, IN_W)),
            pl.BlockSpec((tm, LANES), lambda bb, i: (i, 0)),
            pl.BlockSpec((tm, LANES), lambda bb, i: (i, 0)),
            full((1, LANES)), full((1, LANES)),
        ],
        out_specs=[row(QK_W), row(QK_W), row(V_W), row(FOUR_W)],
        out_shape=[
            jax.ShapeDtypeStruct((b, s_tot, QK_W), BF),
            jax.ShapeDtypeStruct((b, s_tot, QK_W), BF),
            jax.ShapeDtypeStruct((b, s_tot, V_W), BF),
            jax.ShapeDtypeStruct((b, s_tot, FOUR_W), BF),
        ],
        compiler_params=_params(("parallel", "arbitrary"), 48),
        name="proj0",
    )(xcat, nw, sh2, sc2, w_bf, cos, sin_s, qn, kn)


def _attn_kernel(q_ref, k_ref, v_ref, lam_ref, sub_ref, o_ref, *, lam_init):
    q = q_ref[0]
    k = k_ref[0]
    lane = lax.broadcasted_iota(I32, q.shape, 1)
    zero = jnp.zeros_like(q)
    q0 = jnp.where(lane < HEAD_DIM, q, zero)
    q1 = jnp.where(lane < HEAD_DIM, zero, q)
    lv = lam_ref[...]
    t1 = jnp.sum(lv[0:1] * lv[1:2], axis=-1, keepdims=True)
    t2 = jnp.sum(lv[2:3] * lv[3:4], axis=-1, keepdims=True)
    lam = jnp.exp(t1) - jnp.exp(t2) + lam_init

    def softmax_parts(qq):
        s = lax.dot_general(qq, k, NT, preferred_element_type=F32)
        m = jnp.max(s, axis=-1, keepdims=True)
        e = jnp.exp(s - m)
        return e, 1.0 / jnp.sum(e, axis=-1, keepdims=True)

    e0, r0 = softmax_parts(q0)
    e1, r1 = softmax_parts(q1)
    a = e0 * r0 - e1 * (lam * r1)
    o = jnp.dot(a.astype(BF), v_ref[0], preferred_element_type=F32)
    ms = jnp.mean(o * o, axis=-1, keepdims=True)
    o_ref[0] = (o * lax.rsqrt(ms + EPS) * sub_ref[...] * (1.0 - lam_init)).astype(BF)


def _attention(q, k, v, lamv, subln, n_lat, lam_init):
    b, s_tot, _ = k.shape
    tq = 256
    kv_spec = pl.BlockSpec((1, s_tot, V_DIM), lambda bb, hh, i: (bb, 0, hh))
    q_spec = pl.BlockSpec((1, tq, V_DIM), lambda bb, hh, i: (bb, i, hh))
    return pl.pallas_call(
        functools.partial(_attn_kernel, lam_init=lam_init),
        grid=(b, N_HEADS, n_lat // tq),
        in_specs=[
            q_spec, kv_spec, kv_spec,
            pl.BlockSpec((8, LANES), lambda bb, hh, i: (0, 0)),
            pl.BlockSpec((1, LANES), lambda bb, hh, i: (0, 0)),
        ],
        out_specs=q_spec,
        out_shape=jax.ShapeDtypeStruct((b, n_lat, V_W), BF),
        compiler_params=_params(("parallel", "parallel", "arbitrary"), 48),
        name="diff_attn",
    )(q, k, v, lamv, subln)


def _fourier_kernel(f_ref, bd_ref, w_ref, o_ref, g_ref):
    nb, n_lat, _ = f_ref.shape

    @pl.when(pl.program_id(0) == 0)
    def _():
        rows = 1024
        for bb in range(nb):
            for r in range(n_lat // rows):
                fc = jnp.dot(f_ref[bb, r * rows:(r + 1) * rows, :], bd_ref[...],
                             preferred_element_type=F32)
                g_ref[r * rows:(r + 1) * rows, bb * FOUR_W:(bb + 1) * FOUR_W] = fc[:, :FOUR_W].astype(BF)
                g_ref[n_lat + r * rows:n_lat + (r + 1) * rows,
                      bb * FOUR_W:(bb + 1) * FOUR_W] = fc[:, FOUR_W:].astype(BF)

    y = jnp.dot(w_ref[...], g_ref[...], preferred_element_type=F32)
    for bb in range(nb):
        o_ref[bb] = y[:, bb * FOUR_W:(bb + 1) * FOUR_W].astype(BF)


def _fourier(f, bd, wbig, n_lat):
    b = f.shape[0]
    tr = 256
    return pl.pallas_call(
        _fourier_kernel,
        grid=(n_lat // tr,),
        in_specs=[
            pl.BlockSpec((b, n_lat, FOUR_W), lambda i: (0, 0, 0)),
            pl.BlockSpec((FOUR_W, 2 * FOUR_W), lambda i: (0, 0)),
            pl.BlockSpec((tr, 2 * n_lat), lambda i: (i, 0)),
        ],
        out_specs=pl.BlockSpec((b, tr, FOUR_W), lambda i: (0, i, 0)),
        out_shape=jax.ShapeDtypeStruct((b, n_lat, FOUR_W), BF),
        scratch_shapes=[pltpu.VMEM((2 * n_lat, b * FOUR_W), BF)],
        compiler_params=_params(("arbitrary",), 56),
        name="fourier",
    )(f, bd, wbig)


def _residual_router(x, y, gm, nf, shf, scf, wr_t, x1_ref, hf_ref, aff_ref):
    x1 = x + gm * y
    x1_ref[0] = x1
    hf = _modulate(x1, nf, shf, scf)
    hf_ref[0] = hf
    logits = _dot3(wr_t, hf, NT)
    m = jnp.max(logits, axis=0, keepdims=True)
    e = jnp.exp(logits - m)
    aff_ref[0] = e / jnp.sum(e, axis=0, keepdims=True)


def _merge0_kernel(att_ref, four_ref, w_ref, x_ref, gm_ref, nf_ref, shf_ref, scf_ref, wr_ref,
                   x1_ref, hf_ref, aff_ref):
    y = jnp.dot(att_ref[0], w_ref[:V_W, :], preferred_element_type=F32)
    y = y + jnp.dot(four_ref[0], w_ref[V_W:, :], preferred_element_type=F32)
    _residual_router(x_ref[0], y, gm_ref[0], nf_ref[...], shf_ref[0], scf_ref[0], wr_ref[...],
                     x1_ref, hf_ref, aff_ref)


def _router_specs(tm):
    vec = pl.BlockSpec((1, 1, D), lambda bb, i: (bb, 0, 0))
    row = pl.BlockSpec((1, tm, D), lambda bb, i: (bb, i, 0))
    in_specs = [vec, pl.BlockSpec((1, D), lambda bb, i: (0, 0)), vec, vec,
                pl.BlockSpec((N_EXP, D), lambda bb, i: (0, 0))]
    out_specs = [row, row, pl.BlockSpec((1, N_EXP, tm), lambda bb, i: (bb, 0, i))]
    return row, in_specs, out_specs


def _router_out_shape(b, n):
    return [jax.ShapeDtypeStruct((b, n, D), F32), jax.ShapeDtypeStruct((b, n, D), F32),
            jax.ShapeDtypeStruct((b, N_EXP, n), F32)]


def _merge0(att, four, w_bf, x, gm, nf, shf, scf, wr_t):
    b, n, _ = x.shape
    tm = 512
    row, r_in, r_out = _router_specs(tm)
    return pl.pallas_call(
        _merge0_kernel,
        grid=(b, n // tm),
        in_specs=[
            pl.BlockSpec((1, tm, V_W), lambda bb, i: (bb, i, 0)),
            pl.BlockSpec((1, tm, FOUR_W), lambda bb, i: (bb, i, 0)),
            pl.BlockSpec((D, D), lambda bb, i: (0, 0)),
            row,
        ] + r_in,
        out_specs=r_out,
        out_shape=_router_out_shape(b, n),
        compiler_params=_params(("parallel", "arbitrary"), 48),
        name="merge0",
    )(att, four, w_bf, x, gm, nf, shf, scf, wr_t)


def _proj1_kernel(x_ref, nw_ref, sh_ref, sc_ref, w_ref, bg_ref, z_ref):
    h = _modulate(x_ref[0], nw_ref[...], sh_ref[0], sc_ref[0]).astype(BF)
    bg_ref[0] = jnp.dot(h, w_ref[:, :D], preferred_element_type=F32)
    cg = jnp.dot(h, w_ref[:, D:2 * D], preferred_element_type=F32)
    u = jnp.dot(h, w_ref[:, 2 * D:], preferred_element_type=F32)
    z_ref[0] = cg * u


def _proj1(x, nw, sh, sc, w_bf):
    b, n, _ = x.shape
    tm = 512
    vec = pl.BlockSpec((1, 1, D), lambda bb, i: (bb, 0, 0))
    row = pl.BlockSpec((1, tm, D), lambda bb, i: (bb, i, 0))
    return pl.pallas_call(
        _proj1_kernel,
        grid=(b, n // tm),
        in_specs=[row, pl.BlockSpec((1, D), lambda bb, i: (0, 0)), vec, vec,
                  pl.BlockSpec((D, 3 * D), lambda bb, i: (0, 0))],
        out_specs=[row, row],
        out_shape=[jax.ShapeDtypeStruct((b, n, D), F32), jax.ShapeDtypeStruct((b, n, D), F32)],
        compiler_params=_params(("parallel", "arbitrary"), 48),
        name="proj1",
    )(x, nw, sh, sc, w_bf)


def _merge1_kernel(bg_ref, z_ref, zp_ref, zn_ref, cw_ref, w_ref, x_ref, gm_ref, nf_ref, shf_ref,
                   scf_ref, wr_ref, x1_ref, hf_ref, aff_ref):
    i = pl.program_id(1)
    last = pl.num_programs(1) - 1
    z = z_ref[0]
    tm = z.shape[0]
    rowid = lax.broadcasted_iota(I32, z.shape, 0)
    prev_row = jnp.where(i > 0, zp_ref[0, 7:8, :], 0.0)
    next_row = jnp.where(i < last, zn_ref[0, 0:1, :], 0.0)
    z_up = jnp.where(rowid == 0, prev_row, pltpu.roll(z, 1, axis=0))
    z_dn = jnp.where(rowid == tm - 1, next_row, pltpu.roll(z, tm - 1, axis=0))
    cw = cw_ref[...]
    conv = cw[0:1] * z_up + cw[1:2] * z + cw[2:3] * z_dn
    y = jnp.dot((bg_ref[0] * conv).astype(BF), w_ref[...], preferred_element_type=F32)
    _residual_router(x_ref[0], y, gm_ref[0], nf_ref[...], shf_ref[0], scf_ref[0], wr_ref[...],
                     x1_ref, hf_ref, aff_ref)


def _merge1(bg, z, conv_w8, w_bf, x, gm, nf, shf, scf, wr_t):
    b, n, _ = x.shape
    tm = 512
    halo = 8
    per = tm // halo
    n_halo = n // halo
    row, r_in, r_out = _router_specs(tm)
    return pl.pallas_call(
        _merge1_kernel,
        grid=(b, n // tm),
        in_specs=[
            row, row,
            pl.BlockSpec((1, halo, D), lambda bb, i: (bb, jnp.maximum(i * per - 1, 0), 0)),
            pl.BlockSpec((1, halo, D), lambda bb, i: (bb, jnp.minimum((i + 1) * per, n_halo - 1), 0)),
            pl.BlockSpec((8, D), lambda bb, i: (0, 0)),
            pl.BlockSpec((D, D), lambda bb, i: (0, 0)),
            row,
        ] + r_in,
        out_specs=r_out,
        out_shape=_router_out_shape(b, n),
        compiler_params=_params(("parallel", "arbitrary"), 48),
        name="merge1",
    )(bg, z, z, z, conv_w8, w_bf, x, gm, nf, shf, scf, wr_t)


def _select_kernel(a_ref, idx_ref, gate_ref, *, cap):
    a = a_ref[0]
    rows = a.shape[0]
    chunks = rows // N_EXP
    assert chunks & (chunks - 1) == 0
    shift = chunks.bit_length() - 1
    ri = lax.broadcasted_iota(I32, (rows, rows), 0)
    ci = lax.broadcasted_iota(I32, (rows, rows), 1)
    same_e = (ri >> shift) == (ci >> shift)
    same_f = jnp.where(same_e, 1.0, 0.0)
    same = same_f.astype(BF)
    lower = (same_f * jnp.where(ci < ri, 1.0, 0.0)).astype(BF)
    li = lax.broadcasted_iota(I32, (LANES, LANES), 0)
    lj = lax.broadcasted_iota(I32, (LANES, LANES), 1)
    incl = jnp.where(li <= lj, 1.0, 0.0).astype(BF)

    def bcast(col):
        return jnp.broadcast_to(col, (rows, LANES))

    def expert_count(maskf):
        rc = bcast(jnp.sum(maskf, axis=-1, keepdims=True)).astype(BF)
        return jnp.dot(same, rc, preferred_element_type=F32)

    def prefix(maskf):
        cl = jnp.dot(maskf.astype(BF), incl, preferred_element_type=F32)
        tot = bcast(cl[:, LANES - 1:LANES])
        off = jnp.dot(lower, tot.astype(BF), preferred_element_type=F32)
        return cl, off, tot

    min_normal = 0x00800000

    def search(step, t):
        cand = t | jnp.left_shift(jnp.int32(1), 30 - step)
        cnt = expert_count(jnp.where(a >= lax.bitcast_convert_type(cand, F32), 1.0, 0.0))
        return jnp.where(cnt >= cap, jnp.where(cand >= min_normal, cand, t), t)

    thr = lax.fori_loop(0, 31, search, jnp.zeros((rows, LANES), I32))
    thr_f = lax.bitcast_convert_type(thr, F32)
    nxt_f = lax.bitcast_convert_type(jnp.maximum(thr + 1, min_normal), F32)
    gtf = jnp.where(a >= nxt_f, 1.0, 0.0)
    eqf = jnp.where(a >= thr_f, 1.0, 0.0) - gtf
    need = cap - expert_count(gtf)
    cl_eq, off_eq, _ = prefix(eqf)
    self_f = gtf + eqf * jnp.where(cl_eq + off_eq <= need, 1.0, 0.0)
    cl, off, tot = prefix(self_f)
    inc = off + tot
    cl_bf = cl.astype(BF)
    a_hi = a.astype(BF)
    r1 = a - a_hi.astype(F32)
    a_mid = r1.astype(BF)
    a_lo = (r1 - a_mid.astype(F32)).astype(BF)

    slot = lax.broadcasted_iota(I32, (cap, LANES), 0).astype(F32)
    lane_f = lax.broadcasted_iota(I32, (cap, LANES), 1).astype(F32)
    dsub = lax.broadcasted_iota(I32, (chunks, LANES), 0)
    dlane = lax.broadcasted_iota(I32, (chunks, LANES), 1)
    rsel = lax.broadcasted_iota(I32, (cap, rows), 1).astype(F32)
    big = float(2 * cap * N_EXP)
    for e in range(N_EXP):
        r0 = e * chunks
        diag = dsub == dlane
        inc_row = jnp.sum(jnp.where(diag, inc[r0:r0 + chunks], 0.0), axis=0, keepdims=True)
        off_row = jnp.sum(jnp.where(diag, off[r0:r0 + chunks], 0.0), axis=0, keepdims=True)
        inc_row = jnp.where(dlane[0:1] < chunks, inc_row, big)
        jc = jnp.sum(jnp.where(inc_row <= slot, 1.0, 0.0), axis=-1, keepdims=True)
        offsel = jnp.sum(jnp.where(lane_f == jc, off_row, 0.0), axis=-1, keepdims=True)
        pick = jnp.where(rsel == jc + float(r0), 1.0, 0.0).astype(BF)
        clsel = jnp.dot(pick, cl_bf, preferred_element_type=F32)
        lc = jnp.sum(jnp.where(clsel <= slot - offsel, 1.0, 0.0), axis=-1, keepdims=True)
        asel = (jnp.dot(pick, a_hi, preferred_element_type=F32)
                + jnp.dot(pick, a_mid, preferred_element_type=F32)
                + jnp.dot(pick, a_lo, preferred_element_type=F32))
        gate = jnp.sum(jnp.where(lane_f == lc, asel, 0.0), axis=-1, keepdims=True)
        idx_ref[0, e] = jnp.broadcast_to(jc * float(LANES) + lc, (cap, LANES)).astype(I32)
        gate_ref[0, e] = jnp.broadcast_to(gate, (cap, LANES))


def _select(aff_rows, cap):
    b, rows, _ = aff_rows.shape
    out = jax.ShapeDtypeStruct((b, N_EXP, cap, LANES), I32)
    outg = jax.ShapeDtypeStruct((b, N_EXP, cap, LANES), F32)
    spec = pl.BlockSpec((1, N_EXP, cap, LANES), lambda bb: (bb, 0, 0, 0))
    return pl.pallas_call(
        functools.partial(_select_kernel, cap=cap),
        grid=(b,),
        in_specs=[pl.BlockSpec((1, rows, LANES), lambda bb: (bb, 0, 0))],
        out_specs=[spec, spec],
        out_shape=[out, outg],
        compiler_params=_params(("parallel",), 48),
        name="select",
    )(aff_rows)


def _gather_kernel(idx_ref, h_ref, o_ref, rows_ref):
    cap = rows_ref.shape[0]

    def body(c, carry):
        rows_ref[pl.ds(c, 1), :] = h_ref[0, pl.ds(idx_ref[0, 0, c], 1), :]
        return carry

    lax.fori_loop(0, cap, body, 0, unroll=8)
    o_ref[0, 0] = rows_ref[...].astype(BF)


def _gather(idx, hf, cap):
    b, n, _ = hf.shape
    return pl.pallas_call(
        _gather_kernel,
        grid=(b, N_EXP),
        in_specs=[
            pl.BlockSpec((1, 1, cap), lambda bb, e: (bb * N_EXP + e, 0, 0), memory_space=pltpu.SMEM),
            pl.BlockSpec((1, n, D), lambda bb, e: (bb, 0, 0)),
        ],
        out_specs=pl.BlockSpec((1, 1, cap, D), lambda bb, e: (bb, e, 0, 0)),
        out_shape=jax.ShapeDtypeStruct((b, N_EXP, cap, D), BF),
        scratch_shapes=[pltpu.VMEM((cap, D), F32)],
        compiler_params=_params(("parallel", "arbitrary"), 48),
        name="gather",
    )(idx, hf)


def _ffn_kernel(x_ref, wg_ref, wu_ref, wd_ref, o_ref, wg_s, wu_s, wd_s):
    @pl.when(pl.program_id(1) == 0)
    def _():
        wg_s[...] = wg_ref[...].astype(BF)
        wu_s[...] = wu_ref[...].astype(BF)
        wd_s[...] = wd_ref[...].astype(BF)

    x = x_ref[0, 0]
    g = jnp.dot(x, wg_s[...], preferred_element_type=F32)
    u = jnp.dot(x, wu_s[...], preferred_element_type=F32)
    act = (g * (1.0 / (1.0 + jnp.exp(-g))) * u).astype(BF)
    o_ref[0, 0] = jnp.dot(act, wd_s[...], preferred_element_type=F32)


def _ffn(xs, w_gate, w_up, w_down, layer):
    b, _, cap, _ = xs.shape
    dexp = w_gate.shape[-1]
    xspec = pl.BlockSpec((1, 1, cap, D), lambda e, bb: (bb, e, 0, 0))
    wspec_in = pl.BlockSpec((None, None, D, dexp), lambda e, bb: (layer, e, 0, 0))
    wspec_out = pl.BlockSpec((None, None, dexp, D), lambda e, bb: (layer, e, 0, 0))
    return pl.pallas_call(
        _ffn_kernel,
        grid=(N_EXP, b),
        in_specs=[xspec, wspec_in, wspec_in, wspec_out],
        out_specs=xspec,
        out_shape=jax.ShapeDtypeStruct((b, N_EXP, cap, D), F32),
        scratch_shapes=[pltpu.VMEM((D, dexp), BF), pltpu.VMEM((D, dexp), BF), pltpu.VMEM((dexp, D), BF)],
        compiler_params=_params(("arbitrary", "arbitrary"), 56),
        name="expert_ffn",
    )(xs, w_gate, w_up, w_down)


def _scatter_kernel(idx_ref, gate_ref, y_ref, x_ref, gf_ref, o_ref):
    cap = y_ref.shape[2]

    @pl.when(pl.program_id(2) == 0)
    def _():
        o_ref[...] = x_ref[...]

    gf = gf_ref[0]

    def body(c, carry):
        t = idx_ref[0, 0, c]
        coef = gf * gate_ref[0, 0, c]
        o_ref[0, pl.ds(t, 1), :] = o_ref[0, pl.ds(t, 1), :] + coef * y_ref[0, 0, pl.ds(c, 1), :]
        return carry

    lax.fori_loop(0, cap, body, 0, unroll=8)


def _scatter(idx, gate, y, x, gf):
    b, n, _ = x.shape
    cap = y.shape[2]
    half = D // 2
    sm = lambda: pl.BlockSpec((1, 1, cap), lambda bb, j, e: (bb * N_EXP + e, 0, 0),
                              memory_space=pltpu.SMEM)
    xspec = pl.BlockSpec((1, n, half), lambda bb, j, e: (bb, 0, j))
    return pl.pallas_call(
        _scatter_kernel,
        grid=(b, D // half, N_EXP),
        in_specs=[
            sm(), sm(),
            pl.BlockSpec((1, 1, cap, half), lambda bb, j, e: (bb, e, 0, j)),
            xspec,
            pl.BlockSpec((1, 1, half), lambda bb, j, e: (bb, 0, j)),
        ],
        out_specs=xspec,
        out_shape=jax.ShapeDtypeStruct((b, n, D), F32),
        compiler_params=_params(("parallel", "parallel", "arbitrary"), 48),
        name="scatter_add",
    )(idx, gate, y, x, gf)


def _moe(x1, hf, aff_t, gf, w_gate, w_up, w_down, layer):
    b, n, _ = x1.shape
    cap = CAP_FACTOR * n // N_EXP
    aff_rows = aff_t.reshape(b, N_EXP * (n // LANES), LANES)
    idx_b, gate_b = _select(aff_rows, cap)
    idx = idx_b[..., 0].reshape(b * N_EXP, 1, cap)
    gate = gate_b[..., 0].reshape(b * N_EXP, 1, cap)
    xs = _gather(idx, hf, cap)
    y = _ffn(xs, w_gate, w_up, w_down, layer)
    return _scatter(idx, gate, y, x1, gf)


def _rope_tables(n_lat, n_ctx):
    rows = n_lat // GRID_W
    r = jnp.repeat(jnp.arange(rows, dtype=F32), GRID_W)
    col = jnp.tile(jnp.arange(GRID_W, dtype=F32), rows)
    n_freq = HEAD_DIM // 4
    inv = ROPE_BASE ** (-jnp.arange(n_freq, dtype=F32) / n_freq)
    ar = r[:, None] * inv
    ac = col[:, None] * inv
    ang = jnp.concatenate([ar, ar, ac, ac], axis=-1)
    sign = jnp.where((jnp.arange(HEAD_DIM) % 32) < 16, -1.0, 1.0).astype(F32)
    cos = jnp.concatenate([jnp.cos(ang), jnp.ones((n_ctx, HEAD_DIM), F32)], axis=0)
    sin_s = jnp.concatenate([jnp.sin(ang) * sign, jnp.zeros((n_ctx, HEAD_DIM), F32)], axis=0)
    return jnp.tile(cos, (1, 2)), jnp.tile(sin_s, (1, 2))


def _dft_tables(n_lat):
    c = jnp.arange(FOUR_G, dtype=I32)
    ang_c = (2.0 * math.pi / FOUR_G) * ((c[:, None] * c[None, :]) % FOUR_G).astype(F32)
    eye = jnp.eye(FOUR_W // FOUR_G, dtype=F32)
    bd = jnp.concatenate([jnp.kron(eye, jnp.cos(ang_c)), jnp.kron(eye, jnp.sin(ang_c))], axis=1)
    bd = (bd * FOUR_G ** -0.5).astype(BF)
    side = int(round(math.sqrt(n_lat)))
    assert side * side == n_lat
    k = jnp.arange(n_lat, dtype=I32)[:, None]
    s = jnp.arange(side, dtype=I32)[None, :]
    alpha = (2.0 * math.pi / side) * ((k * s) % side).astype(F32)
    beta = (2.0 * math.pi / n_lat) * ((k * s) % n_lat).astype(F32)
    ca, sa, cb, sb = jnp.cos(alpha), jnp.sin(alpha), jnp.cos(beta), jnp.sin(beta)
    cosw = (ca[:, :, None] * cb[:, None, :] - sa[:, :, None] * sb[:, None, :]).reshape(n_lat, n_lat)
    sinw = (sa[:, :, None] * cb[:, None, :] + ca[:, :, None] * sb[:, None, :]).reshape(n_lat, n_lat)
    wbig = (jnp.concatenate([cosw, -sinw], axis=1) * n_lat ** -0.5).astype(BF)
    return bd, wbig


def kernel(x, c, ctx, c_ctx, ada_w, ada_b, norm_mix, norm_ffn, attn_w_in, attn_q_norm, attn_k_norm,
           lam_q1, lam_k1, lam_q2, lam_k2, attn_subln, attn_w_out, conv_w_in, conv_w, conv_w_out,
           router_w, moe_w_gate, moe_w_up, moe_w_down):
    b, n, _ = x.shape
    n_ctx = ctx.shape[1]
    assert x.shape[2] == D and n % 512 == 0 and n_ctx % 256 == 0

    cond8 = jnp.concatenate([c, c_ctx[None, :], jnp.zeros((8 - b - 1, D), F32)], axis=0)
    ada = _ada(cond8, ada_w, ada_b)

    def mods(layer):
        m = ada[layer].reshape(8, 6, D)
        return [m[:, j] for j in range(6)]

    vec = lambda t: t[:b].reshape(b, 1, D)

    sh_m, sc_m, g_m, sh_f, sc_f, g_f = mods(0)
    both = lambda t: jnp.stack([t[:b], jnp.broadcast_to(t[b], (b, D))], axis=1).reshape(b, 2, 1, D)
    cos, sin_s = _rope_tables(n, n_ctx)
    bd, wbig = _dft_tables(n)
    tile2 = lambda t: jnp.tile(t.reshape(1, HEAD_DIM), (1, 2))
    xcat = jnp.concatenate([x, ctx], axis=1)
    q, k, v, f = _proj0(xcat, norm_mix[0].reshape(1, D), both(sh_m), both(sc_m),
                        attn_w_in[0].astype(BF), cos, sin_s,
                        tile2(attn_q_norm[0]), tile2(attn_k_norm[0]), n)
    lam_init = 0.8 - 0.6 * math.exp(-0.3 * 0)
    lamv = jnp.zeros((8, LANES), F32).at[:4, :HEAD_DIM].set(
        jnp.stack([lam_q1[0], lam_k1[0], lam_q2[0], lam_k2[0]]))
    att = _attention(q, k, v, lamv, attn_subln[0].reshape(1, V_DIM), n, lam_init)
    four = _fourier(f, bd, wbig, n)
    x1, hf, aff_t = _merge0(att, four, attn_w_out[0].astype(BF), x, vec(g_m),
                            norm_ffn[0].reshape(1, D), vec(sh_f), vec(sc_f), router_w[0].T)
    x2 = _moe(x1, hf, aff_t, vec(g_f), moe_w_gate, moe_w_up, moe_w_down, 0)

    sh_m, sc_m, g_m, sh_f, sc_f, g_f = mods(1)
    bg, z = _proj1(x2, norm_mix[1].reshape(1, D), vec(sh_m), vec(sc_m), conv_w_in[0].astype(BF))
    conv_w8 = jnp.zeros((8, D), F32).at[:3].set(conv_w[0])
    x3, hf, aff_t = _merge1(bg, z, conv_w8, conv_w_out[0].astype(BF), x2, vec(g_m),
                            norm_ffn[1].reshape(1, D), vec(sh_f), vec(sc_f), router_w[1].T)
    return _moe(x3, hf, aff_t, vec(g_f), moe_w_gate, moe_w_up, moe_w_down, 1)
```

```python
import functools
import math

import jax
import jax.numpy as jnp
from jax import lax
from jax.experimental import pallas as pl
from jax.experimental.pallas import tpu as pltpu

BF = jnp.bfloat16
F32 = jnp.float32
I32 = jnp.int32

D = 1024
GRID_W = 64
N_HEADS = 6
HEAD_DIM = 64
V_DIM = 2 * HEAD_DIM
QK_W = N_HEADS * 2 * HEAD_DIM
V_W = N_HEADS * V_DIM
FOUR_W = 256
FOUR_G = 64
IN_W = 2 * QK_W + V_W + FOUR_W
N_EXP = 16
CAP_FACTOR = 2
ROPE_BASE = 10000.0
EPS = 1e-6
LANES = 128
MIB = 1024 * 1024

NT = (((1,), (1,)), ((), ()))


def _params(sem, vmem_mib):
    return pltpu.CompilerParams(dimension_semantics=sem, vmem_limit_bytes=vmem_mib * MIB)


def _split2(x):
    hi = x.astype(BF)
    lo = (x - hi.astype(F32)).astype(BF)
    return hi, lo


def _dot3(a, b, dims=(((1,), (0,)), ((), ()))):
    ah, al = _split2(a)
    bh, bl = _split2(b)
    dg = functools.partial(lax.dot_general, dimension_numbers=dims, preferred_element_type=F32)
    return dg(ah, bh) + dg(ah, bl) + dg(al, bh)


def _modulate(x, nw, shift, scale):
    ms = jnp.mean(x * x, axis=-1, keepdims=True)
    return (x * lax.rsqrt(ms + EPS) * nw) * (1.0 + scale) + shift


def _ada_kernel(c_ref, w_ref, b_ref, o_ref):
    cv = c_ref[...]
    s = cv * (1.0 / (1.0 + jnp.exp(-cv)))
    o_ref[0] = _dot3(s, w_ref[0]) + b_ref[0]


def _ada(cond8, ada_w, ada_b):
    depth = ada_w.shape[0]
    tn = 1536
    return pl.pallas_call(
        _ada_kernel,
        grid=(depth, 6 * D // tn),
        in_specs=[
            pl.BlockSpec((8, D), lambda l, j: (0, 0)),
            pl.BlockSpec((1, D, tn), lambda l, j: (l, 0, j)),
            pl.BlockSpec((1, 1, tn), lambda l, j: (l, 0, j)),
        ],
        out_specs=pl.BlockSpec((1, 8, tn), lambda l, j: (l, 0, j)),
        out_shape=jax.ShapeDtypeStruct((depth, 8, 6 * D), F32),
        compiler_params=_params(("arbitrary", "arbitrary"), 40),
        name="ada",
    )(cond8, ada_w, ada_b.reshape(depth, 1, 6 * D))


def _norm_rope(xb, wn, cos, sin_s, out_scale):
    lane = lax.broadcasted_iota(I32, xb.shape, 1)
    lo = lane < HEAD_DIM
    ss = xb * xb
    s_lo = jnp.sum(jnp.where(lo, ss, 0.0), axis=-1, keepdims=True)
    s_hi = jnp.sum(jnp.where(lo, 0.0, ss), axis=-1, keepdims=True)
    inv = jnp.where(lo, lax.rsqrt(s_lo * (1.0 / HEAD_DIM) + EPS),
                    lax.rsqrt(s_hi * (1.0 / HEAD_DIM) + EPS))
    y = xb * inv * wn
    fwd = pltpu.roll(y, LANES - 16, axis=1)
    bwd = pltpu.roll(y, 16, axis=1)
    rot = jnp.where((lane & 31) < 16, fwd, bwd)
    out = y * cos + rot * sin_s
    if out_scale != 1.0:
        out = out * out_scale
    return out


def _proj0_kernel(x_ref, nw_ref, sh_ref, sc_ref, w_ref, cos_ref, sin_ref, qn_ref, kn_ref,
                  q_ref, k_ref, v_ref, f_ref):
    h = _modulate(x_ref[0], nw_ref[...], sh_ref[0, 0], sc_ref[0, 0]).astype(BF)
    cos = cos_ref[...]
    sin_s = sin_ref[...]
    qn = qn_ref[...]
    kn = kn_ref[...]
    q_scale = HEAD_DIM ** -0.5 * math.log2(math.e)
    for hh in range(N_HEADS):
        c0 = hh * V_DIM
        pq = jnp.dot(h, w_ref[:, c0:c0 + V_DIM], preferred_element_type=F32)
        q_ref[0, :, c0:c0 + V_DIM] = _norm_rope(pq, qn, cos, sin_s, q_scale).astype(BF)
        pk = jnp.dot(h, w_ref[:, QK_W + c0:QK_W + c0 + V_DIM], preferred_element_type=F32)
        k_ref[0, :, c0:c0 + V_DIM] = _norm_rope(pk, kn, cos, sin_s, 1.0).astype(BF)
    v_ref[0] = jnp.dot(h, w_ref[:, 2 * QK_W:2 * QK_W + V_W], preferred_element_type=F32).astype(BF)
    f_ref[0] = jnp.dot(h, w_ref[:, 2 * QK_W + V_W:], preferred_element_type=F32).astype(BF)


def _proj0(xcat, nw, sh2, sc2, w_bf, cos, sin_s, qn, kn, n_lat):
    b, s_tot, _ = xcat.shape
    tm = 256
    lat_tiles = n_lat // tm
    mod_spec = pl.BlockSpec((1, 1, 1, D), lambda bb, i: (bb, i // lat_tiles, 0, 0))
    full = lambda shape: pl.BlockSpec(shape, lambda bb, i: tuple(0 for _ in shape))
    row = lambda w: pl.BlockSpec((1, tm, w), lambda bb, i: (bb, i, 0))
    return pl.pallas_call(
        _proj0_kernel,
        grid=(b, s_tot // tm),
        in_specs=[
            row(D), full((1, D)), mod_spec, mod_spec, full((D, IN_W)),
            pl.BlockSpec((tm, LANES), lambda bb, i: (i, 0)),
            pl.BlockSpec((tm, LANES), lambda bb, i: (i, 0)),
            full((1, LANES)), full((1, LANES)),
        ],
        out_specs=[row(QK_W), row(QK_W), row(V_W), row(FOUR_W)],
        out_shape=[
            jax.ShapeDtypeStruct((b, s_tot, QK_W), BF),
            jax.ShapeDtypeStruct((b, s_tot, QK_W), BF),
            jax.ShapeDtypeStruct((b, s_tot, V_W), BF),
            jax.ShapeDtypeStruct((b, s_tot, FOUR_W), BF),
        ],
        compiler_params=_params(("parallel", "arbitrary"), 48),
        name="proj0",
    )(xcat, nw, sh2, sc2, w_bf, cos, sin_s, qn, kn)


def _attn_kernel(q_ref, k_ref, v_ref, lam_ref, sub_ref, o_ref, vaug_ref, s_ref, *, lam_init, kc):
    @pl.when(pl.program_id(2) == 0)
    def _():
        vaug_ref[:, :V_DIM] = v_ref[0]
        lane_v = lax.broadcasted_iota(I32, (vaug_ref.shape[0], V_DIM), 1)
        vaug_ref[:, V_DIM:] = jnp.where(lane_v == 0, 1.0, 0.0).astype(BF)

    q = q_ref[0]
    lane = lax.broadcasted_iota(I32, q.shape, 1)
    zero = jnp.zeros_like(q)
    lv = lam_ref[...]
    t1 = jnp.sum(lv[0:1] * lv[1:2], axis=-1, keepdims=True)
    t2 = jnp.sum(lv[2:3] * lv[3:4], axis=-1, keepdims=True)
    lam = jnp.exp(t1) - jnp.exp(t2) + lam_init

    qs = (jnp.where(lane < HEAD_DIM, q, zero), jnp.where(lane < HEAD_DIM, zero, q))
    tq = q.shape[0]
    n_chunks = k_ref.shape[1] // kc

    ms = [jnp.full((tq, LANES), -jnp.inf, F32) for _ in range(2)]
    for c in range(n_chunks):
        k_c = k_ref[0, c * kc:(c + 1) * kc, :]
        for h in range(2):
            s = lax.dot_general(qs[h], k_c, NT, preferred_element_type=F32)
            s_ref[h, c] = s
            for j in range(kc // LANES):
                ms[h] = jnp.maximum(ms[h], s[:, j * LANES:(j + 1) * LANES])
    ms = [jnp.max(m, axis=-1, keepdims=True) for m in ms]

    acc = [jnp.zeros((tq, 2 * V_DIM), F32) for _ in range(2)]
    for c in range(n_chunks):
        v_c = vaug_ref[c * kc:(c + 1) * kc, :]
        for h in range(2):
            e = jnp.exp2((s_ref[h, c] - ms[h]).astype(BF))
            acc[h] = acc[h] + jnp.dot(e, v_c, preferred_element_type=F32)
    o0, o1 = acc
    o = (o0[:, :V_DIM] * (1.0 / o0[:, V_DIM:V_DIM + 1])
         - o1[:, :V_DIM] * (lam / o1[:, V_DIM:V_DIM + 1]))
    ms = jnp.mean(o * o, axis=-1, keepdims=True)
    o_ref[0] = (o * lax.rsqrt(ms + EPS) * sub_ref[...] * (1.0 - lam_init)).astype(BF)


def _attention(q, k, v, lamv, subln, n_lat, lam_init):
    b, s_tot, _ = k.shape
    tq = 512
    kc = 256
    assert s_tot % kc == 0
    kv_spec = pl.BlockSpec((1, s_tot, V_DIM), lambda bb, hh, i: (bb, 0, hh))
    q_spec = pl.BlockSpec((1, tq, V_DIM), lambda bb, hh, i: (bb, i, hh))
    return pl.pallas_call(
        functools.partial(_attn_kernel, lam_init=lam_init, kc=kc),
        grid=(b, N_HEADS, n_lat // tq),
        in_specs=[
            q_spec, kv_spec, kv_spec,
            pl.BlockSpec((8, LANES), lambda bb, hh, i: (0, 0)),
            pl.BlockSpec((1, LANES), lambda bb, hh, i: (0, 0)),
        ],
        out_specs=q_spec,
        out_shape=jax.ShapeDtypeStruct((b, n_lat, V_W), BF),
        scratch_shapes=[pltpu.VMEM((s_tot, 2 * V_DIM), BF),
                        pltpu.VMEM((2, s_tot // kc, tq, kc), F32)],
        compiler_params=_params(("parallel", "parallel", "arbitrary"), 48),
        name="diff_attn",
    )(q, k, v, lamv, subln)


def _fourier_kernel(f_ref, bd_ref, w_ref, o_ref, g_ref):
    nb, n_lat, _ = f_ref.shape

    @pl.when(pl.program_id(0) == 0)
    def _():
        rows = 1024
        for bb in range(nb):
            for r in range(n_lat // rows):
                fc = jnp.dot(f_ref[bb, r * rows:(r + 1) * rows, :], bd_ref[...],
                             preferred_element_type=F32)
                g_ref[r * rows:(r + 1) * rows, bb * FOUR_W:(bb + 1) * FOUR_W] = fc[:, :FOUR_W].astype(BF)
                g_ref[n_lat + r * rows:n_lat + (r + 1) * rows,
                      bb * FOUR_W:(bb + 1) * FOUR_W] = fc[:, FOUR_W:].astype(BF)

    y = jnp.dot(w_ref[...], g_ref[...], preferred_element_type=F32)
    for bb in range(nb):
        o_ref[bb] = y[:, bb * FOUR_W:(bb + 1) * FOUR_W].astype(BF)


def _fourier(f, bd, wbig, n_lat):
    b = f.shape[0]
    tr = 256
    return pl.pallas_call(
        _fourier_kernel,
        grid=(n_lat // tr,),
        in_specs=[
            pl.BlockSpec((b, n_lat, FOUR_W), lambda i: (0, 0, 0)),
            pl.BlockSpec((FOUR_W, 2 * FOUR_W), lambda i: (0, 0)),
            pl.BlockSpec((tr, 2 * n_lat), lambda i: (i, 0)),
        ],
        out_specs=pl.BlockSpec((b, tr, FOUR_W), lambda i: (0, i, 0)),
        out_shape=jax.ShapeDtypeStruct((b, n_lat, FOUR_W), BF),
        scratch_shapes=[pltpu.VMEM((2 * n_lat, b * FOUR_W), BF)],
        compiler_params=_params(("arbitrary",), 56),
        name="fourier",
    )(f, bd, wbig)


def _residual_router(x, y, gm, nf, shf, scf, wr_t, x1_ref, hf_ref, aff_ref):
    x1 = x + gm * y
    x1_ref[0] = x1
    hf = _modulate(x1, nf, shf, scf)
    hf_ref[0] = hf
    logits = _dot3(wr_t, hf, NT)
    m = jnp.max(logits, axis=0, keepdims=True)
    e = jnp.exp(logits - m)
    aff_ref[0] = e / jnp.sum(e, axis=0, keepdims=True)


def _merge0_kernel(att_ref, four_ref, w_ref, x_ref, gm_ref, nf_ref, shf_ref, scf_ref, wr_ref,
                   x1_ref, hf_ref, aff_ref):
    y = jnp.dot(att_ref[0], w_ref[:V_W, :], preferred_element_type=F32)
    y = y + jnp.dot(four_ref[0], w_ref[V_W:, :], preferred_element_type=F32)
    _residual_router(x_ref[0], y, gm_ref[0], nf_ref[...], shf_ref[0], scf_ref[0], wr_ref[...],
                     x1_ref, hf_ref, aff_ref)


def _router_specs(tm):
    vec = pl.BlockSpec((1, 1, D), lambda bb, i: (bb, 0, 0))
    row = pl.BlockSpec((1, tm, D), lambda bb, i: (bb, i, 0))
    in_specs = [vec, pl.BlockSpec((1, D), lambda bb, i: (0, 0)), vec, vec,
                pl.BlockSpec((N_EXP, D), lambda bb, i: (0, 0))]
    out_specs = [row, row, pl.BlockSpec((1, N_EXP, tm), lambda bb, i: (bb, 0, i))]
    return row, in_specs, out_specs


def _router_out_shape(b, n):
    return [jax.ShapeDtypeStruct((b, n, D), F32), jax.ShapeDtypeStruct((b, n, D), F32),
            jax.ShapeDtypeStruct((b, N_EXP, n), F32)]


def _merge0(att, four, w_bf, x, gm, nf, shf, scf, wr_t):
    b, n, _ = x.shape
    tm = 512
    row, r_in, r_out = _router_specs(tm)
    return pl.pallas_call(
        _merge0_kernel,
        grid=(b, n // tm),
        in_specs=[
            pl.BlockSpec((1, tm, V_W), lambda bb, i: (bb, i, 0)),
            pl.BlockSpec((1, tm, FOUR_W), lambda bb, i: (bb, i, 0)),
            pl.BlockSpec((D, D), lambda bb, i: (0, 0)),
            row,
        ] + r_in,
        out_specs=r_out,
        out_shape=_router_out_shape(b, n),
        compiler_params=_params(("parallel", "arbitrary"), 48),
        name="merge0",
    )(att, four, w_bf, x, gm, nf, shf, scf, wr_t)


def _proj1_kernel(x_ref, nw_ref, sh_ref, sc_ref, w_ref, bg_ref, z_ref):
    h = _modulate(x_ref[0], nw_ref[...], sh_ref[0], sc_ref[0]).astype(BF)
    bg_ref[0] = jnp.dot(h, w_ref[:, :D], preferred_element_type=F32)
    cg = jnp.dot(h, w_ref[:, D:2 * D], preferred_element_type=F32)
    u = jnp.dot(h, w_ref[:, 2 * D:], preferred_element_type=F32)
    z_ref[0] = cg * u


def _proj1(x, nw, sh, sc, w_bf):
    b, n, _ = x.shape
    tm = 512
    vec = pl.BlockSpec((1, 1, D), lambda bb, i: (bb, 0, 0))
    row = pl.BlockSpec((1, tm, D), lambda bb, i: (bb, i, 0))
    return pl.pallas_call(
        _proj1_kernel,
        grid=(b, n // tm),
        in_specs=[row, pl.BlockSpec((1, D), lambda bb, i: (0, 0)), vec, vec,
                  pl.BlockSpec((D, 3 * D), lambda bb, i: (0, 0))],
        out_specs=[row, row],
        out_shape=[jax.ShapeDtypeStruct((b, n, D), F32), jax.ShapeDtypeStruct((b, n, D), F32)],
        compiler_params=_params(("parallel", "arbitrary"), 48),
        name="proj1",
    )(x, nw, sh, sc, w_bf)


def _merge1_kernel(bg_ref, z_ref, zp_ref, zn_ref, cw_ref, w_ref, x_ref, gm_ref, nf_ref, shf_ref,
                   scf_ref, wr_ref, x1_ref, hf_ref, aff_ref):
    i = pl.program_id(1)
    last = pl.num_programs(1) - 1
    z = z_ref[0]
    tm = z.shape[0]
    rowid = lax.broadcasted_iota(I32, z.shape, 0)
    prev_row = jnp.where(i > 0, zp_ref[0, 7:8, :], 0.0)
    next_row = jnp.where(i < last, zn_ref[0, 0:1, :], 0.0)
    z_up = jnp.where(rowid == 0, prev_row, pltpu.roll(z, 1, axis=0))
    z_dn = jnp.where(rowid == tm - 1, next_row, pltpu.roll(z, tm - 1, axis=0))
    cw = cw_ref[...]
    conv = cw[0:1] * z_up + cw[1:2] * z + cw[2:3] * z_dn
    y = jnp.dot((bg_ref[0] * conv).astype(BF), w_ref[...], preferred_element_type=F32)
    _residual_router(x_ref[0], y, gm_ref[0], nf_ref[...], shf_ref[0], scf_ref[0], wr_ref[...],
                     x1_ref, hf_ref, aff_ref)


def _merge1(bg, z, conv_w8, w_bf, x, gm, nf, shf, scf, wr_t):
    b, n, _ = x.shape
    tm = 512
    halo = 8
    per = tm // halo
    n_halo = n // halo
    row, r_in, r_out = _router_specs(tm)
    return pl.pallas_call(
        _merge1_kernel,
        grid=(b, n // tm),
        in_specs=[
            row, row,
            pl.BlockSpec((1, halo, D), lambda bb, i: (bb, jnp.maximum(i * per - 1, 0), 0)),
            pl.BlockSpec((1, halo, D), lambda bb, i: (bb, jnp.minimum((i + 1) * per, n_halo - 1), 0)),
            pl.BlockSpec((8, D), lambda bb, i: (0, 0)),
            pl.BlockSpec((D, D), lambda bb, i: (0, 0)),
            row,
        ] + r_in,
        out_specs=r_out,
        out_shape=_router_out_shape(b, n),
        compiler_params=_params(("parallel", "arbitrary"), 48),
        name="merge1",
    )(bg, z, z, z, conv_w8, w_bf, x, gm, nf, shf, scf, wr_t)


def _select_kernel(a_ref, idx_ref, gate_ref, *, cap):
    a = a_ref[0]
    rows = a.shape[0]
    chunks = rows // N_EXP
    assert chunks & (chunks - 1) == 0
    shift = chunks.bit_length() - 1
    ri = lax.broadcasted_iota(I32, (rows, rows), 0)
    ci = lax.broadcasted_iota(I32, (rows, rows), 1)
    same_e = (ri >> shift) == (ci >> shift)
    same_f = jnp.where(same_e, 1.0, 0.0)
    same = same_f.astype(BF)
    lower = (same_f * jnp.where(ci < ri, 1.0, 0.0)).astype(BF)
    li = lax.broadcasted_iota(I32, (LANES, LANES), 0)
    lj = lax.broadcasted_iota(I32, (LANES, LANES), 1)
    incl = jnp.where(li <= lj, 1.0, 0.0).astype(BF)

    def bcast(col):
        return jnp.broadcast_to(col, (rows, LANES))

    def expert_count(maskf):
        rc = bcast(jnp.sum(maskf, axis=-1, keepdims=True)).astype(BF)
        return jnp.dot(same, rc, preferred_element_type=F32)

    def prefix(maskf):
        cl = jnp.dot(maskf.astype(BF), incl, preferred_element_type=F32)
        tot = bcast(cl[:, LANES - 1:LANES])
        off = jnp.dot(lower, tot.astype(BF), preferred_element_type=F32)
        return cl, off, tot

    min_normal = 0x00800000

    def search(step, t):
        cand = t | jnp.left_shift(jnp.int32(1), 30 - step)
        cnt = expert_count(jnp.where(a >= lax.bitcast_convert_type(cand, F32), 1.0, 0.0))
        return jnp.where(cnt >= cap, jnp.where(cand >= min_normal, cand, t), t)

    thr = lax.fori_loop(0, 31, search, jnp.zeros((rows, LANES), I32))
    thr_f = lax.bitcast_convert_type(thr, F32)
    nxt_f = lax.bitcast_convert_type(jnp.maximum(thr + 1, min_normal), F32)
    gtf = jnp.where(a >= nxt_f, 1.0, 0.0)
    eqf = jnp.where(a >= thr_f, 1.0, 0.0) - gtf
    need = cap - expert_count(gtf)
    cl_eq, off_eq, _ = prefix(eqf)
    self_f = gtf + eqf * jnp.where(cl_eq + off_eq <= need, 1.0, 0.0)
    cl, off, tot = prefix(self_f)
    inc = off + tot
    cl_bf = cl.astype(BF)
    a_hi = a.astype(BF)
    r1 = a - a_hi.astype(F32)
    a_mid = r1.astype(BF)
    a_lo = (r1 - a_mid.astype(F32)).astype(BF)

    slot = lax.broadcasted_iota(I32, (cap, LANES), 0).astype(F32)
    lane_f = lax.broadcasted_iota(I32, (cap, LANES), 1).astype(F32)
    dsub = lax.broadcasted_iota(I32, (chunks, LANES), 0)
    dlane = lax.broadcasted_iota(I32, (chunks, LANES), 1)
    rsel = lax.broadcasted_iota(I32, (cap, rows), 1).astype(F32)
    big = float(2 * cap * N_EXP)
    for e in range(N_EXP):
        r0 = e * chunks
        diag = dsub == dlane
        inc_row = jnp.sum(jnp.where(diag, inc[r0:r0 + chunks], 0.0), axis=0, keepdims=True)
        off_row = jnp.sum(jnp.where(diag, off[r0:r0 + chunks], 0.0), axis=0, keepdims=True)
        inc_row = jnp.where(dlane[0:1] < chunks, inc_row, big)
        jc = jnp.sum(jnp.where(inc_row <= slot, 1.0, 0.0), axis=-1, keepdims=True)
        offsel = jnp.sum(jnp.where(lane_f == jc, off_row, 0.0), axis=-1, keepdims=True)
        pick = jnp.where(rsel == jc + float(r0), 1.0, 0.0).astype(BF)
        clsel = jnp.dot(pick, cl_bf, preferred_element_type=F32)
        lc = jnp.sum(jnp.where(clsel <= slot - offsel, 1.0, 0.0), axis=-1, keepdims=True)
        asel = (jnp.dot(pick, a_hi, preferred_element_type=F32)
                + jnp.dot(pick, a_mid, preferred_element_type=F32)
                + jnp.dot(pick, a_lo, preferred_element_type=F32))
        gate = jnp.sum(jnp.where(lane_f == lc, asel, 0.0), axis=-1, keepdims=True)
        idx_ref[0, e] = jnp.broadcast_to(jc * float(LANES) + lc, (cap, LANES)).astype(I32)
        gate_ref[0, e] = jnp.broadcast_to(gate, (cap, LANES))


def _select(aff_rows, cap):
    b, rows, _ = aff_rows.shape
    out = jax.ShapeDtypeStruct((b, N_EXP, cap, LANES), I32)
    outg = jax.ShapeDtypeStruct((b, N_EXP, cap, LANES), F32)
    spec = pl.BlockSpec((1, N_EXP, cap, LANES), lambda bb: (bb, 0, 0, 0))
    return pl.pallas_call(
        functools.partial(_select_kernel, cap=cap),
        grid=(b,),
        in_specs=[pl.BlockSpec((1, rows, LANES), lambda bb: (bb, 0, 0))],
        out_specs=[spec, spec],
        out_shape=[out, outg],
        compiler_params=_params(("parallel",), 48),
        name="select",
    )(aff_rows)


def _gather_kernel(idx_ref, h_ref, o_ref, rows_ref):
    cap = rows_ref.shape[0]

    def body(c, carry):
        rows_ref[pl.ds(c, 1), :] = h_ref[0, pl.ds(idx_ref[0, 0, c], 1), :]
        return carry

    lax.fori_loop(0, cap, body, 0, unroll=8)
    o_ref[0, 0] = rows_ref[...].astype(BF)


def _gather(idx, hf, cap):
    b, n, _ = hf.shape
    return pl.pallas_call(
        _gather_kernel,
        grid=(b, N_EXP),
        in_specs=[
            pl.BlockSpec((1, 1, cap), lambda bb, e: (bb * N_EXP + e, 0, 0), memory_space=pltpu.SMEM),
            pl.BlockSpec((1, n, D), lambda bb, e: (bb, 0, 0)),
        ],
        out_specs=pl.BlockSpec((1, 1, cap, D), lambda bb, e: (bb, e, 0, 0)),
        out_shape=jax.ShapeDtypeStruct((b, N_EXP, cap, D), BF),
        scratch_shapes=[pltpu.VMEM((cap, D), F32)],
        compiler_params=_params(("parallel", "arbitrary"), 48),
        name="gather",
    )(idx, hf)


def _ffn_kernel(x_ref, wg_ref, wu_ref, wd_ref, o_ref, wg_s, wu_s, wd_s):
    @pl.when(pl.program_id(1) == 0)
    def _():
        wg_s[...] = wg_ref[...].astype(BF)
        wu_s[...] = wu_ref[...].astype(BF)
        wd_s[...] = wd_ref[...].astype(BF)

    x = x_ref[0, 0]
    g = jnp.dot(x, wg_s[...], preferred_element_type=F32)
    u = jnp.dot(x, wu_s[...], preferred_element_type=F32)
    act = (g * (1.0 / (1.0 + jnp.exp(-g))) * u).astype(BF)
    o_ref[0, 0] = jnp.dot(act, wd_s[...], preferred_element_type=F32)


def _ffn(xs, w_gate, w_up, w_down, layer):
    b, _, cap, _ = xs.shape
    dexp = w_gate.shape[-1]
    xspec = pl.BlockSpec((1, 1, cap, D), lambda e, bb: (bb, e, 0, 0))
    wspec_in = pl.BlockSpec((None, None, D, dexp), lambda e, bb: (layer, e, 0, 0))
    wspec_out = pl.BlockSpec((None, None, dexp, D), lambda e, bb: (layer, e, 0, 0))
    return pl.pallas_call(
        _ffn_kernel,
        grid=(N_EXP, b),
        in_specs=[xspec, wspec_in, wspec_in, wspec_out],
        out_specs=xspec,
        out_shape=jax.ShapeDtypeStruct((b, N_EXP, cap, D), F32),
        scratch_shapes=[pltpu.VMEM((D, dexp), BF), pltpu.VMEM((D, dexp), BF), pltpu.VMEM((dexp, D), BF)],
        compiler_params=_params(("arbitrary", "arbitrary"), 56),
        name="expert_ffn",
    )(xs, w_gate, w_up, w_down)


def _scatter_kernel(idx_ref, gate_ref, y_ref, x_ref, gf_ref, o_ref):
    cap = y_ref.shape[2]

    @pl.when(pl.program_id(2) == 0)
    def _():
        o_ref[...] = x_ref[...]

    gf = gf_ref[0]

    def body(c, carry):
        t = idx_ref[0, 0, c]
        coef = gf * gate_ref[0, 0, c]
        o_ref[0, pl.ds(t, 1), :] = o_ref[0, pl.ds(t, 1), :] + coef * y_ref[0, 0, pl.ds(c, 1), :]
        return carry

    lax.fori_loop(0, cap, body, 0, unroll=8)


def _scatter(idx, gate, y, x, gf):
    b, n, _ = x.shape
    cap = y.shape[2]
    half = D // 2
    sm = lambda: pl.BlockSpec((1, 1, cap), lambda bb, j, e: (bb * N_EXP + e, 0, 0),
                              memory_space=pltpu.SMEM)
    xspec = pl.BlockSpec((1, n, half), lambda bb, j, e: (bb, 0, j))
    return pl.pallas_call(
        _scatter_kernel,
        grid=(b, D // half, N_EXP),
        in_specs=[
            sm(), sm(),
            pl.BlockSpec((1, 1, cap, half), lambda bb, j, e: (bb, e, 0, j)),
            xspec,
            pl.BlockSpec((1, 1, half), lambda bb, j, e: (bb, 0, j)),
        ],
        out_specs=xspec,
        out_shape=jax.ShapeDtypeStruct((b, n, D), F32),
        compiler_params=_params(("parallel", "parallel", "arbitrary"), 48),
        name="scatter_add",
    )(idx, gate, y, x, gf)


def _moe(x1, hf, aff_t, gf, w_gate, w_up, w_down, layer):
    b, n, _ = x1.shape
    cap = CAP_FACTOR * n // N_EXP
    aff_rows = aff_t.reshape(b, N_EXP * (n // LANES), LANES)
    idx_b, gate_b = _select(aff_rows, cap)
    idx = idx_b[..., 0].reshape(b * N_EXP, 1, cap)
    gate = gate_b[..., 0].reshape(b * N_EXP, 1, cap)
    xs = _gather(idx, hf, cap)
    y = _ffn(xs, w_gate, w_up, w_down, layer)
    return _scatter(idx, gate, y, x1, gf)


def _rope_tables(n_lat, n_ctx):
    rows = n_lat // GRID_W
    r = jnp.repeat(jnp.arange(rows, dtype=F32), GRID_W)
    col = jnp.tile(jnp.arange(GRID_W, dtype=F32), rows)
    n_freq = HEAD_DIM // 4
    inv = ROPE_BASE ** (-jnp.arange(n_freq, dtype=F32) / n_freq)
    ar = r[:, None] * inv
    ac = col[:, None] * inv
    ang = jnp.concatenate([ar, ar, ac, ac], axis=-1)
    sign = jnp.where((jnp.arange(HEAD_DIM) % 32) < 16, -1.0, 1.0).astype(F32)
    cos = jnp.concatenate([jnp.cos(ang), jnp.ones((n_ctx, HEAD_DIM), F32)], axis=0)
    sin_s = jnp.concatenate([jnp.sin(ang) * sign, jnp.zeros((n_ctx, HEAD_DIM), F32)], axis=0)
    return jnp.tile(cos, (1, 2)), jnp.tile(sin_s, (1, 2))


def _dft_tables(n_lat):
    c = jnp.arange(FOUR_G, dtype=I32)
    ang_c = (2.0 * math.pi / FOUR_G) * ((c[:, None] * c[None, :]) % FOUR_G).astype(F32)
    eye = jnp.eye(FOUR_W // FOUR_G, dtype=F32)
    bd = jnp.concatenate([jnp.kron(eye, jnp.cos(ang_c)), jnp.kron(eye, jnp.sin(ang_c))], axis=1)
    bd = (bd * FOUR_G ** -0.5).astype(BF)
    side = int(round(math.sqrt(n_lat)))
    assert side * side == n_lat
    k = jnp.arange(n_lat, dtype=I32)[:, None]
    s = jnp.arange(side, dtype=I32)[None, :]
    alpha = (2.0 * math.pi / side) * ((k * s) % side).astype(F32)
    beta = (2.0 * math.pi / n_lat) * ((k * s) % n_lat).astype(F32)
    ca, sa, cb, sb = jnp.cos(alpha), jnp.sin(alpha), jnp.cos(beta), jnp.sin(beta)
    cosw = (ca[:, :, None] * cb[:, None, :] - sa[:, :, None] * sb[:, None, :]).reshape(n_lat, n_lat)
    sinw = (sa[:, :, None] * cb[:, None, :] + ca[:, :, None] * sb[:, None, :]).reshape(n_lat, n_lat)
    wbig = (jnp.concatenate([cosw, -sinw], axis=1) * n_lat ** -0.5).astype(BF)
    return bd, wbig


def kernel(x, c, ctx, c_ctx, ada_w, ada_b, norm_mix, norm_ffn, attn_w_in, attn_q_norm, attn_k_norm,
           lam_q1, lam_k1, lam_q2, lam_k2, attn_subln, attn_w_out, conv_w_in, conv_w, conv_w_out,
           router_w, moe_w_gate, moe_w_up, moe_w_down):
    b, n, _ = x.shape
    n_ctx = ctx.shape[1]
    assert x.shape[2] == D and n % 512 == 0 and n_ctx % 256 == 0

    cond8 = jnp.concatenate([c, c_ctx[None, :], jnp.zeros((8 - b - 1, D), F32)], axis=0)
    ada = _ada(cond8, ada_w, ada_b)

    def mods(layer):
        m = ada[layer].reshape(8, 6, D)
        return [m[:, j] for j in range(6)]

    vec = lambda t: t[:b].reshape(b, 1, D)

    sh_m, sc_m, g_m, sh_f, sc_f, g_f = mods(0)
    both = lambda t: jnp.stack([t[:b], jnp.broadcast_to(t[b], (b, D))], axis=1).reshape(b, 2, 1, D)
    cos, sin_s = _rope_tables(n, n_ctx)
    bd, wbig = _dft_tables(n)
    tile2 = lambda t: jnp.tile(t.reshape(1, HEAD_DIM), (1, 2))
    xcat = jnp.concatenate([x, ctx], axis=1)
    q, k, v, f = _proj0(xcat, norm_mix[0].reshape(1, D), both(sh_m), both(sc_m),
                        attn_w_in[0].astype(BF), cos, sin_s,
                        tile2(attn_q_norm[0]), tile2(attn_k_norm[0]), n)
    lam_init = 0.8 - 0.6 * math.exp(-0.3 * 0)
    lamv = jnp.zeros((8, LANES), F32).at[:4, :HEAD_DIM].set(
        jnp.stack([lam_q1[0], lam_k1[0], lam_q2[0], lam_k2[0]]))
    att = _attention(q, k, v, lamv, attn_subln[0].reshape(1, V_DIM), n, lam_init)
    four = _fourier(f, bd, wbig, n)
    x1, hf, aff_t = _merge0(att, four, attn_w_out[0].astype(BF), x, vec(g_m),
                            norm_ffn[0].reshape(1, D), vec(sh_f), vec(sc_f), router_w[0].T)
    x2 = _moe(x1, hf, aff_t, vec(g_f), moe_w_gate, moe_w_up, moe_w_down, 0)

    sh_m, sc_m, g_m, sh_f, sc_f, g_f = mods(1)
    bg, z = _proj1(x2, norm_mix[1].reshape(1, D), vec(sh_m), vec(sc_m), conv_w_in[0].astype(BF))
    conv_w8 = jnp.zeros((8, D), F32).at[:3].set(conv_w[0])
    x3, hf, aff_t = _merge1(bg, z, conv_w8, conv_w_out[0].astype(BF), x2, vec(g_m),
                            norm_ffn[1].reshape(1, D), vec(sh_f), vec(sc_f), router_w[1].T)
    return _moe(x3, hf, aff_t, vec(g_f), moe_w_gate, moe_w_up, moe_w_down, 1)
```

```python
import functools
import math

import jax
import jax.numpy as jnp
from jax import lax
from jax.experimental import pallas as pl
from jax.experimental.pallas import tpu as pltpu

BF = jnp.bfloat16
F32 = jnp.float32
I32 = jnp.int32

D = 1024
GRID_W = 64
N_HEADS = 6
HEAD_DIM = 64
V_DIM = 2 * HEAD_DIM
QK_W = N_HEADS * 2 * HEAD_DIM
V_W = N_HEADS * V_DIM
FOUR_W = 256
FOUR_G = 64
IN_W = 2 * QK_W + V_W + FOUR_W
N_EXP = 16
CAP_FACTOR = 2
ROPE_BASE = 10000.0
EPS = 1e-6
LANES = 128
MIB = 1024 * 1024

NT = (((1,), (1,)), ((), ()))


def _params(sem, vmem_mib):
    return pltpu.CompilerParams(dimension_semantics=sem, vmem_limit_bytes=vmem_mib * MIB)


def _split2(x):
    hi = x.astype(BF)
    lo = (x - hi.astype(F32)).astype(BF)
    return hi, lo


def _dot3(a, b, dims=(((1,), (0,)), ((), ()))):
    ah, al = _split2(a)
    bh, bl = _split2(b)
    dg = functools.partial(lax.dot_general, dimension_numbers=dims, preferred_element_type=F32)
    return dg(ah, bh) + dg(ah, bl) + dg(al, bh)


def _modulate(x, nw, shift, scale):
    ms = jnp.mean(x * x, axis=-1, keepdims=True)
    return (x * lax.rsqrt(ms + EPS) * nw) * (1.0 + scale) + shift


def _ada_kernel(c_ref, w_ref, b_ref, o_ref):
    cv = c_ref[...]
    s = cv * (1.0 / (1.0 + jnp.exp(-cv)))
    o_ref[0] = _dot3(s, w_ref[0]) + b_ref[0]


def _ada(cond8, ada_w, ada_b):
    depth = ada_w.shape[0]
    tn = 1536
    return pl.pallas_call(
        _ada_kernel,
        grid=(depth, 6 * D // tn),
        in_specs=[
            pl.BlockSpec((8, D), lambda l, j: (0, 0)),
            pl.BlockSpec((1, D, tn), lambda l, j: (l, 0, j)),
            pl.BlockSpec((1, 1, tn), lambda l, j: (l, 0, j)),
        ],
        out_specs=pl.BlockSpec((1, 8, tn), lambda l, j: (l, 0, j)),
        out_shape=jax.ShapeDtypeStruct((depth, 8, 6 * D), F32),
        compiler_params=_params(("arbitrary", "arbitrary"), 40),
        name="ada",
    )(cond8, ada_w, ada_b.reshape(depth, 1, 6 * D))


def _norm_rope(xb, wn, cos, sin_s, out_scale):
    lane = lax.broadcasted_iota(I32, xb.shape, 1)
    lo = lane < HEAD_DIM
    ss = xb * xb
    s_lo = jnp.sum(jnp.where(lo, ss, 0.0), axis=-1, keepdims=True)
    s_hi = jnp.sum(jnp.where(lo, 0.0, ss), axis=-1, keepdims=True)
    inv = jnp.where(lo, lax.rsqrt(s_lo * (1.0 / HEAD_DIM) + EPS),
                    lax.rsqrt(s_hi * (1.0 / HEAD_DIM) + EPS))
    y = xb * inv * wn
    fwd = pltpu.roll(y, LANES - 16, axis=1)
    bwd = pltpu.roll(y, 16, axis=1)
    rot = jnp.where((lane & 31) < 16, fwd, bwd)
    out = y * cos + rot * sin_s
    if out_scale != 1.0:
        out = out * out_scale
    return out


def _proj0_kernel(x_ref, nw_ref, sh_ref, sc_ref, w_ref, cos_ref, sin_ref, qn_ref, kn_ref,
                  q_ref, k_ref, v_ref, f_ref):
    h = _modulate(x_ref[0], nw_ref[...], sh_ref[0, 0], sc_ref[0, 0]).astype(BF)
    cos = cos_ref[...]
    sin_s = sin_ref[...]
    qn = qn_ref[...]
    kn = kn_ref[...]
    q_scale = HEAD_DIM ** -0.5 * math.log2(math.e)
    for hh in range(N_HEADS):
        c0 = hh * V_DIM
        pq = jnp.dot(h, w_ref[:, c0:c0 + V_DIM], preferred_element_type=F32)
        q_ref[0, :, c0:c0 + V_DIM] = _norm_rope(pq, qn, cos, sin_s, q_scale).astype(BF)
        pk = jnp.dot(h, w_ref[:, QK_W + c0:QK_W + c0 + V_DIM], preferred_element_type=F32)
        k_ref[0, :, c0:c0 + V_DIM] = _norm_rope(pk, kn, cos, sin_s, 1.0).astype(BF)
    v_ref[0] = jnp.dot(h, w_ref[:, 2 * QK_W:2 * QK_W + V_W], preferred_element_type=F32).astype(BF)
    f_ref[0] = jnp.dot(h, w_ref[:, 2 * QK_W + V_W:], preferred_element_type=F32).astype(BF)


def _proj0(xcat, nw, sh2, sc2, w_bf, cos, sin_s, qn, kn, n_lat):
    b, s_tot, _ = xcat.shape
    tm = 256
    lat_tiles = n_lat // tm
    mod_spec = pl.BlockSpec((1, 1, 1, D), lambda bb, i: (bb, i // lat_tiles, 0, 0))
    full = lambda shape: pl.BlockSpec(shape, lambda bb, i: tuple(0 for _ in shape))
    row = lambda w: pl.BlockSpec((1, tm, w), lambda bb, i: (bb, i, 0))
    return pl.pallas_call(
        _proj0_kernel,
        grid=(b, s_tot // tm),
        in_specs=[
            row(D), full((1, D)), mod_spec, mod_spec, full((D, IN_W)),
            pl.BlockSpec((tm, LANES), lambda bb, i: (i, 0)),
            pl.BlockSpec((tm, LANES), lambda bb, i: (i, 0)),
            full((1, LANES)), full((1, LANES)),
        ],
        out_specs=[row(QK_W), row(QK_W), row(V_W), row(FOUR_W)],
        out_shape=[
            jax.ShapeDtypeStruct((b, s_tot, QK_W), BF),
            jax.ShapeDtypeStruct((b, s_tot, QK_W), BF),
            jax.ShapeDtypeStruct((b, s_tot, V_W), BF),
            jax.ShapeDtypeStruct((b, s_tot, FOUR_W), BF),
        ],
        compiler_params=_params(("parallel", "arbitrary"), 48),
        name="proj0",
    )(xcat, nw, sh2, sc2, w_bf, cos, sin_s, qn, kn)


def _attn_kernel(q_ref, k_ref, v_ref, lam_ref, sub_ref, o_ref, vaug_ref, s_ref, *, lam_init, kc):
    @pl.when(pl.program_id(2) == 0)
    def _():
        vaug_ref[:, :V_DIM] = v_ref[0]
        lane_v = lax.broadcasted_iota(I32, (vaug_ref.shape[0], V_DIM), 1)
        vaug_ref[:, V_DIM:] = jnp.where(lane_v == 0, 1.0, 0.0).astype(BF)

    q = q_ref[0]
    lane = lax.broadcasted_iota(I32, q.shape, 1)
    zero = jnp.zeros_like(q)
    lv = lam_ref[...]
    t1 = jnp.sum(lv[0:1] * lv[1:2], axis=-1, keepdims=True)
    t2 = jnp.sum(lv[2:3] * lv[3:4], axis=-1, keepdims=True)
    lam = jnp.exp(t1) - jnp.exp(t2) + lam_init

    qs = (jnp.where(lane < HEAD_DIM, q, zero), jnp.where(lane < HEAD_DIM, zero, q))
    tq = q.shape[0]
    n_chunks = k_ref.shape[1] // kc

    ms = [jnp.full((tq, LANES), -jnp.inf, F32) for _ in range(2)]
    for c in range(n_chunks):
        k_c = k_ref[0, c * kc:(c + 1) * kc, :]
        for h in range(2):
            s = lax.dot_general(qs[h], k_c, NT, preferred_element_type=F32)
            s_ref[h, c] = s
            for j in range(kc // LANES):
                ms[h] = jnp.maximum(ms[h], s[:, j * LANES:(j + 1) * LANES])
    ms = [jnp.max(m, axis=-1, keepdims=True) for m in ms]

    acc = [jnp.zeros((tq, 2 * V_DIM), F32) for _ in range(2)]
    for c in range(n_chunks):
        v_c = vaug_ref[c * kc:(c + 1) * kc, :]
        for h in range(2):
            e = jnp.exp2((s_ref[h, c] - ms[h]).astype(BF))
            acc[h] = acc[h] + jnp.dot(e, v_c, preferred_element_type=F32)
    o0, o1 = acc
    o = (o0[:, :V_DIM] * (1.0 / o0[:, V_DIM:V_DIM + 1])
         - o1[:, :V_DIM] * (lam / o1[:, V_DIM:V_DIM + 1]))
    ms = jnp.mean(o * o, axis=-1, keepdims=True)
    o_ref[0] = (o * lax.rsqrt(ms + EPS) * sub_ref[...] * (1.0 - lam_init)).astype(BF)


def _attention(q, k, v, lamv, subln, n_lat, lam_init):
    b, s_tot, _ = k.shape
    tq = 512
    kc = 256
    assert s_tot % kc == 0
    kv_spec = pl.BlockSpec((1, s_tot, V_DIM), lambda bb, hh, i: (bb, 0, hh))
    q_spec = pl.BlockSpec((1, tq, V_DIM), lambda bb, hh, i: (bb, i, hh))
    return pl.pallas_call(
        functools.partial(_attn_kernel, lam_init=lam_init, kc=kc),
        grid=(b, N_HEADS, n_lat // tq),
        in_specs=[
            q_spec, kv_spec, kv_spec,
            pl.BlockSpec((8, LANES), lambda bb, hh, i: (0, 0)),
            pl.BlockSpec((1, LANES), lambda bb, hh, i: (0, 0)),
        ],
        out_specs=q_spec,
        out_shape=jax.ShapeDtypeStruct((b, n_lat, V_W), BF),
        scratch_shapes=[pltpu.VMEM((s_tot, 2 * V_DIM), BF),
                        pltpu.VMEM((2, s_tot // kc, tq, kc), F32)],
        compiler_params=_params(("parallel", "parallel", "arbitrary"), 48),
        name="diff_attn",
    )(q, k, v, lamv, subln)


def _fourier_kernel(f_ref, bd_ref, w_ref, o_ref, g_ref):
    nb, n_lat, _ = f_ref.shape

    @pl.when(pl.program_id(0) == 0)
    def _():
        rows = 1024
        for bb in range(nb):
            for r in range(n_lat // rows):
                fc = jnp.dot(f_ref[bb, r * rows:(r + 1) * rows, :], bd_ref[...],
                             preferred_element_type=F32)
                g_ref[r * rows:(r + 1) * rows, bb * FOUR_W:(bb + 1) * FOUR_W] = fc[:, :FOUR_W].astype(BF)
                g_ref[n_lat + r * rows:n_lat + (r + 1) * rows,
                      bb * FOUR_W:(bb + 1) * FOUR_W] = fc[:, FOUR_W:].astype(BF)

    y = jnp.dot(w_ref[...], g_ref[...], preferred_element_type=F32)
    for bb in range(nb):
        o_ref[bb] = y[:, bb * FOUR_W:(bb + 1) * FOUR_W].astype(BF)


def _fourier(f, bd, wbig, n_lat):
    b = f.shape[0]
    tr = 256
    return pl.pallas_call(
        _fourier_kernel,
        grid=(n_lat // tr,),
        in_specs=[
            pl.BlockSpec((b, n_lat, FOUR_W), lambda i: (0, 0, 0)),
            pl.BlockSpec((FOUR_W, 2 * FOUR_W), lambda i: (0, 0)),
            pl.BlockSpec((tr, 2 * n_lat), lambda i: (i, 0)),
        ],
        out_specs=pl.BlockSpec((b, tr, FOUR_W), lambda i: (0, i, 0)),
        out_shape=jax.ShapeDtypeStruct((b, n_lat, FOUR_W), BF),
        scratch_shapes=[pltpu.VMEM((2 * n_lat, b * FOUR_W), BF)],
        compiler_params=_params(("arbitrary",), 56),
        name="fourier",
    )(f, bd, wbig)


SUB = D // LANES


def _store_token_tiles(ref, val):
    rows = val.shape[0]
    for j in range(SUB):
        ref[0, pl.ds(j, rows, stride=SUB), :] = val[:, j * LANES:(j + 1) * LANES]


def _load_token_tiles(ref, rows):
    return jnp.concatenate([ref[0, pl.ds(j, rows, stride=SUB), :] for j in range(SUB)], axis=1)


def _residual_router(x, y, gm, nf, shf, scf, wr_t, x1_ref, hf_ref, aff_ref):
    x1 = x + gm * y
    _store_token_tiles(x1_ref, x1)
    hf = _modulate(x1, nf, shf, scf)
    _store_token_tiles(hf_ref, hf)
    logits = _dot3(wr_t, hf, NT)
    m = jnp.max(logits, axis=0, keepdims=True)
    e = jnp.exp(logits - m)
    aff_ref[0] = e / jnp.sum(e, axis=0, keepdims=True)


def _merge0_kernel(att_ref, four_ref, w_ref, x_ref, gm_ref, nf_ref, shf_ref, scf_ref, wr_ref,
                   x1_ref, hf_ref, aff_ref):
    y = jnp.dot(att_ref[0], w_ref[:V_W, :], preferred_element_type=F32)
    y = y + jnp.dot(four_ref[0], w_ref[V_W:, :], preferred_element_type=F32)
    _residual_router(x_ref[0], y, gm_ref[0], nf_ref[...], shf_ref[0], scf_ref[0], wr_ref[...],
                     x1_ref, hf_ref, aff_ref)


def _router_specs(tm):
    vec = pl.BlockSpec((1, 1, D), lambda bb, i: (bb, 0, 0))
    row = pl.BlockSpec((1, tm, D), lambda bb, i: (bb, i, 0))
    in_specs = [vec, pl.BlockSpec((1, D), lambda bb, i: (0, 0)), vec, vec,
                pl.BlockSpec((N_EXP, D), lambda bb, i: (0, 0))]
    tiles = _tile_spec(tm)
    out_specs = [tiles, tiles, pl.BlockSpec((1, N_EXP, tm), lambda bb, i: (bb, 0, i))]
    return row, in_specs, out_specs


def _tile_spec(tm):
    return pl.BlockSpec((1, tm * SUB, LANES), lambda bb, i: (bb, i, 0))


def _router_out_shape(b, n):
    return [jax.ShapeDtypeStruct((b, n * SUB, LANES), F32), jax.ShapeDtypeStruct((b, n * SUB, LANES), F32),
            jax.ShapeDtypeStruct((b, N_EXP, n), F32)]


def _merge0(att, four, w_bf, x, gm, nf, shf, scf, wr_t):
    b, n, _ = x.shape
    tm = 512
    row, r_in, r_out = _router_specs(tm)
    return pl.pallas_call(
        _merge0_kernel,
        grid=(b, n // tm),
        in_specs=[
            pl.BlockSpec((1, tm, V_W), lambda bb, i: (bb, i, 0)),
            pl.BlockSpec((1, tm, FOUR_W), lambda bb, i: (bb, i, 0)),
            pl.BlockSpec((D, D), lambda bb, i: (0, 0)),
            row,
        ] + r_in,
        out_specs=r_out,
        out_shape=_router_out_shape(b, n),
        compiler_params=_params(("parallel", "arbitrary"), 48),
        name="merge0",
    )(att, four, w_bf, x, gm, nf, shf, scf, wr_t)


def _proj1_kernel(x_ref, nw_ref, sh_ref, sc_ref, w_ref, bg_ref, z_ref):
    x = _load_token_tiles(x_ref, bg_ref.shape[1])
    h = _modulate(x, nw_ref[...], sh_ref[0], sc_ref[0]).astype(BF)
    bg_ref[0] = jnp.dot(h, w_ref[:, :D], preferred_element_type=F32)
    cg = jnp.dot(h, w_ref[:, D:2 * D], preferred_element_type=F32)
    u = jnp.dot(h, w_ref[:, 2 * D:], preferred_element_type=F32)
    z_ref[0] = cg * u


def _proj1(x_tiles, nw, sh, sc, w_bf):
    b = x_tiles.shape[0]
    n = x_tiles.shape[1] // SUB
    tm = 512
    vec = pl.BlockSpec((1, 1, D), lambda bb, i: (bb, 0, 0))
    row = pl.BlockSpec((1, tm, D), lambda bb, i: (bb, i, 0))
    return pl.pallas_call(
        _proj1_kernel,
        grid=(b, n // tm),
        in_specs=[_tile_spec(tm), pl.BlockSpec((1, D), lambda bb, i: (0, 0)), vec, vec,
                  pl.BlockSpec((D, 3 * D), lambda bb, i: (0, 0))],
        out_specs=[row, row],
        out_shape=[jax.ShapeDtypeStruct((b, n, D), F32), jax.ShapeDtypeStruct((b, n, D), F32)],
        compiler_params=_params(("parallel", "arbitrary"), 48),
        name="proj1",
    )(x_tiles, nw, sh, sc, w_bf)


def _merge1_kernel(bg_ref, z_ref, zp_ref, zn_ref, cw_ref, w_ref, x_ref, gm_ref, nf_ref, shf_ref,
                   scf_ref, wr_ref, x1_ref, hf_ref, aff_ref):
    i = pl.program_id(1)
    last = pl.num_programs(1) - 1
    z = z_ref[0]
    tm = z.shape[0]
    rowid = lax.broadcasted_iota(I32, z.shape, 0)
    prev_row = jnp.where(i > 0, zp_ref[0, 7:8, :], 0.0)
    next_row = jnp.where(i < last, zn_ref[0, 0:1, :], 0.0)
    z_up = jnp.where(rowid == 0, prev_row, pltpu.roll(z, 1, axis=0))
    z_dn = jnp.where(rowid == tm - 1, next_row, pltpu.roll(z, tm - 1, axis=0))
    cw = cw_ref[...]
    conv = cw[0:1] * z_up + cw[1:2] * z + cw[2:3] * z_dn
    y = jnp.dot((bg_ref[0] * conv).astype(BF), w_ref[...], preferred_element_type=F32)
    _residual_router(_load_token_tiles(x_ref, tm), y, gm_ref[0], nf_ref[...], shf_ref[0], scf_ref[0],
                     wr_ref[...], x1_ref, hf_ref, aff_ref)


def _merge1(bg, z, conv_w8, w_bf, x_tiles, gm, nf, shf, scf, wr_t):
    b, n, _ = bg.shape
    tm = 512
    halo = 8
    per = tm // halo
    n_halo = n // halo
    row, r_in, r_out = _router_specs(tm)
    return pl.pallas_call(
        _merge1_kernel,
        grid=(b, n // tm),
        in_specs=[
            row, row,
            pl.BlockSpec((1, halo, D), lambda bb, i: (bb, jnp.maximum(i * per - 1, 0), 0)),
            pl.BlockSpec((1, halo, D), lambda bb, i: (bb, jnp.minimum((i + 1) * per, n_halo - 1), 0)),
            pl.BlockSpec((8, D), lambda bb, i: (0, 0)),
            pl.BlockSpec((D, D), lambda bb, i: (0, 0)),
            _tile_spec(tm),
        ] + r_in,
        out_specs=r_out,
        out_shape=_router_out_shape(b, n),
        compiler_params=_params(("parallel", "arbitrary"), 48),
        name="merge1",
    )(bg, z, z, z, conv_w8, w_bf, x_tiles, gm, nf, shf, scf, wr_t)


def _select_kernel(a_ref, idx_ref, gate_ref, *, cap):
    a = a_ref[0]
    rows = a.shape[0]
    chunks = rows // N_EXP
    assert chunks & (chunks - 1) == 0
    shift = chunks.bit_length() - 1
    ri = lax.broadcasted_iota(I32, (rows, rows), 0)
    ci = lax.broadcasted_iota(I32, (rows, rows), 1)
    same_e = (ri >> shift) == (ci >> shift)
    same_f = jnp.where(same_e, 1.0, 0.0)
    same = same_f.astype(BF)
    lower = (same_f * jnp.where(ci < ri, 1.0, 0.0)).astype(BF)
    li = lax.broadcasted_iota(I32, (LANES, LANES), 0)
    lj = lax.broadcasted_iota(I32, (LANES, LANES), 1)
    incl = jnp.where(li <= lj, 1.0, 0.0).astype(BF)

    def bcast(col):
        return jnp.broadcast_to(col, (rows, LANES))

    def expert_count(maskf):
        rc = bcast(jnp.sum(maskf, axis=-1, keepdims=True)).astype(BF)
        return jnp.dot(same, rc, preferred_element_type=F32)

    def prefix(maskf):
        cl = jnp.dot(maskf.astype(BF), incl, preferred_element_type=F32)
        tot = bcast(cl[:, LANES - 1:LANES])
        off = jnp.dot(lower, tot.astype(BF), preferred_element_type=F32)
        return cl, off, tot

    min_normal = 0x00800000

    def search(step, t):
        cand = t | jnp.left_shift(jnp.int32(1), 30 - step)
        cnt = expert_count(jnp.where(a >= lax.bitcast_convert_type(cand, F32), 1.0, 0.0))
        return jnp.where(cnt >= cap, jnp.where(cand >= min_normal, cand, t), t)

    thr = lax.fori_loop(0, 31, search, jnp.zeros((rows, LANES), I32))
    thr_f = lax.bitcast_convert_type(thr, F32)
    nxt_f = lax.bitcast_convert_type(jnp.maximum(thr + 1, min_normal), F32)
    gtf = jnp.where(a >= nxt_f, 1.0, 0.0)
    eqf = jnp.where(a >= thr_f, 1.0, 0.0) - gtf
    need = cap - expert_count(gtf)
    cl_eq, off_eq, _ = prefix(eqf)
    self_f = gtf + eqf * jnp.where(cl_eq + off_eq <= need, 1.0, 0.0)
    cl, off, tot = prefix(self_f)
    inc = off + tot
    cl_bf = cl.astype(BF)
    a_hi = a.astype(BF)
    r1 = a - a_hi.astype(F32)
    a_mid = r1.astype(BF)
    a_lo = (r1 - a_mid.astype(F32)).astype(BF)

    slot = lax.broadcasted_iota(I32, (cap, LANES), 0).astype(F32)
    lane_f = lax.broadcasted_iota(I32, (cap, LANES), 1).astype(F32)
    dsub = lax.broadcasted_iota(I32, (chunks, LANES), 0)
    dlane = lax.broadcasted_iota(I32, (chunks, LANES), 1)
    rsel = lax.broadcasted_iota(I32, (cap, rows), 1).astype(F32)
    big = float(2 * cap * N_EXP)
    for e in range(N_EXP):
        r0 = e * chunks
        diag = dsub == dlane
        inc_row = jnp.sum(jnp.where(diag, inc[r0:r0 + chunks], 0.0), axis=0, keepdims=True)
        off_row = jnp.sum(jnp.where(diag, off[r0:r0 + chunks], 0.0), axis=0, keepdims=True)
        inc_row = jnp.where(dlane[0:1] < chunks, inc_row, big)
        jc = jnp.sum(jnp.where(inc_row <= slot, 1.0, 0.0), axis=-1, keepdims=True)
        offsel = jnp.sum(jnp.where(lane_f == jc, off_row, 0.0), axis=-1, keepdims=True)
        pick = jnp.where(rsel == jc + float(r0), 1.0, 0.0).astype(BF)
        clsel = jnp.dot(pick, cl_bf, preferred_element_type=F32)
        lc = jnp.sum(jnp.where(clsel <= slot - offsel, 1.0, 0.0), axis=-1, keepdims=True)
        asel = (jnp.dot(pick, a_hi, preferred_element_type=F32)
                + jnp.dot(pick, a_mid, preferred_element_type=F32)
                + jnp.dot(pick, a_lo, preferred_element_type=F32))
        gate = jnp.sum(jnp.where(lane_f == lc, asel, 0.0), axis=-1, keepdims=True)
        idx_ref[0, e] = jnp.broadcast_to(jc * float(LANES) + lc, (cap, LANES)).astype(I32)
        gate_ref[0, e] = jnp.broadcast_to(gate, (cap, LANES))


def _select(aff_rows, cap):
    b, rows, _ = aff_rows.shape
    out = jax.ShapeDtypeStruct((b, N_EXP, cap, LANES), I32)
    outg = jax.ShapeDtypeStruct((b, N_EXP, cap, LANES), F32)
    spec = pl.BlockSpec((1, N_EXP, cap, LANES), lambda bb: (bb, 0, 0, 0))
    return pl.pallas_call(
        functools.partial(_select_kernel, cap=cap),
        grid=(b,),
        in_specs=[pl.BlockSpec((1, rows, LANES), lambda bb: (bb, 0, 0))],
        out_specs=[spec, spec],
        out_shape=[out, outg],
        compiler_params=_params(("parallel",), 48),
        name="select",
    )(aff_rows)


def _gather_kernel(idx_ref, h_ref, o_ref, rows_ref):
    cap = o_ref.shape[2]

    def body(c, carry):
        t = idx_ref[0, 0, c]
        rows_ref[0, pl.ds(pl.multiple_of(c * SUB, SUB), SUB), :] = (
            h_ref[0, pl.ds(pl.multiple_of(t * SUB, SUB), SUB), :])
        return carry

    lax.fori_loop(0, cap, body, 0, unroll=8)
    o_ref[0, 0] = _load_token_tiles(rows_ref, cap).astype(BF)


def _gather(idx, hf_tiles, cap):
    b, rows, _ = hf_tiles.shape
    return pl.pallas_call(
        _gather_kernel,
        grid=(b, N_EXP),
        in_specs=[
            pl.BlockSpec((1, 1, cap), lambda bb, e: (bb * N_EXP + e, 0, 0), memory_space=pltpu.SMEM),
            pl.BlockSpec((1, rows, LANES), lambda bb, e: (bb, 0, 0)),
        ],
        out_specs=pl.BlockSpec((1, 1, cap, D), lambda bb, e: (bb, e, 0, 0)),
        out_shape=jax.ShapeDtypeStruct((b, N_EXP, cap, D), BF),
        scratch_shapes=[pltpu.VMEM((1, cap * SUB, LANES), F32)],
        compiler_params=_params(("parallel", "arbitrary"), 48),
        name="gather",
    )(idx, hf_tiles)


def _ffn_kernel(x_ref, wg_ref, wu_ref, wd_ref, o_ref, wg_s, wu_s, wd_s):
    @pl.when(pl.program_id(1) == 0)
    def _():
        wg_s[...] = wg_ref[...].astype(BF)
        wu_s[...] = wu_ref[...].astype(BF)
        wd_s[...] = wd_ref[...].astype(BF)

    x = x_ref[0, 0]
    g = jnp.dot(x, wg_s[...], preferred_element_type=F32)
    u = jnp.dot(x, wu_s[...], preferred_element_type=F32)
    act = (g * (1.0 / (1.0 + jnp.exp(-g))) * u).astype(BF)
    _store_token_tiles(o_ref.at[0], jnp.dot(act, wd_s[...], preferred_element_type=F32))


def _ffn(xs, w_gate, w_up, w_down, layer):
    b, _, cap, _ = xs.shape
    dexp = w_gate.shape[-1]
    xspec = pl.BlockSpec((1, 1, cap, D), lambda e, bb: (bb, e, 0, 0))
    yspec = pl.BlockSpec((1, 1, cap * SUB, LANES), lambda e, bb: (bb, e, 0, 0))
    wspec_in = pl.BlockSpec((None, None, D, dexp), lambda e, bb: (layer, e, 0, 0))
    wspec_out = pl.BlockSpec((None, None, dexp, D), lambda e, bb: (layer, e, 0, 0))
    return pl.pallas_call(
        _ffn_kernel,
        grid=(N_EXP, b),
        in_specs=[xspec, wspec_in, wspec_in, wspec_out],
        out_specs=yspec,
        out_shape=jax.ShapeDtypeStruct((b, N_EXP, cap * SUB, LANES), F32),
        scratch_shapes=[pltpu.VMEM((D, dexp), BF), pltpu.VMEM((D, dexp), BF), pltpu.VMEM((dexp, D), BF)],
        compiler_params=_params(("arbitrary", "arbitrary"), 56),
        name="expert_ffn",
    )(xs, w_gate, w_up, w_down)


def _scatter_kernel(idx_ref, gate_ref, y_ref, x_ref, gf_ref, o_ref):
    cap = idx_ref.shape[2]

    @pl.when(pl.program_id(1) == 0)
    def _():
        o_ref[...] = x_ref[...]

    gf = gf_ref[0]

    group = 16

    def body(g, carry):
        c0 = g * group
        ts = [pl.multiple_of(idx_ref[0, 0, c0 + j] * SUB, SUB) for j in range(group)]
        new = []
        for j in range(group):
            y = y_ref[0, 0, pl.ds(pl.multiple_of((c0 + j) * SUB, SUB), SUB), :]
            new.append(o_ref[0, pl.ds(ts[j], SUB), :] + (gf * gate_ref[0, 0, c0 + j]) * y)
        for j in range(group):
            o_ref[0, pl.ds(ts[j], SUB), :] = new[j]
        return carry

    lax.fori_loop(0, cap // group, body, 0)


def _scatter(idx, gate, y_tiles, x_tiles, gf_tile):
    b, rows, _ = x_tiles.shape
    cap = idx.shape[2]
    sm = lambda: pl.BlockSpec((1, 1, cap), lambda bb, e: (bb * N_EXP + e, 0, 0), memory_space=pltpu.SMEM)
    resident = lambda **kw: pl.BlockSpec((1, rows, LANES), lambda bb, e: (bb, 0, 0), **kw)
    return pl.pallas_call(
        _scatter_kernel,
        grid=(b, N_EXP),
        in_specs=[
            sm(), sm(),
            pl.BlockSpec((1, 1, cap * SUB, LANES), lambda bb, e: (bb, e, 0, 0)),
            resident(pipeline_mode=pl.Buffered(1)),
            pl.BlockSpec((1, SUB, LANES), lambda bb, e: (bb, 0, 0)),
        ],
        out_specs=resident(),
        out_shape=jax.ShapeDtypeStruct((b, rows, LANES), F32),
        compiler_params=_params(("parallel", "arbitrary"), 56),
        name="scatter_add",
    )(idx, gate, y_tiles, x_tiles, gf_tile)


def _moe(x1_tiles, hf_tiles, aff_t, gf, w_gate, w_up, w_down, layer):
    b, _, n = aff_t.shape
    cap = CAP_FACTOR * n // N_EXP
    aff_rows = aff_t.reshape(b, N_EXP * (n // LANES), LANES)
    idx_b, gate_b = _select(aff_rows, cap)
    idx = idx_b[..., 0].reshape(b * N_EXP, 1, cap)
    gate = gate_b[..., 0].reshape(b * N_EXP, 1, cap)
    xs = _gather(idx, hf_tiles, cap)
    y_tiles = _ffn(xs, w_gate, w_up, w_down, layer)
    return _scatter(idx, gate, y_tiles, x1_tiles, gf.reshape(b, SUB, LANES))


def _to_rows_kernel(x_ref, o_ref):
    o_ref[0] = _load_token_tiles(x_ref, o_ref.shape[1])


def _to_rows(x_tiles):
    b = x_tiles.shape[0]
    n = x_tiles.shape[1] // SUB
    tm = 512
    return pl.pallas_call(
        _to_rows_kernel,
        grid=(b, n // tm),
        in_specs=[_tile_spec(tm)],
        out_specs=pl.BlockSpec((1, tm, D), lambda bb, i: (bb, i, 0)),
        out_shape=jax.ShapeDtypeStruct((b, n, D), F32),
        compiler_params=_params(("parallel", "arbitrary"), 32),
        name="to_rows",
    )(x_tiles)


def _rope_tables(n_lat, n_ctx):
    rows = n_lat // GRID_W
    r = jnp.repeat(jnp.arange(rows, dtype=F32), GRID_W)
    col = jnp.tile(jnp.arange(GRID_W, dtype=F32), rows)
    n_freq = HEAD_DIM // 4
    inv = ROPE_BASE ** (-jnp.arange(n_freq, dtype=F32) / n_freq)
    ar = r[:, None] * inv
    ac = col[:, None] * inv
    ang = jnp.concatenate([ar, ar, ac, ac], axis=-1)
    sign = jnp.where((jnp.arange(HEAD_DIM) % 32) < 16, -1.0, 1.0).astype(F32)
    cos = jnp.concatenate([jnp.cos(ang), jnp.ones((n_ctx, HEAD_DIM), F32)], axis=0)
    sin_s = jnp.concatenate([jnp.sin(ang) * sign, jnp.zeros((n_ctx, HEAD_DIM), F32)], axis=0)
    return jnp.tile(cos, (1, 2)), jnp.tile(sin_s, (1, 2))


def _dft_tables(n_lat):
    c = jnp.arange(FOUR_G, dtype=I32)
    ang_c = (2.0 * math.pi / FOUR_G) * ((c[:, None] * c[None, :]) % FOUR_G).astype(F32)
    eye = jnp.eye(FOUR_W // FOUR_G, dtype=F32)
    bd = jnp.concatenate([jnp.kron(eye, jnp.cos(ang_c)), jnp.kron(eye, jnp.sin(ang_c))], axis=1)
    bd = (bd * FOUR_G ** -0.5).astype(BF)
    side = int(round(math.sqrt(n_lat)))
    assert side * side == n_lat
    k = jnp.arange(n_lat, dtype=I32)[:, None]
    s = jnp.arange(side, dtype=I32)[None, :]
    alpha = (2.0 * math.pi / side) * ((k * s) % side).astype(F32)
    beta = (2.0 * math.pi / n_lat) * ((k * s) % n_lat).astype(F32)
    ca, sa, cb, sb = jnp.cos(alpha), jnp.sin(alpha), jnp.cos(beta), jnp.sin(beta)
    cosw = (ca[:, :, None] * cb[:, None, :] - sa[:, :, None] * sb[:, None, :]).reshape(n_lat, n_lat)
    sinw = (sa[:, :, None] * cb[:, None, :] + ca[:, :, None] * sb[:, None, :]).reshape(n_lat, n_lat)
    wbig = (jnp.concatenate([cosw, -sinw], axis=1) * n_lat ** -0.5).astype(BF)
    return bd, wbig


def kernel(x, c, ctx, c_ctx, ada_w, ada_b, norm_mix, norm_ffn, attn_w_in, attn_q_norm, attn_k_norm,
           lam_q1, lam_k1, lam_q2, lam_k2, attn_subln, attn_w_out, conv_w_in, conv_w, conv_w_out,
           router_w, moe_w_gate, moe_w_up, moe_w_down):
    b, n, _ = x.shape
    n_ctx = ctx.shape[1]
    assert x.shape[2] == D and n % 512 == 0 and n_ctx % 256 == 0

    cond8 = jnp.concatenate([c, c_ctx[None, :], jnp.zeros((8 - b - 1, D), F32)], axis=0)
    ada = _ada(cond8, ada_w, ada_b)

    def mods(layer):
        m = ada[layer].reshape(8, 6, D)
        return [m[:, j] for j in range(6)]

    vec = lambda t: t[:b].reshape(b, 1, D)

    sh_m, sc_m, g_m, sh_f, sc_f, g_f = mods(0)
    both = lambda t: jnp.stack([t[:b], jnp.broadcast_to(t[b], (b, D))], axis=1).reshape(b, 2, 1, D)
    cos, sin_s = _rope_tables(n, n_ctx)
    bd, wbig = _dft_tables(n)
    tile2 = lambda t: jnp.tile(t.reshape(1, HEAD_DIM), (1, 2))
    xcat = jnp.concatenate([x, ctx], axis=1)
    q, k, v, f = _proj0(xcat, norm_mix[0].reshape(1, D), both(sh_m), both(sc_m),
                        attn_w_in[0].astype(BF), cos, sin_s,
                        tile2(attn_q_norm[0]), tile2(attn_k_norm[0]), n)
    lam_init = 0.8 - 0.6 * math.exp(-0.3 * 0)
    lamv = jnp.zeros((8, LANES), F32).at[:4, :HEAD_DIM].set(
        jnp.stack([lam_q1[0], lam_k1[0], lam_q2[0], lam_k2[0]]))
    att = _attention(q, k, v, lamv, attn_subln[0].reshape(1, V_DIM), n, lam_init)
    four = _fourier(f, bd, wbig, n)
    x1, hf, aff_t = _merge0(att, four, attn_w_out[0].astype(BF), x, vec(g_m),
                            norm_ffn[0].reshape(1, D), vec(sh_f), vec(sc_f), router_w[0].T)
    x2 = _moe(x1, hf, aff_t, vec(g_f), moe_w_gate, moe_w_up, moe_w_down, 0)

    sh_m, sc_m, g_m, sh_f, sc_f, g_f = mods(1)
    bg, z = _proj1(x2, norm_mix[1].reshape(1, D), vec(sh_m), vec(sc_m), conv_w_in[0].astype(BF))
    conv_w8 = jnp.zeros((8, D), F32).at[:3].set(conv_w[0])
    x3, hf, aff_t = _merge1(bg, z, conv_w8, conv_w_out[0].astype(BF), x2, vec(g_m),
                            norm_ffn[1].reshape(1, D), vec(sh_f), vec(sc_f), router_w[1].T)
    return _to_rows(_moe(x3, hf, aff_t, vec(g_f), moe_w_gate, moe_w_up, moe_w_down, 1))
```

```python
import functools
import math

import jax
import jax.numpy as jnp
from jax import lax
from jax.experimental import pallas as pl
from jax.experimental.pallas import tpu as pltpu

BF = jnp.bfloat16
F32 = jnp.float32
I32 = jnp.int32

D = 1024
GRID_W = 64
N_HEADS = 6
HEAD_DIM = 64
V_DIM = 2 * HEAD_DIM
QK_W = N_HEADS * 2 * HEAD_DIM
V_W = N_HEADS * V_DIM
FOUR_W = 256
FOUR_G = 64
IN_W = 2 * QK_W + V_W + FOUR_W
N_EXP = 16
CAP_FACTOR = 2
ROPE_BASE = 10000.0
EPS = 1e-6
LANES = 128
MIB = 1024 * 1024

NT = (((1,), (1,)), ((), ()))


def _params(sem, vmem_mib):
    return pltpu.CompilerParams(dimension_semantics=sem, vmem_limit_bytes=vmem_mib * MIB)


def _split2(x):
    hi = x.astype(BF)
    lo = (x - hi.astype(F32)).astype(BF)
    return hi, lo


def _dot3(a, b, dims=(((1,), (0,)), ((), ()))):
    ah, al = _split2(a)
    bh, bl = _split2(b)
    dg = functools.partial(lax.dot_general, dimension_numbers=dims, preferred_element_type=F32)
    return dg(ah, bh) + dg(ah, bl) + dg(al, bh)


def _modulate(x, nw, shift, scale):
    ms = jnp.mean(x * x, axis=-1, keepdims=True)
    return (x * lax.rsqrt(ms + EPS) * nw) * (1.0 + scale) + shift


def _ada_kernel(c_ref, w_ref, b_ref, o_ref):
    cv = c_ref[...]
    s = cv * (1.0 / (1.0 + jnp.exp(-cv)))
    o_ref[0] = _dot3(s, w_ref[0]) + b_ref[0]


def _ada(cond8, ada_w, ada_b):
    depth = ada_w.shape[0]
    tn = 1536
    return pl.pallas_call(
        _ada_kernel,
        grid=(depth, 6 * D // tn),
        in_specs=[
            pl.BlockSpec((8, D), lambda l, j: (0, 0)),
            pl.BlockSpec((1, D, tn), lambda l, j: (l, 0, j)),
            pl.BlockSpec((1, 1, tn), lambda l, j: (l, 0, j)),
        ],
        out_specs=pl.BlockSpec((1, 8, tn), lambda l, j: (l, 0, j)),
        out_shape=jax.ShapeDtypeStruct((depth, 8, 6 * D), F32),
        compiler_params=_params(("arbitrary", "arbitrary"), 40),
        name="ada",
    )(cond8, ada_w, ada_b.reshape(depth, 1, 6 * D))


def _norm_rope(xb, wn, cos, sin_s, out_scale):
    lane = lax.broadcasted_iota(I32, xb.shape, 1)
    lo = lane < HEAD_DIM
    ss = xb * xb
    s_lo = jnp.sum(jnp.where(lo, ss, 0.0), axis=-1, keepdims=True)
    s_hi = jnp.sum(jnp.where(lo, 0.0, ss), axis=-1, keepdims=True)
    inv = jnp.where(lo, lax.rsqrt(s_lo * (1.0 / HEAD_DIM) + EPS),
                    lax.rsqrt(s_hi * (1.0 / HEAD_DIM) + EPS))
    y = xb * inv * wn
    fwd = pltpu.roll(y, LANES - 16, axis=1)
    bwd = pltpu.roll(y, 16, axis=1)
    rot = jnp.where((lane & 31) < 16, fwd, bwd)
    out = y * cos + rot * sin_s
    if out_scale != 1.0:
        out = out * out_scale
    return out


def _proj0_kernel(x_ref, ctx_ref, nw_ref, sh_ref, sc_ref, w_ref, cos_ref, sin_ref, qn_ref, kn_ref,
                  q_ref, k_ref, v_ref, f_ref, *, lat_tiles):
    x = jnp.where(pl.program_id(1) < lat_tiles, x_ref[0], ctx_ref[0])
    h = _modulate(x, nw_ref[...], sh_ref[0, 0], sc_ref[0, 0]).astype(BF)
    cos = cos_ref[...]
    sin_s = sin_ref[...]
    qn = qn_ref[...]
    kn = kn_ref[...]
    q_scale = HEAD_DIM ** -0.5 * math.log2(math.e)
    pq = jnp.dot(h, w_ref[:, :QK_W], preferred_element_type=F32)
    pk = jnp.dot(h, w_ref[:, QK_W:2 * QK_W], preferred_element_type=F32)
    for hh in range(N_HEADS):
        c0 = hh * V_DIM
        q_ref[0, :, c0:c0 + V_DIM] = _norm_rope(pq[:, c0:c0 + V_DIM], qn, cos, sin_s, q_scale).astype(BF)
        k_ref[0, :, c0:c0 + V_DIM] = _norm_rope(pk[:, c0:c0 + V_DIM], kn, cos, sin_s, 1.0).astype(BF)
    v_ref[0] = jnp.dot(h, w_ref[:, 2 * QK_W:2 * QK_W + V_W], preferred_element_type=F32).astype(BF)
    f_ref[0] = jnp.dot(h, w_ref[:, 2 * QK_W + V_W:], preferred_element_type=F32).astype(BF)


def _proj0(x, ctx, nw, sh2, sc2, w_bf, cos, sin_s, qn, kn):
    b, n_lat, _ = x.shape
    n_ctx = ctx.shape[1]
    s_tot = n_lat + n_ctx
    tm = 256
    lat_tiles = n_lat // tm
    mod_spec = pl.BlockSpec((1, 1, 1, D), lambda bb, i: (bb, i // lat_tiles, 0, 0))
    full = lambda shape: pl.BlockSpec(shape, lambda bb, i: tuple(0 for _ in shape))
    row = lambda w: pl.BlockSpec((1, tm, w), lambda bb, i: (bb, i, 0))
    return pl.pallas_call(
        functools.partial(_proj0_kernel, lat_tiles=lat_tiles),
        grid=(b, s_tot // tm),
        in_specs=[
            pl.BlockSpec((1, tm, D), lambda bb, i: (bb, jnp.minimum(i, lat_tiles - 1), 0)),
            pl.BlockSpec((1, tm, D), lambda bb, i: (bb, jnp.maximum(i - lat_tiles, 0), 0)),
            full((1, D)), mod_spec, mod_spec, full((D, IN_W)),
            pl.BlockSpec((tm, LANES), lambda bb, i: (i, 0)),
            pl.BlockSpec((tm, LANES), lambda bb, i: (i, 0)),
            full((1, LANES)), full((1, LANES)),
        ],
        out_specs=[row(QK_W), row(QK_W), row(V_W), row(FOUR_W)],
        out_shape=[
            jax.ShapeDtypeStruct((b, s_tot, QK_W), BF),
            jax.ShapeDtypeStruct((b, s_tot, QK_W), BF),
            jax.ShapeDtypeStruct((b, s_tot, V_W), BF),
            jax.ShapeDtypeStruct((b, s_tot, FOUR_W), BF),
        ],
        compiler_params=_params(("parallel", "arbitrary"), 48),
        name="proj0",
    )(x, ctx, nw, sh2, sc2, w_bf, cos, sin_s, qn, kn)


def _attn_kernel(q_ref, k_ref, v_ref, lam_ref, sub_ref, o_ref, vaug_ref, s_ref, *, lam_init, kc):
    @pl.when(pl.program_id(2) == 0)
    def _():
        vaug_ref[:, :V_DIM] = v_ref[0]
        lane_v = lax.broadcasted_iota(I32, (vaug_ref.shape[0], V_DIM), 1)
        vaug_ref[:, V_DIM:] = jnp.where(lane_v == 0, 1.0, 0.0).astype(BF)

    q = q_ref[0]
    lane = lax.broadcasted_iota(I32, q.shape, 1)
    zero = jnp.zeros_like(q)
    lv = lam_ref[...]
    t1 = jnp.sum(lv[0:1] * lv[1:2], axis=-1, keepdims=True)
    t2 = jnp.sum(lv[2:3] * lv[3:4], axis=-1, keepdims=True)
    lam = jnp.exp(t1) - jnp.exp(t2) + lam_init

    qs = (jnp.where(lane < HEAD_DIM, q, zero), jnp.where(lane < HEAD_DIM, zero, q))
    tq = q.shape[0]
    n_chunks = k_ref.shape[1] // kc

    ms = [jnp.full((tq, LANES), -jnp.inf, F32) for _ in range(2)]
    for c in range(n_chunks):
        k_c = k_ref[0, c * kc:(c + 1) * kc, :]
        for h in range(2):
            s = lax.dot_general(qs[h], k_c, NT, preferred_element_type=F32)
            s_ref[h, c] = s
            for j in range(kc // LANES):
                ms[h] = jnp.maximum(ms[h], s[:, j * LANES:(j + 1) * LANES])
    ms = [jnp.max(m, axis=-1, keepdims=True) for m in ms]

    acc = [jnp.zeros((tq, 2 * V_DIM), F32) for _ in range(2)]
    for c in range(n_chunks):
        v_c = vaug_ref[c * kc:(c + 1) * kc, :]
        for h in range(2):
            e = jnp.exp2((s_ref[h, c] - ms[h]).astype(BF))
            acc[h] = acc[h] + jnp.dot(e, v_c, preferred_element_type=F32)
    o0, o1 = acc
    o = (o0[:, :V_DIM] * (1.0 / o0[:, V_DIM:V_DIM + 1])
         - o1[:, :V_DIM] * (lam / o1[:, V_DIM:V_DIM + 1]))
    ms = jnp.mean(o * o, axis=-1, keepdims=True)
    o_ref[0] = (o * lax.rsqrt(ms + EPS) * sub_ref[...] * (1.0 - lam_init)).astype(BF)


def _attention(q, k, v, lamv, subln, n_lat, lam_init):
    b, s_tot, _ = k.shape
    tq = 512
    kc = 256
    assert s_tot % kc == 0
    kv_spec = pl.BlockSpec((1, s_tot, V_DIM), lambda bb, hh, i: (bb, 0, hh))
    q_spec = pl.BlockSpec((1, tq, V_DIM), lambda bb, hh, i: (bb, i, hh))
    return pl.pallas_call(
        functools.partial(_attn_kernel, lam_init=lam_init, kc=kc),
        grid=(b, N_HEADS, n_lat // tq),
        in_specs=[
            q_spec, kv_spec, kv_spec,
            pl.BlockSpec((8, LANES), lambda bb, hh, i: (0, 0)),
            pl.BlockSpec((1, LANES), lambda bb, hh, i: (0, 0)),
        ],
        out_specs=q_spec,
        out_shape=jax.ShapeDtypeStruct((b, n_lat, V_W), BF),
        scratch_shapes=[pltpu.VMEM((s_tot, 2 * V_DIM), BF),
                        pltpu.VMEM((2, s_tot // kc, tq, kc), F32)],
        compiler_params=_params(("parallel", "parallel", "arbitrary"), 48),
        name="diff_attn",
    )(q, k, v, lamv, subln)


def _fourier_kernel(f_ref, bd_ref, cb_ref, sb_ref, ca_ref, sa_ref, o_ref, g_ref):
    nb, n_lat, _ = f_ref.shape

    @pl.when(pl.program_id(0) == 0)
    def _():
        rows = 1024
        for bb in range(nb):
            for r in range(n_lat // rows):
                fc = jnp.dot(f_ref[bb, r * rows:(r + 1) * rows, :], bd_ref[...],
                             preferred_element_type=F32)
                g_ref[r * rows:(r + 1) * rows, bb * FOUR_W:(bb + 1) * FOUR_W] = fc[:, :FOUR_W].astype(BF)
                g_ref[n_lat + r * rows:n_lat + (r + 1) * rows,
                      bb * FOUR_W:(bb + 1) * FOUR_W] = fc[:, FOUR_W:].astype(BF)

    ca = ca_ref[0]
    sa = sa_ref[0]
    cb = cb_ref[...]
    sb = sb_ref[...]
    w_cos = (cb * ca - sb * sa).astype(BF)
    w_sin = (sb * ca + cb * sa).astype(BF)
    y = (jnp.dot(w_cos, g_ref[:n_lat, :], preferred_element_type=F32)
         - jnp.dot(w_sin, g_ref[n_lat:, :], preferred_element_type=F32))
    for bb in range(nb):
        o_ref[bb] = y[:, bb * FOUR_W:(bb + 1) * FOUR_W].astype(BF)


def _fourier(f, bd, tables, n_lat):
    b = f.shape[0]
    cos_row, sin_row, cos_tile, sin_tile = tables
    tr = cos_row.shape[0]
    row_tab = pl.BlockSpec((tr, n_lat), lambda i: (0, 0))
    tile_tab = pl.BlockSpec((1, 1, n_lat), lambda i: (i, 0, 0))
    return pl.pallas_call(
        _fourier_kernel,
        grid=(n_lat // tr,),
        in_specs=[
            pl.BlockSpec((b, n_lat, FOUR_W), lambda i: (0, 0, 0)),
            pl.BlockSpec((FOUR_W, 2 * FOUR_W), lambda i: (0, 0)),
            row_tab, row_tab, tile_tab, tile_tab,
        ],
        out_specs=pl.BlockSpec((b, tr, FOUR_W), lambda i: (0, i, 0)),
        out_shape=jax.ShapeDtypeStruct((b, n_lat, FOUR_W), BF),
        scratch_shapes=[pltpu.VMEM((2 * n_lat, b * FOUR_W), BF)],
        compiler_params=_params(("arbitrary",), 56),
        name="fourier",
    )(f, bd, cos_row, sin_row, cos_tile, sin_tile)


SUB = D // LANES


def _store_token_tiles(ref, val):
    rows = val.shape[0]
    for j in range(SUB):
        ref[0, pl.ds(j, rows, stride=SUB), :] = val[:, j * LANES:(j + 1) * LANES]


def _load_token_tiles(ref, rows):
    return jnp.concatenate([ref[0, pl.ds(j, rows, stride=SUB), :] for j in range(SUB)], axis=1)


def _residual_router(x, y, gm, nf, shf, scf, wr_t, x1_ref, hf_ref, aff_ref):
    x1 = x + gm * y
    _store_token_tiles(x1_ref, x1)
    hf = _modulate(x1, nf, shf, scf)
    _store_token_tiles(hf_ref, hf)
    logits = _dot3(wr_t, hf, NT)
    m = jnp.max(logits, axis=0, keepdims=True)
    e = jnp.exp(logits - m)
    aff_ref[0] = e / jnp.sum(e, axis=0, keepdims=True)


def _merge0_kernel(att_ref, four_ref, w_ref, x_ref, gm_ref, nf_ref, shf_ref, scf_ref, wr_ref,
                   x1_ref, hf_ref, aff_ref):
    y = jnp.dot(att_ref[0], w_ref[:V_W, :], preferred_element_type=F32)
    y = y + jnp.dot(four_ref[0], w_ref[V_W:, :], preferred_element_type=F32)
    _residual_router(x_ref[0], y, gm_ref[0], nf_ref[...], shf_ref[0], scf_ref[0], wr_ref[...],
                     x1_ref, hf_ref, aff_ref)


def _router_specs(tm):
    vec = pl.BlockSpec((1, 1, D), lambda bb, i: (bb, 0, 0))
    row = pl.BlockSpec((1, tm, D), lambda bb, i: (bb, i, 0))
    in_specs = [vec, pl.BlockSpec((1, D), lambda bb, i: (0, 0)), vec, vec,
                pl.BlockSpec((N_EXP, D), lambda bb, i: (0, 0))]
    tiles = _tile_spec(tm)
    out_specs = [tiles, tiles, pl.BlockSpec((1, N_EXP, tm), lambda bb, i: (bb, 0, i))]
    return row, in_specs, out_specs


def _tile_spec(tm):
    return pl.BlockSpec((1, tm * SUB, LANES), lambda bb, i: (bb, i, 0))


def _router_out_shape(b, n):
    return [jax.ShapeDtypeStruct((b, n * SUB, LANES), F32), jax.ShapeDtypeStruct((b, n * SUB, LANES), F32),
            jax.ShapeDtypeStruct((b, N_EXP, n), F32)]


def _merge0(att, four, w_bf, x, gm, nf, shf, scf, wr_t):
    b, n, _ = x.shape
    tm = 512
    row, r_in, r_out = _router_specs(tm)
    return pl.pallas_call(
        _merge0_kernel,
        grid=(b, n // tm),
        in_specs=[
            pl.BlockSpec((1, tm, V_W), lambda bb, i: (bb, i, 0)),
            pl.BlockSpec((1, tm, FOUR_W), lambda bb, i: (bb, i, 0)),
            pl.BlockSpec((D, D), lambda bb, i: (0, 0)),
            row,
        ] + r_in,
        out_specs=r_out,
        out_shape=_router_out_shape(b, n),
        compiler_params=_params(("parallel", "arbitrary"), 48),
        name="merge0",
    )(att, four, w_bf, x, gm, nf, shf, scf, wr_t)


def _proj1_kernel(x_ref, nw_ref, sh_ref, sc_ref, w_ref, bg_ref, z_ref):
    x = _load_token_tiles(x_ref, bg_ref.shape[1])
    h = _modulate(x, nw_ref[...], sh_ref[0], sc_ref[0]).astype(BF)
    bg_ref[0] = jnp.dot(h, w_ref[:, :D], preferred_element_type=F32)
    cg = jnp.dot(h, w_ref[:, D:2 * D], preferred_element_type=F32)
    u = jnp.dot(h, w_ref[:, 2 * D:], preferred_element_type=F32)
    z_ref[0] = cg * u


def _proj1(x_tiles, nw, sh, sc, w_bf):
    b = x_tiles.shape[0]
    n = x_tiles.shape[1] // SUB
    tm = 512
    vec = pl.BlockSpec((1, 1, D), lambda bb, i: (bb, 0, 0))
    row = pl.BlockSpec((1, tm, D), lambda bb, i: (bb, i, 0))
    return pl.pallas_call(
        _proj1_kernel,
        grid=(b, n // tm),
        in_specs=[_tile_spec(tm), pl.BlockSpec((1, D), lambda bb, i: (0, 0)), vec, vec,
                  pl.BlockSpec((D, 3 * D), lambda bb, i: (0, 0))],
        out_specs=[row, row],
        out_shape=[jax.ShapeDtypeStruct((b, n, D), F32), jax.ShapeDtypeStruct((b, n, D), F32)],
        compiler_params=_params(("parallel", "arbitrary"), 48),
        name="proj1",
    )(x_tiles, nw, sh, sc, w_bf)


def _merge1_kernel(bg_ref, z_ref, zp_ref, zn_ref, cw_ref, w_ref, x_ref, gm_ref, nf_ref, shf_ref,
                   scf_ref, wr_ref, x1_ref, hf_ref, aff_ref):
    i = pl.program_id(1)
    last = pl.num_programs(1) - 1
    z = z_ref[0]
    tm = z.shape[0]
    rowid = lax.broadcasted_iota(I32, z.shape, 0)
    prev_row = jnp.where(i > 0, zp_ref[0, 7:8, :], 0.0)
    next_row = jnp.where(i < last, zn_ref[0, 0:1, :], 0.0)
    z_up = jnp.where(rowid == 0, prev_row, pltpu.roll(z, 1, axis=0))
    z_dn = jnp.where(rowid == tm - 1, next_row, pltpu.roll(z, tm - 1, axis=0))
    cw = cw_ref[...]
    conv = cw[0:1] * z_up + cw[1:2] * z + cw[2:3] * z_dn
    y = jnp.dot((bg_ref[0] * conv).astype(BF), w_ref[...], preferred_element_type=F32)
    _residual_router(_load_token_tiles(x_ref, tm), y, gm_ref[0], nf_ref[...], shf_ref[0], scf_ref[0],
                     wr_ref[...], x1_ref, hf_ref, aff_ref)


def _merge1(bg, z, conv_w8, w_bf, x_tiles, gm, nf, shf, scf, wr_t):
    b, n, _ = bg.shape
    tm = 512
    halo = 8
    per = tm // halo
    n_halo = n // halo
    row, r_in, r_out = _router_specs(tm)
    return pl.pallas_call(
        _merge1_kernel,
        grid=(b, n // tm),
        in_specs=[
            row, row,
            pl.BlockSpec((1, halo, D), lambda bb, i: (bb, jnp.maximum(i * per - 1, 0), 0)),
            pl.BlockSpec((1, halo, D), lambda bb, i: (bb, jnp.minimum((i + 1) * per, n_halo - 1), 0)),
            pl.BlockSpec((8, D), lambda bb, i: (0, 0)),
            pl.BlockSpec((D, D), lambda bb, i: (0, 0)),
            _tile_spec(tm),
        ] + r_in,
        out_specs=r_out,
        out_shape=_router_out_shape(b, n),
        compiler_params=_params(("parallel", "arbitrary"), 48),
        name="merge1",
    )(bg, z, z, z, conv_w8, w_bf, x_tiles, gm, nf, shf, scf, wr_t)


def _select_kernel(a_ref, idx_ref, gate_ref, cl_s, off_s, inc_s, hi_s, mid_s, lo_s, *, cap, chunks):
    a = a_ref[...]
    rows = a.shape[0]
    groups = rows // chunks
    per_batch = N_EXP * chunks
    assert chunks & (chunks - 1) == 0
    shift = chunks.bit_length() - 1

    def indicator(shape, row_dim):
        r = lax.broadcasted_iota(I32, shape, row_dim)
        g = lax.broadcasted_iota(I32, shape, 1 - row_dim)
        return jnp.where((r >> shift) == g, 1.0, 0.0).astype(BF)

    member = indicator((groups, rows), 1)
    spread = indicator((rows, groups), 0)
    ri = lax.broadcasted_iota(I32, (per_batch, per_batch), 0)
    ci = lax.broadcasted_iota(I32, (per_batch, per_batch), 1)
    same_f = jnp.where((ri >> shift) == (ci >> shift), 1.0, 0.0)
    same = same_f.astype(BF)
    lower = (same_f * jnp.where(ci < ri, 1.0, 0.0)).astype(BF)
    li = lax.broadcasted_iota(I32, (LANES, LANES), 0)
    lj = lax.broadcasted_iota(I32, (LANES, LANES), 1)
    incl = jnp.where(li <= lj, 1.0, 0.0).astype(BF)

    def bcast(col):
        return jnp.broadcast_to(col, (per_batch, LANES))

    def prefix(maskf):
        cl = jnp.dot(maskf.astype(BF), incl, preferred_element_type=F32)
        tot = bcast(cl[:, LANES - 1:LANES])
        off = jnp.dot(lower, tot.astype(BF), preferred_element_type=F32)
        return cl, off, tot

    min_normal = 0x00800000

    def search(step, t):
        cand = t | jnp.left_shift(jnp.int32(1), 30 - step)
        mask = jnp.where(a >= lax.bitcast_convert_type(cand, F32), 1.0, 0.0).astype(BF)
        part = jnp.dot(member, mask, preferred_element_type=F32)
        cnt = jnp.sum(part, axis=-1, keepdims=True)
        ok = jnp.broadcast_to(jnp.where(cnt >= cap, 1.0, 0.0), (groups, LANES)).astype(BF)
        ok_rows = jnp.dot(spread, ok, preferred_element_type=F32)
        return jnp.where(ok_rows > 0.5, jnp.where(cand >= min_normal, cand, t), t)

    thr = lax.fori_loop(0, 31, search, jnp.zeros((rows, LANES), I32))
    thr_f = lax.bitcast_convert_type(thr, F32)
    nxt_f = lax.bitcast_convert_type(jnp.maximum(thr + 1, min_normal), F32)
    gtf = jnp.where(a >= nxt_f, 1.0, 0.0)
    eqf = jnp.where(a >= thr_f, 1.0, 0.0) - gtf
    for bb in range(rows // per_batch):
        sl = slice(bb * per_batch, (bb + 1) * per_batch)
        gt_b, eq_b = gtf[sl], eqf[sl]
        n_gt = jnp.dot(same, bcast(jnp.sum(gt_b, axis=-1, keepdims=True)).astype(BF),
                       preferred_element_type=F32)
        cl_eq, off_eq, _ = prefix(eq_b)
        sel = gt_b + eq_b * jnp.where(cl_eq + off_eq <= cap - n_gt, 1.0, 0.0)
        cl, off, tot = prefix(sel)
        cl_s[sl, :] = cl.astype(BF)
        off_s[sl, :] = off
        inc_s[sl, :] = off + tot
        a_b = a[sl]
        a_hi = a_b.astype(BF)
        r1 = a_b - a_hi.astype(F32)
        a_mid = r1.astype(BF)
        hi_s[sl, :] = a_hi
        mid_s[sl, :] = a_mid
        lo_s[sl, :] = (r1 - a_mid.astype(F32)).astype(BF)

    slot = lax.broadcasted_iota(I32, (chunks, cap), 1).astype(F32)
    chunk_id = lax.broadcasted_iota(I32, (chunks, cap), 0).astype(F32)
    lane_id = lax.broadcasted_iota(I32, (LANES, cap), 0).astype(F32)
    tn = (((0,), (0,)), ((), ()))
    reps = cap // LANES

    def widen(x):
        return jnp.concatenate([x] * reps, axis=1)

    def per_group(g, carry):
        win = pl.ds(pl.multiple_of(g * chunks, chunks), chunks)
        inc_g = widen(inc_s[win, :])
        off_g = widen(off_s[win, :])
        chunk_of = jnp.sum(jnp.where(inc_g <= slot, 1.0, 0.0), axis=0, keepdims=True)
        pick_f = jnp.where(chunk_id == chunk_of, 1.0, 0.0)
        before = jnp.sum(pick_f * off_g, axis=0, keepdims=True)
        pick = pick_f.astype(BF)
        counts = lax.dot_general(cl_s[win, :], pick, tn, preferred_element_type=F32)
        lane_of = jnp.sum(jnp.where(counts <= slot[0:1] - before, 1.0, 0.0), axis=0, keepdims=True)
        aff = (lax.dot_general(hi_s[win, :], pick, tn, preferred_element_type=F32)
               + lax.dot_general(mid_s[win, :], pick, tn, preferred_element_type=F32)
               + lax.dot_general(lo_s[win, :], pick, tn, preferred_element_type=F32))
        gate = jnp.sum(jnp.where(lane_id == lane_of, aff, 0.0), axis=0, keepdims=True)
        idx_ref[pl.ds(g, 1), :] = (chunk_of * float(LANES) + lane_of).astype(I32)
        gate_ref[pl.ds(g, 1), :] = gate
        return carry

    lax.fori_loop(0, groups, per_group, 0, unroll=2)


def _select(aff_rows, cap, chunks):
    rows = aff_rows.shape[0]
    groups = rows // chunks
    whole = lambda shape: pl.BlockSpec(shape, lambda i: (0, 0))
    return pl.pallas_call(
        functools.partial(_select_kernel, cap=cap, chunks=chunks),
        grid=(1,),
        in_specs=[whole((rows, LANES))],
        out_specs=[whole((groups, cap)), whole((groups, cap))],
        out_shape=[jax.ShapeDtypeStruct((groups, cap), I32), jax.ShapeDtypeStruct((groups, cap), F32)],
        scratch_shapes=[pltpu.VMEM((rows, LANES), BF), pltpu.VMEM((rows, LANES), F32),
                        pltpu.VMEM((rows, LANES), F32), pltpu.VMEM((rows, LANES), BF),
                        pltpu.VMEM((rows, LANES), BF), pltpu.VMEM((rows, LANES), BF)],
        compiler_params=_params(("arbitrary",), 48),
        name="select",
    )(aff_rows)


def _gather_kernel(idx_ref, h_ref, o_ref, rows_ref):
    cap = o_ref.shape[2]

    def body(c, carry):
        t = idx_ref[0, 0, c]
        rows_ref[0, pl.ds(pl.multiple_of(c * SUB, SUB), SUB), :] = (
            h_ref[0, pl.ds(pl.multiple_of(t * SUB, SUB), SUB), :])
        return carry

    lax.fori_loop(0, cap, body, 0, unroll=8)
    o_ref[0, 0] = _load_token_tiles(rows_ref, cap).astype(BF)


def _gather(idx, hf_tiles, cap):
    b, rows, _ = hf_tiles.shape
    return pl.pallas_call(
        _gather_kernel,
        grid=(b, N_EXP),
        in_specs=[
            pl.BlockSpec((1, 1, cap), lambda bb, e: (bb * N_EXP + e, 0, 0), memory_space=pltpu.SMEM),
            pl.BlockSpec((1, rows, LANES), lambda bb, e: (bb, 0, 0)),
        ],
        out_specs=pl.BlockSpec((1, 1, cap, D), lambda bb, e: (bb, e, 0, 0)),
        out_shape=jax.ShapeDtypeStruct((b, N_EXP, cap, D), BF),
        scratch_shapes=[pltpu.VMEM((1, cap * SUB, LANES), F32)],
        compiler_params=_params(("parallel", "arbitrary"), 48),
        name="gather",
    )(idx, hf_tiles)


def _ffn_kernel(x_ref, wg_ref, wu_ref, wd_ref, o_ref, wg_s, wu_s, wd_s):
    @pl.when(pl.program_id(1) == 0)
    def _():
        wg_s[...] = wg_ref[...].astype(BF)
        wu_s[...] = wu_ref[...].astype(BF)
        wd_s[...] = wd_ref[...].astype(BF)

    x = x_ref[0, 0]
    g = jnp.dot(x, wg_s[...], preferred_element_type=F32)
    u = jnp.dot(x, wu_s[...], preferred_element_type=F32)
    act = (g * (1.0 / (1.0 + jnp.exp(-g))) * u).astype(BF)
    _store_token_tiles(o_ref.at[0], jnp.dot(act, wd_s[...], preferred_element_type=F32))


def _ffn(xs, w_gate, w_up, w_down, layer):
    b, _, cap, _ = xs.shape
    dexp = w_gate.shape[-1]
    xspec = pl.BlockSpec((1, 1, cap, D), lambda e, bb: (bb, e, 0, 0))
    yspec = pl.BlockSpec((1, 1, cap * SUB, LANES), lambda e, bb: (bb, e, 0, 0))
    wspec_in = pl.BlockSpec((None, None, D, dexp), lambda e, bb: (layer, e, 0, 0))
    wspec_out = pl.BlockSpec((None, None, dexp, D), lambda e, bb: (layer, e, 0, 0))
    return pl.pallas_call(
        _ffn_kernel,
        grid=(N_EXP, b),
        in_specs=[xspec, wspec_in, wspec_in, wspec_out],
        out_specs=yspec,
        out_shape=jax.ShapeDtypeStruct((b, N_EXP, cap * SUB, LANES), F32),
        scratch_shapes=[pltpu.VMEM((D, dexp), BF), pltpu.VMEM((D, dexp), BF), pltpu.VMEM((dexp, D), BF)],
        compiler_params=_params(("arbitrary", "arbitrary"), 56),
        name="expert_ffn",
    )(xs, w_gate, w_up, w_down)


def _scatter_kernel(idx_ref, gate_ref, y_ref, x_ref, gf_ref, o_ref):
    cap = idx_ref.shape[2]

    @pl.when(pl.program_id(1) == 0)
    def _():
        o_ref[...] = x_ref[...]

    gf = gf_ref[0]

    group = 16

    def body(g, carry):
        c0 = g * group
        ts = [pl.multiple_of(idx_ref[0, 0, c0 + j] * SUB, SUB) for j in range(group)]
        new = []
        for j in range(group):
            y = y_ref[0, 0, pl.ds(pl.multiple_of((c0 + j) * SUB, SUB), SUB), :]
            new.append(o_ref[0, pl.ds(ts[j], SUB), :] + (gf * gate_ref[0, 0, c0 + j]) * y)
        for j in range(group):
            o_ref[0, pl.ds(ts[j], SUB), :] = new[j]
        return carry

    lax.fori_loop(0, cap // group, body, 0)


def _scatter(idx, gate, y_tiles, x_tiles, gf_tile):
    b, rows, _ = x_tiles.shape
    cap = idx.shape[2]
    sm = lambda: pl.BlockSpec((1, 1, cap), lambda bb, e: (bb * N_EXP + e, 0, 0), memory_space=pltpu.SMEM)
    resident = lambda **kw: pl.BlockSpec((1, rows, LANES), lambda bb, e: (bb, 0, 0), **kw)
    return pl.pallas_call(
        _scatter_kernel,
        grid=(b, N_EXP),
        in_specs=[
            sm(), sm(),
            pl.BlockSpec((1, 1, cap * SUB, LANES), lambda bb, e: (bb, e, 0, 0)),
            resident(pipeline_mode=pl.Buffered(1)),
            pl.BlockSpec((1, SUB, LANES), lambda bb, e: (bb, 0, 0)),
        ],
        out_specs=resident(),
        out_shape=jax.ShapeDtypeStruct((b, rows, LANES), F32),
        compiler_params=_params(("parallel", "arbitrary"), 56),
        name="scatter_add",
    )(idx, gate, y_tiles, x_tiles, gf_tile)


def _moe(x1_tiles, hf_tiles, aff_t, gf, w_gate, w_up, w_down, layer):
    b, _, n = aff_t.shape
    cap = CAP_FACTOR * n // N_EXP
    chunks = n // LANES
    idx, gate = _select(aff_t.reshape(b * N_EXP * chunks, LANES), cap, chunks)
    idx = idx.reshape(b * N_EXP, 1, cap)
    gate = gate.reshape(b * N_EXP, 1, cap)
    xs = _gather(idx, hf_tiles, cap)
    y_tiles = _ffn(xs, w_gate, w_up, w_down, layer)
    return _scatter(idx, gate, y_tiles, x1_tiles, gf.reshape(b, SUB, LANES))


def _to_rows_kernel(x_ref, o_ref):
    o_ref[0] = _load_token_tiles(x_ref, o_ref.shape[1])


def _to_rows(x_tiles):
    b = x_tiles.shape[0]
    n = x_tiles.shape[1] // SUB
    tm = 512
    return pl.pallas_call(
        _to_rows_kernel,
        grid=(b, n // tm),
        in_specs=[_tile_spec(tm)],
        out_specs=pl.BlockSpec((1, tm, D), lambda bb, i: (bb, i, 0)),
        out_shape=jax.ShapeDtypeStruct((b, n, D), F32),
        compiler_params=_params(("parallel", "arbitrary"), 32),
        name="to_rows",
    )(x_tiles)


def _rope_tables(n_lat, n_ctx):
    rows = n_lat // GRID_W
    r = jnp.repeat(jnp.arange(rows, dtype=F32), GRID_W)
    col = jnp.tile(jnp.arange(GRID_W, dtype=F32), rows)
    n_freq = HEAD_DIM // 4
    inv = ROPE_BASE ** (-jnp.arange(n_freq, dtype=F32) / n_freq)
    ar = r[:, None] * inv
    ac = col[:, None] * inv
    ang = jnp.concatenate([ar, ar, ac, ac], axis=-1)
    sign = jnp.where((jnp.arange(HEAD_DIM) % 32) < 16, -1.0, 1.0).astype(F32)
    cos = jnp.concatenate([jnp.cos(ang), jnp.ones((n_ctx, HEAD_DIM), F32)], axis=0)
    sin_s = jnp.concatenate([jnp.sin(ang) * sign, jnp.zeros((n_ctx, HEAD_DIM), F32)], axis=0)
    return jnp.tile(cos, (1, 2)), jnp.tile(sin_s, (1, 2))


def _dft_tables(n_lat):
    c = jnp.arange(FOUR_G, dtype=I32)
    ang_c = (2.0 * math.pi / FOUR_G) * ((c[:, None] * c[None, :]) % FOUR_G).astype(F32)
    eye = jnp.eye(FOUR_W // FOUR_G, dtype=F32)
    bd = jnp.concatenate([jnp.kron(eye, jnp.cos(ang_c)), jnp.kron(eye, jnp.sin(ang_c))], axis=1)
    bd = (bd * FOUR_G ** -0.5).astype(BF)
    tr = 256
    n = jnp.arange(n_lat, dtype=I32)[None, :]
    r = jnp.arange(tr, dtype=I32)[:, None]
    i = jnp.arange(n_lat // tr, dtype=I32)[:, None]
    row_ang = (2.0 * math.pi / n_lat) * ((r * n) % n_lat).astype(F32)
    tile_ang = (2.0 * math.pi / n_lat) * ((i * tr * n) % n_lat).astype(F32)
    scale = n_lat ** -0.5
    tables = (jnp.cos(row_ang) * scale, jnp.sin(row_ang) * scale,
              jnp.cos(tile_ang)[:, None, :], jnp.sin(tile_ang)[:, None, :])
    return bd, tables


def kernel(x, c, ctx, c_ctx, ada_w, ada_b, norm_mix, norm_ffn, attn_w_in, attn_q_norm, attn_k_norm,
           lam_q1, lam_k1, lam_q2, lam_k2, attn_subln, attn_w_out, conv_w_in, conv_w, conv_w_out,
           router_w, moe_w_gate, moe_w_up, moe_w_down):
    b, n, _ = x.shape
    n_ctx = ctx.shape[1]
    assert x.shape[2] == D and n % 512 == 0 and n_ctx % 256 == 0

    cond8 = jnp.concatenate([c, c_ctx[None, :], jnp.zeros((8 - b - 1, D), F32)], axis=0)
    ada = _ada(cond8, ada_w, ada_b)

    def mods(layer):
        m = ada[layer].reshape(8, 6, D)
        return [m[:, j] for j in range(6)]

    vec = lambda t: t[:b].reshape(b, 1, D)

    sh_m, sc_m, g_m, sh_f, sc_f, g_f = mods(0)
    both = lambda t: jnp.stack([t[:b], jnp.broadcast_to(t[b], (b, D))], axis=1).reshape(b, 2, 1, D)
    cos, sin_s = _rope_tables(n, n_ctx)
    bd, dft_tabs = _dft_tables(n)
    tile2 = lambda t: jnp.tile(t.reshape(1, HEAD_DIM), (1, 2))
    q, k, v, f = _proj0(x, ctx, norm_mix[0].reshape(1, D), both(sh_m), both(sc_m),
                        attn_w_in[0].astype(BF), cos, sin_s,
                        tile2(attn_q_norm[0]), tile2(attn_k_norm[0]))
    lam_init = 0.8 - 0.6 * math.exp(-0.3 * 0)
    lamv = jnp.zeros((8, LANES), F32).at[:4, :HEAD_DIM].set(
        jnp.stack([lam_q1[0], lam_k1[0], lam_q2[0], lam_k2[0]]))
    att = _attention(q, k, v, lamv, attn_subln[0].reshape(1, V_DIM), n, lam_init)
    four = _fourier(f, bd, dft_tabs, n)
    x1, hf, aff_t = _merge0(att, four, attn_w_out[0].astype(BF), x, vec(g_m),
                            norm_ffn[0].reshape(1, D), vec(sh_f), vec(sc_f), router_w[0].T)
    x2 = _moe(x1, hf, aff_t, vec(g_f), moe_w_gate, moe_w_up, moe_w_down, 0)

    sh_m, sc_m, g_m, sh_f, sc_f, g_f = mods(1)
    bg, z = _proj1(x2, norm_mix[1].reshape(1, D), vec(sh_m), vec(sc_m), conv_w_in[0].astype(BF))
    conv_w8 = jnp.zeros((8, D), F32).at[:3].set(conv_w[0])
    x3, hf, aff_t = _merge1(bg, z, conv_w8, conv_w_out[0].astype(BF), x2, vec(g_m),
                            norm_ffn[1].reshape(1, D), vec(sh_f), vec(sc_f), router_w[1].T)
    return _to_rows(_moe(x3, hf, aff_t, vec(g_f), moe_w_gate, moe_w_up, moe_w_down, 1))
```

```python
import functools
import math

import jax
import jax.numpy as jnp
from jax import lax
from jax.experimental import pallas as pl
from jax.experimental.pallas import tpu as pltpu

BF = jnp.bfloat16
F32 = jnp.float32
I32 = jnp.int32

D = 1024
GRID_W = 64
N_HEADS = 6
HEAD_DIM = 64
V_DIM = 2 * HEAD_DIM
QK_W = N_HEADS * 2 * HEAD_DIM
V_W = N_HEADS * V_DIM
FOUR_W = 256
FOUR_G = 64
IN_W = 2 * QK_W + V_W + FOUR_W
N_EXP = 16
CAP_FACTOR = 2
ROPE_BASE = 10000.0
EPS = 1e-6
Q_SCALE = HEAD_DIM ** -0.5 * math.log2(math.e)
ATTN_SHIFT_LIMIT = 40.0
LANES = 128
MIB = 1024 * 1024

NT = (((1,), (1,)), ((), ()))


def _params(sem, vmem_mib):
    return pltpu.CompilerParams(dimension_semantics=sem, vmem_limit_bytes=vmem_mib * MIB)


def _split2(x):
    hi = x.astype(BF)
    lo = (x - hi.astype(F32)).astype(BF)
    return hi, lo


def _dot3(a, b, dims=(((1,), (0,)), ((), ()))):
    ah, al = _split2(a)
    bh, bl = _split2(b)
    dg = functools.partial(lax.dot_general, dimension_numbers=dims, preferred_element_type=F32)
    return dg(ah, bh) + dg(ah, bl) + dg(al, bh)


def _modulate(x, nw, shift, scale):
    ms = jnp.mean(x * x, axis=-1, keepdims=True)
    return (x * lax.rsqrt(ms + EPS) * nw) * (1.0 + scale) + shift


def _ada_kernel(c_ref, w_ref, b_ref, o_ref):
    cv = c_ref[...]
    s = cv * (1.0 / (1.0 + jnp.exp(-cv)))
    o_ref[0] = _dot3(s, w_ref[0]) + b_ref[0]


def _ada(cond8, ada_w, ada_b):
    depth = ada_w.shape[0]
    tn = 1536
    return pl.pallas_call(
        _ada_kernel,
        grid=(depth, 6 * D // tn),
        in_specs=[
            pl.BlockSpec((8, D), lambda l, j: (0, 0)),
            pl.BlockSpec((1, D, tn), lambda l, j: (l, 0, j)),
            pl.BlockSpec((1, 1, tn), lambda l, j: (l, 0, j)),
        ],
        out_specs=pl.BlockSpec((1, 8, tn), lambda l, j: (l, 0, j)),
        out_shape=jax.ShapeDtypeStruct((depth, 8, 6 * D), F32),
        compiler_params=_params(("arbitrary", "arbitrary"), 40),
        name="ada",
    )(cond8, ada_w, ada_b.reshape(depth, 1, 6 * D))


def _norm_rope(xb, wn, cos, sin_s, out_scale):
    lane = lax.broadcasted_iota(I32, xb.shape, 1)
    lo = lane < HEAD_DIM
    ss = xb * xb
    s_lo = jnp.sum(jnp.where(lo, ss, 0.0), axis=-1, keepdims=True)
    s_hi = jnp.sum(jnp.where(lo, 0.0, ss), axis=-1, keepdims=True)
    inv = jnp.where(lo, lax.rsqrt(s_lo * (1.0 / HEAD_DIM) + EPS),
                    lax.rsqrt(s_hi * (1.0 / HEAD_DIM) + EPS))
    y = xb * inv * wn
    fwd = pltpu.roll(y, LANES - 16, axis=1)
    bwd = pltpu.roll(y, 16, axis=1)
    rot = jnp.where((lane & 31) < 16, fwd, bwd)
    out = y * cos + rot * sin_s
    if out_scale != 1.0:
        out = out * out_scale
    return out


def _proj0_kernel(x_ref, ctx_ref, nw_ref, sh_ref, sc_ref, w_ref, cos_ref, sin_ref, qn_ref, kn_ref,
                  q_ref, k_ref, v_ref, f_ref, *, lat_tiles):
    x = jnp.where(pl.program_id(1) < lat_tiles, x_ref[0], ctx_ref[0])
    h = _modulate(x, nw_ref[...], sh_ref[0, 0], sc_ref[0, 0]).astype(BF)
    cos = cos_ref[...]
    sin_s = sin_ref[...]
    qn = qn_ref[...]
    kn = kn_ref[...]
    q_scale = Q_SCALE
    pq = jnp.dot(h, w_ref[:, :QK_W], preferred_element_type=F32)
    pk = jnp.dot(h, w_ref[:, QK_W:2 * QK_W], preferred_element_type=F32)
    for hh in range(N_HEADS):
        c0 = hh * V_DIM
        q_ref[0, :, c0:c0 + V_DIM] = _norm_rope(pq[:, c0:c0 + V_DIM], qn, cos, sin_s, q_scale).astype(BF)
        k_ref[0, :, c0:c0 + V_DIM] = _norm_rope(pk[:, c0:c0 + V_DIM], kn, cos, sin_s, 1.0).astype(BF)
    v_ref[0] = jnp.dot(h, w_ref[:, 2 * QK_W:2 * QK_W + V_W], preferred_element_type=F32).astype(BF)
    f_ref[0] = jnp.dot(h, w_ref[:, 2 * QK_W + V_W:], preferred_element_type=F32).astype(BF)


def _proj0(x, ctx, nw, sh2, sc2, w_bf, cos, sin_s, qn, kn):
    b, n_lat, _ = x.shape
    n_ctx = ctx.shape[1]
    s_tot = n_lat + n_ctx
    tm = 256
    lat_tiles = n_lat // tm
    mod_spec = pl.BlockSpec((1, 1, 1, D), lambda bb, i: (bb, i // lat_tiles, 0, 0))
    full = lambda shape: pl.BlockSpec(shape, lambda bb, i: tuple(0 for _ in shape))
    row = lambda w: pl.BlockSpec((1, tm, w), lambda bb, i: (bb, i, 0))
    return pl.pallas_call(
        functools.partial(_proj0_kernel, lat_tiles=lat_tiles),
        grid=(b, s_tot // tm),
        in_specs=[
            pl.BlockSpec((1, tm, D), lambda bb, i: (bb, jnp.minimum(i, lat_tiles - 1), 0)),
            pl.BlockSpec((1, tm, D), lambda bb, i: (bb, jnp.maximum(i - lat_tiles, 0), 0)),
            full((1, D)), mod_spec, mod_spec, full((D, IN_W)),
            pl.BlockSpec((tm, LANES), lambda bb, i: (i, 0)),
            pl.BlockSpec((tm, LANES), lambda bb, i: (i, 0)),
            full((1, LANES)), full((1, LANES)),
        ],
        out_specs=[row(QK_W), row(QK_W), row(V_W), row(FOUR_W)],
        out_shape=[
            jax.ShapeDtypeStruct((b, s_tot, QK_W), BF),
            jax.ShapeDtypeStruct((b, s_tot, QK_W), BF),
            jax.ShapeDtypeStruct((b, s_tot, V_W), BF),
            jax.ShapeDtypeStruct((b, s_tot, FOUR_W), BF),
        ],
        compiler_params=_params(("parallel", "arbitrary"), 48),
        name="proj0",
    )(x, ctx, nw, sh2, sc2, w_bf, cos, sin_s, qn, kn)


def _attn_kernel(bound_ref, q_ref, k_ref, v_ref, lam_ref, sub_ref, o_ref, vt_ref, m_ref, e_ref,
                 *, lam_init, kc):
    @pl.when(pl.program_id(2) == 0)
    def _():
        vt_ref[...] = v_ref[0].astype(F32).T.astype(BF)

    q = q_ref[0]
    lane = lax.broadcasted_iota(I32, q.shape, 1)
    zero = jnp.zeros_like(q)
    lv = lam_ref[...]
    t1 = jnp.sum(lv[0:1] * lv[1:2], axis=-1, keepdims=True)
    t2 = jnp.sum(lv[2:3] * lv[3:4], axis=-1, keepdims=True)
    lam = jnp.exp(t1) - jnp.exp(t2) + lam_init

    qs = (jnp.where(lane < HEAD_DIM, q, zero), jnp.where(lane < HEAD_DIM, zero, q))
    tq = q.shape[0]
    n_chunks = k_ref.shape[1] // kc

    def scores(h, c):
        return lax.dot_general(k_ref[0, c * kc:(c + 1) * kc, :], qs[h], NT, preferred_element_type=F32)

    def fold(x, op):
        return op(x.reshape(kc // 8, 8, tq), axis=0)

    bound = bound_ref[0]
    small = bound <= ATTN_SHIFT_LIMIT

    @pl.when(small)
    def _():
        m_ref[...] = jnp.zeros(m_ref.shape, F32) + bound

    @pl.when(jnp.logical_not(small))
    def _():
        for h in range(2):
            m = jnp.full((8, tq), -jnp.inf, F32)
            for c in range(n_chunks):
                m = jnp.maximum(m, fold(scores(h, c), jnp.max))
            m_ref[h] = jnp.broadcast_to(jnp.max(m, axis=0, keepdims=True), (8, tq))

    ms = [m_ref[h][0:1, :] for h in range(2)]
    ls = [jnp.zeros((8, tq), F32) for _ in range(2)]
    for c in range(n_chunks):
        for h in range(2):
            e = jnp.exp2(scores(h, c) - ms[h])
            ls[h] = ls[h] + fold(e, jnp.sum)
            e_ref[h, c] = e.astype(BF)
    l0, l1 = [jnp.sum(l, axis=0, keepdims=True) for l in ls]

    beta = (lam * l0 / l1).astype(BF)
    acc = jnp.zeros((V_DIM, tq), F32)
    for c in range(n_chunks):
        a = e_ref[0, c] - beta * e_ref[1, c]
        acc = acc + jnp.dot(vt_ref[:, c * kc:(c + 1) * kc], a, preferred_element_type=F32)
    o = acc * (1.0 / l0)
    ms = jnp.mean(o * o, axis=0, keepdims=True)
    o = o * lax.rsqrt(ms + EPS) * sub_ref[...] * (1.0 - lam_init)
    o_ref[0] = o.T.astype(BF)


def _attention(bound, q, k, v, lamv, subln, n_lat, lam_init):
    b, s_tot, _ = k.shape
    tq = 512
    kc = 256
    assert s_tot % kc == 0
    kv_spec = pl.BlockSpec((1, s_tot, V_DIM), lambda bb, hh, i: (bb, 0, hh))
    q_spec = pl.BlockSpec((1, tq, V_DIM), lambda bb, hh, i: (bb, i, hh))
    return pl.pallas_call(
        functools.partial(_attn_kernel, lam_init=lam_init, kc=kc),
        grid=(b, N_HEADS, n_lat // tq),
        in_specs=[
            pl.BlockSpec(memory_space=pltpu.SMEM),
            q_spec, kv_spec, kv_spec,
            pl.BlockSpec((8, LANES), lambda bb, hh, i: (0, 0)),
            pl.BlockSpec((V_DIM, 1), lambda bb, hh, i: (0, 0)),
        ],
        out_specs=q_spec,
        out_shape=jax.ShapeDtypeStruct((b, n_lat, V_W), BF),
        scratch_shapes=[pltpu.VMEM((V_DIM, s_tot), BF),
                        pltpu.VMEM((2, 8, tq), F32),
                        pltpu.VMEM((2, s_tot // kc, kc, tq), BF)],
        compiler_params=_params(("parallel", "parallel", "arbitrary"), 48),
        name="diff_attn",
    )(bound, q, k, v, lamv, subln)


def _fourier_kernel(f_ref, bd_ref, cb_ref, sb_ref, ca_ref, sa_ref, o_ref, g_ref):
    nb, n_lat, _ = f_ref.shape

    @pl.when(pl.program_id(0) == 0)
    def _():
        rows = 1024
        for bb in range(nb):
            for r in range(n_lat // rows):
                fc = jnp.dot(f_ref[bb, r * rows:(r + 1) * rows, :], bd_ref[...],
                             preferred_element_type=F32)
                g_ref[r * rows:(r + 1) * rows, bb * FOUR_W:(bb + 1) * FOUR_W] = fc[:, :FOUR_W].astype(BF)
                g_ref[n_lat + r * rows:n_lat + (r + 1) * rows,
                      bb * FOUR_W:(bb + 1) * FOUR_W] = fc[:, FOUR_W:].astype(BF)

    ca = ca_ref[0]
    sa = sa_ref[0]
    cb = cb_ref[...]
    sb = sb_ref[...]
    w_cos = (cb * ca - sb * sa).astype(BF)
    w_sin = (sb * ca + cb * sa).astype(BF)
    y = (jnp.dot(w_cos, g_ref[:n_lat, :], preferred_element_type=F32)
         - jnp.dot(w_sin, g_ref[n_lat:, :], preferred_element_type=F32))
    for bb in range(nb):
        o_ref[bb] = y[:, bb * FOUR_W:(bb + 1) * FOUR_W].astype(BF)


def _fourier(f, bd, tables, n_lat):
    b = f.shape[0]
    cos_row, sin_row, cos_tile, sin_tile = tables
    tr = cos_row.shape[0]
    row_tab = pl.BlockSpec((tr, n_lat), lambda i: (0, 0))
    tile_tab = pl.BlockSpec((1, 1, n_lat), lambda i: (i, 0, 0))
    return pl.pallas_call(
        _fourier_kernel,
        grid=(n_lat // tr,),
        in_specs=[
            pl.BlockSpec((b, n_lat, FOUR_W), lambda i: (0, 0, 0)),
            pl.BlockSpec((FOUR_W, 2 * FOUR_W), lambda i: (0, 0)),
            row_tab, row_tab, tile_tab, tile_tab,
        ],
        out_specs=pl.BlockSpec((b, tr, FOUR_W), lambda i: (0, i, 0)),
        out_shape=jax.ShapeDtypeStruct((b, n_lat, FOUR_W), BF),
        scratch_shapes=[pltpu.VMEM((2 * n_lat, b * FOUR_W), BF)],
        compiler_params=_params(("arbitrary",), 56),
        name="fourier",
    )(f, bd, cos_row, sin_row, cos_tile, sin_tile)


SUB = D // LANES


def _store_token_tiles(ref, val):
    rows = val.shape[0]
    for j in range(SUB):
        ref[0, pl.ds(j, rows, stride=SUB), :] = val[:, j * LANES:(j + 1) * LANES]


def _load_token_tiles(ref, rows):
    return jnp.concatenate([ref[0, pl.ds(j, rows, stride=SUB), :] for j in range(SUB)], axis=1)


def _residual_router(x, y, gm, nf, shf, scf, wr_t, x1_ref, hf_ref, aff_ref):
    x1 = x + gm * y
    _store_token_tiles(x1_ref, x1)
    hf = _modulate(x1, nf, shf, scf)
    _store_token_tiles(hf_ref, hf)
    logits = _dot3(wr_t, hf, NT)
    m = jnp.max(logits, axis=0, keepdims=True)
    e = jnp.exp(logits - m)
    aff_ref[0] = e / jnp.sum(e, axis=0, keepdims=True)


def _merge0_kernel(att_ref, four_ref, w_ref, x_ref, gm_ref, nf_ref, shf_ref, scf_ref, wr_ref,
                   x1_ref, hf_ref, aff_ref):
    y = jnp.dot(att_ref[0], w_ref[:V_W, :], preferred_element_type=F32)
    y = y + jnp.dot(four_ref[0], w_ref[V_W:, :], preferred_element_type=F32)
    _residual_router(x_ref[0], y, gm_ref[0], nf_ref[...], shf_ref[0], scf_ref[0], wr_ref[...],
                     x1_ref, hf_ref, aff_ref)


def _router_specs(tm):
    vec = pl.BlockSpec((1, 1, D), lambda bb, i: (bb, 0, 0))
    row = pl.BlockSpec((1, tm, D), lambda bb, i: (bb, i, 0))
    in_specs = [vec, pl.BlockSpec((1, D), lambda bb, i: (0, 0)), vec, vec,
                pl.BlockSpec((N_EXP, D), lambda bb, i: (0, 0))]
    tiles = _tile_spec(tm)
    out_specs = [tiles, tiles, pl.BlockSpec((1, N_EXP, tm), lambda bb, i: (bb, 0, i))]
    return row, in_specs, out_specs


def _tile_spec(tm):
    return pl.BlockSpec((1, tm * SUB, LANES), lambda bb, i: (bb, i, 0))


def _router_out_shape(b, n):
    return [jax.ShapeDtypeStruct((b, n * SUB, LANES), F32), jax.ShapeDtypeStruct((b, n * SUB, LANES), F32),
            jax.ShapeDtypeStruct((b, N_EXP, n), F32)]


def _merge0(att, four, w_bf, x, gm, nf, shf, scf, wr_t):
    b, n, _ = x.shape
    tm = 512
    row, r_in, r_out = _router_specs(tm)
    return pl.pallas_call(
        _merge0_kernel,
        grid=(b, n // tm),
        in_specs=[
            pl.BlockSpec((1, tm, V_W), lambda bb, i: (bb, i, 0)),
            pl.BlockSpec((1, tm, FOUR_W), lambda bb, i: (bb, i, 0)),
            pl.BlockSpec((D, D), lambda bb, i: (0, 0)),
            row,
        ] + r_in,
        out_specs=r_out,
        out_shape=_router_out_shape(b, n),
        compiler_params=_params(("parallel", "arbitrary"), 48),
        name="merge0",
    )(att, four, w_bf, x, gm, nf, shf, scf, wr_t)


def _proj1_kernel(x_ref, nw_ref, sh_ref, sc_ref, w_ref, bg_ref, z_ref):
    x = _load_token_tiles(x_ref, bg_ref.shape[1])
    h = _modulate(x, nw_ref[...], sh_ref[0], sc_ref[0]).astype(BF)
    bg_ref[0] = jnp.dot(h, w_ref[:, :D], preferred_element_type=F32)
    cg = jnp.dot(h, w_ref[:, D:2 * D], preferred_element_type=F32)
    u = jnp.dot(h, w_ref[:, 2 * D:], preferred_element_type=F32)
    z_ref[0] = cg * u


def _proj1(x_tiles, nw, sh, sc, w_bf):
    b = x_tiles.shape[0]
    n = x_tiles.shape[1] // SUB
    tm = 512
    vec = pl.BlockSpec((1, 1, D), lambda bb, i: (bb, 0, 0))
    row = pl.BlockSpec((1, tm, D), lambda bb, i: (bb, i, 0))
    return pl.pallas_call(
        _proj1_kernel,
        grid=(b, n // tm),
        in_specs=[_tile_spec(tm), pl.BlockSpec((1, D), lambda bb, i: (0, 0)), vec, vec,
                  pl.BlockSpec((D, 3 * D), lambda bb, i: (0, 0))],
        out_specs=[row, row],
        out_shape=[jax.ShapeDtypeStruct((b, n, D), F32), jax.ShapeDtypeStruct((b, n, D), F32)],
        compiler_params=_params(("parallel", "arbitrary"), 48),
        name="proj1",
    )(x_tiles, nw, sh, sc, w_bf)


def _merge1_kernel(bg_ref, z_ref, zp_ref, zn_ref, cw_ref, w_ref, x_ref, gm_ref, nf_ref, shf_ref,
                   scf_ref, wr_ref, x1_ref, hf_ref, aff_ref):
    i = pl.program_id(1)
    last = pl.num_programs(1) - 1
    z = z_ref[0]
    tm = z.shape[0]
    rowid = lax.broadcasted_iota(I32, z.shape, 0)
    prev_row = jnp.where(i > 0, zp_ref[0, 7:8, :], 0.0)
    next_row = jnp.where(i < last, zn_ref[0, 0:1, :], 0.0)
    z_up = jnp.where(rowid == 0, prev_row, pltpu.roll(z, 1, axis=0))
    z_dn = jnp.where(rowid == tm - 1, next_row, pltpu.roll(z, tm - 1, axis=0))
    cw = cw_ref[...]
    conv = cw[0:1] * z_up + cw[1:2] * z + cw[2:3] * z_dn
    y = jnp.dot((bg_ref[0] * conv).astype(BF), w_ref[...], preferred_element_type=F32)
    _residual_router(_load_token_tiles(x_ref, tm), y, gm_ref[0], nf_ref[...], shf_ref[0], scf_ref[0],
                     wr_ref[...], x1_ref, hf_ref, aff_ref)


def _merge1(bg, z, conv_w8, w_bf, x_tiles, gm, nf, shf, scf, wr_t):
    b, n, _ = bg.shape
    tm = 512
    halo = 8
    per = tm // halo
    n_halo = n // halo
    row, r_in, r_out = _router_specs(tm)
    return pl.pallas_call(
        _merge1_kernel,
        grid=(b, n // tm),
        in_specs=[
            row, row,
            pl.BlockSpec((1, halo, D), lambda bb, i: (bb, jnp.maximum(i * per - 1, 0), 0)),
            pl.BlockSpec((1, halo, D), lambda bb, i: (bb, jnp.minimum((i + 1) * per, n_halo - 1), 0)),
            pl.BlockSpec((8, D), lambda bb, i: (0, 0)),
            pl.BlockSpec((D, D), lambda bb, i: (0, 0)),
            _tile_spec(tm),
        ] + r_in,
        out_specs=r_out,
        out_shape=_router_out_shape(b, n),
        compiler_params=_params(("parallel", "arbitrary"), 48),
        name="merge1",
    )(bg, z, z, z, conv_w8, w_bf, x_tiles, gm, nf, shf, scf, wr_t)


def _select_kernel(a_ref, idx_ref, gate_ref, cl_s, off_s, inc_s, hi_s, mid_s, lo_s, *, cap, chunks):
    a = a_ref[...]
    rows = a.shape[0]
    groups = rows // chunks
    per_batch = N_EXP * chunks
    assert chunks & (chunks - 1) == 0
    shift = chunks.bit_length() - 1

    def indicator(shape, row_dim):
        r = lax.broadcasted_iota(I32, shape, row_dim)
        g = lax.broadcasted_iota(I32, shape, 1 - row_dim)
        return jnp.where((r >> shift) == g, 1.0, 0.0).astype(BF)

    member = indicator((groups, rows), 1)
    spread = indicator((rows, groups), 0)
    ri = lax.broadcasted_iota(I32, (per_batch, per_batch), 0)
    ci = lax.broadcasted_iota(I32, (per_batch, per_batch), 1)
    same_f = jnp.where((ri >> shift) == (ci >> shift), 1.0, 0.0)
    same = same_f.astype(BF)
    lower = (same_f * jnp.where(ci < ri, 1.0, 0.0)).astype(BF)
    li = lax.broadcasted_iota(I32, (LANES, LANES), 0)
    lj = lax.broadcasted_iota(I32, (LANES, LANES), 1)
    incl = jnp.where(li <= lj, 1.0, 0.0).astype(BF)

    def bcast(col):
        return jnp.broadcast_to(col, (per_batch, LANES))

    def prefix(maskf):
        cl = jnp.dot(maskf.astype(BF), incl, preferred_element_type=F32)
        tot = bcast(cl[:, LANES - 1:LANES])
        off = jnp.dot(lower, tot.astype(BF), preferred_element_type=F32)
        return cl, off, tot

    min_normal = 0x00800000

    def search(step, t):
        cand = t | jnp.left_shift(jnp.int32(1), 30 - step)
        mask = jnp.where(a >= lax.bitcast_convert_type(cand, F32), 1.0, 0.0).astype(BF)
        part = jnp.dot(member, mask, preferred_element_type=F32)
        cnt = jnp.sum(part, axis=-1, keepdims=True)
        ok = jnp.broadcast_to(jnp.where(cnt >= cap, 1.0, 0.0), (groups, LANES)).astype(BF)
        ok_rows = jnp.dot(spread, ok, preferred_element_type=F32)
        return jnp.where(ok_rows > 0.5, jnp.where(cand >= min_normal, cand, t), t)

    thr = lax.fori_loop(0, 31, search, jnp.zeros((rows, LANES), I32))
    thr_f = lax.bitcast_convert_type(thr, F32)
    nxt_f = lax.bitcast_convert_type(jnp.maximum(thr + 1, min_normal), F32)
    gtf = jnp.where(a >= nxt_f, 1.0, 0.0)
    eqf = jnp.where(a >= thr_f, 1.0, 0.0) - gtf
    for bb in range(rows // per_batch):
        sl = slice(bb * per_batch, (bb + 1) * per_batch)
        gt_b, eq_b = gtf[sl], eqf[sl]
        n_gt = jnp.dot(same, bcast(jnp.sum(gt_b, axis=-1, keepdims=True)).astype(BF),
                       preferred_element_type=F32)
        cl_eq, off_eq, _ = prefix(eq_b)
        sel = gt_b + eq_b * jnp.where(cl_eq + off_eq <= cap - n_gt, 1.0, 0.0)
        cl, off, tot = prefix(sel)
        cl_s[sl, :] = cl.astype(BF)
        off_s[sl, :] = off
        inc_s[sl, :] = off + tot
        a_b = a[sl]
        a_hi = a_b.astype(BF)
        r1 = a_b - a_hi.astype(F32)
        a_mid = r1.astype(BF)
        hi_s[sl, :] = a_hi
        mid_s[sl, :] = a_mid
        lo_s[sl, :] = (r1 - a_mid.astype(F32)).astype(BF)

    slot = lax.broadcasted_iota(I32, (chunks, cap), 1).astype(F32)
    chunk_id = lax.broadcasted_iota(I32, (chunks, cap), 0).astype(F32)
    lane_id = lax.broadcasted_iota(I32, (LANES, cap), 0).astype(F32)
    tn = (((0,), (0,)), ((), ()))
    reps = cap // LANES

    def widen(x):
        return jnp.concatenate([x] * reps, axis=1)

    def per_group(g, carry):
        win = pl.ds(pl.multiple_of(g * chunks, chunks), chunks)
        inc_g = widen(inc_s[win, :])
        off_g = widen(off_s[win, :])
        chunk_of = jnp.sum(jnp.where(inc_g <= slot, 1.0, 0.0), axis=0, keepdims=True)
        pick_f = jnp.where(chunk_id == chunk_of, 1.0, 0.0)
        before = jnp.sum(pick_f * off_g, axis=0, keepdims=True)
        pick = pick_f.astype(BF)
        counts = lax.dot_general(cl_s[win, :], pick, tn, preferred_element_type=F32)
        lane_of = jnp.sum(jnp.where(counts <= slot[0:1] - before, 1.0, 0.0), axis=0, keepdims=True)
        aff = (lax.dot_general(hi_s[win, :], pick, tn, preferred_element_type=F32)
               + lax.dot_general(mid_s[win, :], pick, tn, preferred_element_type=F32)
               + lax.dot_general(lo_s[win, :], pick, tn, preferred_element_type=F32))
        gate = jnp.sum(jnp.where(lane_id == lane_of, aff, 0.0), axis=0, keepdims=True)
        idx_ref[pl.ds(g, 1), :] = (chunk_of * float(LANES) + lane_of).astype(I32)
        gate_ref[pl.ds(g, 1), :] = gate
        return carry

    lax.fori_loop(0, groups, per_group, 0, unroll=2)


def _select(aff_rows, cap, chunks):
    rows = aff_rows.shape[0]
    groups = rows // chunks
    whole = lambda shape: pl.BlockSpec(shape, lambda i: (0, 0))
    return pl.pallas_call(
        functools.partial(_select_kernel, cap=cap, chunks=chunks),
        grid=(1,),
        in_specs=[whole((rows, LANES))],
        out_specs=[whole((groups, cap)), whole((groups, cap))],
        out_shape=[jax.ShapeDtypeStruct((groups, cap), I32), jax.ShapeDtypeStruct((groups, cap), F32)],
        scratch_shapes=[pltpu.VMEM((rows, LANES), BF), pltpu.VMEM((rows, LANES), F32),
                        pltpu.VMEM((rows, LANES), F32), pltpu.VMEM((rows, LANES), BF),
                        pltpu.VMEM((rows, LANES), BF), pltpu.VMEM((rows, LANES), BF)],
        compiler_params=_params(("arbitrary",), 48),
        name="select",
    )(aff_rows)


def _gather_kernel(idx_ref, h_ref, o_ref, rows_ref):
    cap = o_ref.shape[2]

    def body(c, carry):
        t = idx_ref[0, 0, c]
        rows_ref[0, pl.ds(pl.multiple_of(c * SUB, SUB), SUB), :] = (
            h_ref[0, pl.ds(pl.multiple_of(t * SUB, SUB), SUB), :])
        return carry

    lax.fori_loop(0, cap, body, 0, unroll=8)
    o_ref[0, 0] = _load_token_tiles(rows_ref, cap).astype(BF)


def _gather(idx, hf_tiles, cap):
    b, rows, _ = hf_tiles.shape
    return pl.pallas_call(
        _gather_kernel,
        grid=(b, N_EXP),
        in_specs=[
            pl.BlockSpec((1, 1, cap), lambda bb, e: (bb * N_EXP + e, 0, 0), memory_space=pltpu.SMEM),
            pl.BlockSpec((1, rows, LANES), lambda bb, e: (bb, 0, 0)),
        ],
        out_specs=pl.BlockSpec((1, 1, cap, D), lambda bb, e: (bb, e, 0, 0)),
        out_shape=jax.ShapeDtypeStruct((b, N_EXP, cap, D), BF),
        scratch_shapes=[pltpu.VMEM((1, cap * SUB, LANES), F32)],
        compiler_params=_params(("parallel", "arbitrary"), 48),
        name="gather",
    )(idx, hf_tiles)


def _ffn_kernel(x_ref, wg_ref, wu_ref, wd_ref, o_ref, wg_s, wu_s, wd_s):
    @pl.when(pl.program_id(1) == 0)
    def _():
        wg_s[...] = wg_ref[...].astype(BF)
        wu_s[...] = wu_ref[...].astype(BF)
        wd_s[...] = wd_ref[...].astype(BF)

    x = x_ref[0, 0]
    dexp = wg_s.shape[1]
    step = 256
    y = None
    for c0 in range(0, dexp, step):
        g = jnp.dot(x, wg_s[:, c0:c0 + step], preferred_element_type=F32)
        u = jnp.dot(x, wu_s[:, c0:c0 + step], preferred_element_type=F32)
        act = (g * (1.0 / (1.0 + jnp.exp(-g))) * u).astype(BF)
        part = jnp.dot(act, wd_s[c0:c0 + step, :], preferred_element_type=F32)
        y = part if y is None else y + part
    _store_token_tiles(o_ref.at[0], y)


def _ffn(xs, w_gate, w_up, w_down, layer):
    b, _, cap, _ = xs.shape
    dexp = w_gate.shape[-1]
    xspec = pl.BlockSpec((1, 1, cap, D), lambda e, bb: (bb, e, 0, 0))
    yspec = pl.BlockSpec((1, 1, cap * SUB, LANES), lambda e, bb: (bb, e, 0, 0))

    def wspec(shape, switch_at):
        def index(e, bb):
            return (layer, jnp.minimum(e + jnp.where(bb >= switch_at, 1, 0), N_EXP - 1), 0, 0)
        return pl.BlockSpec((None, None) + shape, index)

    assert b >= 2
    stagger = [min(j, b - 1) for j in (1, 2, 3)]
    return pl.pallas_call(
        _ffn_kernel,
        grid=(N_EXP, b),
        in_specs=[xspec, wspec((D, dexp), stagger[0]), wspec((D, dexp), stagger[1]),
                  wspec((dexp, D), stagger[2])],
        out_specs=yspec,
        out_shape=jax.ShapeDtypeStruct((b, N_EXP, cap * SUB, LANES), F32),
        scratch_shapes=[pltpu.VMEM((D, dexp), BF), pltpu.VMEM((D, dexp), BF), pltpu.VMEM((dexp, D), BF)],
        compiler_params=_params(("arbitrary", "arbitrary"), 56),
        name="expert_ffn",
    )(xs, w_gate, w_up, w_down)


def _scatter_kernel(idx_ref, gate_ref, y_ref, x_ref, gf_ref, o_ref):
    cap = idx_ref.shape[2]

    @pl.when(pl.program_id(1) == 0)
    def _():
        o_ref[...] = x_ref[...]

    gf = gf_ref[0]

    group = 16

    def body(g, carry):
        c0 = g * group
        ts = [pl.multiple_of(idx_ref[0, 0, c0 + j] * SUB, SUB) for j in range(group)]
        new = []
        for j in range(group):
            y = y_ref[0, 0, pl.ds(pl.multiple_of((c0 + j) * SUB, SUB), SUB), :]
            new.append(o_ref[0, pl.ds(ts[j], SUB), :] + (gf * gate_ref[0, 0, c0 + j]) * y)
        for j in range(group):
            o_ref[0, pl.ds(ts[j], SUB), :] = new[j]
        return carry

    lax.fori_loop(0, cap // group, body, 0)


def _scatter(idx, gate, y_tiles, x_tiles, gf_tile):
    b, rows, _ = x_tiles.shape
    cap = idx.shape[2]
    sm = lambda: pl.BlockSpec((1, 1, cap), lambda bb, e: (bb * N_EXP + e, 0, 0), memory_space=pltpu.SMEM)
    resident = lambda **kw: pl.BlockSpec((1, rows, LANES), lambda bb, e: (bb, 0, 0), **kw)
    return pl.pallas_call(
        _scatter_kernel,
        grid=(b, N_EXP),
        in_specs=[
            sm(), sm(),
            pl.BlockSpec((1, 1, cap * SUB, LANES), lambda bb, e: (bb, e, 0, 0)),
            resident(pipeline_mode=pl.Buffered(1)),
            pl.BlockSpec((1, SUB, LANES), lambda bb, e: (bb, 0, 0)),
        ],
        out_specs=resident(),
        out_shape=jax.ShapeDtypeStruct((b, rows, LANES), F32),
        compiler_params=_params(("parallel", "arbitrary"), 56),
        name="scatter_add",
    )(idx, gate, y_tiles, x_tiles, gf_tile)


def _moe(x1_tiles, hf_tiles, aff_t, gf, w_gate, w_up, w_down, layer):
    b, _, n = aff_t.shape
    cap = CAP_FACTOR * n // N_EXP
    chunks = n // LANES
    idx, gate = _select(aff_t.reshape(b * N_EXP * chunks, LANES), cap, chunks)
    idx = idx.reshape(b * N_EXP, 1, cap)
    gate = gate.reshape(b * N_EXP, 1, cap)
    xs = _gather(idx, hf_tiles, cap)
    y_tiles = _ffn(xs, w_gate, w_up, w_down, layer)
    return _scatter(idx, gate, y_tiles, x1_tiles, gf.reshape(b, SUB, LANES))


def _to_rows_kernel(x_ref, o_ref):
    o_ref[0] = _load_token_tiles(x_ref, o_ref.shape[1])


def _to_rows(x_tiles):
    b = x_tiles.shape[0]
    n = x_tiles.shape[1] // SUB
    tm = 512
    return pl.pallas_call(
        _to_rows_kernel,
        grid=(b, n // tm),
        in_specs=[_tile_spec(tm)],
        out_specs=pl.BlockSpec((1, tm, D), lambda bb, i: (bb, i, 0)),
        out_shape=jax.ShapeDtypeStruct((b, n, D), F32),
        compiler_params=_params(("parallel", "arbitrary"), 32),
        name="to_rows",
    )(x_tiles)


def _rope_tables(n_lat, n_ctx):
    rows = n_lat // GRID_W
    r = jnp.repeat(jnp.arange(rows, dtype=F32), GRID_W)
    col = jnp.tile(jnp.arange(GRID_W, dtype=F32), rows)
    n_freq = HEAD_DIM // 4
    inv = ROPE_BASE ** (-jnp.arange(n_freq, dtype=F32) / n_freq)
    ar = r[:, None] * inv
    ac = col[:, None] * inv
    ang = jnp.concatenate([ar, ar, ac, ac], axis=-1)
    sign = jnp.where((jnp.arange(HEAD_DIM) % 32) < 16, -1.0, 1.0).astype(F32)
    cos = jnp.concatenate([jnp.cos(ang), jnp.ones((n_ctx, HEAD_DIM), F32)], axis=0)
    sin_s = jnp.concatenate([jnp.sin(ang) * sign, jnp.zeros((n_ctx, HEAD_DIM), F32)], axis=0)
    return jnp.tile(cos, (1, 2)), jnp.tile(sin_s, (1, 2))


def _dft_tables(n_lat):
    c = jnp.arange(FOUR_G, dtype=I32)
    ang_c = (2.0 * math.pi / FOUR_G) * ((c[:, None] * c[None, :]) % FOUR_G).astype(F32)
    eye = jnp.eye(FOUR_W // FOUR_G, dtype=F32)
    bd = jnp.concatenate([jnp.kron(eye, jnp.cos(ang_c)), jnp.kron(eye, jnp.sin(ang_c))], axis=1)
    bd = (bd * FOUR_G ** -0.5).astype(BF)
    tr = 256
    n = jnp.arange(n_lat, dtype=I32)[None, :]
    r = jnp.arange(tr, dtype=I32)[:, None]
    i = jnp.arange(n_lat // tr, dtype=I32)[:, None]
    row_ang = (2.0 * math.pi / n_lat) * ((r * n) % n_lat).astype(F32)
    tile_ang = (2.0 * math.pi / n_lat) * ((i * tr * n) % n_lat).astype(F32)
    scale = n_lat ** -0.5
    tables = (jnp.cos(row_ang) * scale, jnp.sin(row_ang) * scale,
              jnp.cos(tile_ang)[:, None, :], jnp.sin(tile_ang)[:, None, :])
    return bd, tables


def kernel(x, c, ctx, c_ctx, ada_w, ada_b, norm_mix, norm_ffn, attn_w_in, attn_q_norm, attn_k_norm,
           lam_q1, lam_k1, lam_q2, lam_k2, attn_subln, attn_w_out, conv_w_in, conv_w, conv_w_out,
           router_w, moe_w_gate, moe_w_up, moe_w_down):
    b, n, _ = x.shape
    n_ctx = ctx.shape[1]
    assert x.shape[2] == D and n % 512 == 0 and n_ctx % 256 == 0

    cond8 = jnp.concatenate([c, c_ctx[None, :], jnp.zeros((8 - b - 1, D), F32)], axis=0)
    ada = _ada(cond8, ada_w, ada_b)

    def mods(layer):
        m = ada[layer].reshape(8, 6, D)
        return [m[:, j] for j in range(6)]

    vec = lambda t: t[:b].reshape(b, 1, D)

    sh_m, sc_m, g_m, sh_f, sc_f, g_f = mods(0)
    both = lambda t: jnp.stack([t[:b], jnp.broadcast_to(t[b], (b, D))], axis=1).reshape(b, 2, 1, D)
    cos, sin_s = _rope_tables(n, n_ctx)
    bd, dft_tabs = _dft_tables(n)
    tile2 = lambda t: jnp.tile(t.reshape(1, HEAD_DIM), (1, 2))
    q, k, v, f = _proj0(x, ctx, norm_mix[0].reshape(1, D), both(sh_m), both(sc_m),
                        attn_w_in[0].astype(BF), cos, sin_s,
                        tile2(attn_q_norm[0]), tile2(attn_k_norm[0]))
    lam_init = 0.8 - 0.6 * math.exp(-0.3 * 0)
    lamv = jnp.zeros((8, LANES), F32).at[:4, :HEAD_DIM].set(
        jnp.stack([lam_q1[0], lam_k1[0], lam_q2[0], lam_k2[0]]))
    score_bound = (1.01 * HEAD_DIM * Q_SCALE * jnp.max(jnp.abs(attn_q_norm[0]))
                   * jnp.max(jnp.abs(attn_k_norm[0])) + 0.1).reshape(1)
    att = _attention(score_bound, q, k, v, lamv, attn_subln[0].reshape(V_DIM, 1), n, lam_init)
    four = _fourier(f, bd, dft_tabs, n)
    x1, hf, aff_t = _merge0(att, four, attn_w_out[0].astype(BF), x, vec(g_m),
                            norm_ffn[0].reshape(1, D), vec(sh_f), vec(sc_f), router_w[0].T)
    x2 = _moe(x1, hf, aff_t, vec(g_f), moe_w_gate, moe_w_up, moe_w_down, 0)

    sh_m, sc_m, g_m, sh_f, sc_f, g_f = mods(1)
    bg, z = _proj1(x2, norm_mix[1].reshape(1, D), vec(sh_m), vec(sc_m), conv_w_in[0].astype(BF))
    conv_w8 = jnp.zeros((8, D), F32).at[:3].set(conv_w[0])
    x3, hf, aff_t = _merge1(bg, z, conv_w8, conv_w_out[0].astype(BF), x2, vec(g_m),
                            norm_ffn[1].reshape(1, D), vec(sh_f), vec(sc_f), router_w[1].T)
    return _to_rows(_moe(x3, hf, aff_t, vec(g_f), moe_w_gate, moe_w_up, moe_w_down, 1))
```

```python
import functools
import math

import jax
import jax.numpy as jnp
from jax import lax
from jax.experimental import pallas as pl
from jax.experimental.pallas import tpu as pltpu

BF = jnp.bfloat16
F32 = jnp.float32
I32 = jnp.int32

D = 1024
GRID_W = 64
N_HEADS = 6
HEAD_DIM = 64
V_DIM = 2 * HEAD_DIM
QK_W = N_HEADS * 2 * HEAD_DIM
V_W = N_HEADS * V_DIM
FOUR_W = 256
FOUR_G = 64
IN_W = 2 * QK_W + V_W + FOUR_W
N_EXP = 16
CAP_FACTOR = 2
ROPE_BASE = 10000.0
EPS = 1e-6
Q_SCALE = HEAD_DIM ** -0.5 * math.log2(math.e)
ATTN_SHIFT_LIMIT = 40.0
LANES = 128
MIB = 1024 * 1024

NT = (((1,), (1,)), ((), ()))


def _params(sem, vmem_mib):
    return pltpu.CompilerParams(dimension_semantics=sem, vmem_limit_bytes=vmem_mib * MIB)


def _split2(x):
    hi = x.astype(BF)
    lo = (x - hi.astype(F32)).astype(BF)
    return hi, lo


def _dot3(a, b, dims=(((1,), (0,)), ((), ()))):
    ah, al = _split2(a)
    bh, bl = _split2(b)
    dg = functools.partial(lax.dot_general, dimension_numbers=dims, preferred_element_type=F32)
    return dg(ah, bh) + dg(ah, bl) + dg(al, bh)


def _modulate(x, nw, shift, scale):
    ms = jnp.mean(x * x, axis=-1, keepdims=True)
    return (x * lax.rsqrt(ms + EPS) * nw) * (1.0 + scale) + shift


def _ada_kernel(c_ref, w_ref, b_ref, o_ref):
    cv = c_ref[...]
    s = cv * (1.0 / (1.0 + jnp.exp(-cv)))
    o_ref[0] = _dot3(s, w_ref[0]) + b_ref[0]


def _ada(cond8, ada_w, ada_b):
    depth = ada_w.shape[0]
    tn = 1536
    return pl.pallas_call(
        _ada_kernel,
        grid=(depth, 6 * D // tn),
        in_specs=[
            pl.BlockSpec((8, D), lambda l, j: (0, 0)),
            pl.BlockSpec((1, D, tn), lambda l, j: (l, 0, j)),
            pl.BlockSpec((1, 1, tn), lambda l, j: (l, 0, j)),
        ],
        out_specs=pl.BlockSpec((1, 8, tn), lambda l, j: (l, 0, j)),
        out_shape=jax.ShapeDtypeStruct((depth, 8, 6 * D), F32),
        compiler_params=_params(("arbitrary", "arbitrary"), 40),
        name="ada",
    )(cond8, ada_w, ada_b.reshape(depth, 1, 6 * D))


def _norm_rope(xb, wn, cos, sin_s, out_scale):
    lane = lax.broadcasted_iota(I32, xb.shape, 1)
    lo = lane < HEAD_DIM
    ss = xb * xb
    s_lo = jnp.sum(jnp.where(lo, ss, 0.0), axis=-1, keepdims=True)
    s_hi = jnp.sum(jnp.where(lo, 0.0, ss), axis=-1, keepdims=True)
    inv = jnp.where(lo, lax.rsqrt(s_lo * (1.0 / HEAD_DIM) + EPS),
                    lax.rsqrt(s_hi * (1.0 / HEAD_DIM) + EPS))
    y = xb * inv * wn
    fwd = pltpu.roll(y, LANES - 16, axis=1)
    bwd = pltpu.roll(y, 16, axis=1)
    rot = jnp.where((lane & 31) < 16, fwd, bwd)
    out = y * cos + rot * sin_s
    if out_scale != 1.0:
        out = out * out_scale
    return out


def _proj0_kernel(x_ref, ctx_ref, nw_ref, sh_ref, sc_ref, w_ref, cos_ref, sin_ref, qn_ref, kn_ref,
                  q_ref, k_ref, v_ref, f_ref, *, lat_tiles):
    x = jnp.where(pl.program_id(1) < lat_tiles, x_ref[0], ctx_ref[0])
    h = _modulate(x, nw_ref[...], sh_ref[0, 0], sc_ref[0, 0]).astype(BF)
    cos = cos_ref[...]
    sin_s = sin_ref[...]
    qn = qn_ref[...]
    kn = kn_ref[...]
    q_scale = Q_SCALE
    pq = jnp.dot(h, w_ref[:, :QK_W], preferred_element_type=F32)
    pk = jnp.dot(h, w_ref[:, QK_W:2 * QK_W], preferred_element_type=F32)
    for hh in range(N_HEADS):
        c0 = hh * V_DIM
        q_ref[0, :, c0:c0 + V_DIM] = _norm_rope(pq[:, c0:c0 + V_DIM], qn, cos, sin_s, q_scale).astype(BF)
        k_ref[0, :, c0:c0 + V_DIM] = _norm_rope(pk[:, c0:c0 + V_DIM], kn, cos, sin_s, 1.0).astype(BF)
    v_ref[0] = jnp.dot(h, w_ref[:, 2 * QK_W:2 * QK_W + V_W], preferred_element_type=F32).astype(BF)
    f_ref[0] = jnp.dot(h, w_ref[:, 2 * QK_W + V_W:], preferred_element_type=F32).astype(BF)


def _proj0(x, ctx, nw, sh2, sc2, w_bf, cos, sin_s, qn, kn):
    b, n_lat, _ = x.shape
    n_ctx = ctx.shape[1]
    s_tot = n_lat + n_ctx
    tm = 256
    lat_tiles = n_lat // tm
    mod_spec = pl.BlockSpec((1, 1, 1, D), lambda bb, i: (bb, i // lat_tiles, 0, 0))
    full = lambda shape: pl.BlockSpec(shape, lambda bb, i: tuple(0 for _ in shape))
    row = lambda w: pl.BlockSpec((1, tm, w), lambda bb, i: (bb, i, 0))
    return pl.pallas_call(
        functools.partial(_proj0_kernel, lat_tiles=lat_tiles),
        grid=(b, s_tot // tm),
        in_specs=[
            pl.BlockSpec((1, tm, D), lambda bb, i: (bb, jnp.minimum(i, lat_tiles - 1), 0)),
            pl.BlockSpec((1, tm, D), lambda bb, i: (bb, jnp.maximum(i - lat_tiles, 0), 0)),
            full((1, D)), mod_spec, mod_spec, full((D, IN_W)),
            pl.BlockSpec((tm, LANES), lambda bb, i: (i, 0)),
            pl.BlockSpec((tm, LANES), lambda bb, i: (i, 0)),
            full((1, LANES)), full((1, LANES)),
        ],
        out_specs=[row(QK_W), row(QK_W), row(V_W), row(FOUR_W)],
        out_shape=[
            jax.ShapeDtypeStruct((b, s_tot, QK_W), BF),
            jax.ShapeDtypeStruct((b, s_tot, QK_W), BF),
            jax.ShapeDtypeStruct((b, s_tot, V_W), BF),
            jax.ShapeDtypeStruct((b, s_tot, FOUR_W), BF),
        ],
        compiler_params=_params(("parallel", "arbitrary"), 48),
        name="proj0",
    )(x, ctx, nw, sh2, sc2, w_bf, cos, sin_s, qn, kn)


def _attn_kernel(bound_ref, q_ref, k_ref, v_ref, lam_ref, sub_ref, o_ref, vt_ref, m_ref, e_ref,
                 *, lam_init, kc):
    @pl.when(pl.program_id(2) == 0)
    def _():
        vt_ref[...] = v_ref[0].astype(F32).T.astype(BF)

    q = q_ref[0]
    lane = lax.broadcasted_iota(I32, q.shape, 1)
    zero = jnp.zeros_like(q)
    lv = lam_ref[...]
    t1 = jnp.sum(lv[0:1] * lv[1:2], axis=-1, keepdims=True)
    t2 = jnp.sum(lv[2:3] * lv[3:4], axis=-1, keepdims=True)
    lam = jnp.exp(t1) - jnp.exp(t2) + lam_init

    qs = (jnp.where(lane < HEAD_DIM, q, zero), jnp.where(lane < HEAD_DIM, zero, q))
    tq = q.shape[0]
    n_chunks = k_ref.shape[1] // kc

    def scores(h, c):
        return lax.dot_general(k_ref[0, c * kc:(c + 1) * kc, :], qs[h], NT, preferred_element_type=F32)

    def fold(x, op):
        return op(x.reshape(kc // 8, 8, tq), axis=0)

    bound = bound_ref[0]
    small = bound <= ATTN_SHIFT_LIMIT

    @pl.when(small)
    def _():
        m_ref[...] = jnp.zeros(m_ref.shape, F32) + bound

    @pl.when(jnp.logical_not(small))
    def _():
        for h in range(2):
            m = jnp.full((8, tq), -jnp.inf, F32)
            for c in range(n_chunks):
                m = jnp.maximum(m, fold(scores(h, c), jnp.max))
            m_ref[h] = jnp.broadcast_to(jnp.max(m, axis=0, keepdims=True), (8, tq))

    ms = [m_ref[h][0:1, :] for h in range(2)]
    ls = [jnp.zeros((8, tq), F32) for _ in range(2)]
    for c in range(n_chunks):
        for h in range(2):
            e = jnp.exp2(scores(h, c) - ms[h])
            ls[h] = ls[h] + fold(e, jnp.sum)
            e_ref[h, c] = e.astype(BF)
    l0, l1 = [jnp.sum(l, axis=0, keepdims=True) for l in ls]

    beta = (lam * l0 / l1).astype(BF)
    acc = jnp.zeros((V_DIM, tq), F32)
    for c in range(n_chunks):
        a = e_ref[0, c] - beta * e_ref[1, c]
        acc = acc + jnp.dot(vt_ref[:, c * kc:(c + 1) * kc], a, preferred_element_type=F32)
    o = acc * (1.0 / l0)
    ms = jnp.mean(o * o, axis=0, keepdims=True)
    o = o * lax.rsqrt(ms + EPS) * sub_ref[...] * (1.0 - lam_init)
    o_ref[0] = o.T.astype(BF)


def _attention(bound, q, k, v, lamv, subln, n_lat, lam_init):
    b, s_tot, _ = k.shape
    tq = 1024
    kc = 256
    assert s_tot % kc == 0
    kv_spec = pl.BlockSpec((1, s_tot, V_DIM), lambda bb, hh, i: (bb, 0, hh))
    q_spec = pl.BlockSpec((1, tq, V_DIM), lambda bb, hh, i: (bb, i, hh))
    return pl.pallas_call(
        functools.partial(_attn_kernel, lam_init=lam_init, kc=kc),
        grid=(b, N_HEADS, n_lat // tq),
        in_specs=[
            pl.BlockSpec(memory_space=pltpu.SMEM),
            q_spec, kv_spec, kv_spec,
            pl.BlockSpec((8, LANES), lambda bb, hh, i: (0, 0)),
            pl.BlockSpec((V_DIM, 1), lambda bb, hh, i: (0, 0)),
        ],
        out_specs=q_spec,
        out_shape=jax.ShapeDtypeStruct((b, n_lat, V_W), BF),
        scratch_shapes=[pltpu.VMEM((V_DIM, s_tot), BF),
                        pltpu.VMEM((2, 8, tq), F32),
                        pltpu.VMEM((2, s_tot // kc, kc, tq), BF)],
        compiler_params=_params(("parallel", "parallel", "arbitrary"), 48),
        name="diff_attn",
    )(bound, q, k, v, lamv, subln)


def _fourier_kernel(f_ref, bd_ref, cb_ref, sb_ref, ca_ref, sa_ref, o_ref, g_ref):
    nb, n_lat, _ = f_ref.shape

    @pl.when(pl.program_id(0) == 0)
    def _():
        rows = 1024
        for bb in range(nb):
            for r in range(n_lat // rows):
                fc = jnp.dot(f_ref[bb, r * rows:(r + 1) * rows, :], bd_ref[...],
                             preferred_element_type=F32)
                g_ref[r * rows:(r + 1) * rows, bb * FOUR_W:(bb + 1) * FOUR_W] = fc[:, :FOUR_W].astype(BF)
                g_ref[n_lat + r * rows:n_lat + (r + 1) * rows,
                      bb * FOUR_W:(bb + 1) * FOUR_W] = fc[:, FOUR_W:].astype(BF)

    ca = ca_ref[0]
    sa = sa_ref[0]
    cb = cb_ref[...]
    sb = sb_ref[...]
    w_cos = (cb * ca - sb * sa).astype(BF)
    w_sin = (sb * ca + cb * sa).astype(BF)
    y = (jnp.dot(w_cos, g_ref[:n_lat, :], preferred_element_type=F32)
         - jnp.dot(w_sin, g_ref[n_lat:, :], preferred_element_type=F32))
    for bb in range(nb):
        o_ref[bb] = y[:, bb * FOUR_W:(bb + 1) * FOUR_W].astype(BF)


def _fourier(f, bd, tables, n_lat):
    b = f.shape[0]
    cos_row, sin_row, cos_tile, sin_tile = tables
    tr = cos_row.shape[0]
    row_tab = pl.BlockSpec((tr, n_lat), lambda i: (0, 0))
    tile_tab = pl.BlockSpec((1, 1, n_lat), lambda i: (i, 0, 0))
    return pl.pallas_call(
        _fourier_kernel,
        grid=(n_lat // tr,),
        in_specs=[
            pl.BlockSpec((b, n_lat, FOUR_W), lambda i: (0, 0, 0)),
            pl.BlockSpec((FOUR_W, 2 * FOUR_W), lambda i: (0, 0)),
            row_tab, row_tab, tile_tab, tile_tab,
        ],
        out_specs=pl.BlockSpec((b, tr, FOUR_W), lambda i: (0, i, 0)),
        out_shape=jax.ShapeDtypeStruct((b, n_lat, FOUR_W), BF),
        scratch_shapes=[pltpu.VMEM((2 * n_lat, b * FOUR_W), BF)],
        compiler_params=_params(("arbitrary",), 56),
        name="fourier",
    )(f, bd, cos_row, sin_row, cos_tile, sin_tile)


SUB = D // LANES


def _store_token_tiles(ref, val):
    rows = val.shape[0]
    for j in range(SUB):
        ref[0, pl.ds(j, rows, stride=SUB), :] = val[:, j * LANES:(j + 1) * LANES]


def _load_token_tiles(ref, rows):
    return jnp.concatenate([ref[0, pl.ds(j, rows, stride=SUB), :] for j in range(SUB)], axis=1)


def _residual_router(x, y, gm, nf, shf, scf, wr_t, x1_ref, hf_ref, aff_ref):
    x1 = x + gm * y
    _store_token_tiles(x1_ref, x1)
    hf = _modulate(x1, nf, shf, scf)
    _store_token_tiles(hf_ref, hf)
    logits = _dot3(wr_t, hf, NT)
    m = jnp.max(logits, axis=0, keepdims=True)
    e = jnp.exp(logits - m)
    aff_ref[0] = e / jnp.sum(e, axis=0, keepdims=True)


def _merge0_kernel(att_ref, four_ref, w_ref, x_ref, gm_ref, nf_ref, shf_ref, scf_ref, wr_ref,
                   x1_ref, hf_ref, aff_ref):
    y = jnp.dot(att_ref[0], w_ref[:V_W, :], preferred_element_type=F32)
    y = y + jnp.dot(four_ref[0], w_ref[V_W:, :], preferred_element_type=F32)
    _residual_router(x_ref[0], y, gm_ref[0], nf_ref[...], shf_ref[0], scf_ref[0], wr_ref[...],
                     x1_ref, hf_ref, aff_ref)


def _router_specs(tm):
    vec = pl.BlockSpec((1, 1, D), lambda bb, i: (bb, 0, 0))
    row = pl.BlockSpec((1, tm, D), lambda bb, i: (bb, i, 0))
    in_specs = [vec, pl.BlockSpec((1, D), lambda bb, i: (0, 0)), vec, vec,
                pl.BlockSpec((N_EXP, D), lambda bb, i: (0, 0))]
    tiles = _tile_spec(tm)
    out_specs = [tiles, tiles, pl.BlockSpec((1, N_EXP, tm), lambda bb, i: (bb, 0, i))]
    return row, in_specs, out_specs


def _tile_spec(tm):
    return pl.BlockSpec((1, tm * SUB, LANES), lambda bb, i: (bb, i, 0))


def _router_out_shape(b, n):
    return [jax.ShapeDtypeStruct((b, n * SUB, LANES), F32), jax.ShapeDtypeStruct((b, n * SUB, LANES), F32),
            jax.ShapeDtypeStruct((b, N_EXP, n), F32)]


def _merge0(att, four, w_bf, x, gm, nf, shf, scf, wr_t):
    b, n, _ = x.shape
    tm = 512
    row, r_in, r_out = _router_specs(tm)
    return pl.pallas_call(
        _merge0_kernel,
        grid=(b, n // tm),
        in_specs=[
            pl.BlockSpec((1, tm, V_W), lambda bb, i: (bb, i, 0)),
            pl.BlockSpec((1, tm, FOUR_W), lambda bb, i: (bb, i, 0)),
            pl.BlockSpec((D, D), lambda bb, i: (0, 0)),
            row,
        ] + r_in,
        out_specs=r_out,
        out_shape=_router_out_shape(b, n),
        compiler_params=_params(("parallel", "arbitrary"), 48),
        name="merge0",
    )(att, four, w_bf, x, gm, nf, shf, scf, wr_t)


def _proj1_kernel(x_ref, nw_ref, sh_ref, sc_ref, w_ref, bg_ref, z_ref):
    x = _load_token_tiles(x_ref, bg_ref.shape[1])
    h = _modulate(x, nw_ref[...], sh_ref[0], sc_ref[0]).astype(BF)
    bg_ref[0] = jnp.dot(h, w_ref[:, :D], preferred_element_type=F32)
    cg = jnp.dot(h, w_ref[:, D:2 * D], preferred_element_type=F32)
    u = jnp.dot(h, w_ref[:, 2 * D:], preferred_element_type=F32)
    z_ref[0] = cg * u


def _proj1(x_tiles, nw, sh, sc, w_bf):
    b = x_tiles.shape[0]
    n = x_tiles.shape[1] // SUB
    tm = 512
    vec = pl.BlockSpec((1, 1, D), lambda bb, i: (bb, 0, 0))
    row = pl.BlockSpec((1, tm, D), lambda bb, i: (bb, i, 0))
    return pl.pallas_call(
        _proj1_kernel,
        grid=(b, n // tm),
        in_specs=[_tile_spec(tm), pl.BlockSpec((1, D), lambda bb, i: (0, 0)), vec, vec,
                  pl.BlockSpec((D, 3 * D), lambda bb, i: (0, 0))],
        out_specs=[row, row],
        out_shape=[jax.ShapeDtypeStruct((b, n, D), F32), jax.ShapeDtypeStruct((b, n, D), F32)],
        compiler_params=_params(("parallel", "arbitrary"), 48),
        name="proj1",
    )(x_tiles, nw, sh, sc, w_bf)


def _merge1_kernel(bg_ref, z_ref, zp_ref, zn_ref, cw_ref, w_ref, x_ref, gm_ref, nf_ref, shf_ref,
                   scf_ref, wr_ref, x1_ref, hf_ref, aff_ref):
    i = pl.program_id(1)
    last = pl.num_programs(1) - 1
    z = z_ref[0]
    tm = z.shape[0]
    rowid = lax.broadcasted_iota(I32, z.shape, 0)
    prev_row = jnp.where(i > 0, zp_ref[0, 7:8, :], 0.0)
    next_row = jnp.where(i < last, zn_ref[0, 0:1, :], 0.0)
    z_up = jnp.where(rowid == 0, prev_row, pltpu.roll(z, 1, axis=0))
    z_dn = jnp.where(rowid == tm - 1, next_row, pltpu.roll(z, tm - 1, axis=0))
    cw = cw_ref[...]
    conv = cw[0:1] * z_up + cw[1:2] * z + cw[2:3] * z_dn
    y = jnp.dot((bg_ref[0] * conv).astype(BF), w_ref[...], preferred_element_type=F32)
    _residual_router(_load_token_tiles(x_ref, tm), y, gm_ref[0], nf_ref[...], shf_ref[0], scf_ref[0],
                     wr_ref[...], x1_ref, hf_ref, aff_ref)


def _merge1(bg, z, conv_w8, w_bf, x_tiles, gm, nf, shf, scf, wr_t):
    b, n, _ = bg.shape
    tm = 512
    halo = 8
    per = tm // halo
    n_halo = n // halo
    row, r_in, r_out = _router_specs(tm)
    return pl.pallas_call(
        _merge1_kernel,
        grid=(b, n // tm),
        in_specs=[
            row, row,
            pl.BlockSpec((1, halo, D), lambda bb, i: (bb, jnp.maximum(i * per - 1, 0), 0)),
            pl.BlockSpec((1, halo, D), lambda bb, i: (bb, jnp.minimum((i + 1) * per, n_halo - 1), 0)),
            pl.BlockSpec((8, D), lambda bb, i: (0, 0)),
            pl.BlockSpec((D, D), lambda bb, i: (0, 0)),
            _tile_spec(tm),
        ] + r_in,
        out_specs=r_out,
        out_shape=_router_out_shape(b, n),
        compiler_params=_params(("parallel", "arbitrary"), 48),
        name="merge1",
    )(bg, z, z, z, conv_w8, w_bf, x_tiles, gm, nf, shf, scf, wr_t)


def _select_kernel(a_ref, idx_ref, gate_ref, cl_s, off_s, inc_s, hi_s, mid_s, lo_s, *, cap, chunks):
    a = a_ref[...]
    rows = a.shape[0]
    groups = rows // chunks
    per_batch = N_EXP * chunks
    assert chunks & (chunks - 1) == 0
    shift = chunks.bit_length() - 1

    def indicator(shape, row_dim):
        r = lax.broadcasted_iota(I32, shape, row_dim)
        g = lax.broadcasted_iota(I32, shape, 1 - row_dim)
        return jnp.where((r >> shift) == g, 1.0, 0.0).astype(BF)

    member = indicator((groups, rows), 1)
    spread = indicator((rows, groups), 0)
    ri = lax.broadcasted_iota(I32, (per_batch, per_batch), 0)
    ci = lax.broadcasted_iota(I32, (per_batch, per_batch), 1)
    same_f = jnp.where((ri >> shift) == (ci >> shift), 1.0, 0.0)
    same = same_f.astype(BF)
    lower = (same_f * jnp.where(ci < ri, 1.0, 0.0)).astype(BF)
    li = lax.broadcasted_iota(I32, (LANES, LANES), 0)
    lj = lax.broadcasted_iota(I32, (LANES, LANES), 1)
    incl = jnp.where(li <= lj, 1.0, 0.0).astype(BF)

    def bcast(col):
        return jnp.broadcast_to(col, (per_batch, LANES))

    def prefix(maskf):
        cl = jnp.dot(maskf.astype(BF), incl, preferred_element_type=F32)
        tot = bcast(cl[:, LANES - 1:LANES])
        off = jnp.dot(lower, tot.astype(BF), preferred_element_type=F32)
        return cl, off, tot

    min_normal = 0x00800000

    def search(step, t):
        cand = t | jnp.left_shift(jnp.int32(1), 30 - step)
        mask = jnp.where(a >= lax.bitcast_convert_type(cand, F32), 1.0, 0.0).astype(BF)
        part = jnp.dot(member, mask, preferred_element_type=F32)
        cnt = jnp.sum(part, axis=-1, keepdims=True)
        ok = jnp.broadcast_to(jnp.where(cnt >= cap, 1.0, 0.0), (groups, LANES)).astype(BF)
        ok_rows = jnp.dot(spread, ok, preferred_element_type=F32)
        return jnp.where(ok_rows > 0.5, jnp.where(cand >= min_normal, cand, t), t)

    thr = lax.fori_loop(0, 31, search, jnp.zeros((rows, LANES), I32))
    thr_f = lax.bitcast_convert_type(thr, F32)
    nxt_f = lax.bitcast_convert_type(jnp.maximum(thr + 1, min_normal), F32)
    gtf = jnp.where(a >= nxt_f, 1.0, 0.0)
    eqf = jnp.where(a >= thr_f, 1.0, 0.0) - gtf
    for bb in range(rows // per_batch):
        sl = slice(bb * per_batch, (bb + 1) * per_batch)
        gt_b, eq_b = gtf[sl], eqf[sl]
        n_gt = jnp.dot(same, bcast(jnp.sum(gt_b, axis=-1, keepdims=True)).astype(BF),
                       preferred_element_type=F32)
        cl_eq, off_eq, _ = prefix(eq_b)
        sel = gt_b + eq_b * jnp.where(cl_eq + off_eq <= cap - n_gt, 1.0, 0.0)
        cl, off, tot = prefix(sel)
        cl_s[sl, :] = cl.astype(BF)
        off_s[sl, :] = off
        inc_s[sl, :] = off + tot
        a_b = a[sl]
        a_hi = a_b.astype(BF)
        r1 = a_b - a_hi.astype(F32)
        a_mid = r1.astype(BF)
        hi_s[sl, :] = a_hi
        mid_s[sl, :] = a_mid
        lo_s[sl, :] = (r1 - a_mid.astype(F32)).astype(BF)

    slot = lax.broadcasted_iota(I32, (chunks, cap), 1).astype(F32)
    chunk_id = lax.broadcasted_iota(I32, (chunks, cap), 0).astype(F32)
    lane_id = lax.broadcasted_iota(I32, (LANES, cap), 0).astype(F32)
    tn = (((0,), (0,)), ((), ()))
    reps = cap // LANES

    def widen(x):
        return jnp.concatenate([x] * reps, axis=1)

    def per_group(g, carry):
        win = pl.ds(pl.multiple_of(g * chunks, chunks), chunks)
        inc_g = widen(inc_s[win, :])
        off_g = widen(off_s[win, :])
        chunk_of = jnp.sum(jnp.where(inc_g <= slot, 1.0, 0.0), axis=0, keepdims=True)
        pick_f = jnp.where(chunk_id == chunk_of, 1.0, 0.0)
        before = jnp.sum(pick_f * off_g, axis=0, keepdims=True)
        pick = pick_f.astype(BF)
        counts = lax.dot_general(cl_s[win, :], pick, tn, preferred_element_type=F32)
        lane_of = jnp.sum(jnp.where(counts <= slot[0:1] - before, 1.0, 0.0), axis=0, keepdims=True)
        aff = (lax.dot_general(hi_s[win, :], pick, tn, preferred_element_type=F32)
               + lax.dot_general(mid_s[win, :], pick, tn, preferred_element_type=F32)
               + lax.dot_general(lo_s[win, :], pick, tn, preferred_element_type=F32))
        gate = jnp.sum(jnp.where(lane_id == lane_of, aff, 0.0), axis=0, keepdims=True)
        idx_ref[pl.ds(g, 1), :] = ((chunk_of * float(LANES) + lane_of) * float(SUB)).astype(I32)
        gate_ref[pl.ds(g, 1), :] = gate
        return carry

    lax.fori_loop(0, groups, per_group, 0, unroll=2)


def _select(aff_rows, cap, chunks):
    rows = aff_rows.shape[0]
    groups = rows // chunks
    whole = lambda shape: pl.BlockSpec(shape, lambda i: (0, 0))
    return pl.pallas_call(
        functools.partial(_select_kernel, cap=cap, chunks=chunks),
        grid=(1,),
        in_specs=[whole((rows, LANES))],
        out_specs=[whole((groups, cap)), whole((groups, cap))],
        out_shape=[jax.ShapeDtypeStruct((groups, cap), I32), jax.ShapeDtypeStruct((groups, cap), F32)],
        scratch_shapes=[pltpu.VMEM((rows, LANES), BF), pltpu.VMEM((rows, LANES), F32),
                        pltpu.VMEM((rows, LANES), F32), pltpu.VMEM((rows, LANES), BF),
                        pltpu.VMEM((rows, LANES), BF), pltpu.VMEM((rows, LANES), BF)],
        compiler_params=_params(("arbitrary",), 48),
        name="select",
    )(aff_rows)


def _gather_kernel(idx_ref, h_ref, o_ref, rows_ref):
    cap = o_ref.shape[2]

    def body(c, carry):
        t = pl.multiple_of(idx_ref[0, 0, c], SUB)
        rows_ref[0, pl.ds(pl.multiple_of(c * SUB, SUB), SUB), :] = h_ref[0, pl.ds(t, SUB), :]
        return carry

    lax.fori_loop(0, cap, body, 0, unroll=8)
    o_ref[0, 0] = _load_token_tiles(rows_ref, cap).astype(BF)


def _gather(idx, hf_tiles, cap):
    b, rows, _ = hf_tiles.shape
    return pl.pallas_call(
        _gather_kernel,
        grid=(b, N_EXP),
        in_specs=[
            pl.BlockSpec((1, 1, cap), lambda bb, e: (bb * N_EXP + e, 0, 0), memory_space=pltpu.SMEM),
            pl.BlockSpec((1, rows, LANES), lambda bb, e: (bb, 0, 0)),
        ],
        out_specs=pl.BlockSpec((1, 1, cap, D), lambda bb, e: (bb, e, 0, 0)),
        out_shape=jax.ShapeDtypeStruct((b, N_EXP, cap, D), BF),
        scratch_shapes=[pltpu.VMEM((1, cap * SUB, LANES), F32)],
        compiler_params=_params(("parallel", "arbitrary"), 48),
        name="gather",
    )(idx, hf_tiles)


def _ffn_kernel(x_ref, wg_ref, wu_ref, wd_ref, o_ref, wg_s, wu_s, wd_s):
    @pl.when(pl.program_id(1) == 0)
    def _():
        wg_s[...] = wg_ref[...].astype(BF)
        wu_s[...] = wu_ref[...].astype(BF)
        wd_s[...] = wd_ref[...].astype(BF)

    x = x_ref[0, 0]
    dexp = wg_s.shape[1]
    step = 256
    y = None
    for c0 in range(0, dexp, step):
        g = jnp.dot(x, wg_s[:, c0:c0 + step], preferred_element_type=F32)
        u = jnp.dot(x, wu_s[:, c0:c0 + step], preferred_element_type=F32)
        act = (g * (1.0 / (1.0 + jnp.exp(-g))) * u).astype(BF)
        part = jnp.dot(act, wd_s[c0:c0 + step, :], preferred_element_type=F32)
        y = part if y is None else y + part
    _store_token_tiles(o_ref.at[0], y)


def _ffn(xs, w_gate, w_up, w_down, layer):
    b, _, cap, _ = xs.shape
    dexp = w_gate.shape[-1]
    xspec = pl.BlockSpec((1, 1, cap, D), lambda e, bb: (bb, e, 0, 0))
    yspec = pl.BlockSpec((1, 1, cap * SUB, LANES), lambda e, bb: (bb, e, 0, 0))

    def wspec(shape, switch_at):
        def index(e, bb):
            return (layer, jnp.minimum(e + jnp.where(bb >= switch_at, 1, 0), N_EXP - 1), 0, 0)
        return pl.BlockSpec((None, None) + shape, index)

    assert b >= 2
    stagger = [min(j, b - 1) for j in (1, 2, 3)]
    return pl.pallas_call(
        _ffn_kernel,
        grid=(N_EXP, b),
        in_specs=[xspec, wspec((D, dexp), stagger[0]), wspec((D, dexp), stagger[1]),
                  wspec((dexp, D), stagger[2])],
        out_specs=yspec,
        out_shape=jax.ShapeDtypeStruct((b, N_EXP, cap * SUB, LANES), F32),
        scratch_shapes=[pltpu.VMEM((D, dexp), BF), pltpu.VMEM((D, dexp), BF), pltpu.VMEM((dexp, D), BF)],
        compiler_params=_params(("arbitrary", "arbitrary"), 56),
        name="expert_ffn",
    )(xs, w_gate, w_up, w_down)


def _scatter_kernel(idx_ref, gate_ref, y_ref, x_ref, gf_ref, o_ref):
    cap = idx_ref.shape[2]

    @pl.when(pl.program_id(1) == 0)
    def _():
        o_ref[...] = x_ref[...]

    gf = gf_ref[0]

    group = 16

    def body(g, carry):
        c0 = g * group
        ts = [pl.multiple_of(idx_ref[0, 0, c0 + j], SUB) for j in range(group)]
        new = []
        for j in range(group):
            y = y_ref[0, 0, pl.ds(pl.multiple_of((c0 + j) * SUB, SUB), SUB), :]
            new.append(o_ref[0, pl.ds(ts[j], SUB), :] + (gf * gate_ref[0, 0, c0 + j]) * y)
        for j in range(group):
            o_ref[0, pl.ds(ts[j], SUB), :] = new[j]
        return carry

    lax.fori_loop(0, cap // group, body, 0)


def _scatter(idx, gate, y_tiles, x_tiles, gf_tile):
    b, rows, _ = x_tiles.shape
    cap = idx.shape[2]
    sm = lambda: pl.BlockSpec((1, 1, cap), lambda bb, e: (bb * N_EXP + e, 0, 0), memory_space=pltpu.SMEM)
    resident = lambda **kw: pl.BlockSpec((1, rows, LANES), lambda bb, e: (bb, 0, 0), **kw)
    return pl.pallas_call(
        _scatter_kernel,
        grid=(b, N_EXP),
        in_specs=[
            sm(), sm(),
            pl.BlockSpec((1, 1, cap * SUB, LANES), lambda bb, e: (bb, e, 0, 0)),
            resident(pipeline_mode=pl.Buffered(1)),
            pl.BlockSpec((1, SUB, LANES), lambda bb, e: (bb, 0, 0)),
        ],
        out_specs=resident(),
        out_shape=jax.ShapeDtypeStruct((b, rows, LANES), F32),
        compiler_params=_params(("parallel", "arbitrary"), 56),
        name="scatter_add",
    )(idx, gate, y_tiles, x_tiles, gf_tile)


def _moe(x1_tiles, hf_tiles, aff_t, gf, w_gate, w_up, w_down, layer):
    b, _, n = aff_t.shape
    cap = CAP_FACTOR * n // N_EXP
    chunks = n // LANES
    idx, gate = _select(aff_t.reshape(b * N_EXP * chunks, LANES), cap, chunks)
    idx = idx.reshape(b * N_EXP, 1, cap)
    gate = gate.reshape(b * N_EXP, 1, cap)
    xs = _gather(idx, hf_tiles, cap)
    y_tiles = _ffn(xs, w_gate, w_up, w_down, layer)
    return _scatter(idx, gate, y_tiles, x1_tiles, gf.reshape(b, SUB, LANES))


def _to_rows_kernel(x_ref, o_ref):
    o_ref[0] = _load_token_tiles(x_ref, o_ref.shape[1])


def _to_rows(x_tiles):
    b = x_tiles.shape[0]
    n = x_tiles.shape[1] // SUB
    tm = 512
    return pl.pallas_call(
        _to_rows_kernel,
        grid=(b, n // tm),
        in_specs=[_tile_spec(tm)],
        out_specs=pl.BlockSpec((1, tm, D), lambda bb, i: (bb, i, 0)),
        out_shape=jax.ShapeDtypeStruct((b, n, D), F32),
        compiler_params=_params(("parallel", "arbitrary"), 32),
        name="to_rows",
    )(x_tiles)


def _rope_tables(n_lat, n_ctx):
    rows = n_lat // GRID_W
    r = jnp.repeat(jnp.arange(rows, dtype=F32), GRID_W)
    col = jnp.tile(jnp.arange(GRID_W, dtype=F32), rows)
    n_freq = HEAD_DIM // 4
    inv = ROPE_BASE ** (-jnp.arange(n_freq, dtype=F32) / n_freq)
    ar = r[:, None] * inv
    ac = col[:, None] * inv
    ang = jnp.concatenate([ar, ar, ac, ac], axis=-1)
    sign = jnp.where((jnp.arange(HEAD_DIM) % 32) < 16, -1.0, 1.0).astype(F32)
    cos = jnp.concatenate([jnp.cos(ang), jnp.ones((n_ctx, HEAD_DIM), F32)], axis=0)
    sin_s = jnp.concatenate([jnp.sin(ang) * sign, jnp.zeros((n_ctx, HEAD_DIM), F32)], axis=0)
    return jnp.tile(cos, (1, 2)), jnp.tile(sin_s, (1, 2))


def _dft_tables(n_lat):
    c = jnp.arange(FOUR_G, dtype=I32)
    ang_c = (2.0 * math.pi / FOUR_G) * ((c[:, None] * c[None, :]) % FOUR_G).astype(F32)
    eye = jnp.eye(FOUR_W // FOUR_G, dtype=F32)
    bd = jnp.concatenate([jnp.kron(eye, jnp.cos(ang_c)), jnp.kron(eye, jnp.sin(ang_c))], axis=1)
    bd = (bd * FOUR_G ** -0.5).astype(BF)
    tr = 256
    n = jnp.arange(n_lat, dtype=I32)[None, :]
    r = jnp.arange(tr, dtype=I32)[:, None]
    i = jnp.arange(n_lat // tr, dtype=I32)[:, None]
    row_ang = (2.0 * math.pi / n_lat) * ((r * n) % n_lat).astype(F32)
    tile_ang = (2.0 * math.pi / n_lat) * ((i * tr * n) % n_lat).astype(F32)
    scale = n_lat ** -0.5
    tables = (jnp.cos(row_ang) * scale, jnp.sin(row_ang) * scale,
              jnp.cos(tile_ang)[:, None, :], jnp.sin(tile_ang)[:, None, :])
    return bd, tables


def kernel(x, c, ctx, c_ctx, ada_w, ada_b, norm_mix, norm_ffn, attn_w_in, attn_q_norm, attn_k_norm,
           lam_q1, lam_k1, lam_q2, lam_k2, attn_subln, attn_w_out, conv_w_in, conv_w, conv_w_out,
           router_w, moe_w_gate, moe_w_up, moe_w_down):
    b, n, _ = x.shape
    n_ctx = ctx.shape[1]
    assert x.shape[2] == D and n % 512 == 0 and n_ctx % 256 == 0

    cond8 = jnp.concatenate([c, c_ctx[None, :], jnp.zeros((8 - b - 1, D), F32)], axis=0)
    ada = _ada(cond8, ada_w, ada_b)

    def mods(layer):
        m = ada[layer].reshape(8, 6, D)
        return [m[:, j] for j in range(6)]

    vec = lambda t: t[:b].reshape(b, 1, D)

    sh_m, sc_m, g_m, sh_f, sc_f, g_f = mods(0)
    both = lambda t: jnp.stack([t[:b], jnp.broadcast_to(t[b], (b, D))], axis=1).reshape(b, 2, 1, D)
    cos, sin_s = _rope_tables(n, n_ctx)
    bd, dft_tabs = _dft_tables(n)
    tile2 = lambda t: jnp.tile(t.reshape(1, HEAD_DIM), (1, 2))
    q, k, v, f = _proj0(x, ctx, norm_mix[0].reshape(1, D), both(sh_m), both(sc_m),
                        attn_w_in[0].astype(BF), cos, sin_s,
                        tile2(attn_q_norm[0]), tile2(attn_k_norm[0]))
    lam_init = 0.8 - 0.6 * math.exp(-0.3 * 0)
    lamv = jnp.zeros((8, LANES), F32).at[:4, :HEAD_DIM].set(
        jnp.stack([lam_q1[0], lam_k1[0], lam_q2[0], lam_k2[0]]))
    score_bound = (1.01 * HEAD_DIM * Q_SCALE * jnp.max(jnp.abs(attn_q_norm[0]))
                   * jnp.max(jnp.abs(attn_k_norm[0])) + 0.1).reshape(1)
    att = _attention(score_bound, q, k, v, lamv, attn_subln[0].reshape(V_DIM, 1), n, lam_init)
    four = _fourier(f, bd, dft_tabs, n)
    x1, hf, aff_t = _merge0(att, four, attn_w_out[0].astype(BF), x, vec(g_m),
                            norm_ffn[0].reshape(1, D), vec(sh_f), vec(sc_f), router_w[0].T)
    x2 = _moe(x1, hf, aff_t, vec(g_f), moe_w_gate, moe_w_up, moe_w_down, 0)

    sh_m, sc_m, g_m, sh_f, sc_f, g_f = mods(1)
    bg, z = _proj1(x2, norm_mix[1].reshape(1, D), vec(sh_m), vec(sc_m), conv_w_in[0].astype(BF))
    conv_w8 = jnp.zeros((8, D), F32).at[:3].set(conv_w[0])
    x3, hf, aff_t = _merge1(bg, z, conv_w8, conv_w_out[0].astype(BF), x2, vec(g_m),
                            norm_ffn[1].reshape(1, D), vec(sh_f), vec(sc_f), router_w[1].T)
    return _to_rows(_moe(x3, hf, aff_t, vec(g_f), moe_w_gate, moe_w_up, moe_w_down, 1))
```

```python
import functools
import math

import jax
import jax.numpy as jnp
from jax import lax
from jax.experimental import pallas as pl
from jax.experimental.pallas import tpu as pltpu

BF = jnp.bfloat16
F32 = jnp.float32
I32 = jnp.int32

D = 1024
GRID_W = 64
N_HEADS = 6
HEAD_DIM = 64
V_DIM = 2 * HEAD_DIM
QK_W = N_HEADS * 2 * HEAD_DIM
V_W = N_HEADS * V_DIM
FOUR_W = 256
FOUR_G = 64
IN_W = 2 * QK_W + V_W + FOUR_W
N_EXP = 16
CAP_FACTOR = 2
ROPE_BASE = 10000.0
EPS = 1e-6
Q_SCALE = HEAD_DIM ** -0.5 * math.log2(math.e)
ATTN_SHIFT_LIMIT = 40.0
LANES = 128
MIB = 1024 * 1024

NT = (((1,), (1,)), ((), ()))


def _params(sem, vmem_mib):
    return pltpu.CompilerParams(dimension_semantics=sem, vmem_limit_bytes=vmem_mib * MIB)


def _split2(x):
    hi = x.astype(BF)
    lo = (x - hi.astype(F32)).astype(BF)
    return hi, lo


def _dot3(a, b, dims=(((1,), (0,)), ((), ()))):
    ah, al = _split2(a)
    bh, bl = _split2(b)
    dg = functools.partial(lax.dot_general, dimension_numbers=dims, preferred_element_type=F32)
    return dg(ah, bh) + dg(ah, bl) + dg(al, bh)


def _modulate(x, nw, shift, scale):
    ms = jnp.mean(x * x, axis=-1, keepdims=True)
    return (x * lax.rsqrt(ms + EPS) * nw) * (1.0 + scale) + shift


def _ada_kernel(c_ref, w_ref, b_ref, o_ref):
    cv = c_ref[...]
    s = cv * (1.0 / (1.0 + jnp.exp(-cv)))
    o_ref[0] = _dot3(s, w_ref[0]) + b_ref[0]


def _ada(cond8, ada_w, ada_b):
    depth = ada_w.shape[0]
    tn = 1536
    return pl.pallas_call(
        _ada_kernel,
        grid=(depth, 6 * D // tn),
        in_specs=[
            pl.BlockSpec((8, D), lambda l, j: (0, 0)),
            pl.BlockSpec((1, D, tn), lambda l, j: (l, 0, j)),
            pl.BlockSpec((1, 1, tn), lambda l, j: (l, 0, j)),
        ],
        out_specs=pl.BlockSpec((1, 8, tn), lambda l, j: (l, 0, j)),
        out_shape=jax.ShapeDtypeStruct((depth, 8, 6 * D), F32),
        compiler_params=_params(("arbitrary", "arbitrary"), 40),
        name="ada",
    )(cond8, ada_w, ada_b.reshape(depth, 1, 6 * D))


def _norm_rope(xb, wn, cos, sin_s, out_scale):
    lane = lax.broadcasted_iota(I32, xb.shape, 1)
    lo = lane < HEAD_DIM
    ss = xb * xb
    s_lo = jnp.sum(jnp.where(lo, ss, 0.0), axis=-1, keepdims=True)
    s_hi = jnp.sum(jnp.where(lo, 0.0, ss), axis=-1, keepdims=True)
    inv = jnp.where(lo, lax.rsqrt(s_lo * (1.0 / HEAD_DIM) + EPS),
                    lax.rsqrt(s_hi * (1.0 / HEAD_DIM) + EPS))
    y = xb * inv * wn
    fwd = pltpu.roll(y, LANES - 16, axis=1)
    bwd = pltpu.roll(y, 16, axis=1)
    rot = jnp.where((lane & 31) < 16, fwd, bwd)
    out = y * cos + rot * sin_s
    if out_scale != 1.0:
        out = out * out_scale
    return out


def _proj0_kernel(x_ref, ctx_ref, nw_ref, sh_ref, sc_ref, w_ref, cos_ref, sin_ref, qn_ref, kn_ref,
                  q_ref, k_ref, v_ref, f_ref, *, lat_tiles):
    x = jnp.where(pl.program_id(1) < lat_tiles, x_ref[0], ctx_ref[0])
    h = _modulate(x, nw_ref[...], sh_ref[0, 0], sc_ref[0, 0]).astype(BF)
    cos = cos_ref[...]
    sin_s = sin_ref[...]
    qn = qn_ref[...]
    kn = kn_ref[...]
    q_scale = Q_SCALE
    pq = jnp.dot(h, w_ref[:, :QK_W], preferred_element_type=F32)
    pk = jnp.dot(h, w_ref[:, QK_W:2 * QK_W], preferred_element_type=F32)
    for hh in range(N_HEADS):
        c0 = hh * V_DIM
        q_ref[0, :, c0:c0 + V_DIM] = _norm_rope(pq[:, c0:c0 + V_DIM], qn, cos, sin_s, q_scale).astype(BF)
        k_ref[0, :, c0:c0 + V_DIM] = _norm_rope(pk[:, c0:c0 + V_DIM], kn, cos, sin_s, 1.0).astype(BF)
    v_ref[0] = jnp.dot(h, w_ref[:, 2 * QK_W:2 * QK_W + V_W], preferred_element_type=F32).astype(BF)
    f_ref[0] = jnp.dot(h, w_ref[:, 2 * QK_W + V_W:], preferred_element_type=F32).astype(BF)


def _proj0(x, ctx, nw, sh2, sc2, w_bf, cos, sin_s, qn, kn):
    b, n_lat, _ = x.shape
    n_ctx = ctx.shape[1]
    s_tot = n_lat + n_ctx
    tm = 256
    lat_tiles = n_lat // tm
    mod_spec = pl.BlockSpec((1, 1, 1, D), lambda bb, i: (bb, i // lat_tiles, 0, 0))
    full = lambda shape: pl.BlockSpec(shape, lambda bb, i: tuple(0 for _ in shape))
    row = lambda w: pl.BlockSpec((1, tm, w), lambda bb, i: (bb, i, 0))
    return pl.pallas_call(
        functools.partial(_proj0_kernel, lat_tiles=lat_tiles),
        grid=(b, s_tot // tm),
        in_specs=[
            pl.BlockSpec((1, tm, D), lambda bb, i: (bb, jnp.minimum(i, lat_tiles - 1), 0)),
            pl.BlockSpec((1, tm, D), lambda bb, i: (bb, jnp.maximum(i - lat_tiles, 0), 0)),
            full((1, D)), mod_spec, mod_spec, full((D, IN_W)),
            pl.BlockSpec((tm, LANES), lambda bb, i: (i, 0)),
            pl.BlockSpec((tm, LANES), lambda bb, i: (i, 0)),
            full((1, LANES)), full((1, LANES)),
        ],
        out_specs=[row(QK_W), row(QK_W), row(V_W), row(FOUR_W)],
        out_shape=[
            jax.ShapeDtypeStruct((b, s_tot, QK_W), BF),
            jax.ShapeDtypeStruct((b, s_tot, QK_W), BF),
            jax.ShapeDtypeStruct((b, s_tot, V_W), BF),
            jax.ShapeDtypeStruct((b, s_tot, FOUR_W), BF),
        ],
        compiler_params=_params(("parallel", "arbitrary"), 48),
        name="proj0",
    )(x, ctx, nw, sh2, sc2, w_bf, cos, sin_s, qn, kn)


def _attn_kernel(bound_ref, q_ref, k_ref, v_ref, lam_ref, sub_ref, o_ref, vt_ref, m_ref, e_ref,
                 *, lam_init, kc):
    @pl.when(pl.program_id(2) == 0)
    def _():
        vt_ref[...] = v_ref[0].astype(F32).T.astype(BF)

    q = q_ref[0]
    lane = lax.broadcasted_iota(I32, q.shape, 1)
    zero = jnp.zeros_like(q)
    lv = lam_ref[...]
    t1 = jnp.sum(lv[0:1] * lv[1:2], axis=-1, keepdims=True)
    t2 = jnp.sum(lv[2:3] * lv[3:4], axis=-1, keepdims=True)
    lam = jnp.exp(t1) - jnp.exp(t2) + lam_init

    qs = (jnp.where(lane < HEAD_DIM, q, zero), jnp.where(lane < HEAD_DIM, zero, q))
    tq = q.shape[0]
    n_chunks = k_ref.shape[1] // kc

    def scores(h, c):
        return lax.dot_general(k_ref[0, c * kc:(c + 1) * kc, :], qs[h], NT, preferred_element_type=F32)

    def fold(x, op):
        return op(x.reshape(kc // 8, 8, tq), axis=0)

    bound = bound_ref[0]
    small = bound <= ATTN_SHIFT_LIMIT

    @pl.when(small)
    def _():
        m_ref[...] = jnp.zeros(m_ref.shape, F32) + bound

    @pl.when(jnp.logical_not(small))
    def _():
        for h in range(2):
            m = jnp.full((8, tq), -jnp.inf, F32)
            for c in range(n_chunks):
                m = jnp.maximum(m, fold(scores(h, c), jnp.max))
            m_ref[h] = jnp.broadcast_to(jnp.max(m, axis=0, keepdims=True), (8, tq))

    ms = [m_ref[h][0:1, :] for h in range(2)]
    ls = [jnp.zeros((8, tq), F32) for _ in range(2)]
    for c in range(n_chunks):
        for h in range(2):
            e = jnp.exp2(scores(h, c) - ms[h])
            ls[h] = ls[h] + fold(e, jnp.sum)
            e_ref[h, c] = e.astype(BF)
    l0, l1 = [jnp.sum(l, axis=0, keepdims=True) for l in ls]

    beta = (lam * l0 / l1).astype(BF)
    acc = jnp.zeros((V_DIM, tq), F32)
    for c in range(n_chunks):
        a = e_ref[0, c] - beta * e_ref[1, c]
        acc = acc + jnp.dot(vt_ref[:, c * kc:(c + 1) * kc], a, preferred_element_type=F32)
    o = acc * (1.0 / l0)
    ms = jnp.mean(o * o, axis=0, keepdims=True)
    o = o * lax.rsqrt(ms + EPS) * sub_ref[...] * (1.0 - lam_init)
    o_ref[0] = o.T.astype(BF)


def _attention(bound, q, k, v, lamv, subln, n_lat, lam_init):
    b, s_tot, _ = k.shape
    tq = 1024
    kc = 256
    assert s_tot % kc == 0
    kv_spec = pl.BlockSpec((1, s_tot, V_DIM), lambda bb, hh, i: (bb, 0, hh))
    q_spec = pl.BlockSpec((1, tq, V_DIM), lambda bb, hh, i: (bb, i, hh))
    return pl.pallas_call(
        functools.partial(_attn_kernel, lam_init=lam_init, kc=kc),
        grid=(b, N_HEADS, n_lat // tq),
        in_specs=[
            pl.BlockSpec(memory_space=pltpu.SMEM),
            q_spec, kv_spec, kv_spec,
            pl.BlockSpec((8, LANES), lambda bb, hh, i: (0, 0)),
            pl.BlockSpec((V_DIM, 1), lambda bb, hh, i: (0, 0)),
        ],
        out_specs=q_spec,
        out_shape=jax.ShapeDtypeStruct((b, n_lat, V_W), BF),
        scratch_shapes=[pltpu.VMEM((V_DIM, s_tot), BF),
                        pltpu.VMEM((2, 8, tq), F32),
                        pltpu.VMEM((2, s_tot // kc, kc, tq), BF)],
        compiler_params=_params(("parallel", "parallel", "arbitrary"), 48),
        name="diff_attn",
    )(bound, q, k, v, lamv, subln)


def _fourier_kernel(f_ref, bd_ref, cb_ref, sb_ref, ca_ref, sa_ref, o_ref, g_ref):
    nb, n_lat, _ = f_ref.shape

    @pl.when(pl.program_id(0) == 0)
    def _():
        rows = 1024
        for bb in range(nb):
            for r in range(n_lat // rows):
                fc = jnp.dot(f_ref[bb, r * rows:(r + 1) * rows, :], bd_ref[...],
                             preferred_element_type=F32)
                g_ref[r * rows:(r + 1) * rows, bb * FOUR_W:(bb + 1) * FOUR_W] = fc[:, :FOUR_W].astype(BF)
                g_ref[n_lat + r * rows:n_lat + (r + 1) * rows,
                      bb * FOUR_W:(bb + 1) * FOUR_W] = fc[:, FOUR_W:].astype(BF)

    ca = ca_ref[0]
    sa = sa_ref[0]
    cb = cb_ref[...]
    sb = sb_ref[...]
    w_cos = (cb * ca - sb * sa).astype(BF)
    w_sin = (sb * ca + cb * sa).astype(BF)
    y = (jnp.dot(w_cos, g_ref[:n_lat, :], preferred_element_type=F32)
         - jnp.dot(w_sin, g_ref[n_lat:, :], preferred_element_type=F32))
    for bb in range(nb):
        o_ref[bb] = y[:, bb * FOUR_W:(bb + 1) * FOUR_W].astype(BF)


def _fourier(f, bd, tables, n_lat):
    b = f.shape[0]
    cos_row, sin_row, cos_tile, sin_tile = tables
    tr = cos_row.shape[0]
    row_tab = pl.BlockSpec((tr, n_lat), lambda i: (0, 0))
    tile_tab = pl.BlockSpec((1, 1, n_lat), lambda i: (i, 0, 0))
    return pl.pallas_call(
        _fourier_kernel,
        grid=(n_lat // tr,),
        in_specs=[
            pl.BlockSpec((b, n_lat, FOUR_W), lambda i: (0, 0, 0)),
            pl.BlockSpec((FOUR_W, 2 * FOUR_W), lambda i: (0, 0)),
            row_tab, row_tab, tile_tab, tile_tab,
        ],
        out_specs=pl.BlockSpec((b, tr, FOUR_W), lambda i: (0, i, 0)),
        out_shape=jax.ShapeDtypeStruct((b, n_lat, FOUR_W), BF),
        scratch_shapes=[pltpu.VMEM((2 * n_lat, b * FOUR_W), BF)],
        compiler_params=_params(("arbitrary",), 56),
        name="fourier",
    )(f, bd, cos_row, sin_row, cos_tile, sin_tile)


SUB = D // LANES


def _store_token_tiles(ref, val):
    rows = val.shape[0]
    for j in range(SUB):
        ref[0, pl.ds(j, rows, stride=SUB), :] = val[:, j * LANES:(j + 1) * LANES]


def _load_token_tiles(ref, rows):
    return jnp.concatenate([ref[0, pl.ds(j, rows, stride=SUB), :] for j in range(SUB)], axis=1)


def _residual_router(x, y, gm, nf, shf, scf, wr_t, x1_ref, hf_ref, aff_ref):
    x1 = x + gm * y
    x1_ref[0] = x1
    hf = _modulate(x1, nf, shf, scf)
    _store_token_tiles(hf_ref, hf)
    logits = _dot3(wr_t, hf, NT)
    m = jnp.max(logits, axis=0, keepdims=True)
    e = jnp.exp(logits - m)
    aff_ref[0] = e / jnp.sum(e, axis=0, keepdims=True)


def _merge0_kernel(att_ref, four_ref, w_ref, x_ref, gm_ref, nf_ref, shf_ref, scf_ref, wr_ref,
                   x1_ref, hf_ref, aff_ref):
    y = jnp.dot(att_ref[0], w_ref[:V_W, :], preferred_element_type=F32)
    y = y + jnp.dot(four_ref[0], w_ref[V_W:, :], preferred_element_type=F32)
    _residual_router(x_ref[0], y, gm_ref[0], nf_ref[...], shf_ref[0], scf_ref[0], wr_ref[...],
                     x1_ref, hf_ref, aff_ref)


def _router_specs(tm):
    vec = pl.BlockSpec((1, 1, D), lambda bb, i: (bb, 0, 0))
    row = pl.BlockSpec((1, tm, D), lambda bb, i: (bb, i, 0))
    in_specs = [vec, pl.BlockSpec((1, D), lambda bb, i: (0, 0)), vec, vec,
                pl.BlockSpec((N_EXP, D), lambda bb, i: (0, 0))]
    tiles = _tile_spec(tm)
    out_specs = [row, tiles, pl.BlockSpec((1, N_EXP, tm), lambda bb, i: (bb, 0, i))]
    return row, in_specs, out_specs


def _tile_spec(tm):
    return pl.BlockSpec((1, tm * SUB, LANES), lambda bb, i: (bb, i, 0))


def _router_out_shape(b, n):
    return [jax.ShapeDtypeStruct((b, n, D), F32), jax.ShapeDtypeStruct((b, n * SUB, LANES), F32),
            jax.ShapeDtypeStruct((b, N_EXP, n), F32)]


def _merge0(att, four, w_bf, x, gm, nf, shf, scf, wr_t):
    b, n, _ = x.shape
    tm = 512
    row, r_in, r_out = _router_specs(tm)
    return pl.pallas_call(
        _merge0_kernel,
        grid=(b, n // tm),
        in_specs=[
            pl.BlockSpec((1, tm, V_W), lambda bb, i: (bb, i, 0)),
            pl.BlockSpec((1, tm, FOUR_W), lambda bb, i: (bb, i, 0)),
            pl.BlockSpec((D, D), lambda bb, i: (0, 0)),
            row,
        ] + r_in,
        out_specs=r_out,
        out_shape=_router_out_shape(b, n),
        compiler_params=_params(("parallel", "arbitrary"), 48),
        name="merge0",
    )(att, four, w_bf, x, gm, nf, shf, scf, wr_t)


def _proj1_kernel(x_ref, moe_ref, nw_ref, sh_ref, sc_ref, w_ref, bg_ref, z_ref, x2_ref):
    x = x_ref[0] + _load_token_tiles(moe_ref, bg_ref.shape[1])
    x2_ref[0] = x
    h = _modulate(x, nw_ref[...], sh_ref[0], sc_ref[0]).astype(BF)
    bg_ref[0] = jnp.dot(h, w_ref[:, :D], preferred_element_type=F32)
    cg = jnp.dot(h, w_ref[:, D:2 * D], preferred_element_type=F32)
    u = jnp.dot(h, w_ref[:, 2 * D:], preferred_element_type=F32)
    z_ref[0] = cg * u


def _proj1(x, moe_tiles, nw, sh, sc, w_bf):
    b, n, _ = x.shape
    tm = 512
    vec = pl.BlockSpec((1, 1, D), lambda bb, i: (bb, 0, 0))
    row = pl.BlockSpec((1, tm, D), lambda bb, i: (bb, i, 0))
    rows = jax.ShapeDtypeStruct((b, n, D), F32)
    return pl.pallas_call(
        _proj1_kernel,
        grid=(b, n // tm),
        in_specs=[row, _tile_spec(tm), pl.BlockSpec((1, D), lambda bb, i: (0, 0)), vec, vec,
                  pl.BlockSpec((D, 3 * D), lambda bb, i: (0, 0))],
        out_specs=[row, row, row],
        out_shape=[rows, rows, rows],
        compiler_params=_params(("parallel", "arbitrary"), 48),
        name="proj1",
    )(x, moe_tiles, nw, sh, sc, w_bf)


def _merge1_kernel(bg_ref, z_ref, zp_ref, zn_ref, cw_ref, w_ref, x_ref, gm_ref, nf_ref, shf_ref,
                   scf_ref, wr_ref, x1_ref, hf_ref, aff_ref):
    i = pl.program_id(1)
    last = pl.num_programs(1) - 1
    z = z_ref[0]
    tm = z.shape[0]
    rowid = lax.broadcasted_iota(I32, z.shape, 0)
    prev_row = jnp.where(i > 0, zp_ref[0, 7:8, :], 0.0)
    next_row = jnp.where(i < last, zn_ref[0, 0:1, :], 0.0)
    z_up = jnp.where(rowid == 0, prev_row, pltpu.roll(z, 1, axis=0))
    z_dn = jnp.where(rowid == tm - 1, next_row, pltpu.roll(z, tm - 1, axis=0))
    cw = cw_ref[...]
    conv = cw[0:1] * z_up + cw[1:2] * z + cw[2:3] * z_dn
    y = jnp.dot((bg_ref[0] * conv).astype(BF), w_ref[...], preferred_element_type=F32)
    _residual_router(x_ref[0], y, gm_ref[0], nf_ref[...], shf_ref[0], scf_ref[0], wr_ref[...],
                     x1_ref, hf_ref, aff_ref)


def _merge1(bg, z, conv_w8, w_bf, x, gm, nf, shf, scf, wr_t):
    b, n, _ = bg.shape
    tm = 512
    halo = 8
    per = tm // halo
    n_halo = n // halo
    row, r_in, r_out = _router_specs(tm)
    return pl.pallas_call(
        _merge1_kernel,
        grid=(b, n // tm),
        in_specs=[
            row, row,
            pl.BlockSpec((1, halo, D), lambda bb, i: (bb, jnp.maximum(i * per - 1, 0), 0)),
            pl.BlockSpec((1, halo, D), lambda bb, i: (bb, jnp.minimum((i + 1) * per, n_halo - 1), 0)),
            pl.BlockSpec((8, D), lambda bb, i: (0, 0)),
            pl.BlockSpec((D, D), lambda bb, i: (0, 0)),
            row,
        ] + r_in,
        out_specs=r_out,
        out_shape=_router_out_shape(b, n),
        compiler_params=_params(("parallel", "arbitrary"), 48),
        name="merge1",
    )(bg, z, z, z, conv_w8, w_bf, x, gm, nf, shf, scf, wr_t)


def _select_kernel(a_ref, idx_ref, gate_ref, cl_s, off_s, inc_s, hi_s, mid_s, lo_s, *, cap, chunks):
    a = a_ref[...]
    rows = a.shape[0]
    groups = rows // chunks
    per_batch = N_EXP * chunks
    assert chunks & (chunks - 1) == 0
    shift = chunks.bit_length() - 1

    def indicator(shape, row_dim):
        r = lax.broadcasted_iota(I32, shape, row_dim)
        g = lax.broadcasted_iota(I32, shape, 1 - row_dim)
        return jnp.where((r >> shift) == g, 1.0, 0.0).astype(BF)

    member = indicator((groups, rows), 1)
    spread = indicator((rows, groups), 0)
    ri = lax.broadcasted_iota(I32, (per_batch, per_batch), 0)
    ci = lax.broadcasted_iota(I32, (per_batch, per_batch), 1)
    same_f = jnp.where((ri >> shift) == (ci >> shift), 1.0, 0.0)
    same = same_f.astype(BF)
    lower = (same_f * jnp.where(ci < ri, 1.0, 0.0)).astype(BF)
    li = lax.broadcasted_iota(I32, (LANES, LANES), 0)
    lj = lax.broadcasted_iota(I32, (LANES, LANES), 1)
    incl = jnp.where(li <= lj, 1.0, 0.0).astype(BF)

    def bcast(col):
        return jnp.broadcast_to(col, (per_batch, LANES))

    def prefix(maskf):
        cl = jnp.dot(maskf.astype(BF), incl, preferred_element_type=F32)
        tot = bcast(cl[:, LANES - 1:LANES])
        off = jnp.dot(lower, tot.astype(BF), preferred_element_type=F32)
        return cl, off, tot

    min_normal = 0x00800000

    def search(step, t):
        cand = t | jnp.left_shift(jnp.int32(1), 30 - step)
        mask = jnp.where(a >= lax.bitcast_convert_type(cand, F32), 1.0, 0.0).astype(BF)
        part = jnp.dot(member, mask, preferred_element_type=F32)
        cnt = jnp.sum(part, axis=-1, keepdims=True)
        ok = jnp.broadcast_to(jnp.where(cnt >= cap, 1.0, 0.0), (groups, LANES)).astype(BF)
        ok_rows = jnp.dot(spread, ok, preferred_element_type=F32)
        return jnp.where(ok_rows > 0.5, jnp.where(cand >= min_normal, cand, t), t)

    thr = lax.fori_loop(0, 31, search, jnp.zeros((rows, LANES), I32))
    thr_f = lax.bitcast_convert_type(thr, F32)
    nxt_f = lax.bitcast_convert_type(jnp.maximum(thr + 1, min_normal), F32)
    gtf = jnp.where(a >= nxt_f, 1.0, 0.0)
    eqf = jnp.where(a >= thr_f, 1.0, 0.0) - gtf
    for bb in range(rows // per_batch):
        sl = slice(bb * per_batch, (bb + 1) * per_batch)
        gt_b, eq_b = gtf[sl], eqf[sl]
        n_gt = jnp.dot(same, bcast(jnp.sum(gt_b, axis=-1, keepdims=True)).astype(BF),
                       preferred_element_type=F32)
        cl_eq, off_eq, _ = prefix(eq_b)
        sel = gt_b + eq_b * jnp.where(cl_eq + off_eq <= cap - n_gt, 1.0, 0.0)
        cl, off, tot = prefix(sel)
        cl_s[sl, :] = cl.astype(BF)
        off_s[sl, :] = off
        inc_s[sl, :] = off + tot
        a_b = a[sl]
        a_hi = a_b.astype(BF)
        r1 = a_b - a_hi.astype(F32)
        a_mid = r1.astype(BF)
        hi_s[sl, :] = a_hi
        mid_s[sl, :] = a_mid
        lo_s[sl, :] = (r1 - a_mid.astype(F32)).astype(BF)

    slot = lax.broadcasted_iota(I32, (chunks, cap), 1).astype(F32)
    chunk_id = lax.broadcasted_iota(I32, (chunks, cap), 0).astype(F32)
    lane_id = lax.broadcasted_iota(I32, (LANES, cap), 0).astype(F32)
    tn = (((0,), (0,)), ((), ()))
    reps = cap // LANES

    def widen(x):
        return jnp.concatenate([x] * reps, axis=1)

    def per_group(g, carry):
        win = pl.ds(pl.multiple_of(g * chunks, chunks), chunks)
        inc_g = widen(inc_s[win, :])
        off_g = widen(off_s[win, :])
        chunk_of = jnp.sum(jnp.where(inc_g <= slot, 1.0, 0.0), axis=0, keepdims=True)
        pick_f = jnp.where(chunk_id == chunk_of, 1.0, 0.0)
        before = jnp.sum(pick_f * off_g, axis=0, keepdims=True)
        pick = pick_f.astype(BF)
        counts = lax.dot_general(cl_s[win, :], pick, tn, preferred_element_type=F32)
        lane_of = jnp.sum(jnp.where(counts <= slot[0:1] - before, 1.0, 0.0), axis=0, keepdims=True)
        aff = (lax.dot_general(hi_s[win, :], pick, tn, preferred_element_type=F32)
               + lax.dot_general(mid_s[win, :], pick, tn, preferred_element_type=F32)
               + lax.dot_general(lo_s[win, :], pick, tn, preferred_element_type=F32))
        gate = jnp.sum(jnp.where(lane_id == lane_of, aff, 0.0), axis=0, keepdims=True)
        idx_ref[pl.ds(g, 1), :] = ((chunk_of * float(LANES) + lane_of) * float(SUB)).astype(I32)
        gate_ref[pl.ds(g, 1), :] = gate
        return carry

    lax.fori_loop(0, groups, per_group, 0, unroll=2)


def _select(aff_rows, cap, chunks):
    rows = aff_rows.shape[0]
    groups = rows // chunks
    whole = lambda shape: pl.BlockSpec(shape, lambda i: (0, 0))
    return pl.pallas_call(
        functools.partial(_select_kernel, cap=cap, chunks=chunks),
        grid=(1,),
        in_specs=[whole((rows, LANES))],
        out_specs=[whole((groups, cap)), whole((groups, cap))],
        out_shape=[jax.ShapeDtypeStruct((groups, cap), I32), jax.ShapeDtypeStruct((groups, cap), F32)],
        scratch_shapes=[pltpu.VMEM((rows, LANES), BF), pltpu.VMEM((rows, LANES), F32),
                        pltpu.VMEM((rows, LANES), F32), pltpu.VMEM((rows, LANES), BF),
                        pltpu.VMEM((rows, LANES), BF), pltpu.VMEM((rows, LANES), BF)],
        compiler_params=_params(("arbitrary",), 48),
        name="select",
    )(aff_rows)


def _gather_kernel(idx_ref, h_ref, o_ref, rows_ref):
    cap = o_ref.shape[2]

    def body(c, carry):
        t = pl.multiple_of(idx_ref[0, 0, c], SUB)
        rows_ref[0, pl.ds(pl.multiple_of(c * SUB, SUB), SUB), :] = h_ref[0, pl.ds(t, SUB), :]
        return carry

    lax.fori_loop(0, cap, body, 0, unroll=8)
    o_ref[0, 0] = _load_token_tiles(rows_ref, cap).astype(BF)


def _gather(idx, hf_tiles, cap):
    b, rows, _ = hf_tiles.shape
    return pl.pallas_call(
        _gather_kernel,
        grid=(b, N_EXP),
        in_specs=[
            pl.BlockSpec((1, 1, cap), lambda bb, e: (bb * N_EXP + e, 0, 0), memory_space=pltpu.SMEM),
            pl.BlockSpec((1, rows, LANES), lambda bb, e: (bb, 0, 0)),
        ],
        out_specs=pl.BlockSpec((1, 1, cap, D), lambda bb, e: (bb, e, 0, 0)),
        out_shape=jax.ShapeDtypeStruct((b, N_EXP, cap, D), BF),
        scratch_shapes=[pltpu.VMEM((1, cap * SUB, LANES), F32)],
        compiler_params=_params(("parallel", "arbitrary"), 48),
        name="gather",
    )(idx, hf_tiles)


def _ffn_kernel(x_ref, wg_ref, wu_ref, wd_ref, o_ref, wg_s, wu_s, wd_s):
    @pl.when(pl.program_id(1) == 0)
    def _():
        wg_s[...] = wg_ref[...].astype(BF)
        wu_s[...] = wu_ref[...].astype(BF)
        wd_s[...] = wd_ref[...].astype(BF)

    x = x_ref[0, 0]
    dexp = wg_s.shape[1]
    step = 256
    y = None
    for c0 in range(0, dexp, step):
        g = jnp.dot(x, wg_s[:, c0:c0 + step], preferred_element_type=F32)
        u = jnp.dot(x, wu_s[:, c0:c0 + step], preferred_element_type=F32)
        act = (g * (1.0 / (1.0 + jnp.exp(-g))) * u).astype(BF)
        part = jnp.dot(act, wd_s[c0:c0 + step, :], preferred_element_type=F32)
        y = part if y is None else y + part
    _store_token_tiles(o_ref.at[0], y)


def _ffn(xs, w_gate, w_up, w_down, layer):
    b, _, cap, _ = xs.shape
    dexp = w_gate.shape[-1]
    xspec = pl.BlockSpec((1, 1, cap, D), lambda e, bb: (bb, e, 0, 0))
    yspec = pl.BlockSpec((1, 1, cap * SUB, LANES), lambda e, bb: (bb, e, 0, 0))

    def wspec(shape, switch_at):
        def index(e, bb):
            return (layer, jnp.minimum(e + jnp.where(bb >= switch_at, 1, 0), N_EXP - 1), 0, 0)
        return pl.BlockSpec((None, None) + shape, index)

    assert b >= 2
    stagger = [min(j, b - 1) for j in (1, 2, 3)]
    return pl.pallas_call(
        _ffn_kernel,
        grid=(N_EXP, b),
        in_specs=[xspec, wspec((D, dexp), stagger[0]), wspec((D, dexp), stagger[1]),
                  wspec((dexp, D), stagger[2])],
        out_specs=yspec,
        out_shape=jax.ShapeDtypeStruct((b, N_EXP, cap * SUB, LANES), F32),
        scratch_shapes=[pltpu.VMEM((D, dexp), BF), pltpu.VMEM((D, dexp), BF), pltpu.VMEM((dexp, D), BF)],
        compiler_params=_params(("arbitrary", "arbitrary"), 56),
        name="expert_ffn",
    )(xs, w_gate, w_up, w_down)


def _scatter_kernel(idx_ref, gate_ref, y_ref, gf_ref, *rest):
    if len(rest) == 1:
        (o_ref,), x_ref = rest, None
        acc_ref = o_ref
    else:
        x_ref, o_ref, acc_ref = rest
    cap = idx_ref.shape[2]
    step = pl.program_id(1)

    @pl.when(step == 0)
    def _():
        acc_ref[...] = jnp.zeros(acc_ref.shape, F32)

    gf = gf_ref[0]

    group = 16

    def body(g, carry):
        c0 = g * group
        ts = [pl.multiple_of(idx_ref[0, 0, c0 + j], SUB) for j in range(group)]
        new = []
        for j in range(group):
            y = y_ref[0, 0, pl.ds(pl.multiple_of((c0 + j) * SUB, SUB), SUB), :]
            new.append(acc_ref[0, pl.ds(ts[j], SUB), :] + (gf * gate_ref[0, 0, c0 + j]) * y)
        for j in range(group):
            acc_ref[0, pl.ds(ts[j], SUB), :] = new[j]
        return carry

    @pl.when(step < N_EXP)
    def _():
        lax.fori_loop(0, cap // group, body, 0)

    if x_ref is not None:
        @pl.when(step >= N_EXP)
        def _():
            rows = o_ref.shape[1]
            base = (step - N_EXP) * (rows * SUB)
            o_ref[0] = x_ref[0] + jnp.concatenate(
                [acc_ref[0, pl.ds(base + j, rows, stride=SUB), :] for j in range(SUB)], axis=1)


def _scatter(idx, gate, y_tiles, gf_tile, n, x=None):
    b = gf_tile.shape[0]
    cap = idx.shape[2]
    rows = n * SUB
    tm = 512
    extra = n // tm if x is not None else 0
    expert = lambda e: jnp.minimum(e, N_EXP - 1)
    sm = lambda: pl.BlockSpec((1, 1, cap), lambda bb, e: (bb * N_EXP + expert(e), 0, 0),
                              memory_space=pltpu.SMEM)
    in_specs = [
        sm(), sm(),
        pl.BlockSpec((1, 1, cap * SUB, LANES), lambda bb, e: (bb, expert(e), 0, 0)),
        pl.BlockSpec((1, SUB, LANES), lambda bb, e: (bb, 0, 0)),
    ]
    args = [idx, gate, y_tiles, gf_tile]
    if x is not None:
        piece = pl.BlockSpec((1, tm, D), lambda bb, e: (bb, jnp.maximum(e - N_EXP, 0), 0))
        in_specs.append(piece)
        args.append(x)
        out_spec, out_shape = piece, jax.ShapeDtypeStruct((b, n, D), F32)
        scratch = [pltpu.VMEM((1, rows, LANES), F32)]
    else:
        out_spec = pl.BlockSpec((1, rows, LANES), lambda bb, e: (bb, 0, 0))
        out_shape, scratch = jax.ShapeDtypeStruct((b, rows, LANES), F32), []
    return pl.pallas_call(
        _scatter_kernel,
        grid=(b, N_EXP + extra),
        in_specs=in_specs,
        out_specs=out_spec,
        out_shape=out_shape,
        scratch_shapes=scratch,
        compiler_params=_params(("parallel", "arbitrary"), 56),
        name="scatter_add",
    )(*args)


def _moe(hf_tiles, aff_t, gf, w_gate, w_up, w_down, layer, x=None):
    b, _, n = aff_t.shape
    cap = CAP_FACTOR * n // N_EXP
    chunks = n // LANES
    idx, gate = _select(aff_t.reshape(b * N_EXP * chunks, LANES), cap, chunks)
    idx = idx.reshape(b * N_EXP, 1, cap)
    gate = gate.reshape(b * N_EXP, 1, cap)
    xs = _gather(idx, hf_tiles, cap)
    y_tiles = _ffn(xs, w_gate, w_up, w_down, layer)
    return _scatter(idx, gate, y_tiles, gf.reshape(b, SUB, LANES), n, x)


def _rope_tables(n_lat, n_ctx):
    rows = n_lat // GRID_W
    r = jnp.repeat(jnp.arange(rows, dtype=F32), GRID_W)
    col = jnp.tile(jnp.arange(GRID_W, dtype=F32), rows)
    n_freq = HEAD_DIM // 4
    inv = ROPE_BASE ** (-jnp.arange(n_freq, dtype=F32) / n_freq)
    ar = r[:, None] * inv
    ac = col[:, None] * inv
    ang = jnp.concatenate([ar, ar, ac, ac], axis=-1)
    sign = jnp.where((jnp.arange(HEAD_DIM) % 32) < 16, -1.0, 1.0).astype(F32)
    cos = jnp.concatenate([jnp.cos(ang), jnp.ones((n_ctx, HEAD_DIM), F32)], axis=0)
    sin_s = jnp.concatenate([jnp.sin(ang) * sign, jnp.zeros((n_ctx, HEAD_DIM), F32)], axis=0)
    return jnp.tile(cos, (1, 2)), jnp.tile(sin_s, (1, 2))


def _dft_tables(n_lat):
    c = jnp.arange(FOUR_G, dtype=I32)
    ang_c = (2.0 * math.pi / FOUR_G) * ((c[:, None] * c[None, :]) % FOUR_G).astype(F32)
    eye = jnp.eye(FOUR_W // FOUR_G, dtype=F32)
    bd = jnp.concatenate([jnp.kron(eye, jnp.cos(ang_c)), jnp.kron(eye, jnp.sin(ang_c))], axis=1)
    bd = (bd * FOUR_G ** -0.5).astype(BF)
    tr = 256
    n = jnp.arange(n_lat, dtype=I32)[None, :]
    r = jnp.arange(tr, dtype=I32)[:, None]
    i = jnp.arange(n_lat // tr, dtype=I32)[:, None]
    row_ang = (2.0 * math.pi / n_lat) * ((r * n) % n_lat).astype(F32)
    tile_ang = (2.0 * math.pi / n_lat) * ((i * tr * n) % n_lat).astype(F32)
    scale = n_lat ** -0.5
    tables = (jnp.cos(row_ang) * scale, jnp.sin(row_ang) * scale,
              jnp.cos(tile_ang)[:, None, :], jnp.sin(tile_ang)[:, None, :])
    return bd, tables


def kernel(x, c, ctx, c_ctx, ada_w, ada_b, norm_mix, norm_ffn, attn_w_in, attn_q_norm, attn_k_norm,
           lam_q1, lam_k1, lam_q2, lam_k2, attn_subln, attn_w_out, conv_w_in, conv_w, conv_w_out,
           router_w, moe_w_gate, moe_w_up, moe_w_down):
    b, n, _ = x.shape
    n_ctx = ctx.shape[1]
    assert x.shape[2] == D and n % 512 == 0 and n_ctx % 256 == 0

    cond8 = jnp.concatenate([c, c_ctx[None, :], jnp.zeros((8 - b - 1, D), F32)], axis=0)
    ada = _ada(cond8, ada_w, ada_b)

    def mods(layer):
        m = ada[layer].reshape(8, 6, D)
        return [m[:, j] for j in range(6)]

    vec = lambda t: t[:b].reshape(b, 1, D)

    sh_m, sc_m, g_m, sh_f, sc_f, g_f = mods(0)
    both = lambda t: jnp.stack([t[:b], jnp.broadcast_to(t[b], (b, D))], axis=1).reshape(b, 2, 1, D)
    cos, sin_s = _rope_tables(n, n_ctx)
    bd, dft_tabs = _dft_tables(n)
    tile2 = lambda t: jnp.tile(t.reshape(1, HEAD_DIM), (1, 2))
    q, k, v, f = _proj0(x, ctx, norm_mix[0].reshape(1, D), both(sh_m), both(sc_m),
                        attn_w_in[0].astype(BF), cos, sin_s,
                        tile2(attn_q_norm[0]), tile2(attn_k_norm[0]))
    lam_init = 0.8 - 0.6 * math.exp(-0.3 * 0)
    lamv = jnp.zeros((8, LANES), F32).at[:4, :HEAD_DIM].set(
        jnp.stack([lam_q1[0], lam_k1[0], lam_q2[0], lam_k2[0]]))
    score_bound = (1.01 * HEAD_DIM * Q_SCALE * jnp.max(jnp.abs(attn_q_norm[0]))
                   * jnp.max(jnp.abs(attn_k_norm[0])) + 0.1).reshape(1)
    att = _attention(score_bound, q, k, v, lamv, attn_subln[0].reshape(V_DIM, 1), n, lam_init)
    four = _fourier(f, bd, dft_tabs, n)
    x1, hf, aff_t = _merge0(att, four, attn_w_out[0].astype(BF), x, vec(g_m),
                            norm_ffn[0].reshape(1, D), vec(sh_f), vec(sc_f), router_w[0].T)
    moe0 = _moe(hf, aff_t, vec(g_f), moe_w_gate, moe_w_up, moe_w_down, 0)

    sh_m, sc_m, g_m, sh_f, sc_f, g_f = mods(1)
    bg, z, x2 = _proj1(x1, moe0, norm_mix[1].reshape(1, D), vec(sh_m), vec(sc_m),
                       conv_w_in[0].astype(BF))
    conv_w8 = jnp.zeros((8, D), F32).at[:3].set(conv_w[0])
    x3, hf, aff_t = _merge1(bg, z, conv_w8, conv_w_out[0].astype(BF), x2, vec(g_m),
                            norm_ffn[1].reshape(1, D), vec(sh_f), vec(sc_f), router_w[1].T)
    return _moe(hf, aff_t, vec(g_f), moe_w_gate, moe_w_up, moe_w_down, 1, x3)
```

```python
import functools
import math

import jax
import jax.numpy as jnp
from jax import lax
from jax.experimental import pallas as pl
from jax.experimental.pallas import tpu as pltpu

BF = jnp.bfloat16
F32 = jnp.float32
I32 = jnp.int32

D = 1024
GRID_W = 64
N_HEADS = 6
HEAD_DIM = 64
V_DIM = 2 * HEAD_DIM
QK_W = N_HEADS * 2 * HEAD_DIM
V_W = N_HEADS * V_DIM
FOUR_W = 256
FOUR_G = 64
IN_W = 2 * QK_W + V_W + FOUR_W
N_EXP = 16
EXPERTS_PER_STEP = 4
CAP_FACTOR = 2
ROPE_BASE = 10000.0
EPS = 1e-6
Q_SCALE = HEAD_DIM ** -0.5 * math.log2(math.e)
ATTN_SHIFT_LIMIT = 40.0
LANES = 128
MIB = 1024 * 1024

NT = (((1,), (1,)), ((), ()))


def _params(sem, vmem_mib):
    return pltpu.CompilerParams(dimension_semantics=sem, vmem_limit_bytes=vmem_mib * MIB)


def _split2(x):
    hi = x.astype(BF)
    lo = (x - hi.astype(F32)).astype(BF)
    return hi, lo


def _dot3(a, b, dims=(((1,), (0,)), ((), ()))):
    ah, al = _split2(a)
    bh, bl = _split2(b)
    dg = functools.partial(lax.dot_general, dimension_numbers=dims, preferred_element_type=F32)
    return dg(ah, bh) + dg(ah, bl) + dg(al, bh)


def _modulate(x, nw, shift, scale):
    ms = jnp.mean(x * x, axis=-1, keepdims=True)
    return (x * lax.rsqrt(ms + EPS) * nw) * (1.0 + scale) + shift


def _ada_kernel(c_ref, w_ref, b_ref, o_ref):
    cv = c_ref[...]
    s = cv * (1.0 / (1.0 + jnp.exp(-cv)))
    o_ref[0] = _dot3(s, w_ref[0]) + b_ref[0]


def _ada(cond8, ada_w, ada_b):
    depth = ada_w.shape[0]
    tn = 1536
    return pl.pallas_call(
        _ada_kernel,
        grid=(depth, 6 * D // tn),
        in_specs=[
            pl.BlockSpec((8, D), lambda l, j: (0, 0)),
            pl.BlockSpec((1, D, tn), lambda l, j: (l, 0, j)),
            pl.BlockSpec((1, 1, tn), lambda l, j: (l, 0, j)),
        ],
        out_specs=pl.BlockSpec((1, 8, tn), lambda l, j: (l, 0, j)),
        out_shape=jax.ShapeDtypeStruct((depth, 8, 6 * D), F32),
        compiler_params=_params(("arbitrary", "arbitrary"), 40),
        name="ada",
    )(cond8, ada_w, ada_b.reshape(depth, 1, 6 * D))


def _norm_rope(xb, wn, cos, sin_s, out_scale):
    lane = lax.broadcasted_iota(I32, xb.shape, 1)
    lo = lane < HEAD_DIM
    ss = xb * xb
    s_lo = jnp.sum(jnp.where(lo, ss, 0.0), axis=-1, keepdims=True)
    s_hi = jnp.sum(jnp.where(lo, 0.0, ss), axis=-1, keepdims=True)
    inv = jnp.where(lo, lax.rsqrt(s_lo * (1.0 / HEAD_DIM) + EPS),
                    lax.rsqrt(s_hi * (1.0 / HEAD_DIM) + EPS))
    y = xb * inv * wn
    fwd = pltpu.roll(y, LANES - 16, axis=1)
    bwd = pltpu.roll(y, 16, axis=1)
    rot = jnp.where((lane & 31) < 16, fwd, bwd)
    out = y * cos + rot * sin_s
    if out_scale != 1.0:
        out = out * out_scale
    return out


def _proj0_kernel(x_ref, ctx_ref, nw_ref, sh_ref, sc_ref, w_ref, cos_ref, sin_ref, qn_ref, kn_ref,
                  q_ref, k_ref, v_ref, f_ref, *, lat_tiles):
    x = jnp.where(pl.program_id(1) < lat_tiles, x_ref[0], ctx_ref[0])
    h = _modulate(x, nw_ref[...], sh_ref[0, 0], sc_ref[0, 0]).astype(BF)
    cos = cos_ref[...]
    sin_s = sin_ref[...]
    qn = qn_ref[...]
    kn = kn_ref[...]
    q_scale = Q_SCALE
    pq = jnp.dot(h, w_ref[:, :QK_W], preferred_element_type=F32)
    pk = jnp.dot(h, w_ref[:, QK_W:2 * QK_W], preferred_element_type=F32)
    for hh in range(N_HEADS):
        c0 = hh * V_DIM
        q_ref[0, :, c0:c0 + V_DIM] = _norm_rope(pq[:, c0:c0 + V_DIM], qn, cos, sin_s, q_scale).astype(BF)
        k_ref[0, :, c0:c0 + V_DIM] = _norm_rope(pk[:, c0:c0 + V_DIM], kn, cos, sin_s, 1.0).astype(BF)
    v_ref[0] = jnp.dot(h, w_ref[:, 2 * QK_W:2 * QK_W + V_W], preferred_element_type=F32).astype(BF)
    f_ref[0] = jnp.dot(h, w_ref[:, 2 * QK_W + V_W:], preferred_element_type=F32).astype(BF)


def _proj0(x, ctx, nw, sh2, sc2, w_bf, cos, sin_s, qn, kn):
    b, n_lat, _ = x.shape
    n_ctx = ctx.shape[1]
    s_tot = n_lat + n_ctx
    tm = 256
    lat_tiles = n_lat // tm
    mod_spec = pl.BlockSpec((1, 1, 1, D), lambda bb, i: (bb, i // lat_tiles, 0, 0))
    full = lambda shape: pl.BlockSpec(shape, lambda bb, i: tuple(0 for _ in shape))
    row = lambda w: pl.BlockSpec((1, tm, w), lambda bb, i: (bb, i, 0))
    return pl.pallas_call(
        functools.partial(_proj0_kernel, lat_tiles=lat_tiles),
        grid=(b, s_tot // tm),
        in_specs=[
            pl.BlockSpec((1, tm, D), lambda bb, i: (bb, jnp.minimum(i, lat_tiles - 1), 0)),
            pl.BlockSpec((1, tm, D), lambda bb, i: (bb, jnp.maximum(i - lat_tiles, 0), 0)),
            full((1, D)), mod_spec, mod_spec, full((D, IN_W)),
            pl.BlockSpec((tm, LANES), lambda bb, i: (i, 0)),
            pl.BlockSpec((tm, LANES), lambda bb, i: (i, 0)),
            full((1, LANES)), full((1, LANES)),
        ],
        out_specs=[row(QK_W), row(QK_W), row(V_W), row(FOUR_W)],
        out_shape=[
            jax.ShapeDtypeStruct((b, s_tot, QK_W), BF),
            jax.ShapeDtypeStruct((b, s_tot, QK_W), BF),
            jax.ShapeDtypeStruct((b, s_tot, V_W), BF),
            jax.ShapeDtypeStruct((b, s_tot, FOUR_W), BF),
        ],
        compiler_params=_params(("parallel", "arbitrary"), 48),
        name="proj0",
    )(x, ctx, nw, sh2, sc2, w_bf, cos, sin_s, qn, kn)


def _attn_kernel(bound_ref, q_ref, k_ref, v_ref, lam_ref, sub_ref, o_ref, vt_ref, m_ref, e_ref,
                 *, lam_init, kc):
    @pl.when(pl.program_id(2) == 0)
    def _():
        vt_ref[...] = v_ref[0].astype(F32).T.astype(BF)

    q = q_ref[0]
    lane = lax.broadcasted_iota(I32, q.shape, 1)
    zero = jnp.zeros_like(q)
    lv = lam_ref[...]
    t1 = jnp.sum(lv[0:1] * lv[1:2], axis=-1, keepdims=True)
    t2 = jnp.sum(lv[2:3] * lv[3:4], axis=-1, keepdims=True)
    lam = jnp.exp(t1) - jnp.exp(t2) + lam_init

    qs = (jnp.where(lane < HEAD_DIM, q, zero), jnp.where(lane < HEAD_DIM, zero, q))
    tq = q.shape[0]
    n_chunks = k_ref.shape[1] // kc

    def scores(h, c):
        return lax.dot_general(k_ref[0, c * kc:(c + 1) * kc, :], qs[h], NT, preferred_element_type=F32)

    def fold(x, op):
        return op(x.reshape(kc // 8, 8, tq), axis=0)

    bound = bound_ref[0]
    small = bound <= ATTN_SHIFT_LIMIT

    @pl.when(small)
    def _():
        m_ref[...] = jnp.zeros(m_ref.shape, F32) + bound

    @pl.when(jnp.logical_not(small))
    def _():
        for h in range(2):
            m = jnp.full((8, tq), -jnp.inf, F32)
            for c in range(n_chunks):
                m = jnp.maximum(m, fold(scores(h, c), jnp.max))
            m_ref[h] = jnp.broadcast_to(jnp.max(m, axis=0, keepdims=True), (8, tq))

    ms = [m_ref[h][0:1, :] for h in range(2)]
    ls = [jnp.zeros((8, tq), F32) for _ in range(2)]
    for c in range(n_chunks):
        for h in range(2):
            e = jnp.exp2(scores(h, c) - ms[h])
            ls[h] = ls[h] + fold(e, jnp.sum)
            e_ref[h, c] = e.astype(BF)
    l0, l1 = [jnp.sum(l, axis=0, keepdims=True) for l in ls]

    beta = (lam * l0 / l1).astype(BF)
    acc = jnp.zeros((V_DIM, tq), F32)
    for c in range(n_chunks):
        a = e_ref[0, c] - beta * e_ref[1, c]
        acc = acc + jnp.dot(vt_ref[:, c * kc:(c + 1) * kc], a, preferred_element_type=F32)
    o = acc * (1.0 / l0)
    ms = jnp.mean(o * o, axis=0, keepdims=True)
    o = o * lax.rsqrt(ms + EPS) * sub_ref[...] * (1.0 - lam_init)
    o_ref[0] = o.T.astype(BF)


def _attention(bound, q, k, v, lamv, subln, n_lat, lam_init):
    b, s_tot, _ = k.shape
    tq = 1024
    kc = 256
    assert s_tot % kc == 0
    kv_spec = pl.BlockSpec((1, s_tot, V_DIM), lambda bb, hh, i: (bb, 0, hh))
    q_spec = pl.BlockSpec((1, tq, V_DIM), lambda bb, hh, i: (bb, i, hh))
    return pl.pallas_call(
        functools.partial(_attn_kernel, lam_init=lam_init, kc=kc),
        grid=(b, N_HEADS, n_lat // tq),
        in_specs=[
            pl.BlockSpec(memory_space=pltpu.SMEM),
            q_spec, kv_spec, kv_spec,
            pl.BlockSpec((8, LANES), lambda bb, hh, i: (0, 0)),
            pl.BlockSpec((V_DIM, 1), lambda bb, hh, i: (0, 0)),
        ],
        out_specs=q_spec,
        out_shape=jax.ShapeDtypeStruct((b, n_lat, V_W), BF),
        scratch_shapes=[pltpu.VMEM((V_DIM, s_tot), BF),
                        pltpu.VMEM((2, 8, tq), F32),
                        pltpu.VMEM((2, s_tot // kc, kc, tq), BF)],
        compiler_params=_params(("parallel", "parallel", "arbitrary"), 48),
        name="diff_attn",
    )(bound, q, k, v, lamv, subln)


def _fourier_kernel(f_ref, bd_ref, cb_ref, sb_ref, ca_ref, sa_ref, o_ref, g_ref):
    nb, n_lat, _ = f_ref.shape

    @pl.when(pl.program_id(0) == 0)
    def _():
        rows = 1024
        for bb in range(nb):
            for r in range(n_lat // rows):
                fc = jnp.dot(f_ref[bb, r * rows:(r + 1) * rows, :], bd_ref[...],
                             preferred_element_type=F32)
                g_ref[r * rows:(r + 1) * rows, bb * FOUR_W:(bb + 1) * FOUR_W] = fc[:, :FOUR_W].astype(BF)
                g_ref[n_lat + r * rows:n_lat + (r + 1) * rows,
                      bb * FOUR_W:(bb + 1) * FOUR_W] = fc[:, FOUR_W:].astype(BF)

    ca = ca_ref[0]
    sa = sa_ref[0]
    cb = cb_ref[...]
    sb = sb_ref[...]
    w_cos = (cb * ca - sb * sa).astype(BF)
    w_sin = (sb * ca + cb * sa).astype(BF)
    y = (jnp.dot(w_cos, g_ref[:n_lat, :], preferred_element_type=F32)
         - jnp.dot(w_sin, g_ref[n_lat:, :], preferred_element_type=F32))
    for bb in range(nb):
        o_ref[bb] = y[:, bb * FOUR_W:(bb + 1) * FOUR_W].astype(BF)


def _fourier(f, bd, tables, n_lat):
    b = f.shape[0]
    cos_row, sin_row, cos_tile, sin_tile = tables
    tr = cos_row.shape[0]
    row_tab = pl.BlockSpec((tr, n_lat), lambda i: (0, 0))
    tile_tab = pl.BlockSpec((1, 1, n_lat), lambda i: (i, 0, 0))
    return pl.pallas_call(
        _fourier_kernel,
        grid=(n_lat // tr,),
        in_specs=[
            pl.BlockSpec((b, n_lat, FOUR_W), lambda i: (0, 0, 0)),
            pl.BlockSpec((FOUR_W, 2 * FOUR_W), lambda i: (0, 0)),
            row_tab, row_tab, tile_tab, tile_tab,
        ],
        out_specs=pl.BlockSpec((b, tr, FOUR_W), lambda i: (0, i, 0)),
        out_shape=jax.ShapeDtypeStruct((b, n_lat, FOUR_W), BF),
        scratch_shapes=[pltpu.VMEM((2 * n_lat, b * FOUR_W), BF)],
        compiler_params=_params(("arbitrary",), 56),
        name="fourier",
    )(f, bd, cos_row, sin_row, cos_tile, sin_tile)


SUB = D // LANES


def _store_token_tiles(ref, val):
    rows = val.shape[0]
    for j in range(SUB):
        ref[0, pl.ds(j, rows, stride=SUB), :] = val[:, j * LANES:(j + 1) * LANES]


def _load_token_tiles(ref, rows):
    return jnp.concatenate([ref[0, pl.ds(j, rows, stride=SUB), :] for j in range(SUB)], axis=1)


def _residual_router(x, y, gm, nf, shf, scf, wr_t, x1_ref, hf_ref, aff_ref):
    x1 = x + gm * y
    x1_ref[0] = x1
    hf = _modulate(x1, nf, shf, scf)
    _store_token_tiles(hf_ref, hf)
    logits = _dot3(wr_t, hf, NT)
    m = jnp.max(logits, axis=0, keepdims=True)
    e = jnp.exp(logits - m)
    aff_ref[0] = e / jnp.sum(e, axis=0, keepdims=True)


def _merge0_kernel(att_ref, four_ref, w_ref, x_ref, gm_ref, nf_ref, shf_ref, scf_ref, wr_ref,
                   x1_ref, hf_ref, aff_ref):
    y = jnp.dot(att_ref[0], w_ref[:V_W, :], preferred_element_type=F32)
    y = y + jnp.dot(four_ref[0], w_ref[V_W:, :], preferred_element_type=F32)
    _residual_router(x_ref[0], y, gm_ref[0], nf_ref[...], shf_ref[0], scf_ref[0], wr_ref[...],
                     x1_ref, hf_ref, aff_ref)


def _router_specs(tm):
    vec = pl.BlockSpec((1, 1, D), lambda bb, i: (bb, 0, 0))
    row = pl.BlockSpec((1, tm, D), lambda bb, i: (bb, i, 0))
    in_specs = [vec, pl.BlockSpec((1, D), lambda bb, i: (0, 0)), vec, vec,
                pl.BlockSpec((N_EXP, D), lambda bb, i: (0, 0))]
    tiles = _tile_spec(tm)
    out_specs = [row, tiles, pl.BlockSpec((1, N_EXP, tm), lambda bb, i: (bb, 0, i))]
    return row, in_specs, out_specs


def _tile_spec(tm):
    return pl.BlockSpec((1, tm * SUB, LANES), lambda bb, i: (bb, i, 0))


def _router_out_shape(b, n):
    return [jax.ShapeDtypeStruct((b, n, D), F32), jax.ShapeDtypeStruct((b, n * SUB, LANES), F32),
            jax.ShapeDtypeStruct((b, N_EXP, n), F32)]


def _merge0(att, four, w_bf, x, gm, nf, shf, scf, wr_t):
    b, n, _ = x.shape
    tm = 512
    row, r_in, r_out = _router_specs(tm)
    return pl.pallas_call(
        _merge0_kernel,
        grid=(b, n // tm),
        in_specs=[
            pl.BlockSpec((1, tm, V_W), lambda bb, i: (bb, i, 0)),
            pl.BlockSpec((1, tm, FOUR_W), lambda bb, i: (bb, i, 0)),
            pl.BlockSpec((D, D), lambda bb, i: (0, 0)),
            row,
        ] + r_in,
        out_specs=r_out,
        out_shape=_router_out_shape(b, n),
        compiler_params=_params(("parallel", "arbitrary"), 48),
        name="merge0",
    )(att, four, w_bf, x, gm, nf, shf, scf, wr_t)


def _proj1_kernel(x_ref, moe_ref, nw_ref, sh_ref, sc_ref, w_ref, bg_ref, z_ref, x2_ref):
    x = x_ref[0] + _load_token_tiles(moe_ref, bg_ref.shape[1])
    x2_ref[0] = x
    h = _modulate(x, nw_ref[...], sh_ref[0], sc_ref[0]).astype(BF)
    bg_ref[0] = jnp.dot(h, w_ref[:, :D], preferred_element_type=F32)
    cg = jnp.dot(h, w_ref[:, D:2 * D], preferred_element_type=F32)
    u = jnp.dot(h, w_ref[:, 2 * D:], preferred_element_type=F32)
    z_ref[0] = cg * u


def _proj1(x, moe_tiles, nw, sh, sc, w_bf):
    b, n, _ = x.shape
    tm = 512
    vec = pl.BlockSpec((1, 1, D), lambda bb, i: (bb, 0, 0))
    row = pl.BlockSpec((1, tm, D), lambda bb, i: (bb, i, 0))
    rows = jax.ShapeDtypeStruct((b, n, D), F32)
    return pl.pallas_call(
        _proj1_kernel,
        grid=(b, n // tm),
        in_specs=[row, _tile_spec(tm), pl.BlockSpec((1, D), lambda bb, i: (0, 0)), vec, vec,
                  pl.BlockSpec((D, 3 * D), lambda bb, i: (0, 0))],
        out_specs=[row, row, row],
        out_shape=[rows, rows, rows],
        compiler_params=_params(("parallel", "arbitrary"), 48),
        name="proj1",
    )(x, moe_tiles, nw, sh, sc, w_bf)


def _merge1_kernel(bg_ref, z_ref, zp_ref, zn_ref, cw_ref, w_ref, x_ref, gm_ref, nf_ref, shf_ref,
                   scf_ref, wr_ref, x1_ref, hf_ref, aff_ref):
    i = pl.program_id(1)
    last = pl.num_programs(1) - 1
    z = z_ref[0]
    tm = z.shape[0]
    rowid = lax.broadcasted_iota(I32, z.shape, 0)
    prev_row = jnp.where(i > 0, zp_ref[0, 7:8, :], 0.0)
    next_row = jnp.where(i < last, zn_ref[0, 0:1, :], 0.0)
    z_up = jnp.where(rowid == 0, prev_row, pltpu.roll(z, 1, axis=0))
    z_dn = jnp.where(rowid == tm - 1, next_row, pltpu.roll(z, tm - 1, axis=0))
    cw = cw_ref[...]
    conv = cw[0:1] * z_up + cw[1:2] * z + cw[2:3] * z_dn
    y = jnp.dot((bg_ref[0] * conv).astype(BF), w_ref[...], preferred_element_type=F32)
    _residual_router(x_ref[0], y, gm_ref[0], nf_ref[...], shf_ref[0], scf_ref[0], wr_ref[...],
                     x1_ref, hf_ref, aff_ref)


def _merge1(bg, z, conv_w8, w_bf, x, gm, nf, shf, scf, wr_t):
    b, n, _ = bg.shape
    tm = 512
    halo = 8
    per = tm // halo
    n_halo = n // halo
    row, r_in, r_out = _router_specs(tm)
    return pl.pallas_call(
        _merge1_kernel,
        grid=(b, n // tm),
        in_specs=[
            row, row,
            pl.BlockSpec((1, halo, D), lambda bb, i: (bb, jnp.maximum(i * per - 1, 0), 0)),
            pl.BlockSpec((1, halo, D), lambda bb, i: (bb, jnp.minimum((i + 1) * per, n_halo - 1), 0)),
            pl.BlockSpec((8, D), lambda bb, i: (0, 0)),
            pl.BlockSpec((D, D), lambda bb, i: (0, 0)),
            row,
        ] + r_in,
        out_specs=r_out,
        out_shape=_router_out_shape(b, n),
        compiler_params=_params(("parallel", "arbitrary"), 48),
        name="merge1",
    )(bg, z, z, z, conv_w8, w_bf, x, gm, nf, shf, scf, wr_t)


def _select_kernel(a_ref, idx_ref, gate_ref, cl_s, off_s, inc_s, hi_s, mid_s, lo_s, *, cap, chunks):
    a = a_ref[...]
    rows = a.shape[0]
    groups = rows // chunks
    per_batch = N_EXP * chunks
    assert chunks & (chunks - 1) == 0
    shift = chunks.bit_length() - 1

    def indicator(shape, row_dim):
        r = lax.broadcasted_iota(I32, shape, row_dim)
        g = lax.broadcasted_iota(I32, shape, 1 - row_dim)
        return jnp.where((r >> shift) == g, 1.0, 0.0).astype(BF)

    member = indicator((groups, rows), 1)
    spread = indicator((rows, groups), 0)
    ri = lax.broadcasted_iota(I32, (per_batch, per_batch), 0)
    ci = lax.broadcasted_iota(I32, (per_batch, per_batch), 1)
    same_f = jnp.where((ri >> shift) == (ci >> shift), 1.0, 0.0)
    same = same_f.astype(BF)
    lower = (same_f * jnp.where(ci < ri, 1.0, 0.0)).astype(BF)
    li = lax.broadcasted_iota(I32, (LANES, LANES), 0)
    lj = lax.broadcasted_iota(I32, (LANES, LANES), 1)
    incl = jnp.where(li <= lj, 1.0, 0.0).astype(BF)

    def bcast(col):
        return jnp.broadcast_to(col, (per_batch, LANES))

    def prefix(maskf):
        cl = jnp.dot(maskf.astype(BF), incl, preferred_element_type=F32)
        tot = bcast(cl[:, LANES - 1:LANES])
        off = jnp.dot(lower, tot.astype(BF), preferred_element_type=F32)
        return cl, off, tot

    min_normal = 0x00800000

    def search(step, t):
        cand = t | jnp.left_shift(jnp.int32(1), 30 - step)
        mask = jnp.where(a >= lax.bitcast_convert_type(cand, F32), 1.0, 0.0).astype(BF)
        part = jnp.dot(member, mask, preferred_element_type=F32)
        cnt = jnp.sum(part, axis=-1, keepdims=True)
        ok = jnp.broadcast_to(jnp.where(cnt >= cap, 1.0, 0.0), (groups, LANES)).astype(BF)
        ok_rows = jnp.dot(spread, ok, preferred_element_type=F32)
        return jnp.where(ok_rows > 0.5, jnp.where(cand >= min_normal, cand, t), t)

    thr = lax.fori_loop(0, 31, search, jnp.zeros((rows, LANES), I32))
    thr_f = lax.bitcast_convert_type(thr, F32)
    nxt_f = lax.bitcast_convert_type(jnp.maximum(thr + 1, min_normal), F32)
    gtf = jnp.where(a >= nxt_f, 1.0, 0.0)
    eqf = jnp.where(a >= thr_f, 1.0, 0.0) - gtf
    for bb in range(rows // per_batch):
        sl = slice(bb * per_batch, (bb + 1) * per_batch)
        gt_b, eq_b = gtf[sl], eqf[sl]
        n_gt = jnp.dot(same, bcast(jnp.sum(gt_b, axis=-1, keepdims=True)).astype(BF),
                       preferred_element_type=F32)
        cl_eq, off_eq, _ = prefix(eq_b)
        sel = gt_b + eq_b * jnp.where(cl_eq + off_eq <= cap - n_gt, 1.0, 0.0)
        cl, off, tot = prefix(sel)
        cl_s[sl, :] = cl.astype(BF)
        off_s[sl, :] = off
        inc_s[sl, :] = off + tot
        a_b = a[sl]
        a_hi = a_b.astype(BF)
        r1 = a_b - a_hi.astype(F32)
        a_mid = r1.astype(BF)
        hi_s[sl, :] = a_hi
        mid_s[sl, :] = a_mid
        lo_s[sl, :] = (r1 - a_mid.astype(F32)).astype(BF)

    slot = lax.broadcasted_iota(I32, (chunks, cap), 1).astype(F32)
    chunk_id = lax.broadcasted_iota(I32, (chunks, cap), 0).astype(F32)
    lane_id = lax.broadcasted_iota(I32, (LANES, cap), 0).astype(F32)
    tn = (((0,), (0,)), ((), ()))
    reps = cap // LANES

    def widen(x):
        return jnp.concatenate([x] * reps, axis=1)

    def per_group(g, carry):
        win = pl.ds(pl.multiple_of(g * chunks, chunks), chunks)
        inc_g = widen(inc_s[win, :])
        off_g = widen(off_s[win, :])
        chunk_of = jnp.sum(jnp.where(inc_g <= slot, 1.0, 0.0), axis=0, keepdims=True)
        pick_f = jnp.where(chunk_id == chunk_of, 1.0, 0.0)
        before = jnp.sum(pick_f * off_g, axis=0, keepdims=True)
        pick = pick_f.astype(BF)
        counts = lax.dot_general(cl_s[win, :], pick, tn, preferred_element_type=F32)
        lane_of = jnp.sum(jnp.where(counts <= slot[0:1] - before, 1.0, 0.0), axis=0, keepdims=True)
        aff = (lax.dot_general(hi_s[win, :], pick, tn, preferred_element_type=F32)
               + lax.dot_general(mid_s[win, :], pick, tn, preferred_element_type=F32)
               + lax.dot_general(lo_s[win, :], pick, tn, preferred_element_type=F32))
        gate = jnp.sum(jnp.where(lane_id == lane_of, aff, 0.0), axis=0, keepdims=True)
        idx_ref[pl.ds(g, 1), :] = ((chunk_of * float(LANES) + lane_of) * float(SUB)).astype(I32)
        gate_ref[pl.ds(g, 1), :] = gate
        return carry

    lax.fori_loop(0, groups, per_group, 0, unroll=2)


def _select(aff_rows, cap, chunks):
    rows = aff_rows.shape[0]
    groups = rows // chunks
    whole = lambda shape: pl.BlockSpec(shape, lambda i: (0, 0))
    return pl.pallas_call(
        functools.partial(_select_kernel, cap=cap, chunks=chunks),
        grid=(1,),
        in_specs=[whole((rows, LANES))],
        out_specs=[whole((groups, cap)), whole((groups, cap))],
        out_shape=[jax.ShapeDtypeStruct((groups, cap), I32), jax.ShapeDtypeStruct((groups, cap), F32)],
        scratch_shapes=[pltpu.VMEM((rows, LANES), BF), pltpu.VMEM((rows, LANES), F32),
                        pltpu.VMEM((rows, LANES), F32), pltpu.VMEM((rows, LANES), BF),
                        pltpu.VMEM((rows, LANES), BF), pltpu.VMEM((rows, LANES), BF)],
        compiler_params=_params(("arbitrary",), 48),
        name="select",
    )(aff_rows)


def _gather_kernel(idx_ref, h_ref, o_ref, rows_ref):
    slots = idx_ref.shape[2]

    def body(c, carry):
        t = pl.multiple_of(idx_ref[0, 0, c], SUB)
        rows_ref[0, pl.ds(pl.multiple_of(c * SUB, SUB), SUB), :] = h_ref[0, pl.ds(t, SUB), :]
        return carry

    lax.fori_loop(0, slots, body, 0, unroll=8)
    o_ref[0, 0] = _load_token_tiles(rows_ref, slots).astype(BF)


def _gather(idx, hf_tiles, cap):
    b, rows, _ = hf_tiles.shape
    steps = N_EXP // EXPERTS_PER_STEP
    slots = EXPERTS_PER_STEP * cap
    xs = pl.pallas_call(
        _gather_kernel,
        grid=(b, steps),
        in_specs=[
            pl.BlockSpec((1, 1, slots), lambda bb, s: (bb * steps + s, 0, 0), memory_space=pltpu.SMEM),
            pl.BlockSpec((1, rows, LANES), lambda bb, s: (bb, 0, 0)),
        ],
        out_specs=pl.BlockSpec((1, 1, slots, D), lambda bb, s: (bb, s, 0, 0)),
        out_shape=jax.ShapeDtypeStruct((b, steps, slots, D), BF),
        scratch_shapes=[pltpu.VMEM((1, slots * SUB, LANES), F32)],
        compiler_params=_params(("parallel", "arbitrary"), 48),
        name="gather",
    )(idx, hf_tiles)
    return xs.reshape(b, N_EXP, cap, D)


def _ffn_kernel(x_ref, gate_ref, gf_ref, wg_ref, wu_ref, wd_ref, o_ref, wg_s, wu_s, wd_s):
    @pl.when(pl.program_id(1) == 0)
    def _():
        wg_s[...] = wg_ref[...].astype(BF)
        wu_s[...] = wu_ref[...].astype(BF)
        wd_s[...] = wd_ref[...].astype(BF)

    x = x_ref[0, 0]
    dexp = wg_s.shape[1]
    step = 256
    y = None
    for c0 in range(0, dexp, step):
        g = jnp.dot(x, wg_s[:, c0:c0 + step], preferred_element_type=F32)
        u = jnp.dot(x, wu_s[:, c0:c0 + step], preferred_element_type=F32)
        act = (g * (1.0 / (1.0 + jnp.exp(-g))) * u).astype(BF)
        part = jnp.dot(act, wd_s[c0:c0 + step, :], preferred_element_type=F32)
        y = part if y is None else y + part
    cap = x.shape[0]
    gate_col = jnp.broadcast_to(gate_ref[0], (LANES, cap)).T
    y = y * jnp.concatenate([gate_col] * SUB, axis=1) * gf_ref[0]
    _store_token_tiles(o_ref.at[0], y)


def _ffn(xs, gate, gf, w_gate, w_up, w_down, layer):
    b, _, cap, _ = xs.shape
    dexp = w_gate.shape[-1]
    xspec = pl.BlockSpec((1, 1, cap, D), lambda e, bb: (bb, e, 0, 0))
    gspec = pl.BlockSpec((1, 1, cap), lambda e, bb: (bb * N_EXP + e, 0, 0))
    gfspec = pl.BlockSpec((1, 1, D), lambda e, bb: (bb, 0, 0))
    yspec = pl.BlockSpec((1, 1, cap * SUB, LANES), lambda e, bb: (bb, e, 0, 0))

    def wspec(shape, switch_at):
        def index(e, bb):
            return (layer, jnp.minimum(e + jnp.where(bb >= switch_at, 1, 0), N_EXP - 1), 0, 0)
        return pl.BlockSpec((None, None) + shape, index)

    assert b >= 2
    stagger = [min(j, b - 1) for j in (1, 2, 3)]
    return pl.pallas_call(
        _ffn_kernel,
        grid=(N_EXP, b),
        in_specs=[xspec, gspec, gfspec, wspec((D, dexp), stagger[0]), wspec((D, dexp), stagger[1]),
                  wspec((dexp, D), stagger[2])],
        out_specs=yspec,
        out_shape=jax.ShapeDtypeStruct((b, N_EXP, cap * SUB, LANES), F32),
        scratch_shapes=[pltpu.VMEM((D, dexp), BF), pltpu.VMEM((D, dexp), BF), pltpu.VMEM((dexp, D), BF)],
        compiler_params=_params(("arbitrary", "arbitrary"), 56),
        name="expert_ffn",
    )(xs, gate, gf, w_gate, w_up, w_down)


def _scatter_kernel(idx_ref, y_ref, *rest):
    if len(rest) == 1:
        (o_ref,), x_ref = rest, None
        acc_ref = o_ref
    else:
        x_ref, o_ref, acc_ref = rest
    slots = idx_ref.shape[2]
    expert_steps = N_EXP // EXPERTS_PER_STEP
    step = pl.program_id(1)

    @pl.when(step == 0)
    def _():
        acc_ref[...] = jnp.zeros(acc_ref.shape, F32)

    group = 16

    def body(g, carry):
        c0 = g * group
        ts = [pl.multiple_of(idx_ref[0, 0, c0 + j], SUB) for j in range(group)]
        new = []
        for j in range(group):
            y = y_ref[0, 0, pl.ds(pl.multiple_of((c0 + j) * SUB, SUB), SUB), :]
            new.append(acc_ref[0, pl.ds(ts[j], SUB), :] + y)
        for j in range(group):
            acc_ref[0, pl.ds(ts[j], SUB), :] = new[j]
        return carry

    @pl.when(step < expert_steps)
    def _():
        lax.fori_loop(0, slots // group, body, 0)

    if x_ref is not None:
        @pl.when(step >= expert_steps)
        def _():
            rows = o_ref.shape[1]
            base = (step - expert_steps) * (rows * SUB)
            o_ref[0] = x_ref[0] + jnp.concatenate(
                [acc_ref[0, pl.ds(base + j, rows, stride=SUB), :] for j in range(SUB)], axis=1)


def _scatter(idx, y_tiles, n, x=None):
    b = y_tiles.shape[0]
    slots = idx.shape[2]
    assert (slots // EXPERTS_PER_STEP) % 16 == 0
    rows = n * SUB
    tm = 512
    steps = N_EXP // EXPERTS_PER_STEP
    extra = n // tm if x is not None else 0
    expert_step = lambda s: jnp.minimum(s, steps - 1)
    in_specs = [
        pl.BlockSpec((1, 1, slots), lambda bb, s: (bb * steps + expert_step(s), 0, 0),
                     memory_space=pltpu.SMEM),
        pl.BlockSpec((1, 1, slots * SUB, LANES), lambda bb, s: (bb, expert_step(s), 0, 0)),
    ]
    args = [idx, y_tiles.reshape(b, steps, slots * SUB, LANES)]
    if x is not None:
        piece = pl.BlockSpec((1, tm, D), lambda bb, s: (bb, jnp.maximum(s - steps, 0), 0))
        in_specs.append(piece)
        args.append(x)
        out_spec, out_shape = piece, jax.ShapeDtypeStruct((b, n, D), F32)
        scratch = [pltpu.VMEM((1, rows, LANES), F32)]
    else:
        out_spec = pl.BlockSpec((1, rows, LANES), lambda bb, s: (bb, 0, 0))
        out_shape, scratch = jax.ShapeDtypeStruct((b, rows, LANES), F32), []
    return pl.pallas_call(
        _scatter_kernel,
        grid=(b, steps + extra),
        in_specs=in_specs,
        out_specs=out_spec,
        out_shape=out_shape,
        scratch_shapes=scratch,
        compiler_params=_params(("parallel", "arbitrary"), 56),
        name="scatter_add",
    )(*args)


def _moe(hf_tiles, aff_t, gf, w_gate, w_up, w_down, layer, x=None):
    b, _, n = aff_t.shape
    cap = CAP_FACTOR * n // N_EXP
    chunks = n // LANES
    idx, gate = _select(aff_t.reshape(b * N_EXP * chunks, LANES), cap, chunks)
    idx = idx.reshape(b * N_EXP // EXPERTS_PER_STEP, 1, EXPERTS_PER_STEP * cap)
    gate = gate.reshape(b * N_EXP, 1, cap)
    xs = _gather(idx, hf_tiles, cap)
    y_tiles = _ffn(xs, gate, gf, w_gate, w_up, w_down, layer)
    return _scatter(idx, y_tiles, n, x)


def _rope_tables(n_lat, n_ctx):
    rows = n_lat // GRID_W
    r = jnp.repeat(jnp.arange(rows, dtype=F32), GRID_W)
    col = jnp.tile(jnp.arange(GRID_W, dtype=F32), rows)
    n_freq = HEAD_DIM // 4
    inv = ROPE_BASE ** (-jnp.arange(n_freq, dtype=F32) / n_freq)
    ar = r[:, None] * inv
    ac = col[:, None] * inv
    ang = jnp.concatenate([ar, ar, ac, ac], axis=-1)
    sign = jnp.where((jnp.arange(HEAD_DIM) % 32) < 16, -1.0, 1.0).astype(F32)
    cos = jnp.concatenate([jnp.cos(ang), jnp.ones((n_ctx, HEAD_DIM), F32)], axis=0)
    sin_s = jnp.concatenate([jnp.sin(ang) * sign, jnp.zeros((n_ctx, HEAD_DIM), F32)], axis=0)
    return jnp.tile(cos, (1, 2)), jnp.tile(sin_s, (1, 2))


def _dft_tables(n_lat):
    c = jnp.arange(FOUR_G, dtype=I32)
    ang_c = (2.0 * math.pi / FOUR_G) * ((c[:, None] * c[None, :]) % FOUR_G).astype(F32)
    eye = jnp.eye(FOUR_W // FOUR_G, dtype=F32)
    bd = jnp.concatenate([jnp.kron(eye, jnp.cos(ang_c)), jnp.kron(eye, jnp.sin(ang_c))], axis=1)
    bd = (bd * FOUR_G ** -0.5).astype(BF)
    tr = 256
    n = jnp.arange(n_lat, dtype=I32)[None, :]
    r = jnp.arange(tr, dtype=I32)[:, None]
    i = jnp.arange(n_lat // tr, dtype=I32)[:, None]
    row_ang = (2.0 * math.pi / n_lat) * ((r * n) % n_lat).astype(F32)
    tile_ang = (2.0 * math.pi / n_lat) * ((i * tr * n) % n_lat).astype(F32)
    scale = n_lat ** -0.5
    tables = (jnp.cos(row_ang) * scale, jnp.sin(row_ang) * scale,
              jnp.cos(tile_ang)[:, None, :], jnp.sin(tile_ang)[:, None, :])
    return bd, tables


def kernel(x, c, ctx, c_ctx, ada_w, ada_b, norm_mix, norm_ffn, attn_w_in, attn_q_norm, attn_k_norm,
           lam_q1, lam_k1, lam_q2, lam_k2, attn_subln, attn_w_out, conv_w_in, conv_w, conv_w_out,
           router_w, moe_w_gate, moe_w_up, moe_w_down):
    b, n, _ = x.shape
    n_ctx = ctx.shape[1]
    assert x.shape[2] == D and n % 512 == 0 and n_ctx % 256 == 0

    cond8 = jnp.concatenate([c, c_ctx[None, :], jnp.zeros((8 - b - 1, D), F32)], axis=0)
    ada = _ada(cond8, ada_w, ada_b)

    def mods(layer):
        m = ada[layer].reshape(8, 6, D)
        return [m[:, j] for j in range(6)]

    vec = lambda t: t[:b].reshape(b, 1, D)

    sh_m, sc_m, g_m, sh_f, sc_f, g_f = mods(0)
    both = lambda t: jnp.stack([t[:b], jnp.broadcast_to(t[b], (b, D))], axis=1).reshape(b, 2, 1, D)
    cos, sin_s = _rope_tables(n, n_ctx)
    bd, dft_tabs = _dft_tables(n)
    tile2 = lambda t: jnp.tile(t.reshape(1, HEAD_DIM), (1, 2))
    q, k, v, f = _proj0(x, ctx, norm_mix[0].reshape(1, D), both(sh_m), both(sc_m),
                        attn_w_in[0].astype(BF), cos, sin_s,
                        tile2(attn_q_norm[0]), tile2(attn_k_norm[0]))
    lam_init = 0.8 - 0.6 * math.exp(-0.3 * 0)
    lamv = jnp.zeros((8, LANES), F32).at[:4, :HEAD_DIM].set(
        jnp.stack([lam_q1[0], lam_k1[0], lam_q2[0], lam_k2[0]]))
    score_bound = (1.01 * HEAD_DIM * Q_SCALE * jnp.max(jnp.abs(attn_q_norm[0]))
                   * jnp.max(jnp.abs(attn_k_norm[0])) + 0.1).reshape(1)
    att = _attention(score_bound, q, k, v, lamv, attn_subln[0].reshape(V_DIM, 1), n, lam_init)
    four = _fourier(f, bd, dft_tabs, n)
    x1, hf, aff_t = _merge0(att, four, attn_w_out[0].astype(BF), x, vec(g_m),
                            norm_ffn[0].reshape(1, D), vec(sh_f), vec(sc_f), router_w[0].T)
    moe0 = _moe(hf, aff_t, vec(g_f), moe_w_gate, moe_w_up, moe_w_down, 0)

    sh_m, sc_m, g_m, sh_f, sc_f, g_f = mods(1)
    bg, z, x2 = _proj1(x1, moe0, norm_mix[1].reshape(1, D), vec(sh_m), vec(sc_m),
                       conv_w_in[0].astype(BF))
    conv_w8 = jnp.zeros((8, D), F32).at[:3].set(conv_w[0])
    x3, hf, aff_t = _merge1(bg, z, conv_w8, conv_w_out[0].astype(BF), x2, vec(g_m),
                            norm_ffn[1].reshape(1, D), vec(sh_f), vec(sc_f), router_w[1].T)
    return _moe(hf, aff_t, vec(g_f), moe_w_gate, moe_w_up, moe_w_down, 1, x3)
```

```python
import functools
import math

import jax
import jax.numpy as jnp
from jax import lax
from jax.experimental import pallas as pl
from jax.experimental.pallas import tpu as pltpu

BF = jnp.bfloat16
F32 = jnp.float32
I32 = jnp.int32

D = 1024
GRID_W = 64
N_HEADS = 6
HEAD_DIM = 64
V_DIM = 2 * HEAD_DIM
QK_W = N_HEADS * 2 * HEAD_DIM
V_W = N_HEADS * V_DIM
FOUR_W = 256
FOUR_G = 64
IN_W = 2 * QK_W + V_W + FOUR_W
N_EXP = 16
EXPERTS_PER_STEP = 4
CAP_FACTOR = 2
ROPE_BASE = 10000.0
EPS = 1e-6
Q_SCALE = HEAD_DIM ** -0.5 * math.log2(math.e)
ATTN_SHIFT_LIMIT = 40.0
LANES = 128
MIB = 1024 * 1024

NT = (((1,), (1,)), ((), ()))


def _params(sem, vmem_mib):
    return pltpu.CompilerParams(dimension_semantics=sem, vmem_limit_bytes=vmem_mib * MIB)


def _split2(x):
    hi = x.astype(BF)
    lo = (x - hi.astype(F32)).astype(BF)
    return hi, lo


def _dot3(a, b, dims=(((1,), (0,)), ((), ()))):
    ah, al = _split2(a)
    bh, bl = _split2(b)
    dg = functools.partial(lax.dot_general, dimension_numbers=dims, preferred_element_type=F32)
    return dg(ah, bh) + dg(ah, bl) + dg(al, bh)


def _modulate(x, nw, shift, scale):
    ms = jnp.mean(x * x, axis=-1, keepdims=True)
    return (x * lax.rsqrt(ms + EPS) * nw) * (1.0 + scale) + shift


def _ada_kernel(c_ref, w_ref, b_ref, o_ref):
    cv = c_ref[...]
    s = cv * (1.0 / (1.0 + jnp.exp(-cv)))
    o_ref[0] = _dot3(s, w_ref[0]) + b_ref[0]


def _ada(cond8, ada_w, ada_b):
    depth = ada_w.shape[0]
    tn = 1536
    return pl.pallas_call(
        _ada_kernel,
        grid=(depth, 6 * D // tn),
        in_specs=[
            pl.BlockSpec((8, D), lambda l, j: (0, 0)),
            pl.BlockSpec((1, D, tn), lambda l, j: (l, 0, j)),
            pl.BlockSpec((1, 1, tn), lambda l, j: (l, 0, j)),
        ],
        out_specs=pl.BlockSpec((1, 8, tn), lambda l, j: (l, 0, j)),
        out_shape=jax.ShapeDtypeStruct((depth, 8, 6 * D), F32),
        compiler_params=_params(("arbitrary", "arbitrary"), 40),
        name="ada",
    )(cond8, ada_w, ada_b.reshape(depth, 1, 6 * D))


def _norm_rope(xb, wn, cos, sin_s, out_scale):
    lane = lax.broadcasted_iota(I32, xb.shape, 1)
    lo = lane < HEAD_DIM
    ss = xb * xb
    s_lo = jnp.sum(jnp.where(lo, ss, 0.0), axis=-1, keepdims=True)
    s_hi = jnp.sum(jnp.where(lo, 0.0, ss), axis=-1, keepdims=True)
    inv = jnp.where(lo, lax.rsqrt(s_lo * (1.0 / HEAD_DIM) + EPS),
                    lax.rsqrt(s_hi * (1.0 / HEAD_DIM) + EPS))
    y = xb * inv * wn
    fwd = pltpu.roll(y, LANES - 16, axis=1)
    bwd = pltpu.roll(y, 16, axis=1)
    rot = jnp.where((lane & 31) < 16, fwd, bwd)
    out = y * cos + rot * sin_s
    if out_scale != 1.0:
        out = out * out_scale
    return out


def _norm_rope_t(xt, wn_col, cos_t, sin_t):
    q4 = HEAD_DIM // 4
    halves = []
    for s in range(2):
        x = xt[s * HEAD_DIM:(s + 1) * HEAD_DIM]
        ms = jnp.sum(x * x, axis=0, keepdims=True) * (1.0 / HEAD_DIM)
        y = x * lax.rsqrt(ms + EPS) * wn_col[s * HEAD_DIM:(s + 1) * HEAD_DIM]
        rot = jnp.concatenate([y[q4:2 * q4], y[:q4], y[3 * q4:], y[2 * q4:3 * q4]], axis=0)
        halves.append(y * cos_t[s * HEAD_DIM:(s + 1) * HEAD_DIM] + rot * sin_t[s * HEAD_DIM:(s + 1) * HEAD_DIM])
    return jnp.concatenate(halves, axis=0) * Q_SCALE


def _proj0_kernel(x_ref, ctx_ref, nw_ref, sh_ref, sc_ref, w_ref, wqt_ref, cos_ref, sin_ref,
                  cost_ref, sint_ref, qn_ref, kn_ref, q_ref, k_ref, v_ref, f_ref, *, lat_tiles):
    x = jnp.where(pl.program_id(1) < lat_tiles, x_ref[0], ctx_ref[0])
    h = _modulate(x, nw_ref[...], sh_ref[0, 0], sc_ref[0, 0]).astype(BF)
    cos = cos_ref[...]
    sin_s = sin_ref[...]
    kn = kn_ref[...]
    pq_t = lax.dot_general(wqt_ref[...], h, NT, preferred_element_type=F32)
    cos_t = cost_ref[...]
    sin_t = sint_ref[...]
    qn_col = qn_ref[...]
    pk = jnp.dot(h, w_ref[:, QK_W:2 * QK_W], preferred_element_type=F32)
    for hh in range(N_HEADS):
        c0 = hh * V_DIM
        q_ref[0, c0:c0 + V_DIM, :] = _norm_rope_t(pq_t[c0:c0 + V_DIM], qn_col, cos_t, sin_t).astype(BF)
        k_ref[0, :, c0:c0 + V_DIM] = _norm_rope(pk[:, c0:c0 + V_DIM], kn, cos, sin_s, 1.0).astype(BF)
    v_ref[0] = jnp.dot(h, w_ref[:, 2 * QK_W:2 * QK_W + V_W], preferred_element_type=F32).astype(BF)
    f_ref[0] = jnp.dot(h, w_ref[:, 2 * QK_W + V_W:], preferred_element_type=F32).astype(BF)


def _proj0(x, ctx, nw, sh2, sc2, w_bf, wq_t, cos, sin_s, qn_col, kn):
    b, n_lat, _ = x.shape
    n_ctx = ctx.shape[1]
    s_tot = n_lat + n_ctx
    tm = 256
    lat_tiles = n_lat // tm
    mod_spec = pl.BlockSpec((1, 1, 1, D), lambda bb, i: (bb, i // lat_tiles, 0, 0))
    full = lambda shape: pl.BlockSpec(shape, lambda bb, i: tuple(0 for _ in shape))
    row = lambda w: pl.BlockSpec((1, tm, w), lambda bb, i: (bb, i, 0))
    return pl.pallas_call(
        functools.partial(_proj0_kernel, lat_tiles=lat_tiles),
        grid=(b, s_tot // tm),
        in_specs=[
            pl.BlockSpec((1, tm, D), lambda bb, i: (bb, jnp.minimum(i, lat_tiles - 1), 0)),
            pl.BlockSpec((1, tm, D), lambda bb, i: (bb, jnp.maximum(i - lat_tiles, 0), 0)),
            full((1, D)), mod_spec, mod_spec, full((D, IN_W)), full((QK_W, D)),
            pl.BlockSpec((tm, LANES), lambda bb, i: (i, 0)),
            pl.BlockSpec((tm, LANES), lambda bb, i: (i, 0)),
            pl.BlockSpec((LANES, tm), lambda bb, i: (0, i)),
            pl.BlockSpec((LANES, tm), lambda bb, i: (0, i)),
            full((LANES, 1)), full((1, LANES)),
        ],
        out_specs=[pl.BlockSpec((1, QK_W, tm), lambda bb, i: (bb, 0, i)),
                   row(QK_W), row(V_W), row(FOUR_W)],
        out_shape=[
            jax.ShapeDtypeStruct((b, QK_W, s_tot), BF),
            jax.ShapeDtypeStruct((b, s_tot, QK_W), BF),
            jax.ShapeDtypeStruct((b, s_tot, V_W), BF),
            jax.ShapeDtypeStruct((b, s_tot, FOUR_W), BF),
        ],
        compiler_params=_params(("parallel", "arbitrary"), 48),
        name="proj0",
    )(x, ctx, nw, sh2, sc2, w_bf, wq_t, cos, sin_s, cos.T, sin_s.T, qn_col, kn)


def _attn_kernel(bound_ref, q_ref, k_ref, v_ref, lam_ref, sub_ref, o_ref, vt_ref, m_ref, e_ref,
                 *, lam_init, kc):
    @pl.when(pl.program_id(2) == 0)
    def _():
        vt_ref[...] = v_ref[0].astype(F32).T.astype(BF)

    q = q_ref[0]
    feat = lax.broadcasted_iota(I32, q.shape, 0)
    zero = jnp.zeros_like(q)
    lv = lam_ref[...]
    t1 = jnp.sum(lv[0:1] * lv[1:2], axis=-1, keepdims=True)
    t2 = jnp.sum(lv[2:3] * lv[3:4], axis=-1, keepdims=True)
    lam = jnp.exp(t1) - jnp.exp(t2) + lam_init

    qs = (jnp.where(feat < HEAD_DIM, q, zero), jnp.where(feat < HEAD_DIM, zero, q))
    tq = q.shape[1]
    n_chunks = k_ref.shape[1] // kc

    def scores(h, c):
        return jnp.dot(k_ref[0, c * kc:(c + 1) * kc, :], qs[h], preferred_element_type=F32)

    def fold(x, op):
        return op(x.reshape(kc // 8, 8, tq), axis=0)

    bound = bound_ref[0]
    small = bound <= ATTN_SHIFT_LIMIT

    @pl.when(small)
    def _():
        m_ref[...] = jnp.zeros(m_ref.shape, F32) + bound

    @pl.when(jnp.logical_not(small))
    def _():
        for h in range(2):
            m = jnp.full((8, tq), -jnp.inf, F32)
            for c in range(n_chunks):
                m = jnp.maximum(m, fold(scores(h, c), jnp.max))
            m_ref[h] = jnp.broadcast_to(jnp.max(m, axis=0, keepdims=True), (8, tq))

    ms = [m_ref[h][0:1, :] for h in range(2)]
    ls = [jnp.zeros((8, tq), F32) for _ in range(2)]
    for c in range(n_chunks):
        for h in range(2):
            e = jnp.exp2(scores(h, c) - ms[h])
            ls[h] = ls[h] + fold(e, jnp.sum)
            e_ref[h, c] = e.astype(BF)
    l0, l1 = [jnp.sum(l, axis=0, keepdims=True) for l in ls]

    beta = (lam * l0 / l1).astype(BF)
    acc = jnp.zeros((V_DIM, tq), F32)
    for c in range(n_chunks):
        a = e_ref[0, c] - beta * e_ref[1, c]
        acc = acc + jnp.dot(vt_ref[:, c * kc:(c + 1) * kc], a, preferred_element_type=F32)
    o = acc * (1.0 / l0)
    ms = jnp.mean(o * o, axis=0, keepdims=True)
    o = o * lax.rsqrt(ms + EPS) * sub_ref[...] * (1.0 - lam_init)
    o_ref[0] = o.T.astype(BF)


def _attention(bound, q, k, v, lamv, subln, n_lat, lam_init):
    b, s_tot, _ = k.shape
    tq = 1024
    kc = 256
    assert s_tot % kc == 0
    kv_spec = pl.BlockSpec((1, s_tot, V_DIM), lambda bb, hh, i: (bb, 0, hh))
    q_spec = pl.BlockSpec((1, tq, V_DIM), lambda bb, hh, i: (bb, i, hh))
    return pl.pallas_call(
        functools.partial(_attn_kernel, lam_init=lam_init, kc=kc),
        grid=(b, N_HEADS, n_lat // tq),
        in_specs=[
            pl.BlockSpec(memory_space=pltpu.SMEM),
            pl.BlockSpec((1, V_DIM, tq), lambda bb, hh, i: (bb, hh, i)),
            kv_spec, kv_spec,
            pl.BlockSpec((8, LANES), lambda bb, hh, i: (0, 0)),
            pl.BlockSpec((V_DIM, 1), lambda bb, hh, i: (0, 0)),
        ],
        out_specs=q_spec,
        out_shape=jax.ShapeDtypeStruct((b, n_lat, V_W), BF),
        scratch_shapes=[pltpu.VMEM((V_DIM, s_tot), BF),
                        pltpu.VMEM((2, 8, tq), F32),
                        pltpu.VMEM((2, s_tot // kc, kc, tq), BF)],
        compiler_params=_params(("parallel", "parallel", "arbitrary"), 48),
        name="diff_attn",
    )(bound, q, k, v, lamv, subln)


def _fourier_kernel(f_ref, bd_ref, cb_ref, sb_ref, ca_ref, sa_ref, o_ref, g_ref):
    nb, n_lat, _ = f_ref.shape

    @pl.when(pl.program_id(0) == 0)
    def _():
        rows = 1024
        for bb in range(nb):
            for r in range(n_lat // rows):
                fc = jnp.dot(f_ref[bb, r * rows:(r + 1) * rows, :], bd_ref[...],
                             preferred_element_type=F32)
                g_ref[r * rows:(r + 1) * rows, bb * FOUR_W:(bb + 1) * FOUR_W] = fc[:, :FOUR_W].astype(BF)
                g_ref[n_lat + r * rows:n_lat + (r + 1) * rows,
                      bb * FOUR_W:(bb + 1) * FOUR_W] = fc[:, FOUR_W:].astype(BF)

    ca = ca_ref[0]
    sa = sa_ref[0]
    cb = cb_ref[...]
    sb = sb_ref[...]
    w_cos = (cb * ca - sb * sa).astype(BF)
    w_sin = (sb * ca + cb * sa).astype(BF)
    y = (jnp.dot(w_cos, g_ref[:n_lat, :], preferred_element_type=F32)
         - jnp.dot(w_sin, g_ref[n_lat:, :], preferred_element_type=F32))
    for bb in range(nb):
        o_ref[bb] = y[:, bb * FOUR_W:(bb + 1) * FOUR_W].astype(BF)


def _fourier(f, bd, tables, n_lat):
    b = f.shape[0]
    cos_row, sin_row, cos_tile, sin_tile = tables
    tr = cos_row.shape[0]
    row_tab = pl.BlockSpec((tr, n_lat), lambda i: (0, 0))
    tile_tab = pl.BlockSpec((1, 1, n_lat), lambda i: (i, 0, 0))
    return pl.pallas_call(
        _fourier_kernel,
        grid=(n_lat // tr,),
        in_specs=[
            pl.BlockSpec((b, n_lat, FOUR_W), lambda i: (0, 0, 0)),
            pl.BlockSpec((FOUR_W, 2 * FOUR_W), lambda i: (0, 0)),
            row_tab, row_tab, tile_tab, tile_tab,
        ],
        out_specs=pl.BlockSpec((b, tr, FOUR_W), lambda i: (0, i, 0)),
        out_shape=jax.ShapeDtypeStruct((b, n_lat, FOUR_W), BF),
        scratch_shapes=[pltpu.VMEM((2 * n_lat, b * FOUR_W), BF)],
        compiler_params=_params(("arbitrary",), 56),
        name="fourier",
    )(f, bd, cos_row, sin_row, cos_tile, sin_tile)


SUB = D // LANES


def _store_token_tiles(ref, val):
    rows = val.shape[0]
    for j in range(SUB):
        ref[0, pl.ds(j, rows, stride=SUB), :] = val[:, j * LANES:(j + 1) * LANES]


def _load_token_tiles(ref, rows):
    return jnp.concatenate([ref[0, pl.ds(j, rows, stride=SUB), :] for j in range(SUB)], axis=1)


def _residual_router(x, y, gm, nf, shf, scf, wr_t, x1_ref, hf_ref, aff_ref):
    x1 = x + gm * y
    x1_ref[0] = x1
    hf = _modulate(x1, nf, shf, scf)
    _store_token_tiles(hf_ref, hf)
    logits = _dot3(wr_t, hf, NT)
    m = jnp.max(logits, axis=0, keepdims=True)
    e = jnp.exp(logits - m)
    aff_ref[0] = e / jnp.sum(e, axis=0, keepdims=True)


def _merge0_kernel(att_ref, four_ref, w_ref, x_ref, gm_ref, nf_ref, shf_ref, scf_ref, wr_ref,
                   x1_ref, hf_ref, aff_ref):
    y = jnp.dot(att_ref[0], w_ref[:V_W, :], preferred_element_type=F32)
    y = y + jnp.dot(four_ref[0], w_ref[V_W:, :], preferred_element_type=F32)
    _residual_router(x_ref[0], y, gm_ref[0], nf_ref[...], shf_ref[0], scf_ref[0], wr_ref[...],
                     x1_ref, hf_ref, aff_ref)


def _router_specs(tm):
    vec = pl.BlockSpec((1, 1, D), lambda bb, i: (bb, 0, 0))
    row = pl.BlockSpec((1, tm, D), lambda bb, i: (bb, i, 0))
    in_specs = [vec, pl.BlockSpec((1, D), lambda bb, i: (0, 0)), vec, vec,
                pl.BlockSpec((N_EXP, D), lambda bb, i: (0, 0))]
    tiles = _tile_spec(tm)
    out_specs = [row, tiles, pl.BlockSpec((1, N_EXP, tm), lambda bb, i: (bb, 0, i))]
    return row, in_specs, out_specs


def _tile_spec(tm):
    return pl.BlockSpec((1, tm * SUB, LANES), lambda bb, i: (bb, i, 0))


def _router_out_shape(b, n):
    return [jax.ShapeDtypeStruct((b, n, D), F32), jax.ShapeDtypeStruct((b, n * SUB, LANES), F32),
            jax.ShapeDtypeStruct((b, N_EXP, n), F32)]


def _merge0(att, four, w_bf, x, gm, nf, shf, scf, wr_t):
    b, n, _ = x.shape
    tm = 512
    row, r_in, r_out = _router_specs(tm)
    return pl.pallas_call(
        _merge0_kernel,
        grid=(b, n // tm),
        in_specs=[
            pl.BlockSpec((1, tm, V_W), lambda bb, i: (bb, i, 0)),
            pl.BlockSpec((1, tm, FOUR_W), lambda bb, i: (bb, i, 0)),
            pl.BlockSpec((D, D), lambda bb, i: (0, 0)),
            row,
        ] + r_in,
        out_specs=r_out,
        out_shape=_router_out_shape(b, n),
        compiler_params=_params(("parallel", "arbitrary"), 48),
        name="merge0",
    )(att, four, w_bf, x, gm, nf, shf, scf, wr_t)


def _proj1_kernel(x_ref, moe_ref, nw_ref, sh_ref, sc_ref, w_ref, bg_ref, z_ref, x2_ref):
    x = x_ref[0] + _load_token_tiles(moe_ref, bg_ref.shape[1])
    x2_ref[0] = x
    h = _modulate(x, nw_ref[...], sh_ref[0], sc_ref[0]).astype(BF)
    bg_ref[0] = jnp.dot(h, w_ref[:, :D], preferred_element_type=F32)
    cg = jnp.dot(h, w_ref[:, D:2 * D], preferred_element_type=F32)
    u = jnp.dot(h, w_ref[:, 2 * D:], preferred_element_type=F32)
    z_ref[0] = cg * u


def _proj1(x, moe_tiles, nw, sh, sc, w_bf):
    b, n, _ = x.shape
    tm = 512
    vec = pl.BlockSpec((1, 1, D), lambda bb, i: (bb, 0, 0))
    row = pl.BlockSpec((1, tm, D), lambda bb, i: (bb, i, 0))
    rows = jax.ShapeDtypeStruct((b, n, D), F32)
    return pl.pallas_call(
        _proj1_kernel,
        grid=(b, n // tm),
        in_specs=[row, _tile_spec(tm), pl.BlockSpec((1, D), lambda bb, i: (0, 0)), vec, vec,
                  pl.BlockSpec((D, 3 * D), lambda bb, i: (0, 0))],
        out_specs=[row, row, row],
        out_shape=[rows, rows, rows],
        compiler_params=_params(("parallel", "arbitrary"), 48),
        name="proj1",
    )(x, moe_tiles, nw, sh, sc, w_bf)


def _merge1_kernel(bg_ref, z_ref, zp_ref, zn_ref, cw_ref, w_ref, x_ref, gm_ref, nf_ref, shf_ref,
                   scf_ref, wr_ref, x1_ref, hf_ref, aff_ref):
    i = pl.program_id(1)
    last = pl.num_programs(1) - 1
    z = z_ref[0]
    tm = z.shape[0]
    rowid = lax.broadcasted_iota(I32, z.shape, 0)
    prev_row = jnp.where(i > 0, zp_ref[0, 7:8, :], 0.0)
    next_row = jnp.where(i < last, zn_ref[0, 0:1, :], 0.0)
    z_up = jnp.where(rowid == 0, prev_row, pltpu.roll(z, 1, axis=0))
    z_dn = jnp.where(rowid == tm - 1, next_row, pltpu.roll(z, tm - 1, axis=0))
    cw = cw_ref[...]
    conv = cw[0:1] * z_up + cw[1:2] * z + cw[2:3] * z_dn
    y = jnp.dot((bg_ref[0] * conv).astype(BF), w_ref[...], preferred_element_type=F32)
    _residual_router(x_ref[0], y, gm_ref[0], nf_ref[...], shf_ref[0], scf_ref[0], wr_ref[...],
                     x1_ref, hf_ref, aff_ref)


def _merge1(bg, z, conv_w8, w_bf, x, gm, nf, shf, scf, wr_t):
    b, n, _ = bg.shape
    tm = 512
    halo = 8
    per = tm // halo
    n_halo = n // halo
    row, r_in, r_out = _router_specs(tm)
    return pl.pallas_call(
        _merge1_kernel,
        grid=(b, n // tm),
        in_specs=[
            row, row,
            pl.BlockSpec((1, halo, D), lambda bb, i: (bb, jnp.maximum(i * per - 1, 0), 0)),
            pl.BlockSpec((1, halo, D), lambda bb, i: (bb, jnp.minimum((i + 1) * per, n_halo - 1), 0)),
            pl.BlockSpec((8, D), lambda bb, i: (0, 0)),
            pl.BlockSpec((D, D), lambda bb, i: (0, 0)),
            row,
        ] + r_in,
        out_specs=r_out,
        out_shape=_router_out_shape(b, n),
        compiler_params=_params(("parallel", "arbitrary"), 48),
        name="merge1",
    )(bg, z, z, z, conv_w8, w_bf, x, gm, nf, shf, scf, wr_t)


def _select_kernel(a_ref, idx_ref, gate_ref, cl_s, off_s, inc_s, hi_s, mid_s, lo_s, *, cap, chunks):
    a = a_ref[...]
    rows = a.shape[0]
    groups = rows // chunks
    per_batch = N_EXP * chunks
    assert chunks & (chunks - 1) == 0
    shift = chunks.bit_length() - 1

    def indicator(shape, row_dim):
        r = lax.broadcasted_iota(I32, shape, row_dim)
        g = lax.broadcasted_iota(I32, shape, 1 - row_dim)
        return jnp.where((r >> shift) == g, 1.0, 0.0).astype(BF)

    member = indicator((groups, rows), 1)
    spread = indicator((rows, groups), 0)
    ri = lax.broadcasted_iota(I32, (per_batch, per_batch), 0)
    ci = lax.broadcasted_iota(I32, (per_batch, per_batch), 1)
    same_f = jnp.where((ri >> shift) == (ci >> shift), 1.0, 0.0)
    same = same_f.astype(BF)
    lower = (same_f * jnp.where(ci < ri, 1.0, 0.0)).astype(BF)
    li = lax.broadcasted_iota(I32, (LANES, LANES), 0)
    lj = lax.broadcasted_iota(I32, (LANES, LANES), 1)
    incl = jnp.where(li <= lj, 1.0, 0.0).astype(BF)

    def bcast(col):
        return jnp.broadcast_to(col, (per_batch, LANES))

    def prefix(maskf):
        cl = jnp.dot(maskf.astype(BF), incl, preferred_element_type=F32)
        tot = bcast(cl[:, LANES - 1:LANES])
        off = jnp.dot(lower, tot.astype(BF), preferred_element_type=F32)
        return cl, off, tot

    min_normal = 0x00800000

    def search(step, t):
        cand = t | jnp.left_shift(jnp.int32(1), 30 - step)
        mask = jnp.where(a >= lax.bitcast_convert_type(cand, F32), 1.0, 0.0).astype(BF)
        part = jnp.dot(member, mask, preferred_element_type=F32)
        cnt = jnp.sum(part, axis=-1, keepdims=True)
        ok = jnp.broadcast_to(jnp.where(cnt >= cap, 1.0, 0.0), (groups, LANES)).astype(BF)
        ok_rows = jnp.dot(spread, ok, preferred_element_type=F32)
        return jnp.where(ok_rows > 0.5, jnp.where(cand >= min_normal, cand, t), t)

    thr = lax.fori_loop(0, 31, search, jnp.zeros((rows, LANES), I32))
    thr_f = lax.bitcast_convert_type(thr, F32)
    nxt_f = lax.bitcast_convert_type(jnp.maximum(thr + 1, min_normal), F32)
    gtf = jnp.where(a >= nxt_f, 1.0, 0.0)
    eqf = jnp.where(a >= thr_f, 1.0, 0.0) - gtf
    for bb in range(rows // per_batch):
        sl = slice(bb * per_batch, (bb + 1) * per_batch)
        gt_b, eq_b = gtf[sl], eqf[sl]
        n_gt = jnp.dot(same, bcast(jnp.sum(gt_b, axis=-1, keepdims=True)).astype(BF),
                       preferred_element_type=F32)
        cl_eq, off_eq, _ = prefix(eq_b)
        sel = gt_b + eq_b * jnp.where(cl_eq + off_eq <= cap - n_gt, 1.0, 0.0)
        cl, off, tot = prefix(sel)
        cl_s[sl, :] = cl.astype(BF)
        off_s[sl, :] = off
        inc_s[sl, :] = off + tot
        a_b = a[sl]
        a_hi = a_b.astype(BF)
        r1 = a_b - a_hi.astype(F32)
        a_mid = r1.astype(BF)
        hi_s[sl, :] = a_hi
        mid_s[sl, :] = a_mid
        lo_s[sl, :] = (r1 - a_mid.astype(F32)).astype(BF)

    slot = lax.broadcasted_iota(I32, (chunks, cap), 1).astype(F32)
    chunk_id = lax.broadcasted_iota(I32, (chunks, cap), 0).astype(F32)
    lane_id = lax.broadcasted_iota(I32, (LANES, cap), 0).astype(F32)
    tn = (((0,), (0,)), ((), ()))
    reps = cap // LANES

    def widen(x):
        return jnp.concatenate([x] * reps, axis=1)

    def per_group(g, carry):
        win = pl.ds(pl.multiple_of(g * chunks, chunks), chunks)
        inc_g = widen(inc_s[win, :])
        off_g = widen(off_s[win, :])
        chunk_of = jnp.sum(jnp.where(inc_g <= slot, 1.0, 0.0), axis=0, keepdims=True)
        pick_f = jnp.where(chunk_id == chunk_of, 1.0, 0.0)
        before = jnp.sum(pick_f * off_g, axis=0, keepdims=True)
        pick = pick_f.astype(BF)
        counts = lax.dot_general(cl_s[win, :], pick, tn, preferred_element_type=F32)
        lane_of = jnp.sum(jnp.where(counts <= slot[0:1] - before, 1.0, 0.0), axis=0, keepdims=True)
        aff = (lax.dot_general(hi_s[win, :], pick, tn, preferred_element_type=F32)
               + lax.dot_general(mid_s[win, :], pick, tn, preferred_element_type=F32)
               + lax.dot_general(lo_s[win, :], pick, tn, preferred_element_type=F32))
        gate = jnp.sum(jnp.where(lane_id == lane_of, aff, 0.0), axis=0, keepdims=True)
        idx_ref[pl.ds(g, 1), :] = ((chunk_of * float(LANES) + lane_of) * float(SUB)).astype(I32)
        gate_ref[pl.ds(g, 1), :] = gate
        return carry

    lax.fori_loop(0, groups, per_group, 0, unroll=2)


def _select(aff_rows, cap, chunks):
    rows = aff_rows.shape[0]
    groups = rows // chunks
    whole = lambda shape: pl.BlockSpec(shape, lambda i: (0, 0))
    return pl.pallas_call(
        functools.partial(_select_kernel, cap=cap, chunks=chunks),
        grid=(1,),
        in_specs=[whole((rows, LANES))],
        out_specs=[whole((groups, cap)), whole((groups, cap))],
        out_shape=[jax.ShapeDtypeStruct((groups, cap), I32), jax.ShapeDtypeStruct((groups, cap), F32)],
        scratch_shapes=[pltpu.VMEM((rows, LANES), BF), pltpu.VMEM((rows, LANES), F32),
                        pltpu.VMEM((rows, LANES), F32), pltpu.VMEM((rows, LANES), BF),
                        pltpu.VMEM((rows, LANES), BF), pltpu.VMEM((rows, LANES), BF)],
        compiler_params=_params(("arbitrary",), 48),
        name="select",
    )(aff_rows)


def _gather_kernel(idx_ref, h_ref, o_ref, rows_ref):
    slots = idx_ref.shape[2]

    def body(c, carry):
        t = pl.multiple_of(idx_ref[0, 0, c], SUB)
        rows_ref[0, pl.ds(pl.multiple_of(c * SUB, SUB), SUB), :] = h_ref[0, pl.ds(t, SUB), :]
        return carry

    lax.fori_loop(0, slots, body, 0, unroll=8)
    o_ref[0, 0] = _load_token_tiles(rows_ref, slots).astype(BF)


def _gather(idx, hf_tiles, cap):
    b, rows, _ = hf_tiles.shape
    steps = N_EXP // EXPERTS_PER_STEP
    slots = EXPERTS_PER_STEP * cap
    xs = pl.pallas_call(
        _gather_kernel,
        grid=(b, steps),
        in_specs=[
            pl.BlockSpec((1, 1, slots), lambda bb, s: (bb * steps + s, 0, 0), memory_space=pltpu.SMEM),
            pl.BlockSpec((1, rows, LANES), lambda bb, s: (bb, 0, 0)),
        ],
        out_specs=pl.BlockSpec((1, 1, slots, D), lambda bb, s: (bb, s, 0, 0)),
        out_shape=jax.ShapeDtypeStruct((b, steps, slots, D), BF),
        scratch_shapes=[pltpu.VMEM((1, slots * SUB, LANES), F32)],
        compiler_params=_params(("parallel", "arbitrary"), 48),
        name="gather",
    )(idx, hf_tiles)
    return xs.reshape(b, N_EXP, cap, D)


def _ffn_kernel(x_ref, gate_ref, gf_ref, wg_ref, wu_ref, wd_ref, o_ref, wg_s, wu_s, wd_s):
    @pl.when(pl.program_id(1) == 0)
    def _():
        wg_s[...] = wg_ref[...].astype(BF)
        wu_s[...] = wu_ref[...].astype(BF)
        wd_s[...] = wd_ref[...].astype(BF)

    x = x_ref[0, 0]
    dexp = wg_s.shape[1]
    step = 256
    y = None
    for c0 in range(0, dexp, step):
        g = jnp.dot(x, wg_s[:, c0:c0 + step], preferred_element_type=F32)
        u = jnp.dot(x, wu_s[:, c0:c0 + step], preferred_element_type=F32)
        act = (g * (1.0 / (1.0 + jnp.exp(-g))) * u).astype(BF)
        part = jnp.dot(act, wd_s[c0:c0 + step, :], preferred_element_type=F32)
        y = part if y is None else y + part
    cap = x.shape[0]
    gate_col = jnp.broadcast_to(gate_ref[0], (LANES, cap)).T
    y = y * jnp.concatenate([gate_col] * SUB, axis=1) * gf_ref[0]
    _store_token_tiles(o_ref.at[0], y)


def _ffn(xs, gate, gf, w_gate, w_up, w_down, layer):
    b, _, cap, _ = xs.shape
    dexp = w_gate.shape[-1]
    xspec = pl.BlockSpec((1, 1, cap, D), lambda e, bb: (bb, e, 0, 0))
    gspec = pl.BlockSpec((1, 1, cap), lambda e, bb: (bb * N_EXP + e, 0, 0))
    gfspec = pl.BlockSpec((1, 1, D), lambda e, bb: (bb, 0, 0))
    yspec = pl.BlockSpec((1, 1, cap * SUB, LANES), lambda e, bb: (bb, e, 0, 0))

    def wspec(shape, switch_at):
        def index(e, bb):
            return (layer, jnp.minimum(e + jnp.where(bb >= switch_at, 1, 0), N_EXP - 1), 0, 0)
        return pl.BlockSpec((None, None) + shape, index)

    assert b >= 2
    stagger = [min(j, b - 1) for j in (1, 2, 3)]
    return pl.pallas_call(
        _ffn_kernel,
        grid=(N_EXP, b),
        in_specs=[xspec, gspec, gfspec, wspec((D, dexp), stagger[0]), wspec((D, dexp), stagger[1]),
                  wspec((dexp, D), stagger[2])],
        out_specs=yspec,
        out_shape=jax.ShapeDtypeStruct((b, N_EXP, cap * SUB, LANES), F32),
        scratch_shapes=[pltpu.VMEM((D, dexp), BF), pltpu.VMEM((D, dexp), BF), pltpu.VMEM((dexp, D), BF)],
        compiler_params=_params(("arbitrary", "arbitrary"), 56),
        name="expert_ffn",
    )(xs, gate, gf, w_gate, w_up, w_down)


def _scatter_kernel(idx_ref, y_ref, *rest):
    if len(rest) == 1:
        (o_ref,), x_ref = rest, None
        acc_ref = o_ref
    else:
        x_ref, o_ref, acc_ref = rest
    slots = idx_ref.shape[2]
    expert_steps = N_EXP // EXPERTS_PER_STEP
    step = pl.program_id(1)

    @pl.when(step == 0)
    def _():
        acc_ref[...] = jnp.zeros(acc_ref.shape, F32)

    group = 16

    def body(g, carry):
        c0 = g * group
        ts = [pl.multiple_of(idx_ref[0, 0, c0 + j], SUB) for j in range(group)]
        new = []
        for j in range(group):
            y = y_ref[0, 0, pl.ds(pl.multiple_of((c0 + j) * SUB, SUB), SUB), :]
            new.append(acc_ref[0, pl.ds(ts[j], SUB), :] + y)
        for j in range(group):
            acc_ref[0, pl.ds(ts[j], SUB), :] = new[j]
        return carry

    @pl.when(step < expert_steps)
    def _():
        lax.fori_loop(0, slots // group, body, 0)

    if x_ref is not None:
        @pl.when(step >= expert_steps)
        def _():
            rows = o_ref.shape[1]
            base = (step - expert_steps) * (rows * SUB)
            o_ref[0] = x_ref[0] + jnp.concatenate(
                [acc_ref[0, pl.ds(base + j, rows, stride=SUB), :] for j in range(SUB)], axis=1)


def _scatter(idx, y_tiles, n, x=None):
    b = y_tiles.shape[0]
    slots = idx.shape[2]
    assert (slots // EXPERTS_PER_STEP) % 16 == 0
    rows = n * SUB
    tm = 512
    steps = N_EXP // EXPERTS_PER_STEP
    extra = n // tm if x is not None else 0
    expert_step = lambda s: jnp.minimum(s, steps - 1)
    in_specs = [
        pl.BlockSpec((1, 1, slots), lambda bb, s: (bb * steps + expert_step(s), 0, 0),
                     memory_space=pltpu.SMEM),
        pl.BlockSpec((1, 1, slots * SUB, LANES), lambda bb, s: (bb, expert_step(s), 0, 0)),
    ]
    args = [idx, y_tiles.reshape(b, steps, slots * SUB, LANES)]
    if x is not None:
        piece = pl.BlockSpec((1, tm, D), lambda bb, s: (bb, jnp.maximum(s - steps, 0), 0))
        in_specs.append(piece)
        args.append(x)
        out_spec, out_shape = piece, jax.ShapeDtypeStruct((b, n, D), F32)
        scratch = [pltpu.VMEM((1, rows, LANES), F32)]
    else:
        out_spec = pl.BlockSpec((1, rows, LANES), lambda bb, s: (bb, 0, 0))
        out_shape, scratch = jax.ShapeDtypeStruct((b, rows, LANES), F32), []
    return pl.pallas_call(
        _scatter_kernel,
        grid=(b, steps + extra),
        in_specs=in_specs,
        out_specs=out_spec,
        out_shape=out_shape,
        scratch_shapes=scratch,
        compiler_params=_params(("parallel", "arbitrary"), 56),
        name="scatter_add",
    )(*args)


def _moe(hf_tiles, aff_t, gf, w_gate, w_up, w_down, layer, x=None):
    b, _, n = aff_t.shape
    cap = CAP_FACTOR * n // N_EXP
    chunks = n // LANES
    idx, gate = _select(aff_t.reshape(b * N_EXP * chunks, LANES), cap, chunks)
    idx = idx.reshape(b * N_EXP // EXPERTS_PER_STEP, 1, EXPERTS_PER_STEP * cap)
    gate = gate.reshape(b * N_EXP, 1, cap)
    xs = _gather(idx, hf_tiles, cap)
    y_tiles = _ffn(xs, gate, gf, w_gate, w_up, w_down, layer)
    return _scatter(idx, y_tiles, n, x)


def _rope_tables(n_lat, n_ctx):
    rows = n_lat // GRID_W
    r = jnp.repeat(jnp.arange(rows, dtype=F32), GRID_W)
    col = jnp.tile(jnp.arange(GRID_W, dtype=F32), rows)
    n_freq = HEAD_DIM // 4
    inv = ROPE_BASE ** (-jnp.arange(n_freq, dtype=F32) / n_freq)
    ar = r[:, None] * inv
    ac = col[:, None] * inv
    ang = jnp.concatenate([ar, ar, ac, ac], axis=-1)
    sign = jnp.where((jnp.arange(HEAD_DIM) % 32) < 16, -1.0, 1.0).astype(F32)
    cos = jnp.concatenate([jnp.cos(ang), jnp.ones((n_ctx, HEAD_DIM), F32)], axis=0)
    sin_s = jnp.concatenate([jnp.sin(ang) * sign, jnp.zeros((n_ctx, HEAD_DIM), F32)], axis=0)
    return jnp.tile(cos, (1, 2)), jnp.tile(sin_s, (1, 2))


def _dft_tables(n_lat):
    c = jnp.arange(FOUR_G, dtype=I32)
    ang_c = (2.0 * math.pi / FOUR_G) * ((c[:, None] * c[None, :]) % FOUR_G).astype(F32)
    eye = jnp.eye(FOUR_W // FOUR_G, dtype=F32)
    bd = jnp.concatenate([jnp.kron(eye, jnp.cos(ang_c)), jnp.kron(eye, jnp.sin(ang_c))], axis=1)
    bd = (bd * FOUR_G ** -0.5).astype(BF)
    tr = 256
    n = jnp.arange(n_lat, dtype=I32)[None, :]
    r = jnp.arange(tr, dtype=I32)[:, None]
    i = jnp.arange(n_lat // tr, dtype=I32)[:, None]
    row_ang = (2.0 * math.pi / n_lat) * ((r * n) % n_lat).astype(F32)
    tile_ang = (2.0 * math.pi / n_lat) * ((i * tr * n) % n_lat).astype(F32)
    scale = n_lat ** -0.5
    tables = (jnp.cos(row_ang) * scale, jnp.sin(row_ang) * scale,
              jnp.cos(tile_ang)[:, None, :], jnp.sin(tile_ang)[:, None, :])
    return bd, tables


def kernel(x, c, ctx, c_ctx, ada_w, ada_b, norm_mix, norm_ffn, attn_w_in, attn_q_norm, attn_k_norm,
           lam_q1, lam_k1, lam_q2, lam_k2, attn_subln, attn_w_out, conv_w_in, conv_w, conv_w_out,
           router_w, moe_w_gate, moe_w_up, moe_w_down):
    b, n, _ = x.shape
    n_ctx = ctx.shape[1]
    assert x.shape[2] == D and n % 512 == 0 and n_ctx % 256 == 0

    cond8 = jnp.concatenate([c, c_ctx[None, :], jnp.zeros((8 - b - 1, D), F32)], axis=0)
    ada = _ada(cond8, ada_w, ada_b)

    def mods(layer):
        m = ada[layer].reshape(8, 6, D)
        return [m[:, j] for j in range(6)]

    vec = lambda t: t[:b].reshape(b, 1, D)

    sh_m, sc_m, g_m, sh_f, sc_f, g_f = mods(0)
    both = lambda t: jnp.stack([t[:b], jnp.broadcast_to(t[b], (b, D))], axis=1).reshape(b, 2, 1, D)
    cos, sin_s = _rope_tables(n, n_ctx)
    bd, dft_tabs = _dft_tables(n)
    tile2 = lambda t: jnp.tile(t.reshape(1, HEAD_DIM), (1, 2))
    q, k, v, f = _proj0(x, ctx, norm_mix[0].reshape(1, D), both(sh_m), both(sc_m),
                        attn_w_in[0].astype(BF), attn_w_in[0][:, :QK_W].T.astype(BF), cos, sin_s,
                        tile2(attn_q_norm[0]).reshape(V_DIM, 1), tile2(attn_k_norm[0]))
    lam_init = 0.8 - 0.6 * math.exp(-0.3 * 0)
    lamv = jnp.zeros((8, LANES), F32).at[:4, :HEAD_DIM].set(
        jnp.stack([lam_q1[0], lam_k1[0], lam_q2[0], lam_k2[0]]))
    score_bound = (1.01 * HEAD_DIM * Q_SCALE * jnp.max(jnp.abs(attn_q_norm[0]))
                   * jnp.max(jnp.abs(attn_k_norm[0])) + 0.1).reshape(1)
    att = _attention(score_bound, q, k, v, lamv, attn_subln[0].reshape(V_DIM, 1), n, lam_init)
    four = _fourier(f, bd, dft_tabs, n)
    x1, hf, aff_t = _merge0(att, four, attn_w_out[0].astype(BF), x, vec(g_m),
                            norm_ffn[0].reshape(1, D), vec(sh_f), vec(sc_f), router_w[0].T)
    moe0 = _moe(hf, aff_t, vec(g_f), moe_w_gate, moe_w_up, moe_w_down, 0)

    sh_m, sc_m, g_m, sh_f, sc_f, g_f = mods(1)
    bg, z, x2 = _proj1(x1, moe0, norm_mix[1].reshape(1, D), vec(sh_m), vec(sc_m),
                       conv_w_in[0].astype(BF))
    conv_w8 = jnp.zeros((8, D), F32).at[:3].set(conv_w[0])
    x3, hf, aff_t = _merge1(bg, z, conv_w8, conv_w_out[0].astype(BF), x2, vec(g_m),
                            norm_ffn[1].reshape(1, D), vec(sh_f), vec(sc_f), router_w[1].T)
    return _moe(hf, aff_t, vec(g_f), moe_w_gate, moe_w_up, moe_w_down, 1, x3)
```

```python
import functools
import math

import jax
import jax.numpy as jnp
from jax import lax
from jax.experimental import pallas as pl
from jax.experimental.pallas import tpu as pltpu

BF = jnp.bfloat16
F32 = jnp.float32
I32 = jnp.int32

D = 1024
GRID_W = 64
N_HEADS = 6
HEAD_DIM = 64
V_DIM = 2 * HEAD_DIM
QK_W = N_HEADS * 2 * HEAD_DIM
V_W = N_HEADS * V_DIM
FOUR_W = 256
FOUR_G = 64
IN_W = 2 * QK_W + V_W + FOUR_W
N_EXP = 16
EXPERTS_PER_STEP = 4
CAP_FACTOR = 2
ROPE_BASE = 10000.0
EPS = 1e-6
Q_SCALE = HEAD_DIM ** -0.5 * math.log2(math.e)
ATTN_SHIFT_LIMIT = 40.0
LANES = 128
MIB = 1024 * 1024

NT = (((1,), (1,)), ((), ()))


def _params(sem, vmem_mib):
    return pltpu.CompilerParams(dimension_semantics=sem, vmem_limit_bytes=vmem_mib * MIB)


def _split2(x):
    hi = x.astype(BF)
    lo = (x - hi.astype(F32)).astype(BF)
    return hi, lo


def _dot3(a, b, dims=(((1,), (0,)), ((), ()))):
    ah, al = _split2(a)
    bh, bl = _split2(b)
    dg = functools.partial(lax.dot_general, dimension_numbers=dims, preferred_element_type=F32)
    return dg(ah, bh) + dg(ah, bl) + dg(al, bh)


def _modulate(x, nw, shift, scale):
    ms = jnp.mean(x * x, axis=-1, keepdims=True)
    return (x * lax.rsqrt(ms + EPS) * nw) * (1.0 + scale) + shift


def _ada_kernel(c_ref, w_ref, b_ref, o_ref):
    cv = c_ref[...]
    s = cv * (1.0 / (1.0 + jnp.exp(-cv)))
    o_ref[0] = _dot3(s, w_ref[0]) + b_ref[0]


def _ada(cond8, ada_w, ada_b):
    depth = ada_w.shape[0]
    tn = 1536
    return pl.pallas_call(
        _ada_kernel,
        grid=(depth, 6 * D // tn),
        in_specs=[
            pl.BlockSpec((8, D), lambda l, j: (0, 0)),
            pl.BlockSpec((1, D, tn), lambda l, j: (l, 0, j)),
            pl.BlockSpec((1, 1, tn), lambda l, j: (l, 0, j)),
        ],
        out_specs=pl.BlockSpec((1, 8, tn), lambda l, j: (l, 0, j)),
        out_shape=jax.ShapeDtypeStruct((depth, 8, 6 * D), F32),
        compiler_params=_params(("arbitrary", "arbitrary"), 40),
        name="ada",
    )(cond8, ada_w, ada_b.reshape(depth, 1, 6 * D))


def _norm_rope_t(xt, wn_col, cos_t, sin_t, out_scale):
    q4 = HEAD_DIM // 4
    halves = []
    for s in range(2):
        x = xt[s * HEAD_DIM:(s + 1) * HEAD_DIM]
        ms = jnp.sum(x * x, axis=0, keepdims=True) * (1.0 / HEAD_DIM)
        y = x * lax.rsqrt(ms + EPS) * wn_col[s * HEAD_DIM:(s + 1) * HEAD_DIM]
        rot = jnp.concatenate([y[q4:2 * q4], y[:q4], y[3 * q4:], y[2 * q4:3 * q4]], axis=0)
        halves.append(y * cos_t[s * HEAD_DIM:(s + 1) * HEAD_DIM] + rot * sin_t[s * HEAD_DIM:(s + 1) * HEAD_DIM])
    out = jnp.concatenate(halves, axis=0)
    return out if out_scale == 1.0 else out * out_scale


def _proj0_kernel(x_ref, ctx_ref, nw_ref, sh_ref, sc_ref, wf_ref, wt_ref, cost_ref, sint_ref,
                  qn_ref, kn_ref, q_ref, k_ref, v_ref, f_ref, *, lat_tiles):
    x = jnp.where(pl.program_id(1) < lat_tiles, x_ref[0], ctx_ref[0])
    h = _modulate(x, nw_ref[...], sh_ref[0, 0], sc_ref[0, 0]).astype(BF)
    def project_t(r0, width):
        return lax.dot_general(wt_ref[r0:r0 + width, :], h, NT, preferred_element_type=F32)

    cos_t = cost_ref[...]
    sin_t = sint_ref[...]
    for out_ref, wn_ref, r0, scale in ((q_ref, qn_ref, 0, Q_SCALE), (k_ref, kn_ref, QK_W, 1.0)):
        p_t = project_t(r0, QK_W)
        wn_col = wn_ref[...]
        for hh in range(N_HEADS):
            c0 = hh * V_DIM
            out_ref[0, c0:c0 + V_DIM, :] = _norm_rope_t(
                p_t[c0:c0 + V_DIM], wn_col, cos_t, sin_t, scale).astype(BF)
    v_ref[0] = project_t(2 * QK_W, V_W).astype(BF)
    f_ref[0] = jnp.dot(h, wf_ref[...], preferred_element_type=F32).astype(BF)


def _proj0(x, ctx, nw, sh2, sc2, w_four, w_qkv_t, cos_t, sin_t, qn_col, kn_col):
    b, n_lat, _ = x.shape
    n_ctx = ctx.shape[1]
    s_tot = n_lat + n_ctx
    tm = 256
    lat_tiles = n_lat // tm
    mod_spec = pl.BlockSpec((1, 1, 1, D), lambda bb, i: (bb, i // lat_tiles, 0, 0))
    full = lambda shape: pl.BlockSpec(shape, lambda bb, i: tuple(0 for _ in shape))
    feat = lambda w: pl.BlockSpec((1, w, tm), lambda bb, i: (bb, 0, i))
    feat_shape = lambda w: jax.ShapeDtypeStruct((b, w, s_tot), BF)
    return pl.pallas_call(
        functools.partial(_proj0_kernel, lat_tiles=lat_tiles),
        grid=(b, s_tot // tm),
        in_specs=[
            pl.BlockSpec((1, tm, D), lambda bb, i: (bb, jnp.minimum(i, lat_tiles - 1), 0)),
            pl.BlockSpec((1, tm, D), lambda bb, i: (bb, jnp.maximum(i - lat_tiles, 0), 0)),
            full((1, D)), mod_spec, mod_spec, full((D, FOUR_W)), full((2 * QK_W + V_W, D)),
            pl.BlockSpec((LANES, tm), lambda bb, i: (0, i)),
            pl.BlockSpec((LANES, tm), lambda bb, i: (0, i)),
            full((LANES, 1)), full((LANES, 1)),
        ],
        out_specs=[feat(QK_W), feat(QK_W), feat(V_W),
                   pl.BlockSpec((1, tm, FOUR_W), lambda bb, i: (bb, i, 0))],
        out_shape=[feat_shape(QK_W), feat_shape(QK_W), feat_shape(V_W),
                   jax.ShapeDtypeStruct((b, s_tot, FOUR_W), BF)],
        compiler_params=_params(("parallel", "arbitrary"), 48),
        name="proj0",
    )(x, ctx, nw, sh2, sc2, w_four, w_qkv_t, cos_t, sin_t, qn_col, kn_col)


def _attn_kernel(bound_ref, q_ref, kt_ref, vt_ref, lam_ref, sub_ref, o_ref, k_ref, m_ref, e_ref,
                 *, lam_init, kc):
    @pl.when(pl.program_id(2) == 0)
    def _():
        k_ref[...] = kt_ref[0].astype(F32).T.astype(BF)

    q = q_ref[0]
    feat = lax.broadcasted_iota(I32, q.shape, 0)
    zero = jnp.zeros_like(q)
    lv = lam_ref[...]
    t1 = jnp.sum(lv[0:1] * lv[1:2], axis=-1, keepdims=True)
    t2 = jnp.sum(lv[2:3] * lv[3:4], axis=-1, keepdims=True)
    lam = jnp.exp(t1) - jnp.exp(t2) + lam_init

    qs = (jnp.where(feat < HEAD_DIM, q, zero), jnp.where(feat < HEAD_DIM, zero, q))
    tq = q.shape[1]
    n_chunks = k_ref.shape[0] // kc

    def scores(h, c):
        return jnp.dot(k_ref[c * kc:(c + 1) * kc, :], qs[h], preferred_element_type=F32)

    def fold(x, op):
        return op(x.reshape(kc // 8, 8, tq), axis=0)

    bound = bound_ref[0]
    small = bound <= ATTN_SHIFT_LIMIT

    @pl.when(small)
    def _():
        m_ref[...] = jnp.zeros(m_ref.shape, F32) + bound

    @pl.when(jnp.logical_not(small))
    def _():
        for h in range(2):
            m = jnp.full((8, tq), -jnp.inf, F32)
            for c in range(n_chunks):
                m = jnp.maximum(m, fold(scores(h, c), jnp.max))
            m_ref[h] = jnp.broadcast_to(jnp.max(m, axis=0, keepdims=True), (8, tq))

    ms = [m_ref[h][0:1, :] for h in range(2)]
    ls = [jnp.zeros((8, tq), F32) for _ in range(2)]
    for c in range(n_chunks):
        for h in range(2):
            e = jnp.exp2(scores(h, c) - ms[h])
            ls[h] = ls[h] + fold(e, jnp.sum)
            e_ref[h, c] = e.astype(BF)
    l0, l1 = [jnp.sum(l, axis=0, keepdims=True) for l in ls]

    beta = (lam * l0 / l1).astype(BF)
    acc = jnp.zeros((V_DIM, tq), F32)
    for c in range(n_chunks):
        a = e_ref[0, c] - beta * e_ref[1, c]
        acc = acc + jnp.dot(vt_ref[0, :, c * kc:(c + 1) * kc], a, preferred_element_type=F32)
    o = acc * (1.0 / l0)
    ms = jnp.mean(o * o, axis=0, keepdims=True)
    o = o * lax.rsqrt(ms + EPS) * sub_ref[...] * (1.0 - lam_init)
    o_ref[0] = o.T.astype(BF)


def _attention(bound, q, k, v, lamv, subln, n_lat, lam_init):
    b, _, s_tot = k.shape
    tq = 1024
    kc = 256
    assert s_tot % kc == 0
    kv_spec = pl.BlockSpec((1, V_DIM, s_tot), lambda bb, hh, i: (bb, hh, 0))
    return pl.pallas_call(
        functools.partial(_attn_kernel, lam_init=lam_init, kc=kc),
        grid=(b, N_HEADS, n_lat // tq),
        in_specs=[
            pl.BlockSpec(memory_space=pltpu.SMEM),
            pl.BlockSpec((1, V_DIM, tq), lambda bb, hh, i: (bb, hh, i)),
            kv_spec, kv_spec,
            pl.BlockSpec((8, LANES), lambda bb, hh, i: (0, 0)),
            pl.BlockSpec((V_DIM, 1), lambda bb, hh, i: (0, 0)),
        ],
        out_specs=pl.BlockSpec((1, tq, V_DIM), lambda bb, hh, i: (bb, i, hh)),
        out_shape=jax.ShapeDtypeStruct((b, n_lat, V_W), BF),
        scratch_shapes=[pltpu.VMEM((s_tot, V_DIM), BF),
                        pltpu.VMEM((2, 8, tq), F32),
                        pltpu.VMEM((2, s_tot // kc, kc, tq), BF)],
        compiler_params=_params(("parallel", "parallel", "arbitrary"), 48),
        name="diff_attn",
    )(bound, q, k, v, lamv, subln)


def _fourier_kernel(f_ref, bd_ref, cb_ref, sb_ref, ca_ref, sa_ref, o_ref, g_ref):
    nb, n_lat, _ = f_ref.shape

    @pl.when(pl.program_id(0) == 0)
    def _():
        rows = 1024
        for bb in range(nb):
            for r in range(n_lat // rows):
                fc = jnp.dot(f_ref[bb, r * rows:(r + 1) * rows, :], bd_ref[...],
                             preferred_element_type=F32)
                g_ref[r * rows:(r + 1) * rows, bb * FOUR_W:(bb + 1) * FOUR_W] = fc[:, :FOUR_W].astype(BF)
                g_ref[n_lat + r * rows:n_lat + (r + 1) * rows,
                      bb * FOUR_W:(bb + 1) * FOUR_W] = fc[:, FOUR_W:].astype(BF)

    ca = ca_ref[0]
    sa = sa_ref[0]
    cb = cb_ref[...]
    sb = sb_ref[...]
    w_cos = (cb * ca - sb * sa).astype(BF)
    w_sin = (sb * ca + cb * sa).astype(BF)
    y = (jnp.dot(w_cos, g_ref[:n_lat, :], preferred_element_type=F32)
         - jnp.dot(w_sin, g_ref[n_lat:, :], preferred_element_type=F32))
    for bb in range(nb):
        o_ref[bb] = y[:, bb * FOUR_W:(bb + 1) * FOUR_W].astype(BF)


def _fourier(f, bd, tables, n_lat):
    b = f.shape[0]
    cos_row, sin_row, cos_tile, sin_tile = tables
    tr = cos_row.shape[0]
    row_tab = pl.BlockSpec((tr, n_lat), lambda i: (0, 0))
    tile_tab = pl.BlockSpec((1, 1, n_lat), lambda i: (i, 0, 0))
    return pl.pallas_call(
        _fourier_kernel,
        grid=(n_lat // tr,),
        in_specs=[
            pl.BlockSpec((b, n_lat, FOUR_W), lambda i: (0, 0, 0)),
            pl.BlockSpec((FOUR_W, 2 * FOUR_W), lambda i: (0, 0)),
            row_tab, row_tab, tile_tab, tile_tab,
        ],
        out_specs=pl.BlockSpec((b, tr, FOUR_W), lambda i: (0, i, 0)),
        out_shape=jax.ShapeDtypeStruct((b, n_lat, FOUR_W), BF),
        scratch_shapes=[pltpu.VMEM((2 * n_lat, b * FOUR_W), BF)],
        compiler_params=_params(("arbitrary",), 56),
        name="fourier",
    )(f, bd, cos_row, sin_row, cos_tile, sin_tile)


SUB = D // LANES


def _store_token_tiles(ref, val):
    rows = val.shape[0]
    for j in range(SUB):
        ref[0, pl.ds(j, rows, stride=SUB), :] = val[:, j * LANES:(j + 1) * LANES]


def _load_token_tiles(ref, rows):
    return jnp.concatenate([ref[0, pl.ds(j, rows, stride=SUB), :] for j in range(SUB)], axis=1)


def _residual_router(x, y, gm, nf, shf, scf, wr_t, x1_ref, hf_ref, aff_ref):
    x1 = x + gm * y
    x1_ref[0] = x1
    hf = _modulate(x1, nf, shf, scf)
    _store_token_tiles(hf_ref, hf)
    logits = _dot3(wr_t, hf, NT)
    m = jnp.max(logits, axis=0, keepdims=True)
    e = jnp.exp(logits - m)
    aff_ref[0] = e / jnp.sum(e, axis=0, keepdims=True)


def _merge0_kernel(att_ref, four_ref, w_ref, x_ref, gm_ref, nf_ref, shf_ref, scf_ref, wr_ref,
                   x1_ref, hf_ref, aff_ref):
    y = jnp.dot(att_ref[0], w_ref[:V_W, :], preferred_element_type=F32)
    y = y + jnp.dot(four_ref[0], w_ref[V_W:, :], preferred_element_type=F32)
    _residual_router(x_ref[0], y, gm_ref[0], nf_ref[...], shf_ref[0], scf_ref[0], wr_ref[...],
                     x1_ref, hf_ref, aff_ref)


def _router_specs(tm):
    vec = pl.BlockSpec((1, 1, D), lambda bb, i: (bb, 0, 0))
    row = pl.BlockSpec((1, tm, D), lambda bb, i: (bb, i, 0))
    in_specs = [vec, pl.BlockSpec((1, D), lambda bb, i: (0, 0)), vec, vec,
                pl.BlockSpec((N_EXP, D), lambda bb, i: (0, 0))]
    tiles = _tile_spec(tm)
    out_specs = [row, tiles, pl.BlockSpec((1, N_EXP, tm), lambda bb, i: (bb, 0, i))]
    return row, in_specs, out_specs


def _tile_spec(tm):
    return pl.BlockSpec((1, tm * SUB, LANES), lambda bb, i: (bb, i, 0))


def _router_out_shape(b, n):
    return [jax.ShapeDtypeStruct((b, n, D), F32), jax.ShapeDtypeStruct((b, n * SUB, LANES), F32),
            jax.ShapeDtypeStruct((b, N_EXP, n), F32)]


def _merge0(att, four, w_bf, x, gm, nf, shf, scf, wr_t):
    b, n, _ = x.shape
    tm = 512
    row, r_in, r_out = _router_specs(tm)
    return pl.pallas_call(
        _merge0_kernel,
        grid=(b, n // tm),
        in_specs=[
            pl.BlockSpec((1, tm, V_W), lambda bb, i: (bb, i, 0)),
            pl.BlockSpec((1, tm, FOUR_W), lambda bb, i: (bb, i, 0)),
            pl.BlockSpec((D, D), lambda bb, i: (0, 0)),
            row,
        ] + r_in,
        out_specs=r_out,
        out_shape=_router_out_shape(b, n),
        compiler_params=_params(("parallel", "arbitrary"), 48),
        name="merge0",
    )(att, four, w_bf, x, gm, nf, shf, scf, wr_t)


def _proj1_kernel(x_ref, moe_ref, nw_ref, sh_ref, sc_ref, w_ref, bg_ref, z_ref, x2_ref):
    x = x_ref[0] + _load_token_tiles(moe_ref, bg_ref.shape[1])
    x2_ref[0] = x
    h = _modulate(x, nw_ref[...], sh_ref[0], sc_ref[0]).astype(BF)
    bg_ref[0] = jnp.dot(h, w_ref[:, :D], preferred_element_type=F32)
    cg = jnp.dot(h, w_ref[:, D:2 * D], preferred_element_type=F32)
    u = jnp.dot(h, w_ref[:, 2 * D:], preferred_element_type=F32)
    z_ref[0] = cg * u


def _proj1(x, moe_tiles, nw, sh, sc, w_bf):
    b, n, _ = x.shape
    tm = 512
    vec = pl.BlockSpec((1, 1, D), lambda bb, i: (bb, 0, 0))
    row = pl.BlockSpec((1, tm, D), lambda bb, i: (bb, i, 0))
    rows = jax.ShapeDtypeStruct((b, n, D), F32)
    return pl.pallas_call(
        _proj1_kernel,
        grid=(b, n // tm),
        in_specs=[row, _tile_spec(tm), pl.BlockSpec((1, D), lambda bb, i: (0, 0)), vec, vec,
                  pl.BlockSpec((D, 3 * D), lambda bb, i: (0, 0))],
        out_specs=[row, row, row],
        out_shape=[rows, rows, rows],
        compiler_params=_params(("parallel", "arbitrary"), 48),
        name="proj1",
    )(x, moe_tiles, nw, sh, sc, w_bf)


def _merge1_kernel(bg_ref, z_ref, zp_ref, zn_ref, cw_ref, w_ref, x_ref, gm_ref, nf_ref, shf_ref,
                   scf_ref, wr_ref, x1_ref, hf_ref, aff_ref):
    i = pl.program_id(1)
    last = pl.num_programs(1) - 1
    z = z_ref[0]
    tm = z.shape[0]
    rowid = lax.broadcasted_iota(I32, z.shape, 0)
    prev_row = jnp.where(i > 0, zp_ref[0, 7:8, :], 0.0)
    next_row = jnp.where(i < last, zn_ref[0, 0:1, :], 0.0)
    z_up = jnp.where(rowid == 0, prev_row, pltpu.roll(z, 1, axis=0))
    z_dn = jnp.where(rowid == tm - 1, next_row, pltpu.roll(z, tm - 1, axis=0))
    cw = cw_ref[...]
    conv = cw[0:1] * z_up + cw[1:2] * z + cw[2:3] * z_dn
    y = jnp.dot((bg_ref[0] * conv).astype(BF), w_ref[...], preferred_element_type=F32)
    _residual_router(x_ref[0], y, gm_ref[0], nf_ref[...], shf_ref[0], scf_ref[0], wr_ref[...],
                     x1_ref, hf_ref, aff_ref)


def _merge1(bg, z, conv_w8, w_bf, x, gm, nf, shf, scf, wr_t):
    b, n, _ = bg.shape
    tm = 512
    halo = 8
    per = tm // halo
    n_halo = n // halo
    row, r_in, r_out = _router_specs(tm)
    return pl.pallas_call(
        _merge1_kernel,
        grid=(b, n // tm),
        in_specs=[
            row, row,
            pl.BlockSpec((1, halo, D), lambda bb, i: (bb, jnp.maximum(i * per - 1, 0), 0)),
            pl.BlockSpec((1, halo, D), lambda bb, i: (bb, jnp.minimum((i + 1) * per, n_halo - 1), 0)),
            pl.BlockSpec((8, D), lambda bb, i: (0, 0)),
            pl.BlockSpec((D, D), lambda bb, i: (0, 0)),
            row,
        ] + r_in,
        out_specs=r_out,
        out_shape=_router_out_shape(b, n),
        compiler_params=_params(("parallel", "arbitrary"), 48),
        name="merge1",
    )(bg, z, z, z, conv_w8, w_bf, x, gm, nf, shf, scf, wr_t)


def _select_kernel(a_ref, idx_ref, gate_ref, cl_s, off_s, inc_s, hi_s, mid_s, lo_s, *, cap, chunks):
    a = a_ref[...]
    rows = a.shape[0]
    groups = rows // chunks
    per_batch = N_EXP * chunks
    assert chunks & (chunks - 1) == 0
    shift = chunks.bit_length() - 1

    def indicator(shape, row_dim):
        r = lax.broadcasted_iota(I32, shape, row_dim)
        g = lax.broadcasted_iota(I32, shape, 1 - row_dim)
        return jnp.where((r >> shift) == g, 1.0, 0.0).astype(BF)

    member = indicator((groups, rows), 1)
    spread = indicator((rows, groups), 0)
    ri = lax.broadcasted_iota(I32, (per_batch, per_batch), 0)
    ci = lax.broadcasted_iota(I32, (per_batch, per_batch), 1)
    same_f = jnp.where((ri >> shift) == (ci >> shift), 1.0, 0.0)
    same = same_f.astype(BF)
    lower = (same_f * jnp.where(ci < ri, 1.0, 0.0)).astype(BF)
    li = lax.broadcasted_iota(I32, (LANES, LANES), 0)
    lj = lax.broadcasted_iota(I32, (LANES, LANES), 1)
    incl = jnp.where(li <= lj, 1.0, 0.0).astype(BF)

    def bcast(col):
        return jnp.broadcast_to(col, (per_batch, LANES))

    def prefix(maskf):
        cl = jnp.dot(maskf.astype(BF), incl, preferred_element_type=F32)
        tot = bcast(cl[:, LANES - 1:LANES])
        off = jnp.dot(lower, tot.astype(BF), preferred_element_type=F32)
        return cl, off, tot

    min_normal = 0x00800000

    def search(step, t):
        cand = t | jnp.left_shift(jnp.int32(1), 30 - step)
        mask = jnp.where(a >= lax.bitcast_convert_type(cand, F32), 1.0, 0.0).astype(BF)
        part = jnp.dot(member, mask, preferred_element_type=F32)
        cnt = jnp.sum(part, axis=-1, keepdims=True)
        ok = jnp.broadcast_to(jnp.where(cnt >= cap, 1.0, 0.0), (groups, LANES)).astype(BF)
        ok_rows = jnp.dot(spread, ok, preferred_element_type=F32)
        return jnp.where(ok_rows > 0.5, jnp.where(cand >= min_normal, cand, t), t)

    thr = lax.fori_loop(0, 31, search, jnp.zeros((rows, LANES), I32))
    thr_f = lax.bitcast_convert_type(thr, F32)
    nxt_f = lax.bitcast_convert_type(jnp.maximum(thr + 1, min_normal), F32)
    gtf = jnp.where(a >= nxt_f, 1.0, 0.0)
    eqf = jnp.where(a >= thr_f, 1.0, 0.0) - gtf
    for bb in range(rows // per_batch):
        sl = slice(bb * per_batch, (bb + 1) * per_batch)
        gt_b, eq_b = gtf[sl], eqf[sl]
        n_gt = jnp.dot(same, bcast(jnp.sum(gt_b, axis=-1, keepdims=True)).astype(BF),
                       preferred_element_type=F32)
        cl_eq, off_eq, _ = prefix(eq_b)
        sel = gt_b + eq_b * jnp.where(cl_eq + off_eq <= cap - n_gt, 1.0, 0.0)
        cl, off, tot = prefix(sel)
        cl_s[sl, :] = cl.astype(BF)
        off_s[sl, :] = off
        inc_s[sl, :] = off + tot
        a_b = a[sl]
        a_hi = a_b.astype(BF)
        r1 = a_b - a_hi.astype(F32)
        a_mid = r1.astype(BF)
        hi_s[sl, :] = a_hi
        mid_s[sl, :] = a_mid
        lo_s[sl, :] = (r1 - a_mid.astype(F32)).astype(BF)

    slot = lax.broadcasted_iota(I32, (chunks, cap), 1).astype(F32)
    chunk_id = lax.broadcasted_iota(I32, (chunks, cap), 0).astype(F32)
    lane_id = lax.broadcasted_iota(I32, (LANES, cap), 0).astype(F32)
    tn = (((0,), (0,)), ((), ()))
    reps = cap // LANES

    def widen(x):
        return jnp.concatenate([x] * reps, axis=1)

    def per_group(g, carry):
        win = pl.ds(pl.multiple_of(g * chunks, chunks), chunks)
        inc_g = widen(inc_s[win, :])
        off_g = widen(off_s[win, :])
        chunk_of = jnp.sum(jnp.where(inc_g <= slot, 1.0, 0.0), axis=0, keepdims=True)
        pick_f = jnp.where(chunk_id == chunk_of, 1.0, 0.0)
        before = jnp.sum(pick_f * off_g, axis=0, keepdims=True)
        pick = pick_f.astype(BF)
        counts = lax.dot_general(cl_s[win, :], pick, tn, preferred_element_type=F32)
        lane_of = jnp.sum(jnp.where(counts <= slot[0:1] - before, 1.0, 0.0), axis=0, keepdims=True)
        aff = (lax.dot_general(hi_s[win, :], pick, tn, preferred_element_type=F32)
               + lax.dot_general(mid_s[win, :], pick, tn, preferred_element_type=F32)
               + lax.dot_general(lo_s[win, :], pick, tn, preferred_element_type=F32))
        gate = jnp.sum(jnp.where(lane_id == lane_of, aff, 0.0), axis=0, keepdims=True)
        idx_ref[pl.ds(g, 1), :] = ((chunk_of * float(LANES) + lane_of) * float(SUB)).astype(I32)
        gate_ref[pl.ds(g, 1), :] = gate
        return carry

    lax.fori_loop(0, groups, per_group, 0, unroll=2)


def _select(aff_rows, cap, chunks):
    rows = aff_rows.shape[0]
    groups = rows // chunks
    whole = lambda shape: pl.BlockSpec(shape, lambda i: (0, 0))
    return pl.pallas_call(
        functools.partial(_select_kernel, cap=cap, chunks=chunks),
        grid=(1,),
        in_specs=[whole((rows, LANES))],
        out_specs=[whole((groups, cap)), whole((groups, cap))],
        out_shape=[jax.ShapeDtypeStruct((groups, cap), I32), jax.ShapeDtypeStruct((groups, cap), F32)],
        scratch_shapes=[pltpu.VMEM((rows, LANES), BF), pltpu.VMEM((rows, LANES), F32),
                        pltpu.VMEM((rows, LANES), F32), pltpu.VMEM((rows, LANES), BF),
                        pltpu.VMEM((rows, LANES), BF), pltpu.VMEM((rows, LANES), BF)],
        compiler_params=_params(("arbitrary",), 48),
        name="select",
    )(aff_rows)


def _gather_kernel(idx_ref, h_ref, o_ref, rows_ref):
    slots = idx_ref.shape[2]

    def body(c, carry):
        t = pl.multiple_of(idx_ref[0, 0, c], SUB)
        rows_ref[0, pl.ds(pl.multiple_of(c * SUB, SUB), SUB), :] = h_ref[0, pl.ds(t, SUB), :]
        return carry

    lax.fori_loop(0, slots, body, 0, unroll=8)
    o_ref[0, 0] = _load_token_tiles(rows_ref, slots).astype(BF)


def _gather(idx, hf_tiles, cap):
    b, rows, _ = hf_tiles.shape
    steps = N_EXP // EXPERTS_PER_STEP
    slots = EXPERTS_PER_STEP * cap
    xs = pl.pallas_call(
        _gather_kernel,
        grid=(b, steps),
        in_specs=[
            pl.BlockSpec((1, 1, slots), lambda bb, s: (bb * steps + s, 0, 0), memory_space=pltpu.SMEM),
            pl.BlockSpec((1, rows, LANES), lambda bb, s: (bb, 0, 0)),
        ],
        out_specs=pl.BlockSpec((1, 1, slots, D), lambda bb, s: (bb, s, 0, 0)),
        out_shape=jax.ShapeDtypeStruct((b, steps, slots, D), BF),
        scratch_shapes=[pltpu.VMEM((1, slots * SUB, LANES), F32)],
        compiler_params=_params(("parallel", "arbitrary"), 48),
        name="gather",
    )(idx, hf_tiles)
    return xs.reshape(b, N_EXP, cap, D)


def _ffn_kernel(x_ref, gate_ref, gf_ref, wg_ref, wu_ref, wd_ref, o_ref, wg_s, wu_s, wd_s):
    @pl.when(pl.program_id(1) == 0)
    def _():
        wg_s[...] = wg_ref[...].astype(BF)
        wu_s[...] = wu_ref[...].astype(BF)
        wd_s[...] = wd_ref[...].astype(BF)

    x = x_ref[0, 0]
    dexp = wg_s.shape[1]
    step = 256
    y = None
    for c0 in range(0, dexp, step):
        g = jnp.dot(x, wg_s[:, c0:c0 + step], preferred_element_type=F32)
        u = jnp.dot(x, wu_s[:, c0:c0 + step], preferred_element_type=F32)
        act = (g * (1.0 / (1.0 + jnp.exp(-g))) * u).astype(BF)
        part = jnp.dot(act, wd_s[c0:c0 + step, :], preferred_element_type=F32)
        y = part if y is None else y + part
    cap = x.shape[0]
    gate_col = jnp.broadcast_to(gate_ref[0], (LANES, cap)).T
    y = y * jnp.concatenate([gate_col] * SUB, axis=1) * gf_ref[0]
    _store_token_tiles(o_ref.at[0], y)


def _ffn(xs, gate, gf, w_gate, w_up, w_down, layer):
    b, _, cap, _ = xs.shape
    dexp = w_gate.shape[-1]
    xspec = pl.BlockSpec((1, 1, cap, D), lambda e, bb: (bb, e, 0, 0))
    gspec = pl.BlockSpec((1, 1, cap), lambda e, bb: (bb * N_EXP + e, 0, 0))
    gfspec = pl.BlockSpec((1, 1, D), lambda e, bb: (bb, 0, 0))
    yspec = pl.BlockSpec((1, 1, cap * SUB, LANES), lambda e, bb: (bb, e, 0, 0))

    def wspec(shape, switch_at):
        def index(e, bb):
            return (layer, jnp.minimum(e + jnp.where(bb >= switch_at, 1, 0), N_EXP - 1), 0, 0)
        return pl.BlockSpec((None, None) + shape, index)

    assert b >= 2
    stagger = [min(j, b - 1) for j in (1, 2, 3)]
    return pl.pallas_call(
        _ffn_kernel,
        grid=(N_EXP, b),
        in_specs=[xspec, gspec, gfspec, wspec((D, dexp), stagger[0]), wspec((D, dexp), stagger[1]),
                  wspec((dexp, D), stagger[2])],
        out_specs=yspec,
        out_shape=jax.ShapeDtypeStruct((b, N_EXP, cap * SUB, LANES), F32),
        scratch_shapes=[pltpu.VMEM((D, dexp), BF), pltpu.VMEM((D, dexp), BF), pltpu.VMEM((dexp, D), BF)],
        compiler_params=_params(("arbitrary", "arbitrary"), 56),
        name="expert_ffn",
    )(xs, gate, gf, w_gate, w_up, w_down)


def _scatter_kernel(idx_ref, y_ref, *rest):
    if len(rest) == 1:
        (o_ref,), x_ref = rest, None
        acc_ref = o_ref
    else:
        x_ref, o_ref, acc_ref = rest
    slots = idx_ref.shape[2]
    expert_steps = N_EXP // EXPERTS_PER_STEP
    step = pl.program_id(1)

    @pl.when(step == 0)
    def _():
        acc_ref[...] = jnp.zeros(acc_ref.shape, F32)

    group = 16

    def body(g, carry):
        c0 = g * group
        ts = [pl.multiple_of(idx_ref[0, 0, c0 + j], SUB) for j in range(group)]
        new = []
        for j in range(group):
            y = y_ref[0, 0, pl.ds(pl.multiple_of((c0 + j) * SUB, SUB), SUB), :]
            new.append(acc_ref[0, pl.ds(ts[j], SUB), :] + y)
        for j in range(group):
            acc_ref[0, pl.ds(ts[j], SUB), :] = new[j]
        return carry

    @pl.when(step < expert_steps)
    def _():
        lax.fori_loop(0, slots // group, body, 0)

    if x_ref is not None:
        @pl.when(step >= expert_steps)
        def _():
            rows = o_ref.shape[1]
            base = (step - expert_steps) * (rows * SUB)
            o_ref[0] = x_ref[0] + jnp.concatenate(
                [acc_ref[0, pl.ds(base + j, rows, stride=SUB), :] for j in range(SUB)], axis=1)


def _scatter(idx, y_tiles, n, x=None):
    b = y_tiles.shape[0]
    slots = idx.shape[2]
    assert (slots // EXPERTS_PER_STEP) % 16 == 0
    rows = n * SUB
    tm = 512
    steps = N_EXP // EXPERTS_PER_STEP
    extra = n // tm if x is not None else 0
    expert_step = lambda s: jnp.minimum(s, steps - 1)
    in_specs = [
        pl.BlockSpec((1, 1, slots), lambda bb, s: (bb * steps + expert_step(s), 0, 0),
                     memory_space=pltpu.SMEM),
        pl.BlockSpec((1, 1, slots * SUB, LANES), lambda bb, s: (bb, expert_step(s), 0, 0)),
    ]
    args = [idx, y_tiles.reshape(b, steps, slots * SUB, LANES)]
    if x is not None:
        piece = pl.BlockSpec((1, tm, D), lambda bb, s: (bb, jnp.maximum(s - steps, 0), 0))
        in_specs.append(piece)
        args.append(x)
        out_spec, out_shape = piece, jax.ShapeDtypeStruct((b, n, D), F32)
        scratch = [pltpu.VMEM((1, rows, LANES), F32)]
    else:
        out_spec = pl.BlockSpec((1, rows, LANES), lambda bb, s: (bb, 0, 0))
        out_shape, scratch = jax.ShapeDtypeStruct((b, rows, LANES), F32), []
    return pl.pallas_call(
        _scatter_kernel,
        grid=(b, steps + extra),
        in_specs=in_specs,
        out_specs=out_spec,
        out_shape=out_shape,
        scratch_shapes=scratch,
        compiler_params=_params(("parallel", "arbitrary"), 56),
        name="scatter_add",
    )(*args)


def _moe(hf_tiles, aff_t, gf, w_gate, w_up, w_down, layer, x=None):
    b, _, n = aff_t.shape
    cap = CAP_FACTOR * n // N_EXP
    chunks = n // LANES
    idx, gate = _select(aff_t.reshape(b * N_EXP * chunks, LANES), cap, chunks)
    idx = idx.reshape(b * N_EXP // EXPERTS_PER_STEP, 1, EXPERTS_PER_STEP * cap)
    gate = gate.reshape(b * N_EXP, 1, cap)
    xs = _gather(idx, hf_tiles, cap)
    y_tiles = _ffn(xs, gate, gf, w_gate, w_up, w_down, layer)
    return _scatter(idx, y_tiles, n, x)


def _rope_tables(n_lat, n_ctx):
    rows = n_lat // GRID_W
    r = jnp.repeat(jnp.arange(rows, dtype=F32), GRID_W)
    col = jnp.tile(jnp.arange(GRID_W, dtype=F32), rows)
    n_freq = HEAD_DIM // 4
    inv = ROPE_BASE ** (-jnp.arange(n_freq, dtype=F32) / n_freq)
    ar = r[:, None] * inv
    ac = col[:, None] * inv
    ang = jnp.concatenate([ar, ar, ac, ac], axis=-1)
    sign = jnp.where((jnp.arange(HEAD_DIM) % 32) < 16, -1.0, 1.0).astype(F32)
    cos = jnp.concatenate([jnp.cos(ang), jnp.ones((n_ctx, HEAD_DIM), F32)], axis=0)
    sin_s = jnp.concatenate([jnp.sin(ang) * sign, jnp.zeros((n_ctx, HEAD_DIM), F32)], axis=0)
    return jnp.tile(cos, (1, 2)).T, jnp.tile(sin_s, (1, 2)).T


def _dft_tables(n_lat):
    c = jnp.arange(FOUR_G, dtype=I32)
    ang_c = (2.0 * math.pi / FOUR_G) * ((c[:, None] * c[None, :]) % FOUR_G).astype(F32)
    eye = jnp.eye(FOUR_W // FOUR_G, dtype=F32)
    bd = jnp.concatenate([jnp.kron(eye, jnp.cos(ang_c)), jnp.kron(eye, jnp.sin(ang_c))], axis=1)
    bd = (bd * FOUR_G ** -0.5).astype(BF)
    tr = 256
    n = jnp.arange(n_lat, dtype=I32)[None, :]
    r = jnp.arange(tr, dtype=I32)[:, None]
    i = jnp.arange(n_lat // tr, dtype=I32)[:, None]
    row_ang = (2.0 * math.pi / n_lat) * ((r * n) % n_lat).astype(F32)
    tile_ang = (2.0 * math.pi / n_lat) * ((i * tr * n) % n_lat).astype(F32)
    scale = n_lat ** -0.5
    tables = (jnp.cos(row_ang) * scale, jnp.sin(row_ang) * scale,
              jnp.cos(tile_ang)[:, None, :], jnp.sin(tile_ang)[:, None, :])
    return bd, tables


def kernel(x, c, ctx, c_ctx, ada_w, ada_b, norm_mix, norm_ffn, attn_w_in, attn_q_norm, attn_k_norm,
           lam_q1, lam_k1, lam_q2, lam_k2, attn_subln, attn_w_out, conv_w_in, conv_w, conv_w_out,
           router_w, moe_w_gate, moe_w_up, moe_w_down):
    b, n, _ = x.shape
    n_ctx = ctx.shape[1]
    assert x.shape[2] == D and n % 512 == 0 and n_ctx % 256 == 0

    cond8 = jnp.concatenate([c, c_ctx[None, :], jnp.zeros((8 - b - 1, D), F32)], axis=0)
    ada = _ada(cond8, ada_w, ada_b)

    def mods(layer):
        m = ada[layer].reshape(8, 6, D)
        return [m[:, j] for j in range(6)]

    vec = lambda t: t[:b].reshape(b, 1, D)

    sh_m, sc_m, g_m, sh_f, sc_f, g_f = mods(0)
    both = lambda t: jnp.stack([t[:b], jnp.broadcast_to(t[b], (b, D))], axis=1).reshape(b, 2, 1, D)
    cos_t, sin_t = _rope_tables(n, n_ctx)
    bd, dft_tabs = _dft_tables(n)
    col2 = lambda t: jnp.tile(t.reshape(HEAD_DIM, 1), (2, 1))
    w_in = attn_w_in[0]
    q, k, v, f = _proj0(x, ctx, norm_mix[0].reshape(1, D), both(sh_m), both(sc_m),
                        w_in[:, 2 * QK_W + V_W:].astype(BF), w_in[:, :2 * QK_W + V_W].T.astype(BF),
                        cos_t, sin_t, col2(attn_q_norm[0]), col2(attn_k_norm[0]))
    lam_init = 0.8 - 0.6 * math.exp(-0.3 * 0)
    lamv = jnp.zeros((8, LANES), F32).at[:4, :HEAD_DIM].set(
        jnp.stack([lam_q1[0], lam_k1[0], lam_q2[0], lam_k2[0]]))
    score_bound = (1.01 * HEAD_DIM * Q_SCALE * jnp.max(jnp.abs(attn_q_norm[0]))
                   * jnp.max(jnp.abs(attn_k_norm[0])) + 0.1).reshape(1)
    att = _attention(score_bound, q, k, v, lamv, attn_subln[0].reshape(V_DIM, 1), n, lam_init)
    four = _fourier(f, bd, dft_tabs, n)
    x1, hf, aff_t = _merge0(att, four, attn_w_out[0].astype(BF), x, vec(g_m),
                            norm_ffn[0].reshape(1, D), vec(sh_f), vec(sc_f), router_w[0].T)
    moe0 = _moe(hf, aff_t, vec(g_f), moe_w_gate, moe_w_up, moe_w_down, 0)

    sh_m, sc_m, g_m, sh_f, sc_f, g_f = mods(1)
    bg, z, x2 = _proj1(x1, moe0, norm_mix[1].reshape(1, D), vec(sh_m), vec(sc_m),
                       conv_w_in[0].astype(BF))
    conv_w8 = jnp.zeros((8, D), F32).at[:3].set(conv_w[0])
    x3, hf, aff_t = _merge1(bg, z, conv_w8, conv_w_out[0].astype(BF), x2, vec(g_m),
                            norm_ffn[1].reshape(1, D), vec(sh_f), vec(sc_f), router_w[1].T)
    return _moe(hf, aff_t, vec(g_f), moe_w_gate, moe_w_up, moe_w_down, 1, x3)
```

```python
import functools
import math

import jax
import jax.numpy as jnp
from jax import lax
from jax.experimental import pallas as pl
from jax.experimental.pallas import tpu as pltpu

BF = jnp.bfloat16
F32 = jnp.float32
I32 = jnp.int32

D = 1024
GRID_W = 64
N_HEADS = 6
HEAD_DIM = 64
V_DIM = 2 * HEAD_DIM
QK_W = N_HEADS * 2 * HEAD_DIM
V_W = N_HEADS * V_DIM
FOUR_W = 256
FOUR_G = 64
IN_W = 2 * QK_W + V_W + FOUR_W
N_EXP = 16
EXPERTS_PER_STEP = 4
CAP_FACTOR = 2
ROPE_BASE = 10000.0
EPS = 1e-6
Q_SCALE = HEAD_DIM ** -0.5 * math.log2(math.e)
ATTN_SHIFT_LIMIT = 40.0
LANES = 128
MIB = 1024 * 1024

NT = (((1,), (1,)), ((), ()))


def _params(sem, vmem_mib):
    return pltpu.CompilerParams(dimension_semantics=sem, vmem_limit_bytes=vmem_mib * MIB)


def _split2(x):
    hi = x.astype(BF)
    lo = (x - hi.astype(F32)).astype(BF)
    return hi, lo


def _dot3(a, b, dims=(((1,), (0,)), ((), ()))):
    ah, al = _split2(a)
    bh, bl = _split2(b)
    dg = functools.partial(lax.dot_general, dimension_numbers=dims, preferred_element_type=F32)
    return dg(ah, bh) + dg(ah, bl) + dg(al, bh)


def _modulate(x, nw, shift, scale):
    ms = jnp.mean(x * x, axis=-1, keepdims=True)
    return (x * lax.rsqrt(ms + EPS) * nw) * (1.0 + scale) + shift


def _ada_kernel(c_ref, w_ref, b_ref, o_ref):
    cv = c_ref[...]
    s = cv * (1.0 / (1.0 + jnp.exp(-cv)))
    o_ref[0] = _dot3(s, w_ref[0]) + b_ref[0]


def _ada(cond8, ada_w, ada_b):
    depth = ada_w.shape[0]
    tn = 1536
    return pl.pallas_call(
        _ada_kernel,
        grid=(depth, 6 * D // tn),
        in_specs=[
            pl.BlockSpec((8, D), lambda l, j: (0, 0)),
            pl.BlockSpec((1, D, tn), lambda l, j: (l, 0, j)),
            pl.BlockSpec((1, 1, tn), lambda l, j: (l, 0, j)),
        ],
        out_specs=pl.BlockSpec((1, 8, tn), lambda l, j: (l, 0, j)),
        out_shape=jax.ShapeDtypeStruct((depth, 8, 6 * D), F32),
        compiler_params=_params(("arbitrary", "arbitrary"), 40),
        name="ada",
    )(cond8, ada_w, ada_b.reshape(depth, 1, 6 * D))


def _norm_rope_t(xt, wn_col, cos_t, sin_t, out_scale):
    q4 = HEAD_DIM // 4
    halves = []
    for s in range(2):
        x = xt[s * HEAD_DIM:(s + 1) * HEAD_DIM]
        ms = jnp.sum(x * x, axis=0, keepdims=True) * (1.0 / HEAD_DIM)
        y = x * lax.rsqrt(ms + EPS) * wn_col[s * HEAD_DIM:(s + 1) * HEAD_DIM]
        rot = jnp.concatenate([y[q4:2 * q4], y[:q4], y[3 * q4:], y[2 * q4:3 * q4]], axis=0)
        halves.append(y * cos_t[s * HEAD_DIM:(s + 1) * HEAD_DIM] + rot * sin_t[s * HEAD_DIM:(s + 1) * HEAD_DIM])
    out = jnp.concatenate(halves, axis=0)
    return out if out_scale == 1.0 else out * out_scale


def _proj0_kernel(x_ref, ctx_ref, nw_ref, sh_ref, sc_ref, wf_ref, wt_ref, cost_ref, sint_ref,
                  qn_ref, kn_ref, q_ref, k_ref, v_ref, f_ref, *, lat_tiles):
    x = jnp.where(pl.program_id(1) < lat_tiles, x_ref[0], ctx_ref[0])
    h = _modulate(x, nw_ref[...], sh_ref[0, 0], sc_ref[0, 0]).astype(BF)
    def project_t(r0, width):
        return lax.dot_general(wt_ref[r0:r0 + width, :], h, NT, preferred_element_type=F32)

    cos_t = cost_ref[...]
    sin_t = sint_ref[...]
    for out_ref, wn_ref, r0, scale in ((q_ref, qn_ref, 0, Q_SCALE), (k_ref, kn_ref, QK_W, 1.0)):
        p_t = project_t(r0, QK_W)
        wn_col = wn_ref[...]
        for hh in range(N_HEADS):
            c0 = hh * V_DIM
            out_ref[0, c0:c0 + V_DIM, :] = _norm_rope_t(
                p_t[c0:c0 + V_DIM], wn_col, cos_t, sin_t, scale).astype(BF)
    v_ref[0] = project_t(2 * QK_W, V_W).astype(BF)
    f_ref[0] = jnp.dot(h, wf_ref[...], preferred_element_type=F32).astype(BF)


def _proj0(x, ctx, nw, sh2, sc2, w_four, w_qkv_t, cos_t, sin_t, qn_col, kn_col):
    b, n_lat, _ = x.shape
    n_ctx = ctx.shape[1]
    s_tot = n_lat + n_ctx
    tm = 256
    lat_tiles = n_lat // tm
    mod_spec = pl.BlockSpec((1, 1, 1, D), lambda bb, i: (bb, i // lat_tiles, 0, 0))
    full = lambda shape: pl.BlockSpec(shape, lambda bb, i: tuple(0 for _ in shape))
    feat = lambda w: pl.BlockSpec((1, w, tm), lambda bb, i: (bb, 0, i))
    feat_shape = lambda w: jax.ShapeDtypeStruct((b, w, s_tot), BF)
    return pl.pallas_call(
        functools.partial(_proj0_kernel, lat_tiles=lat_tiles),
        grid=(b, s_tot // tm),
        in_specs=[
            pl.BlockSpec((1, tm, D), lambda bb, i: (bb, jnp.minimum(i, lat_tiles - 1), 0)),
            pl.BlockSpec((1, tm, D), lambda bb, i: (bb, jnp.maximum(i - lat_tiles, 0), 0)),
            full((1, D)), mod_spec, mod_spec, full((D, FOUR_W)), full((2 * QK_W + V_W, D)),
            pl.BlockSpec((LANES, tm), lambda bb, i: (0, i)),
            pl.BlockSpec((LANES, tm), lambda bb, i: (0, i)),
            full((LANES, 1)), full((LANES, 1)),
        ],
        out_specs=[feat(QK_W), feat(QK_W), feat(V_W),
                   pl.BlockSpec((1, tm, FOUR_W), lambda bb, i: (bb, i, 0))],
        out_shape=[feat_shape(QK_W), feat_shape(QK_W), feat_shape(V_W),
                   jax.ShapeDtypeStruct((b, s_tot, FOUR_W), BF)],
        compiler_params=_params(("parallel", "arbitrary"), 48),
        name="proj0",
    )(x, ctx, nw, sh2, sc2, w_four, w_qkv_t, cos_t, sin_t, qn_col, kn_col)


def _attn_kernel(bound_ref, q_ref, kt_ref, vt_ref, lam_ref, sub_ref, o_ref, k_ref, m_ref, e_ref,
                 *, lam_init, kc):
    @pl.when(pl.program_id(2) == 0)
    def _():
        k_ref[...] = kt_ref[0].astype(F32).T.astype(BF)

    q = q_ref[0]
    feat = lax.broadcasted_iota(I32, q.shape, 0)
    zero = jnp.zeros_like(q)
    lv = lam_ref[...]
    t1 = jnp.sum(lv[0:1] * lv[1:2], axis=-1, keepdims=True)
    t2 = jnp.sum(lv[2:3] * lv[3:4], axis=-1, keepdims=True)
    lam = jnp.exp(t1) - jnp.exp(t2) + lam_init

    qs = (jnp.where(feat < HEAD_DIM, q, zero), jnp.where(feat < HEAD_DIM, zero, q))
    tq = q.shape[1]
    n_chunks = k_ref.shape[0] // kc

    def scores(h, c):
        return jnp.dot(k_ref[c * kc:(c + 1) * kc, :], qs[h], preferred_element_type=F32)

    def fold(x, op):
        return op(x.reshape(kc // 8, 8, tq), axis=0)

    bound = bound_ref[0]
    small = bound <= ATTN_SHIFT_LIMIT

    @pl.when(small)
    def _():
        m_ref[...] = jnp.zeros(m_ref.shape, F32) + bound

    @pl.when(jnp.logical_not(small))
    def _():
        for h in range(2):
            m = jnp.full((8, tq), -jnp.inf, F32)
            for c in range(n_chunks):
                m = jnp.maximum(m, fold(scores(h, c), jnp.max))
            m_ref[h] = jnp.broadcast_to(jnp.max(m, axis=0, keepdims=True), (8, tq))

    ms = [m_ref[h][0:1, :] for h in range(2)]
    ls = [jnp.zeros((8, tq), F32) for _ in range(2)]
    for c in range(n_chunks):
        for h in range(2):
            e = jnp.exp2(scores(h, c) - ms[h])
            ls[h] = ls[h] + fold(e, jnp.sum)
            e_ref[h, c] = e.astype(BF)
    l0, l1 = [jnp.sum(l, axis=0, keepdims=True) for l in ls]

    beta = (lam * l0 / l1).astype(BF)
    acc = jnp.zeros((V_DIM, tq), F32)
    for c in range(n_chunks):
        a = e_ref[0, c] - beta * e_ref[1, c]
        acc = acc + jnp.dot(vt_ref[0, :, c * kc:(c + 1) * kc], a, preferred_element_type=F32)
    o = acc * (1.0 / l0)
    ms = jnp.mean(o * o, axis=0, keepdims=True)
    o = o * lax.rsqrt(ms + EPS) * sub_ref[...] * (1.0 - lam_init)
    o_ref[0] = o.T.astype(BF)


def _attention(bound, q, k, v, lamv, subln, n_lat, lam_init):
    b, _, s_tot = k.shape
    tq = 1024
    kc = 256
    assert s_tot % kc == 0
    kv_spec = pl.BlockSpec((1, V_DIM, s_tot), lambda bb, hh, i: (bb, hh, 0))
    return pl.pallas_call(
        functools.partial(_attn_kernel, lam_init=lam_init, kc=kc),
        grid=(b, N_HEADS, n_lat // tq),
        in_specs=[
            pl.BlockSpec(memory_space=pltpu.SMEM),
            pl.BlockSpec((1, V_DIM, tq), lambda bb, hh, i: (bb, hh, i)),
            kv_spec, kv_spec,
            pl.BlockSpec((8, LANES), lambda bb, hh, i: (0, 0)),
            pl.BlockSpec((V_DIM, 1), lambda bb, hh, i: (0, 0)),
        ],
        out_specs=pl.BlockSpec((1, tq, V_DIM), lambda bb, hh, i: (bb, i, hh)),
        out_shape=jax.ShapeDtypeStruct((b, n_lat, V_W), BF),
        scratch_shapes=[pltpu.VMEM((s_tot, V_DIM), BF),
                        pltpu.VMEM((2, 8, tq), F32),
                        pltpu.VMEM((2, s_tot // kc, kc, tq), BF)],
        compiler_params=_params(("parallel", "parallel", "arbitrary"), 48),
        name="diff_attn",
    )(bound, q, k, v, lamv, subln)


def _fourier_kernel(f_ref, bd_ref, cb_ref, sb_ref, ca_ref, sa_ref, o_ref, g_ref):
    nb, n_lat, _ = f_ref.shape

    @pl.when(pl.program_id(0) == 0)
    def _():
        rows = 1024
        for bb in range(nb):
            for r in range(n_lat // rows):
                fc = jnp.dot(f_ref[bb, r * rows:(r + 1) * rows, :], bd_ref[...],
                             preferred_element_type=F32)
                g_ref[r * rows:(r + 1) * rows, bb * FOUR_W:(bb + 1) * FOUR_W] = fc[:, :FOUR_W].astype(BF)
                g_ref[n_lat + r * rows:n_lat + (r + 1) * rows,
                      bb * FOUR_W:(bb + 1) * FOUR_W] = fc[:, FOUR_W:].astype(BF)

    ca = ca_ref[0]
    sa = sa_ref[0]
    cb = cb_ref[...]
    sb = sb_ref[...]
    w_cos = (cb * ca - sb * sa).astype(BF)
    w_sin = (sb * ca + cb * sa).astype(BF)
    y = (jnp.dot(w_cos, g_ref[:n_lat, :], preferred_element_type=F32)
         - jnp.dot(w_sin, g_ref[n_lat:, :], preferred_element_type=F32))
    for bb in range(nb):
        o_ref[bb] = y[:, bb * FOUR_W:(bb + 1) * FOUR_W].astype(BF)


def _fourier(f, bd, tables, n_lat):
    b = f.shape[0]
    cos_row, sin_row, cos_tile, sin_tile = tables
    tr = cos_row.shape[0]
    row_tab = pl.BlockSpec((tr, n_lat), lambda i: (0, 0))
    tile_tab = pl.BlockSpec((1, 1, n_lat), lambda i: (i, 0, 0))
    return pl.pallas_call(
        _fourier_kernel,
        grid=(n_lat // tr,),
        in_specs=[
            pl.BlockSpec((b, n_lat, FOUR_W), lambda i: (0, 0, 0)),
            pl.BlockSpec((FOUR_W, 2 * FOUR_W), lambda i: (0, 0)),
            row_tab, row_tab, tile_tab, tile_tab,
        ],
        out_specs=pl.BlockSpec((b, tr, FOUR_W), lambda i: (0, i, 0)),
        out_shape=jax.ShapeDtypeStruct((b, n_lat, FOUR_W), BF),
        scratch_shapes=[pltpu.VMEM((2 * n_lat, b * FOUR_W), BF)],
        compiler_params=_params(("arbitrary",), 56),
        name="fourier",
    )(f, bd, cos_row, sin_row, cos_tile, sin_tile)


SUB = D // LANES


def _store_token_tiles(ref, val):
    rows = val.shape[0]
    for j in range(SUB):
        ref[0, pl.ds(j, rows, stride=SUB), :] = val[:, j * LANES:(j + 1) * LANES]


def _load_token_tiles(ref, rows):
    return jnp.concatenate([ref[0, pl.ds(j, rows, stride=SUB), :] for j in range(SUB)], axis=1)


def _residual_router(x, y, gm, nf, shf, scf, wr_t, x1_ref, hf_ref, aff_ref):
    x1 = x + gm * y
    x1_ref[0] = x1
    hf = _modulate(x1, nf, shf, scf)
    _store_token_tiles(hf_ref, hf)
    logits = _dot3(wr_t, hf, NT)
    m = jnp.max(logits, axis=0, keepdims=True)
    e = jnp.exp(logits - m)
    aff_ref[0] = e / jnp.sum(e, axis=0, keepdims=True)


def _merge0_kernel(att_ref, four_ref, w_ref, x_ref, gm_ref, nf_ref, shf_ref, scf_ref, wr_ref,
                   x1_ref, hf_ref, aff_ref):
    y = jnp.dot(att_ref[0], w_ref[:V_W, :], preferred_element_type=F32)
    y = y + jnp.dot(four_ref[0], w_ref[V_W:, :], preferred_element_type=F32)
    _residual_router(x_ref[0], y, gm_ref[0], nf_ref[...], shf_ref[0], scf_ref[0], wr_ref[...],
                     x1_ref, hf_ref, aff_ref)


def _router_specs(tm):
    vec = pl.BlockSpec((1, 1, D), lambda bb, i: (bb, 0, 0))
    row = pl.BlockSpec((1, tm, D), lambda bb, i: (bb, i, 0))
    in_specs = [vec, pl.BlockSpec((1, D), lambda bb, i: (0, 0)), vec, vec,
                pl.BlockSpec((N_EXP, D), lambda bb, i: (0, 0))]
    tiles = _tile_spec(tm)
    out_specs = [row, tiles, pl.BlockSpec((1, N_EXP, tm), lambda bb, i: (bb, 0, i))]
    return row, in_specs, out_specs


def _tile_spec(tm):
    return pl.BlockSpec((1, tm * SUB, LANES), lambda bb, i: (bb, i, 0))


def _router_out_shape(b, n):
    return [jax.ShapeDtypeStruct((b, n, D), F32), jax.ShapeDtypeStruct((b, n * SUB, LANES), F32),
            jax.ShapeDtypeStruct((b, N_EXP, n), F32)]


def _merge0(att, four, w_bf, x, gm, nf, shf, scf, wr_t):
    b, n, _ = x.shape
    tm = 512
    row, r_in, r_out = _router_specs(tm)
    return pl.pallas_call(
        _merge0_kernel,
        grid=(b, n // tm),
        in_specs=[
            pl.BlockSpec((1, tm, V_W), lambda bb, i: (bb, i, 0)),
            pl.BlockSpec((1, tm, FOUR_W), lambda bb, i: (bb, i, 0)),
            pl.BlockSpec((D, D), lambda bb, i: (0, 0)),
            row,
        ] + r_in,
        out_specs=r_out,
        out_shape=_router_out_shape(b, n),
        compiler_params=_params(("parallel", "arbitrary"), 48),
        name="merge0",
    )(att, four, w_bf, x, gm, nf, shf, scf, wr_t)


def _proj1_kernel(x_ref, moe_ref, nw_ref, sh_ref, sc_ref, w_ref, bg_ref, z_ref, x2_ref):
    x = x_ref[0] + _load_token_tiles(moe_ref, bg_ref.shape[1])
    x2_ref[0] = x
    h = _modulate(x, nw_ref[...], sh_ref[0], sc_ref[0]).astype(BF)
    bg_ref[0] = jnp.dot(h, w_ref[:, :D], preferred_element_type=F32)
    cg = jnp.dot(h, w_ref[:, D:2 * D], preferred_element_type=F32)
    u = jnp.dot(h, w_ref[:, 2 * D:], preferred_element_type=F32)
    z_ref[0] = cg * u


def _proj1(x, moe_tiles, nw, sh, sc, w_bf):
    b, n, _ = x.shape
    tm = 512
    vec = pl.BlockSpec((1, 1, D), lambda bb, i: (bb, 0, 0))
    row = pl.BlockSpec((1, tm, D), lambda bb, i: (bb, i, 0))
    rows = jax.ShapeDtypeStruct((b, n, D), F32)
    return pl.pallas_call(
        _proj1_kernel,
        grid=(b, n // tm),
        in_specs=[row, _tile_spec(tm), pl.BlockSpec((1, D), lambda bb, i: (0, 0)), vec, vec,
                  pl.BlockSpec((D, 3 * D), lambda bb, i: (0, 0))],
        out_specs=[row, row, row],
        out_shape=[rows, rows, rows],
        compiler_params=_params(("parallel", "arbitrary"), 48),
        name="proj1",
    )(x, moe_tiles, nw, sh, sc, w_bf)


def _merge1_kernel(bg_ref, z_ref, zp_ref, zn_ref, cw_ref, w_ref, x_ref, gm_ref, nf_ref, shf_ref,
                   scf_ref, wr_ref, x1_ref, hf_ref, aff_ref):
    i = pl.program_id(1)
    last = pl.num_programs(1) - 1
    z = z_ref[0]
    tm = z.shape[0]
    rowid = lax.broadcasted_iota(I32, z.shape, 0)
    prev_row = jnp.where(i > 0, zp_ref[0, 7:8, :], 0.0)
    next_row = jnp.where(i < last, zn_ref[0, 0:1, :], 0.0)
    z_up = jnp.where(rowid == 0, prev_row, pltpu.roll(z, 1, axis=0))
    z_dn = jnp.where(rowid == tm - 1, next_row, pltpu.roll(z, tm - 1, axis=0))
    cw = cw_ref[...]
    conv = cw[0:1] * z_up + cw[1:2] * z + cw[2:3] * z_dn
    y = jnp.dot((bg_ref[0] * conv).astype(BF), w_ref[...], preferred_element_type=F32)
    _residual_router(x_ref[0], y, gm_ref[0], nf_ref[...], shf_ref[0], scf_ref[0], wr_ref[...],
                     x1_ref, hf_ref, aff_ref)


def _merge1(bg, z, conv_w8, w_bf, x, gm, nf, shf, scf, wr_t):
    b, n, _ = bg.shape
    tm = 512
    halo = 8
    per = tm // halo
    n_halo = n // halo
    row, r_in, r_out = _router_specs(tm)
    return pl.pallas_call(
        _merge1_kernel,
        grid=(b, n // tm),
        in_specs=[
            row, row,
            pl.BlockSpec((1, halo, D), lambda bb, i: (bb, jnp.maximum(i * per - 1, 0), 0)),
            pl.BlockSpec((1, halo, D), lambda bb, i: (bb, jnp.minimum((i + 1) * per, n_halo - 1), 0)),
            pl.BlockSpec((8, D), lambda bb, i: (0, 0)),
            pl.BlockSpec((D, D), lambda bb, i: (0, 0)),
            row,
        ] + r_in,
        out_specs=r_out,
        out_shape=_router_out_shape(b, n),
        compiler_params=_params(("parallel", "arbitrary"), 48),
        name="merge1",
    )(bg, z, z, z, conv_w8, w_bf, x, gm, nf, shf, scf, wr_t)


def _select_kernel(a_ref, idx_ref, gate_ref, cl_s, off_s, inc_s, hi_s, mid_s, lo_s, *, cap, chunks):
    a = a_ref[...]
    rows = a.shape[0]
    groups = rows // chunks
    per_batch = N_EXP * chunks
    assert chunks & (chunks - 1) == 0
    shift = chunks.bit_length() - 1

    def indicator(shape, row_dim):
        r = lax.broadcasted_iota(I32, shape, row_dim)
        g = lax.broadcasted_iota(I32, shape, 1 - row_dim)
        return jnp.where((r >> shift) == g, 1.0, 0.0).astype(BF)

    member = indicator((groups, rows), 1)
    spread = indicator((rows, groups), 0)
    ri = lax.broadcasted_iota(I32, (per_batch, per_batch), 0)
    ci = lax.broadcasted_iota(I32, (per_batch, per_batch), 1)
    same_f = jnp.where((ri >> shift) == (ci >> shift), 1.0, 0.0)
    same = same_f.astype(BF)
    lower = (same_f * jnp.where(ci < ri, 1.0, 0.0)).astype(BF)
    li = lax.broadcasted_iota(I32, (LANES, LANES), 0)
    lj = lax.broadcasted_iota(I32, (LANES, LANES), 1)
    incl = jnp.where(li <= lj, 1.0, 0.0).astype(BF)

    def bcast(col):
        return jnp.broadcast_to(col, (per_batch, LANES))

    def prefix(maskf):
        cl = jnp.dot(maskf.astype(BF), incl, preferred_element_type=F32)
        tot = bcast(cl[:, LANES - 1:LANES])
        off = jnp.dot(lower, tot.astype(BF), preferred_element_type=F32)
        return cl, off, tot

    min_normal = 0x00800000

    def search(step, t):
        cand = t | jnp.left_shift(jnp.int32(1), 30 - step)
        mask = jnp.where(a >= lax.bitcast_convert_type(cand, F32), 1.0, 0.0).astype(BF)
        part = jnp.dot(member, mask, preferred_element_type=F32)
        cnt = jnp.sum(part, axis=-1, keepdims=True)
        ok = jnp.broadcast_to(jnp.where(cnt >= cap, 1.0, 0.0), (groups, LANES)).astype(BF)
        ok_rows = jnp.dot(spread, ok, preferred_element_type=F32)
        return jnp.where(ok_rows > 0.5, jnp.where(cand >= min_normal, cand, t), t)

    thr = lax.fori_loop(0, 31, search, jnp.zeros((rows, LANES), I32))
    thr_f = lax.bitcast_convert_type(thr, F32)
    nxt_f = lax.bitcast_convert_type(jnp.maximum(thr + 1, min_normal), F32)
    gtf = jnp.where(a >= nxt_f, 1.0, 0.0)
    eqf = jnp.where(a >= thr_f, 1.0, 0.0) - gtf
    for bb in range(rows // per_batch):
        sl = slice(bb * per_batch, (bb + 1) * per_batch)
        gt_b, eq_b = gtf[sl], eqf[sl]
        n_gt = jnp.dot(same, bcast(jnp.sum(gt_b, axis=-1, keepdims=True)).astype(BF),
                       preferred_element_type=F32)
        cl_eq, off_eq, _ = prefix(eq_b)
        sel = gt_b + eq_b * jnp.where(cl_eq + off_eq <= cap - n_gt, 1.0, 0.0)
        cl, off, tot = prefix(sel)
        cl_s[sl, :] = cl.astype(BF)
        off_s[sl, :] = off
        inc_s[sl, :] = off + tot
        a_b = a[sl]
        a_hi = a_b.astype(BF)
        r1 = a_b - a_hi.astype(F32)
        a_mid = r1.astype(BF)
        hi_s[sl, :] = a_hi
        mid_s[sl, :] = a_mid
        lo_s[sl, :] = (r1 - a_mid.astype(F32)).astype(BF)

    slot = lax.broadcasted_iota(I32, (chunks, cap), 1).astype(F32)
    chunk_id = lax.broadcasted_iota(I32, (chunks, cap), 0).astype(F32)
    lane_id = lax.broadcasted_iota(I32, (LANES, cap), 0).astype(F32)
    tn = (((0,), (0,)), ((), ()))
    reps = cap // LANES

    def widen(x):
        return jnp.concatenate([x] * reps, axis=1)

    def per_group(g, carry):
        win = pl.ds(pl.multiple_of(g * chunks, chunks), chunks)
        inc_g = widen(inc_s[win, :])
        off_g = widen(off_s[win, :])
        chunk_of = jnp.sum(jnp.where(inc_g <= slot, 1.0, 0.0), axis=0, keepdims=True)
        pick_f = jnp.where(chunk_id == chunk_of, 1.0, 0.0)
        before = jnp.sum(pick_f * off_g, axis=0, keepdims=True)
        pick = pick_f.astype(BF)
        counts = lax.dot_general(cl_s[win, :], pick, tn, preferred_element_type=F32)
        lane_of = jnp.sum(jnp.where(counts <= slot[0:1] - before, 1.0, 0.0), axis=0, keepdims=True)
        aff = (lax.dot_general(hi_s[win, :], pick, tn, preferred_element_type=F32)
               + lax.dot_general(mid_s[win, :], pick, tn, preferred_element_type=F32)
               + lax.dot_general(lo_s[win, :], pick, tn, preferred_element_type=F32))
        gate = jnp.sum(jnp.where(lane_id == lane_of, aff, 0.0), axis=0, keepdims=True)
        idx_ref[pl.ds(g, 1), :] = ((chunk_of * float(LANES) + lane_of) * float(SUB)).astype(I32)
        gate_ref[pl.ds(g, 1), :] = gate
        return carry

    lax.fori_loop(0, groups, per_group, 0, unroll=2)


def _select(aff_rows, cap, chunks):
    rows = aff_rows.shape[0]
    groups = rows // chunks
    whole = lambda shape: pl.BlockSpec(shape, lambda i: (0, 0))
    return pl.pallas_call(
        functools.partial(_select_kernel, cap=cap, chunks=chunks),
        grid=(1,),
        in_specs=[whole((rows, LANES))],
        out_specs=[whole((groups, cap)), whole((groups, cap))],
        out_shape=[jax.ShapeDtypeStruct((groups, cap), I32), jax.ShapeDtypeStruct((groups, cap), F32)],
        scratch_shapes=[pltpu.VMEM((rows, LANES), BF), pltpu.VMEM((rows, LANES), F32),
                        pltpu.VMEM((rows, LANES), F32), pltpu.VMEM((rows, LANES), BF),
                        pltpu.VMEM((rows, LANES), BF), pltpu.VMEM((rows, LANES), BF)],
        compiler_params=_params(("arbitrary",), 48),
        name="select",
    )(aff_rows)


def _gather_kernel(idx_ref, h_ref, o_ref):
    slots = idx_ref.shape[2]

    def body(c, carry):
        t = pl.multiple_of(idx_ref[0, 0, c], SUB)
        o_ref[0, 0, pl.ds(pl.multiple_of(c * SUB, SUB), SUB), :] = h_ref[0, pl.ds(t, SUB), :]
        return carry

    lax.fori_loop(0, slots, body, 0, unroll=8)


def _gather(idx, hf_tiles, cap):
    b, rows, _ = hf_tiles.shape
    steps = N_EXP // EXPERTS_PER_STEP
    slots = EXPERTS_PER_STEP * cap
    xs = pl.pallas_call(
        _gather_kernel,
        grid=(b, steps),
        in_specs=[
            pl.BlockSpec((1, 1, slots), lambda bb, s: (bb * steps + s, 0, 0), memory_space=pltpu.SMEM),
            pl.BlockSpec((1, rows, LANES), lambda bb, s: (bb, 0, 0)),
        ],
        out_specs=pl.BlockSpec((1, 1, slots * SUB, LANES), lambda bb, s: (bb, s, 0, 0)),
        out_shape=jax.ShapeDtypeStruct((b, steps, slots * SUB, LANES), F32),
        compiler_params=_params(("parallel", "arbitrary"), 56),
        name="gather",
    )(idx, hf_tiles)
    return xs.reshape(b, N_EXP, cap * SUB, LANES)


def _ffn_kernel(x_ref, gate_ref, gf_ref, wg_ref, wu_ref, wd_ref, o_ref, wg_s, wu_s, wd_s):
    @pl.when(pl.program_id(1) == 0)
    def _():
        wg_s[...] = wg_ref[...].astype(BF)
        wu_s[...] = wu_ref[...].astype(BF)
        wd_s[...] = wd_ref[...].astype(BF)

    cap = x_ref.shape[2] // SUB
    x = _load_token_tiles(x_ref.at[0], cap).astype(BF)
    dexp = wg_s.shape[1]
    step = 256
    y = None
    for c0 in range(0, dexp, step):
        g = jnp.dot(x, wg_s[:, c0:c0 + step], preferred_element_type=F32)
        u = jnp.dot(x, wu_s[:, c0:c0 + step], preferred_element_type=F32)
        act = (g * (1.0 / (1.0 + jnp.exp(-g))) * u).astype(BF)
        part = jnp.dot(act, wd_s[c0:c0 + step, :], preferred_element_type=F32)
        y = part if y is None else y + part
    gate_col =jnp.broadcast_to(gate_ref[0], (LANES, cap)).T
    y = y * jnp.concatenate([gate_col] * SUB, axis=1) * gf_ref[0]
    _store_token_tiles(o_ref.at[0], y)


def _ffn(xs, gate, gf, w_gate, w_up, w_down, layer):
    b, _, tile_rows, _ = xs.shape
    cap = tile_rows // SUB
    dexp = w_gate.shape[-1]
    gspec = pl.BlockSpec((1, 1, cap), lambda e, bb: (bb * N_EXP + e, 0, 0))
    gfspec = pl.BlockSpec((1, 1, D), lambda e, bb: (bb, 0, 0))
    yspec = pl.BlockSpec((1, 1, tile_rows, LANES), lambda e, bb: (bb, e, 0, 0))
    xspec = yspec

    def wspec(shape, switch_at):
        def index(e, bb):
            return (layer, jnp.minimum(e + jnp.where(bb >= switch_at, 1, 0), N_EXP - 1), 0, 0)
        return pl.BlockSpec((None, None) + shape, index)

    assert b >= 2
    stagger = [min(j, b - 1) for j in (1, 2, 3)]
    return pl.pallas_call(
        _ffn_kernel,
        grid=(N_EXP, b),
        in_specs=[xspec, gspec, gfspec, wspec((D, dexp), stagger[0]), wspec((D, dexp), stagger[1]),
                  wspec((dexp, D), stagger[2])],
        out_specs=yspec,
        out_shape=jax.ShapeDtypeStruct((b, N_EXP, cap * SUB, LANES), F32),
        scratch_shapes=[pltpu.VMEM((D, dexp), BF), pltpu.VMEM((D, dexp), BF), pltpu.VMEM((dexp, D), BF)],
        compiler_params=_params(("arbitrary", "arbitrary"), 56),
        name="expert_ffn",
    )(xs, gate, gf, w_gate, w_up, w_down)


def _scatter_kernel(idx_ref, y_ref, *rest):
    if len(rest) == 1:
        (o_ref,), x_ref = rest, None
        acc_ref = o_ref
    else:
        x_ref, o_ref, acc_ref = rest
    slots = idx_ref.shape[2]
    expert_steps = N_EXP // EXPERTS_PER_STEP
    step = pl.program_id(1)

    @pl.when(step == 0)
    def _():
        acc_ref[...] = jnp.zeros(acc_ref.shape, F32)

    group = 16

    def body(g, carry):
        c0 = g * group
        ts = [pl.multiple_of(idx_ref[0, 0, c0 + j], SUB) for j in range(group)]
        new = []
        for j in range(group):
            y = y_ref[0, 0, pl.ds(pl.multiple_of((c0 + j) * SUB, SUB), SUB), :]
            new.append(acc_ref[0, pl.ds(ts[j], SUB), :] + y)
        for j in range(group):
            acc_ref[0, pl.ds(ts[j], SUB), :] = new[j]
        return carry

    @pl.when(step < expert_steps)
    def _():
        lax.fori_loop(0, slots // group, body, 0)

    if x_ref is not None:
        @pl.when(step >= expert_steps)
        def _():
            rows = o_ref.shape[1]
            base = (step - expert_steps) * (rows * SUB)
            o_ref[0] = x_ref[0] + jnp.concatenate(
                [acc_ref[0, pl.ds(base + j, rows, stride=SUB), :] for j in range(SUB)], axis=1)


def _scatter(idx, y_tiles, n, x=None):
    b = y_tiles.shape[0]
    slots = idx.shape[2]
    assert (slots // EXPERTS_PER_STEP) % 16 == 0
    rows = n * SUB
    tm = 512
    steps = N_EXP // EXPERTS_PER_STEP
    extra = n // tm if x is not None else 0
    expert_step = lambda s: jnp.minimum(s, steps - 1)
    in_specs = [
        pl.BlockSpec((1, 1, slots), lambda bb, s: (bb * steps + expert_step(s), 0, 0),
                     memory_space=pltpu.SMEM),
        pl.BlockSpec((1, 1, slots * SUB, LANES), lambda bb, s: (bb, expert_step(s), 0, 0)),
    ]
    args = [idx, y_tiles.reshape(b, steps, slots * SUB, LANES)]
    if x is not None:
        piece = pl.BlockSpec((1, tm, D), lambda bb, s: (bb, jnp.maximum(s - steps, 0), 0))
        in_specs.append(piece)
        args.append(x)
        out_spec, out_shape = piece, jax.ShapeDtypeStruct((b, n, D), F32)
        scratch = [pltpu.VMEM((1, rows, LANES), F32)]
    else:
        out_spec = pl.BlockSpec((1, rows, LANES), lambda bb, s: (bb, 0, 0))
        out_shape, scratch = jax.ShapeDtypeStruct((b, rows, LANES), F32), []
    return pl.pallas_call(
        _scatter_kernel,
        grid=(b, steps + extra),
        in_specs=in_specs,
        out_specs=out_spec,
        out_shape=out_shape,
        scratch_shapes=scratch,
        compiler_params=_params(("parallel", "arbitrary"), 56),
        name="scatter_add",
    )(*args)


def _moe(hf_tiles, aff_t, gf, w_gate, w_up, w_down, layer, x=None):
    b, _, n = aff_t.shape
    cap = CAP_FACTOR * n // N_EXP
    chunks = n // LANES
    idx, gate = _select(aff_t.reshape(b * N_EXP * chunks, LANES), cap, chunks)
    idx = idx.reshape(b * N_EXP // EXPERTS_PER_STEP, 1, EXPERTS_PER_STEP * cap)
    gate = gate.reshape(b * N_EXP, 1, cap)
    xs = _gather(idx, hf_tiles, cap)
    y_tiles = _ffn(xs, gate, gf, w_gate, w_up, w_down, layer)
    return _scatter(idx, y_tiles, n, x)


def _rope_tables(n_lat, n_ctx):
    rows = n_lat // GRID_W
    r = jnp.repeat(jnp.arange(rows, dtype=F32), GRID_W)
    col = jnp.tile(jnp.arange(GRID_W, dtype=F32), rows)
    n_freq = HEAD_DIM // 4
    inv = ROPE_BASE ** (-jnp.arange(n_freq, dtype=F32) / n_freq)
    ar = r[:, None] * inv
    ac = col[:, None] * inv
    ang = jnp.concatenate([ar, ar, ac, ac], axis=-1)
    sign = jnp.where((jnp.arange(HEAD_DIM) % 32) < 16, -1.0, 1.0).astype(F32)
    cos = jnp.concatenate([jnp.cos(ang), jnp.ones((n_ctx, HEAD_DIM), F32)], axis=0)
    sin_s = jnp.concatenate([jnp.sin(ang) * sign, jnp.zeros((n_ctx, HEAD_DIM), F32)], axis=0)
    return jnp.tile(cos, (1, 2)).T, jnp.tile(sin_s, (1, 2)).T


def _dft_tables(n_lat):
    c = jnp.arange(FOUR_G, dtype=I32)
    ang_c = (2.0 * math.pi / FOUR_G) * ((c[:, None] * c[None, :]) % FOUR_G).astype(F32)
    eye = jnp.eye(FOUR_W // FOUR_G, dtype=F32)
    bd = jnp.concatenate([jnp.kron(eye, jnp.cos(ang_c)), jnp.kron(eye, jnp.sin(ang_c))], axis=1)
    bd = (bd * FOUR_G ** -0.5).astype(BF)
    tr = 256
    n = jnp.arange(n_lat, dtype=I32)[None, :]
    r = jnp.arange(tr, dtype=I32)[:, None]
    i = jnp.arange(n_lat // tr, dtype=I32)[:, None]
    row_ang = (2.0 * math.pi / n_lat) * ((r * n) % n_lat).astype(F32)
    tile_ang = (2.0 * math.pi / n_lat) * ((i * tr * n) % n_lat).astype(F32)
    scale = n_lat ** -0.5
    tables = (jnp.cos(row_ang) * scale, jnp.sin(row_ang) * scale,
              jnp.cos(tile_ang)[:, None, :], jnp.sin(tile_ang)[:, None, :])
    return bd, tables


def kernel(x, c, ctx, c_ctx, ada_w, ada_b, norm_mix, norm_ffn, attn_w_in, attn_q_norm, attn_k_norm,
           lam_q1, lam_k1, lam_q2, lam_k2, attn_subln, attn_w_out, conv_w_in, conv_w, conv_w_out,
           router_w, moe_w_gate, moe_w_up, moe_w_down):
    b, n, _ = x.shape
    n_ctx = ctx.shape[1]
    assert x.shape[2] == D and n % 512 == 0 and n_ctx % 256 == 0

    cond8 = jnp.concatenate([c, c_ctx[None, :], jnp.zeros((8 - b - 1, D), F32)], axis=0)
    ada = _ada(cond8, ada_w, ada_b)

    def mods(layer):
        m = ada[layer].reshape(8, 6, D)
        return [m[:, j] for j in range(6)]

    vec = lambda t: t[:b].reshape(b, 1, D)

    sh_m, sc_m, g_m, sh_f, sc_f, g_f = mods(0)
    both = lambda t: jnp.stack([t[:b], jnp.broadcast_to(t[b], (b, D))], axis=1).reshape(b, 2, 1, D)
    cos_t, sin_t = _rope_tables(n, n_ctx)
    bd, dft_tabs = _dft_tables(n)
    col2 = lambda t: jnp.tile(t.reshape(HEAD_DIM, 1), (2, 1))
    w_in = attn_w_in[0]
    q, k, v, f = _proj0(x, ctx, norm_mix[0].reshape(1, D), both(sh_m), both(sc_m),
                        w_in[:, 2 * QK_W + V_W:].astype(BF), w_in[:, :2 * QK_W + V_W].T.astype(BF),
                        cos_t, sin_t, col2(attn_q_norm[0]), col2(attn_k_norm[0]))
    lam_init = 0.8 - 0.6 * math.exp(-0.3 * 0)
    lamv = jnp.zeros((8, LANES), F32).at[:4, :HEAD_DIM].set(
        jnp.stack([lam_q1[0], lam_k1[0], lam_q2[0], lam_k2[0]]))
    score_bound = (1.01 * HEAD_DIM * Q_SCALE * jnp.max(jnp.abs(attn_q_norm[0]))
                   * jnp.max(jnp.abs(attn_k_norm[0])) + 0.1).reshape(1)
    att = _attention(score_bound, q, k, v, lamv, attn_subln[0].reshape(V_DIM, 1), n, lam_init)
    four = _fourier(f, bd, dft_tabs, n)
    x1, hf, aff_t = _merge0(att, four, attn_w_out[0].astype(BF), x, vec(g_m),
                            norm_ffn[0].reshape(1, D), vec(sh_f), vec(sc_f), router_w[0].T)
    moe0 = _moe(hf, aff_t, vec(g_f), moe_w_gate, moe_w_up, moe_w_down, 0)

    sh_m, sc_m, g_m, sh_f, sc_f, g_f = mods(1)
    bg, z, x2 = _proj1(x1, moe0, norm_mix[1].reshape(1, D), vec(sh_m), vec(sc_m),
                       conv_w_in[0].astype(BF))
    conv_w8 = jnp.zeros((8, D), F32).at[:3].set(conv_w[0])
    x3, hf, aff_t = _merge1(bg, z, conv_w8, conv_w_out[0].astype(BF), x2, vec(g_m),
                            norm_ffn[1].reshape(1, D), vec(sh_f), vec(sc_f), router_w[1].T)
    return _moe(hf, aff_t, vec(g_f), moe_w_gate, moe_w_up, moe_w_down, 1, x3)
```

```python
import functools
import math

import jax
import jax.numpy as jnp
from jax import lax
from jax.experimental import pallas as pl
from jax.experimental.pallas import tpu as pltpu

BF = jnp.bfloat16
F32 = jnp.float32
I32 = jnp.int32

D = 1024
GRID_W = 64
N_HEADS = 6
HEAD_DIM = 64
V_DIM = 2 * HEAD_DIM
QK_W = N_HEADS * 2 * HEAD_DIM
V_W = N_HEADS * V_DIM
FOUR_W = 256
FOUR_G = 64
IN_W = 2 * QK_W + V_W + FOUR_W
N_EXP = 16
EXPERTS_PER_STEP = 4
CAP_FACTOR = 2
ROPE_BASE = 10000.0
EPS = 1e-6
Q_SCALE = HEAD_DIM ** -0.5 * math.log2(math.e)
ATTN_SHIFT_LIMIT = 40.0
LANES = 128
MIB = 1024 * 1024

NT = (((1,), (1,)), ((), ()))


def _params(sem, vmem_mib):
    return pltpu.CompilerParams(dimension_semantics=sem, vmem_limit_bytes=vmem_mib * MIB)


def _split2(x):
    hi = x.astype(BF)
    lo = (x - hi.astype(F32)).astype(BF)
    return hi, lo


def _dot3(a, b, dims=(((1,), (0,)), ((), ()))):
    ah, al = _split2(a)
    bh, bl = _split2(b)
    dg = functools.partial(lax.dot_general, dimension_numbers=dims, preferred_element_type=F32)
    return dg(ah, bh) + dg(ah, bl) + dg(al, bh)


def _modulate(x, nw, shift, scale):
    ms = jnp.mean(x * x, axis=-1, keepdims=True)
    return (x * lax.rsqrt(ms + EPS) * nw) * (1.0 + scale) + shift


def _ada_kernel(c_ref, w_ref, b_ref, o_ref):
    cv = c_ref[...]
    s = cv * (1.0 / (1.0 + jnp.exp(-cv)))
    o_ref[0] = _dot3(s, w_ref[0]) + b_ref[0]


def _ada(cond8, ada_w, ada_b):
    depth = ada_w.shape[0]
    tn = 1536
    return pl.pallas_call(
        _ada_kernel,
        grid=(depth, 6 * D // tn),
        in_specs=[
            pl.BlockSpec((8, D), lambda l, j: (0, 0)),
            pl.BlockSpec((1, D, tn), lambda l, j: (l, 0, j)),
            pl.BlockSpec((1, 1, tn), lambda l, j: (l, 0, j)),
        ],
        out_specs=pl.BlockSpec((1, 8, tn), lambda l, j: (l, 0, j)),
        out_shape=jax.ShapeDtypeStruct((depth, 8, 6 * D), F32),
        compiler_params=_params(("arbitrary", "arbitrary"), 40),
        name="ada",
    )(cond8, ada_w, ada_b.reshape(depth, 1, 6 * D))


def _norm_rope_t(xt, wn_col, cos_t, sin_t, out_scale):
    q4 = HEAD_DIM // 4
    halves = []
    for s in range(2):
        x = xt[s * HEAD_DIM:(s + 1) * HEAD_DIM]
        ms = jnp.sum(x * x, axis=0, keepdims=True) * (1.0 / HEAD_DIM)
        y = x * lax.rsqrt(ms + EPS) * wn_col[s * HEAD_DIM:(s + 1) * HEAD_DIM]
        rot = jnp.concatenate([y[q4:2 * q4], y[:q4], y[3 * q4:], y[2 * q4:3 * q4]], axis=0)
        halves.append(y * cos_t[s * HEAD_DIM:(s + 1) * HEAD_DIM] + rot * sin_t[s * HEAD_DIM:(s + 1) * HEAD_DIM])
    out = jnp.concatenate(halves, axis=0)
    return out if out_scale == 1.0 else out * out_scale


def _proj0_kernel(x_ref, ctx_ref, nw_ref, sh_ref, sc_ref, wf_ref, wt_ref, cost_ref, sint_ref,
                  qn_ref, kn_ref, q_ref, k_ref, v_ref, f_ref, *, lat_tiles):
    x = jnp.where(pl.program_id(1) < lat_tiles, x_ref[0], ctx_ref[0])
    h = _modulate(x, nw_ref[...], sh_ref[0, 0], sc_ref[0, 0]).astype(BF)
    def project_t(r0, width):
        return lax.dot_general(wt_ref[r0:r0 + width, :], h, NT, preferred_element_type=F32)

    cos_t = cost_ref[...]
    sin_t = sint_ref[...]
    for out_ref, wn_ref, r0, scale in ((q_ref, qn_ref, 0, Q_SCALE), (k_ref, kn_ref, QK_W, 1.0)):
        p_t = project_t(r0, QK_W)
        wn_col = wn_ref[...]
        for hh in range(N_HEADS):
            c0 = hh * V_DIM
            out_ref[0, c0:c0 + V_DIM, :] = _norm_rope_t(
                p_t[c0:c0 + V_DIM], wn_col, cos_t, sin_t, scale).astype(BF)
    v_ref[0] = project_t(2 * QK_W, V_W).astype(BF)
    f_ref[0] = jnp.dot(h, wf_ref[...], preferred_element_type=F32).astype(BF)


def _proj0(x, ctx, nw, sh2, sc2, w_four, w_qkv_t, cos_t, sin_t, qn_col, kn_col):
    b, n_lat, _ = x.shape
    n_ctx = ctx.shape[1]
    s_tot = n_lat + n_ctx
    tm = 256
    lat_tiles = n_lat // tm
    mod_spec = pl.BlockSpec((1, 1, 1, D), lambda bb, i: (bb, i // lat_tiles, 0, 0))
    full = lambda shape: pl.BlockSpec(shape, lambda bb, i: tuple(0 for _ in shape))
    feat = lambda w: pl.BlockSpec((1, w, tm), lambda bb, i: (bb, 0, i))
    feat_shape = lambda w: jax.ShapeDtypeStruct((b, w, s_tot), BF)
    return pl.pallas_call(
        functools.partial(_proj0_kernel, lat_tiles=lat_tiles),
        grid=(b, s_tot // tm),
        in_specs=[
            pl.BlockSpec((1, tm, D), lambda bb, i: (bb, jnp.minimum(i, lat_tiles - 1), 0)),
            pl.BlockSpec((1, tm, D), lambda bb, i: (bb, jnp.maximum(i - lat_tiles, 0), 0)),
            full((1, D)), mod_spec, mod_spec, full((D, FOUR_W)), full((2 * QK_W + V_W, D)),
            pl.BlockSpec((LANES, tm), lambda bb, i: (0, i)),
            pl.BlockSpec((LANES, tm), lambda bb, i: (0, i)),
            full((LANES, 1)), full((LANES, 1)),
        ],
        out_specs=[feat(QK_W), feat(QK_W), feat(V_W),
                   pl.BlockSpec((1, tm, FOUR_W), lambda bb, i: (bb, i, 0))],
        out_shape=[feat_shape(QK_W), feat_shape(QK_W), feat_shape(V_W),
                   jax.ShapeDtypeStruct((b, s_tot, FOUR_W), BF)],
        compiler_params=_params(("parallel", "arbitrary"), 48),
        name="proj0",
    )(x, ctx, nw, sh2, sc2, w_four, w_qkv_t, cos_t, sin_t, qn_col, kn_col)


def _attn_kernel(bound_ref, q_ref, kt_ref, vt_ref, lam_ref, sub_ref, o_ref, k_ref, m_ref, e_ref,
                 *, lam_init, kc):
    @pl.when(pl.program_id(2) == 0)
    def _():
        k_ref[...] = kt_ref[0].astype(F32).T.astype(BF)

    q = q_ref[0]
    feat = lax.broadcasted_iota(I32, q.shape, 0)
    zero = jnp.zeros_like(q)
    lv = lam_ref[...]
    t1 = jnp.sum(lv[0:1] * lv[1:2], axis=-1, keepdims=True)
    t2 = jnp.sum(lv[2:3] * lv[3:4], axis=-1, keepdims=True)
    lam = jnp.exp(t1) - jnp.exp(t2) + lam_init

    qs = (jnp.where(feat < HEAD_DIM, q, zero), jnp.where(feat < HEAD_DIM, zero, q))
    tq = q.shape[1]
    n_chunks = k_ref.shape[0] // kc

    def scores(h, c):
        return jnp.dot(k_ref[c * kc:(c + 1) * kc, :], qs[h], preferred_element_type=F32)

    def fold(x, op):
        return op(x.reshape(kc // 8, 8, tq), axis=0)

    bound = bound_ref[0]
    small = bound <= ATTN_SHIFT_LIMIT

    @pl.when(small)
    def _():
        m_ref[...] = jnp.zeros(m_ref.shape, F32) + bound

    @pl.when(jnp.logical_not(small))
    def _():
        for h in range(2):
            m = jnp.full((8, tq), -jnp.inf, F32)
            for c in range(n_chunks):
                m = jnp.maximum(m, fold(scores(h, c), jnp.max))
            m_ref[h] = jnp.broadcast_to(jnp.max(m, axis=0, keepdims=True), (8, tq))

    ms = [m_ref[h][0:1, :] for h in range(2)]
    ls = [jnp.zeros((8, tq), F32) for _ in range(2)]
    for c in range(n_chunks):
        for h in range(2):
            e = jnp.exp2(scores(h, c) - ms[h])
            ls[h] = ls[h] + fold(e, jnp.sum)
            e_ref[h, c] = e.astype(BF)
    l0, l1 = [jnp.sum(l, axis=0, keepdims=True) for l in ls]

    beta = (lam * l0 / l1).astype(BF)
    acc = jnp.zeros((V_DIM, tq), F32)
    for c in range(n_chunks):
        a = e_ref[0, c] - beta * e_ref[1, c]
        acc = acc + jnp.dot(vt_ref[0, :, c * kc:(c + 1) * kc], a, preferred_element_type=F32)
    o = acc * (1.0 / l0)
    ms = jnp.mean(o * o, axis=0, keepdims=True)
    o = o * lax.rsqrt(ms + EPS) * sub_ref[...] * (1.0 - lam_init)
    o_ref[0] = o.T.astype(BF)


def _attention(bound, q, k, v, lamv, subln, n_lat, lam_init):
    b, _, s_tot = k.shape
    tq = 1024
    kc = 256
    assert s_tot % kc == 0
    kv_spec = pl.BlockSpec((1, V_DIM, s_tot), lambda bb, hh, i: (bb, hh, 0))
    return pl.pallas_call(
        functools.partial(_attn_kernel, lam_init=lam_init, kc=kc),
        grid=(b, N_HEADS, n_lat // tq),
        in_specs=[
            pl.BlockSpec(memory_space=pltpu.SMEM),
            pl.BlockSpec((1, V_DIM, tq), lambda bb, hh, i: (bb, hh, i)),
            kv_spec, kv_spec,
            pl.BlockSpec((8, LANES), lambda bb, hh, i: (0, 0)),
            pl.BlockSpec((V_DIM, 1), lambda bb, hh, i: (0, 0)),
        ],
        out_specs=pl.BlockSpec((1, tq, V_DIM), lambda bb, hh, i: (bb, i, hh)),
        out_shape=jax.ShapeDtypeStruct((b, n_lat, V_W), BF),
        scratch_shapes=[pltpu.VMEM((s_tot, V_DIM), BF),
                        pltpu.VMEM((2, 8, tq), F32),
                        pltpu.VMEM((2, s_tot // kc, kc, tq), BF)],
        compiler_params=_params(("parallel", "parallel", "arbitrary"), 48),
        name="diff_attn",
    )(bound, q, k, v, lamv, subln)


def _fourier_kernel(f_ref, bd_ref, cb_ref, sb_ref, ca_ref, sa_ref, o_ref, g_ref):
    nb, n_lat, _ = f_ref.shape

    @pl.when(pl.program_id(0) == 0)
    def _():
        rows = 1024
        for bb in range(nb):
            for r in range(n_lat // rows):
                fc = jnp.dot(f_ref[bb, r * rows:(r + 1) * rows, :], bd_ref[...],
                             preferred_element_type=F32)
                g_ref[r * rows:(r + 1) * rows, bb * FOUR_W:(bb + 1) * FOUR_W] = fc[:, :FOUR_W].astype(BF)
                g_ref[n_lat + r * rows:n_lat + (r + 1) * rows,
                      bb * FOUR_W:(bb + 1) * FOUR_W] = fc[:, FOUR_W:].astype(BF)

    ca = ca_ref[0]
    sa = sa_ref[0]
    cb = cb_ref[...]
    sb = sb_ref[...]
    w_cos = (cb * ca - sb * sa).astype(BF)
    w_sin = (sb * ca + cb * sa).astype(BF)
    y = (jnp.dot(w_cos, g_ref[:n_lat, :], preferred_element_type=F32)
         - jnp.dot(w_sin, g_ref[n_lat:, :], preferred_element_type=F32))
    for bb in range(nb):
        o_ref[bb] = y[:, bb * FOUR_W:(bb + 1) * FOUR_W].astype(BF)


def _fourier(f, bd, tables, n_lat):
    b = f.shape[0]
    cos_row, sin_row, cos_tile, sin_tile = tables
    tr = cos_row.shape[0]
    row_tab = pl.BlockSpec((tr, n_lat), lambda i: (0, 0))
    tile_tab = pl.BlockSpec((1, 1, n_lat), lambda i: (i, 0, 0))
    return pl.pallas_call(
        _fourier_kernel,
        grid=(n_lat // tr,),
        in_specs=[
            pl.BlockSpec((b, n_lat, FOUR_W), lambda i: (0, 0, 0)),
            pl.BlockSpec((FOUR_W, 2 * FOUR_W), lambda i: (0, 0)),
            row_tab, row_tab, tile_tab, tile_tab,
        ],
        out_specs=pl.BlockSpec((b, tr, FOUR_W), lambda i: (0, i, 0)),
        out_shape=jax.ShapeDtypeStruct((b, n_lat, FOUR_W), BF),
        scratch_shapes=[pltpu.VMEM((2 * n_lat, b * FOUR_W), BF)],
        compiler_params=_params(("arbitrary",), 56),
        name="fourier",
    )(f, bd, cos_row, sin_row, cos_tile, sin_tile)


SUB = D // LANES


def _store_token_tiles(ref, val):
    rows = val.shape[0]
    for j in range(SUB):
        ref[0, pl.ds(j, rows, stride=SUB), :] = val[:, j * LANES:(j + 1) * LANES]


def _load_token_tiles(ref, rows):
    return jnp.concatenate([ref[0, pl.ds(j, rows, stride=SUB), :] for j in range(SUB)], axis=1)


def _residual_router(x, y, gm, nf, shf, scf, wr_t, x1_ref, hf_ref, aff_ref):
    x1 = x + gm * y
    x1_ref[0] = x1
    hf = _modulate(x1, nf, shf, scf)
    _store_token_tiles(hf_ref, hf)
    logits = _dot3(wr_t, hf, NT)
    m = jnp.max(logits, axis=0, keepdims=True)
    e = jnp.exp(logits - m)
    aff_ref[0] = e / jnp.sum(e, axis=0, keepdims=True)


def _merge0_kernel(att_ref, four_ref, w_ref, x_ref, gm_ref, nf_ref, shf_ref, scf_ref, wr_ref,
                   x1_ref, hf_ref, aff_ref):
    y = jnp.dot(att_ref[0], w_ref[:V_W, :], preferred_element_type=F32)
    y = y + jnp.dot(four_ref[0], w_ref[V_W:, :], preferred_element_type=F32)
    _residual_router(x_ref[0], y, gm_ref[0], nf_ref[...], shf_ref[0], scf_ref[0], wr_ref[...],
                     x1_ref, hf_ref, aff_ref)


def _router_specs(tm):
    vec = pl.BlockSpec((1, 1, D), lambda bb, i: (bb, 0, 0))
    row = pl.BlockSpec((1, tm, D), lambda bb, i: (bb, i, 0))
    in_specs = [vec, pl.BlockSpec((1, D), lambda bb, i: (0, 0)), vec, vec,
                pl.BlockSpec((N_EXP, D), lambda bb, i: (0, 0))]
    tiles = _tile_spec(tm)
    out_specs = [row, tiles, pl.BlockSpec((1, N_EXP, tm), lambda bb, i: (bb, 0, i))]
    return row, in_specs, out_specs


def _tile_spec(tm):
    return pl.BlockSpec((1, tm * SUB, LANES), lambda bb, i: (bb, i, 0))


def _router_out_shape(b, n):
    return [jax.ShapeDtypeStruct((b, n, D), F32), jax.ShapeDtypeStruct((b, n * SUB, LANES), F32),
            jax.ShapeDtypeStruct((b, N_EXP, n), F32)]


def _merge0(att, four, w_bf, x, gm, nf, shf, scf, wr_t):
    b, n, _ = x.shape
    tm = 512
    row, r_in, r_out = _router_specs(tm)
    return pl.pallas_call(
        _merge0_kernel,
        grid=(b, n // tm),
        in_specs=[
            pl.BlockSpec((1, tm, V_W), lambda bb, i: (bb, i, 0)),
            pl.BlockSpec((1, tm, FOUR_W), lambda bb, i: (bb, i, 0)),
            pl.BlockSpec((D, D), lambda bb, i: (0, 0)),
            row,
        ] + r_in,
        out_specs=r_out,
        out_shape=_router_out_shape(b, n),
        compiler_params=_params(("parallel", "arbitrary"), 48),
        name="merge0",
    )(att, four, w_bf, x, gm, nf, shf, scf, wr_t)


HALO = 8


def _mixer1_kernel(x_ref, xp_ref, xn_ref, moe_ref, moep_ref, moen_ref, nw_ref, sh_ref, sc_ref, win_ref,
                   cw_ref, wout_ref, gm_ref, nf_ref, shf_ref, scf_ref, wr_ref, x1_ref, hf_ref, aff_ref):
    i = pl.program_id(1)
    last = pl.num_programs(1) - 1
    tm = x_ref.shape[1]
    x = x_ref[0] + _load_token_tiles(moe_ref, tm)
    x_prev = xp_ref[0] + _load_token_tiles(moep_ref, HALO)
    x_next = xn_ref[0] + _load_token_tiles(moen_ref, HALO)
    x_all = jnp.concatenate([x_prev, x, x_next], axis=0)
    h_all = _modulate(x_all, nw_ref[...], sh_ref[0], sc_ref[0]).astype(BF)
    cg = jnp.dot(h_all, win_ref[:, D:2 * D], preferred_element_type=F32)
    u = jnp.dot(h_all, win_ref[:, 2 * D:], preferred_element_type=F32)
    z_raw = cg * u
    core = slice(HALO, HALO + tm)
    z_all = jnp.concatenate([z_raw[:HALO] * jnp.where(i > 0, 1.0, 0.0), z_raw[core],
                             z_raw[HALO + tm:] * jnp.where(i < last, 1.0, 0.0)], axis=0)
    z_up = pltpu.roll(z_all, 1, axis=0)[core]
    z_dn = pltpu.roll(z_all, tm + 2 * HALO - 1, axis=0)[core]
    cw = cw_ref[...]
    conv = cw[0:1] * z_up + cw[1:2] * z_all[core] + cw[2:3] * z_dn
    bg = jnp.dot(h_all[core], win_ref[:, :D], preferred_element_type=F32)
    y = jnp.dot((bg * conv).astype(BF), wout_ref[...], preferred_element_type=F32)
    _residual_router(x, y, gm_ref[0], nf_ref[...], shf_ref[0], scf_ref[0], wr_ref[...],
                     x1_ref, hf_ref, aff_ref)


def _mixer1(x, moe_tiles, nw, sh, sc, w_in, conv_w8, w_out, gm, nf, shf, scf, wr_t):
    b, n, _ = x.shape
    tm = 512
    per = tm // HALO
    n_halo = n // HALO
    prev_blk = lambda bb, i: (bb, jnp.maximum(i * per - 1, 0), 0)
    next_blk = lambda bb, i: (bb, jnp.minimum((i + 1) * per, n_halo - 1), 0)
    row, r_in, r_out = _router_specs(tm)
    vec = pl.BlockSpec((1, 1, D), lambda bb, i: (bb, 0, 0))
    full = lambda shape: pl.BlockSpec(shape, lambda bb, i: (0, 0))
    return pl.pallas_call(
        _mixer1_kernel,
        grid=(b, n // tm),
        in_specs=[
            row, pl.BlockSpec((1, HALO, D), prev_blk), pl.BlockSpec((1, HALO, D), next_blk),
            _tile_spec(tm), pl.BlockSpec((1, HALO * SUB, LANES), prev_blk),
            pl.BlockSpec((1, HALO * SUB, LANES), next_blk),
            full((1, D)), vec, vec, full((D, 3 * D)), full((8, D)), full((D, D)),
        ] + r_in,
        out_specs=r_out,
        out_shape=_router_out_shape(b, n),
        compiler_params=_params(("parallel", "arbitrary"), 56),
        name="mixer1",
    )(x, x, x, moe_tiles, moe_tiles, moe_tiles, nw, sh, sc, w_in, conv_w8, w_out, gm, nf, shf, scf, wr_t)


def _select_kernel(a_ref, idx_ref, gate_ref, cl_s, off_s, inc_s, hi_s, mid_s, lo_s, *, cap, chunks):
    a = a_ref[...]
    rows = a.shape[0]
    groups = rows // chunks
    per_batch = N_EXP * chunks
    assert chunks & (chunks - 1) == 0
    shift = chunks.bit_length() - 1

    member = jnp.where((lax.broadcasted_iota(I32, (groups, rows), 1) >> shift)
                       == lax.broadcasted_iota(I32, (groups, rows), 0), 1.0, 0.0).astype(BF)
    ri =lax.broadcasted_iota(I32, (per_batch, per_batch), 0)
    ci = lax.broadcasted_iota(I32, (per_batch, per_batch), 1)
    same_f = jnp.where((ri >> shift) == (ci >> shift), 1.0, 0.0)
    same = same_f.astype(BF)
    lower = (same_f * jnp.where(ci < ri, 1.0, 0.0)).astype(BF)
    li = lax.broadcasted_iota(I32, (LANES, LANES), 0)
    lj = lax.broadcasted_iota(I32, (LANES, LANES), 1)
    incl = jnp.where(li <= lj, 1.0, 0.0).astype(BF)

    def bcast(col):
        return jnp.broadcast_to(col, (per_batch, LANES))

    def prefix(maskf):
        cl = jnp.dot(maskf.astype(BF), incl, preferred_element_type=F32)
        tot = bcast(cl[:, LANES - 1:LANES])
        off = jnp.dot(lower, tot.astype(BF), preferred_element_type=F32)
        return cl, off, tot

    min_normal = 0x00800000

    a3 = a.reshape(groups, chunks, LANES)

    def at_least(bits):
        thr_g = lax.bitcast_convert_type(bits, F32)[:, None, :]
        return jnp.where(a3 >= thr_g, 1.0, 0.0).reshape(rows, LANES)

    def search(step, t):
        cand = t | jnp.left_shift(jnp.int32(1), 30 - step)
        part = jnp.dot(member, at_least(cand).astype(BF), preferred_element_type=F32)
        cnt = jnp.sum(part, axis=-1, keepdims=True)
        return jnp.where(cnt >= cap, jnp.where(cand >= min_normal, cand, t), t)

    thr = lax.fori_loop(0, 31, search, jnp.zeros((groups, LANES), I32))
    gtf = at_least(jnp.maximum(thr + 1, min_normal))
    eqf = at_least(thr) - gtf
    for bb in range(rows // per_batch):
        sl = slice(bb * per_batch, (bb + 1) * per_batch)
        gt_b, eq_b = gtf[sl], eqf[sl]
        n_gt = jnp.dot(same, bcast(jnp.sum(gt_b, axis=-1, keepdims=True)).astype(BF),
                       preferred_element_type=F32)
        cl_eq, off_eq, _ = prefix(eq_b)
        sel = gt_b + eq_b * jnp.where(cl_eq + off_eq <= cap - n_gt, 1.0, 0.0)
        cl, off, tot = prefix(sel)
        cl_s[sl, :] = cl.astype(BF)
        off_s[sl, :] = off
        inc_s[sl, :] = off + tot
        a_b = a[sl]
        a_hi = a_b.astype(BF)
        r1 = a_b - a_hi.astype(F32)
        a_mid = r1.astype(BF)
        hi_s[sl, :] = a_hi
        mid_s[sl, :] = a_mid
        lo_s[sl, :] = (r1 - a_mid.astype(F32)).astype(BF)

    slot = lax.broadcasted_iota(I32, (chunks, cap), 1).astype(F32)
    chunk_id = lax.broadcasted_iota(I32, (chunks, cap), 0).astype(F32)
    lane_id = lax.broadcasted_iota(I32, (LANES, cap), 0).astype(F32)
    tn = (((0,), (0,)), ((), ()))
    reps = cap // LANES

    def widen(x):
        return jnp.concatenate([x] * reps, axis=1)

    def per_group(g, carry):
        win = pl.ds(pl.multiple_of(g * chunks, chunks), chunks)
        inc_g = widen(inc_s[win, :])
        off_g = widen(off_s[win, :])
        chunk_of = jnp.sum(jnp.where(inc_g <= slot, 1.0, 0.0), axis=0, keepdims=True)
        pick_f = jnp.where(chunk_id == chunk_of, 1.0, 0.0)
        before = jnp.sum(pick_f * off_g, axis=0, keepdims=True)
        pick = pick_f.astype(BF)
        counts = lax.dot_general(cl_s[win, :], pick, tn, preferred_element_type=F32)
        lane_of = jnp.sum(jnp.where(counts <= slot[0:1] - before, 1.0, 0.0), axis=0, keepdims=True)
        aff = (lax.dot_general(hi_s[win, :], pick, tn, preferred_element_type=F32)
               + lax.dot_general(mid_s[win, :], pick, tn, preferred_element_type=F32)
               + lax.dot_general(lo_s[win, :], pick, tn, preferred_element_type=F32))
        gate = jnp.sum(jnp.where(lane_id == lane_of, aff, 0.0), axis=0, keepdims=True)
        idx_ref[pl.ds(g, 1), :] = ((chunk_of * float(LANES) + lane_of) * float(SUB)).astype(I32)
        gate_ref[pl.ds(g, 1), :] = gate
        return carry

    lax.fori_loop(0, groups, per_group, 0, unroll=2)


def _select(aff_rows, cap, chunks):
    rows = aff_rows.shape[0]
    groups = rows // chunks
    whole = lambda shape: pl.BlockSpec(shape, lambda i: (0, 0))
    return pl.pallas_call(
        functools.partial(_select_kernel, cap=cap, chunks=chunks),
        grid=(1,),
        in_specs=[whole((rows, LANES))],
        out_specs=[whole((groups, cap)), whole((groups, cap))],
        out_shape=[jax.ShapeDtypeStruct((groups, cap), I32), jax.ShapeDtypeStruct((groups, cap), F32)],
        scratch_shapes=[pltpu.VMEM((rows, LANES), BF), pltpu.VMEM((rows, LANES), F32),
                        pltpu.VMEM((rows, LANES), F32), pltpu.VMEM((rows, LANES), BF),
                        pltpu.VMEM((rows, LANES), BF), pltpu.VMEM((rows, LANES), BF)],
        compiler_params=_params(("arbitrary",), 48),
        name="select",
    )(aff_rows)


def _gather_kernel(idx_ref, h_ref, o_ref):
    slots = idx_ref.shape[2]

    def body(c, carry):
        t = pl.multiple_of(idx_ref[0, 0, c], SUB)
        o_ref[0, 0, pl.ds(pl.multiple_of(c * SUB, SUB), SUB), :] = h_ref[0, pl.ds(t, SUB), :]
        return carry

    lax.fori_loop(0, slots, body, 0, unroll=8)


def _gather(idx, hf_tiles, cap):
    b, rows, _ = hf_tiles.shape
    steps = N_EXP // EXPERTS_PER_STEP
    slots = EXPERTS_PER_STEP * cap
    xs = pl.pallas_call(
        _gather_kernel,
        grid=(b, steps),
        in_specs=[
            pl.BlockSpec((1, 1, slots), lambda bb, s: (bb * steps + s, 0, 0), memory_space=pltpu.SMEM),
            pl.BlockSpec((1, rows, LANES), lambda bb, s: (bb, 0, 0)),
        ],
        out_specs=pl.BlockSpec((1, 1, slots * SUB, LANES), lambda bb, s: (bb, s, 0, 0)),
        out_shape=jax.ShapeDtypeStruct((b, steps, slots * SUB, LANES), F32),
        compiler_params=_params(("parallel", "arbitrary"), 56),
        name="gather",
    )(idx, hf_tiles)
    return xs.reshape(b, N_EXP, cap * SUB, LANES)


def _ffn_kernel(x_ref, gate_ref, gf_ref, wg_ref, wu_ref, wd_ref, o_ref, wg_s, wu_s, wd_s):
    @pl.when(pl.program_id(1) == 0)
    def _():
        wg_s[...] = wg_ref[...].astype(BF)
        wu_s[...] = wu_ref[...].astype(BF)
        wd_s[...] = wd_ref[...].astype(BF)

    cap = x_ref.shape[2] // SUB
    x = _load_token_tiles(x_ref.at[0], cap).astype(BF)
    dexp = wg_s.shape[1]
    step = 256
    y = None
    for c0 in range(0, dexp, step):
        g = jnp.dot(x, wg_s[:, c0:c0 + step], preferred_element_type=F32)
        u = jnp.dot(x, wu_s[:, c0:c0 + step], preferred_element_type=F32)
        act = (g * (1.0 / (1.0 + jnp.exp(-g))) * u).astype(BF)
        part = jnp.dot(act, wd_s[c0:c0 + step, :], preferred_element_type=F32)
        y = part if y is None else y + part
    gate_col =jnp.broadcast_to(gate_ref[0], (LANES, cap)).T
    y = y * jnp.concatenate([gate_col] * SUB, axis=1) * gf_ref[0]
    _store_token_tiles(o_ref.at[0], y)


def _ffn(xs, gate, gf, w_gate, w_up, w_down, layer):
    b, _, tile_rows, _ = xs.shape
    cap = tile_rows // SUB
    dexp = w_gate.shape[-1]
    gspec = pl.BlockSpec((1, 1, cap), lambda e, bb: (bb * N_EXP + e, 0, 0))
    gfspec = pl.BlockSpec((1, 1, D), lambda e, bb: (bb, 0, 0))
    yspec = pl.BlockSpec((1, 1, tile_rows, LANES), lambda e, bb: (bb, e, 0, 0))
    xspec = yspec

    def wspec(shape, switch_at):
        def index(e, bb):
            return (layer, jnp.minimum(e + jnp.where(bb >= switch_at, 1, 0), N_EXP - 1), 0, 0)
        return pl.BlockSpec((None, None) + shape, index)

    assert b >= 2
    stagger = [min(j, b - 1) for j in (1, 2, 3)]
    return pl.pallas_call(
        _ffn_kernel,
        grid=(N_EXP, b),
        in_specs=[xspec, gspec, gfspec, wspec((D, dexp), stagger[0]), wspec((D, dexp), stagger[1]),
                  wspec((dexp, D), stagger[2])],
        out_specs=yspec,
        out_shape=jax.ShapeDtypeStruct((b, N_EXP, cap * SUB, LANES), F32),
        scratch_shapes=[pltpu.VMEM((D, dexp), BF), pltpu.VMEM((D, dexp), BF), pltpu.VMEM((dexp, D), BF)],
        compiler_params=_params(("arbitrary", "arbitrary"), 56),
        name="expert_ffn",
    )(xs, gate, gf, w_gate, w_up, w_down)


def _scatter_kernel(idx_ref, y_ref, *rest):
    if len(rest) == 1:
        (o_ref,), x_ref = rest, None
        acc_ref = o_ref
    else:
        x_ref, o_ref, acc_ref = rest
    slots = idx_ref.shape[2]
    expert_steps = N_EXP // EXPERTS_PER_STEP
    step = pl.program_id(1)

    @pl.when(step == 0)
    def _():
        acc_ref[...] = jnp.zeros(acc_ref.shape, F32)

    group = 16

    def body(g, carry):
        c0 = g * group
        ts = [pl.multiple_of(idx_ref[0, 0, c0 + j], SUB) for j in range(group)]
        new = []
        for j in range(group):
            y = y_ref[0, 0, pl.ds(pl.multiple_of((c0 + j) * SUB, SUB), SUB), :]
            new.append(acc_ref[0, pl.ds(ts[j], SUB), :] + y)
        for j in range(group):
            acc_ref[0, pl.ds(ts[j], SUB), :] = new[j]
        return carry

    @pl.when(step < expert_steps)
    def _():
        lax.fori_loop(0, slots // group, body, 0)

    if x_ref is not None:
        @pl.when(step >= expert_steps)
        def _():
            rows = o_ref.shape[1]
            base = (step - expert_steps) * (rows * SUB)
            o_ref[0] = x_ref[0] + jnp.concatenate(
                [acc_ref[0, pl.ds(base + j, rows, stride=SUB), :] for j in range(SUB)], axis=1)


def _scatter(idx, y_tiles, n, x=None):
    b = y_tiles.shape[0]
    slots = idx.shape[2]
    assert (slots // EXPERTS_PER_STEP) % 16 == 0
    rows = n * SUB
    tm = 512
    steps = N_EXP // EXPERTS_PER_STEP
    extra = n // tm if x is not None else 0
    expert_step = lambda s: jnp.minimum(s, steps - 1)
    in_specs = [
        pl.BlockSpec((1, 1, slots), lambda bb, s: (bb * steps + expert_step(s), 0, 0),
                     memory_space=pltpu.SMEM),
        pl.BlockSpec((1, 1, slots * SUB, LANES), lambda bb, s: (bb, expert_step(s), 0, 0)),
    ]
    args = [idx, y_tiles.reshape(b, steps, slots * SUB, LANES)]
    if x is not None:
        piece = pl.BlockSpec((1, tm, D), lambda bb, s: (bb, jnp.maximum(s - steps, 0), 0))
        in_specs.append(piece)
        args.append(x)
        out_spec, out_shape = piece, jax.ShapeDtypeStruct((b, n, D), F32)
        scratch = [pltpu.VMEM((1, rows, LANES), F32)]
    else:
        out_spec = pl.BlockSpec((1, rows, LANES), lambda bb, s: (bb, 0, 0))
        out_shape, scratch = jax.ShapeDtypeStruct((b, rows, LANES), F32), []
    return pl.pallas_call(
        _scatter_kernel,
        grid=(b, steps + extra),
        in_specs=in_specs,
        out_specs=out_spec,
        out_shape=out_shape,
        scratch_shapes=scratch,
        compiler_params=_params(("parallel", "arbitrary"), 56),
        name="scatter_add",
    )(*args)


def _moe(hf_tiles, aff_t, gf, w_gate, w_up, w_down, layer, x=None):
    b, _, n = aff_t.shape
    cap = CAP_FACTOR * n // N_EXP
    chunks = n // LANES
    idx, gate = _select(aff_t.reshape(b * N_EXP * chunks, LANES), cap, chunks)
    idx = idx.reshape(b * N_EXP // EXPERTS_PER_STEP, 1, EXPERTS_PER_STEP * cap)
    gate = gate.reshape(b * N_EXP, 1, cap)
    xs = _gather(idx, hf_tiles, cap)
    y_tiles = _ffn(xs, gate, gf, w_gate, w_up, w_down, layer)
    return _scatter(idx, y_tiles, n, x)


def _rope_tables(n_lat, n_ctx):
    rows = n_lat // GRID_W
    r = jnp.repeat(jnp.arange(rows, dtype=F32), GRID_W)
    col = jnp.tile(jnp.arange(GRID_W, dtype=F32), rows)
    n_freq = HEAD_DIM // 4
    inv = ROPE_BASE ** (-jnp.arange(n_freq, dtype=F32) / n_freq)
    ar = r[:, None] * inv
    ac = col[:, None] * inv
    ang = jnp.concatenate([ar, ar, ac, ac], axis=-1)
    sign = jnp.where((jnp.arange(HEAD_DIM) % 32) < 16, -1.0, 1.0).astype(F32)
    cos = jnp.concatenate([jnp.cos(ang), jnp.ones((n_ctx, HEAD_DIM), F32)], axis=0)
    sin_s = jnp.concatenate([jnp.sin(ang) * sign, jnp.zeros((n_ctx, HEAD_DIM), F32)], axis=0)
    return jnp.tile(cos, (1, 2)).T, jnp.tile(sin_s, (1, 2)).T


def _dft_tables(n_lat):
    c = jnp.arange(FOUR_G, dtype=I32)
    ang_c = (2.0 * math.pi / FOUR_G) * ((c[:, None] * c[None, :]) % FOUR_G).astype(F32)
    eye = jnp.eye(FOUR_W // FOUR_G, dtype=F32)
    bd = jnp.concatenate([jnp.kron(eye, jnp.cos(ang_c)), jnp.kron(eye, jnp.sin(ang_c))], axis=1)
    bd = (bd * FOUR_G ** -0.5).astype(BF)
    tr = 256
    n = jnp.arange(n_lat, dtype=I32)[None, :]
    r = jnp.arange(tr, dtype=I32)[:, None]
    i = jnp.arange(n_lat // tr, dtype=I32)[:, None]
    row_ang = (2.0 * math.pi / n_lat) * ((r * n) % n_lat).astype(F32)
    tile_ang = (2.0 * math.pi / n_lat) * ((i * tr * n) % n_lat).astype(F32)
    scale = n_lat ** -0.5
    tables = (jnp.cos(row_ang) * scale, jnp.sin(row_ang) * scale,
              jnp.cos(tile_ang)[:, None, :], jnp.sin(tile_ang)[:, None, :])
    return bd, tables


def kernel(x, c, ctx, c_ctx, ada_w, ada_b, norm_mix, norm_ffn, attn_w_in, attn_q_norm, attn_k_norm,
           lam_q1, lam_k1, lam_q2, lam_k2, attn_subln, attn_w_out, conv_w_in, conv_w, conv_w_out,
           router_w, moe_w_gate, moe_w_up, moe_w_down):
    b, n, _ = x.shape
    n_ctx = ctx.shape[1]
    assert x.shape[2] == D and n % 512 == 0 and n_ctx % 256 == 0

    cond8 = jnp.concatenate([c, c_ctx[None, :], jnp.zeros((8 - b - 1, D), F32)], axis=0)
    ada = _ada(cond8, ada_w, ada_b)

    def mods(layer):
        m = ada[layer].reshape(8, 6, D)
        return [m[:, j] for j in range(6)]

    vec = lambda t: t[:b].reshape(b, 1, D)

    sh_m, sc_m, g_m, sh_f, sc_f, g_f = mods(0)
    both = lambda t: jnp.stack([t[:b], jnp.broadcast_to(t[b], (b, D))], axis=1).reshape(b, 2, 1, D)
    cos_t, sin_t = _rope_tables(n, n_ctx)
    bd, dft_tabs = _dft_tables(n)
    col2 = lambda t: jnp.tile(t.reshape(HEAD_DIM, 1), (2, 1))
    w_in = attn_w_in[0]
    q, k, v, f = _proj0(x, ctx, norm_mix[0].reshape(1, D), both(sh_m), both(sc_m),
                        w_in[:, 2 * QK_W + V_W:].astype(BF), w_in[:, :2 * QK_W + V_W].T.astype(BF),
                        cos_t, sin_t, col2(attn_q_norm[0]), col2(attn_k_norm[0]))
    lam_init = 0.8 - 0.6 * math.exp(-0.3 * 0)
    lamv = jnp.zeros((8, LANES), F32).at[:4, :HEAD_DIM].set(
        jnp.stack([lam_q1[0], lam_k1[0], lam_q2[0], lam_k2[0]]))
    score_bound = (1.01 * HEAD_DIM * Q_SCALE * jnp.max(jnp.abs(attn_q_norm[0]))
                   * jnp.max(jnp.abs(attn_k_norm[0])) + 0.1).reshape(1)
    att = _attention(score_bound, q, k, v, lamv, attn_subln[0].reshape(V_DIM, 1), n, lam_init)
    four = _fourier(f, bd, dft_tabs, n)
    x1, hf, aff_t = _merge0(att, four, attn_w_out[0].astype(BF), x, vec(g_m),
                            norm_ffn[0].reshape(1, D), vec(sh_f), vec(sc_f), router_w[0].T)
    moe0 = _moe(hf, aff_t, vec(g_f), moe_w_gate, moe_w_up, moe_w_down, 0)

    sh_m, sc_m, g_m, sh_f, sc_f, g_f = mods(1)
    conv_w8 = jnp.zeros((8, D), F32).at[:3].set(conv_w[0])
    x3, hf, aff_t = _mixer1(x1, moe0, norm_mix[1].reshape(1, D), vec(sh_m), vec(sc_m),
                            conv_w_in[0].astype(BF), conv_w8, conv_w_out[0].astype(BF), vec(g_m),
                            norm_ffn[1].reshape(1, D), vec(sh_f), vec(sc_f), router_w[1].T)
    return _moe(hf, aff_t, vec(g_f), moe_w_gate, moe_w_up, moe_w_down, 1, x3)
```

```python
import functools
import math

import jax
import jax.numpy as jnp
from jax import lax
from jax.experimental import pallas as pl
from jax.experimental.pallas import tpu as pltpu

BF = jnp.bfloat16
F32 = jnp.float32
I32 = jnp.int32

D = 1024
GRID_W = 64
N_HEADS = 6
HEAD_DIM = 64
V_DIM = 2 * HEAD_DIM
QK_W = N_HEADS * 2 * HEAD_DIM
V_W = N_HEADS * V_DIM
FOUR_W = 256
FOUR_G = 64
N_EXP = 16
EXPERTS_PER_STEP = 4
CAP_FACTOR = 2
ROPE_BASE = 10000.0
EPS = 1e-6
Q_SCALE = HEAD_DIM ** -0.5 * math.log2(math.e)
ATTN_SHIFT_LIMIT = 40.0
LANES = 128
MIB = 1024 * 1024

NT = (((1,), (1,)), ((), ()))


def _params(sem, vmem_mib):
    return pltpu.CompilerParams(dimension_semantics=sem, vmem_limit_bytes=vmem_mib * MIB)


def _split2(x):
    hi = x.astype(BF)
    lo = (x - hi.astype(F32)).astype(BF)
    return hi, lo


def _dot3(a, b, dims=(((1,), (0,)), ((), ()))):
    ah, al = _split2(a)
    bh, bl = _split2(b)
    dg = functools.partial(lax.dot_general, dimension_numbers=dims, preferred_element_type=F32)
    return dg(ah, bh) + dg(ah, bl) + dg(al, bh)


def _modulate(x, nw, shift, scale):
    ms = jnp.mean(x * x, axis=-1, keepdims=True)
    return (x * lax.rsqrt(ms + EPS) * nw) * (1.0 + scale) + shift


def _ada_kernel(c_ref, w_ref, b_ref, o_ref):
    cv = c_ref[...]
    s = cv * (1.0 / (1.0 + jnp.exp(-cv)))
    o_ref[0] = _dot3(s, w_ref[0]) + b_ref[0]


def _ada(cond8, ada_w, ada_b):
    depth = ada_w.shape[0]
    tn = 1536
    return pl.pallas_call(
        _ada_kernel,
        grid=(depth, 6 * D // tn),
        in_specs=[
            pl.BlockSpec((8, D), lambda l, j: (0, 0)),
            pl.BlockSpec((1, D, tn), lambda l, j: (l, 0, j)),
            pl.BlockSpec((1, 1, tn), lambda l, j: (l, 0, j)),
        ],
        out_specs=pl.BlockSpec((1, 8, tn), lambda l, j: (l, 0, j)),
        out_shape=jax.ShapeDtypeStruct((depth, 8, 6 * D), F32),
        compiler_params=_params(("arbitrary", "arbitrary"), 40),
        name="ada",
    )(cond8, ada_w, ada_b.reshape(depth, 1, 6 * D))


def _norm_rope_t(xt, wn_col, cos_t, sin_t, out_scale):
    q4 = HEAD_DIM // 4
    halves = []
    for s in range(2):
        x = xt[s * HEAD_DIM:(s + 1) * HEAD_DIM]
        ms = jnp.sum(x * x, axis=0, keepdims=True) * (1.0 / HEAD_DIM)
        y = x * lax.rsqrt(ms + EPS) * wn_col[s * HEAD_DIM:(s + 1) * HEAD_DIM]
        rot = jnp.concatenate([y[q4:2 * q4], y[:q4], y[3 * q4:], y[2 * q4:3 * q4]], axis=0)
        halves.append(y * cos_t[s * HEAD_DIM:(s + 1) * HEAD_DIM] + rot * sin_t[s * HEAD_DIM:(s + 1) * HEAD_DIM])
    out = jnp.concatenate(halves, axis=0)
    return out if out_scale == 1.0 else out * out_scale


def _proj0_kernel(x_ref, ctx_ref, nw_ref, sh_ref, sc_ref, wf_ref, wt_ref, cost_ref, sint_ref,
                  qn_ref, kn_ref, q_ref, k_ref, v_ref, f_ref, *, lat_tiles):
    x = jnp.where(pl.program_id(1) < lat_tiles, x_ref[0], ctx_ref[0])
    h = _modulate(x, nw_ref[...], sh_ref[0, 0], sc_ref[0, 0]).astype(BF)
    def project_t(r0, width):
        return lax.dot_general(wt_ref[r0:r0 + width, :], h, NT, preferred_element_type=F32)

    cos_t = cost_ref[...]
    sin_t = sint_ref[...]
    for out_ref, wn_ref, r0, scale in ((q_ref, qn_ref, 0, Q_SCALE), (k_ref, kn_ref, QK_W, 1.0)):
        p_t = project_t(r0, QK_W)
        wn_col = wn_ref[...]
        for hh in range(N_HEADS):
            c0 = hh * V_DIM
            out_ref[0, c0:c0 + V_DIM, :] = _norm_rope_t(
                p_t[c0:c0 + V_DIM], wn_col, cos_t, sin_t, scale).astype(BF)
    v_ref[0] = project_t(2 * QK_W, V_W).astype(BF)
    f_ref[0] = jnp.dot(h, wf_ref[...], preferred_element_type=F32).astype(BF)


def _proj0(x, ctx, nw, sh2, sc2, w_four, w_qkv_t, cos_t, sin_t, qn_col, kn_col):
    b, n_lat, _ = x.shape
    n_ctx = ctx.shape[1]
    s_tot = n_lat + n_ctx
    tm = 256
    lat_tiles = n_lat // tm
    mod_spec = pl.BlockSpec((1, 1, 1, D), lambda bb, i: (bb, i // lat_tiles, 0, 0))
    full = lambda shape: pl.BlockSpec(shape, lambda bb, i: tuple(0 for _ in shape))
    feat = lambda w: pl.BlockSpec((1, w, tm), lambda bb, i: (bb, 0, i))
    feat_shape = lambda w: jax.ShapeDtypeStruct((b, w, s_tot), BF)
    return pl.pallas_call(
        functools.partial(_proj0_kernel, lat_tiles=lat_tiles),
        grid=(b, s_tot // tm),
        in_specs=[
            pl.BlockSpec((1, tm, D), lambda bb, i: (bb, jnp.minimum(i, lat_tiles - 1), 0)),
            pl.BlockSpec((1, tm, D), lambda bb, i: (bb, jnp.maximum(i - lat_tiles, 0), 0)),
            full((1, D)), mod_spec, mod_spec, full((D, FOUR_W)), full((2 * QK_W + V_W, D)),
            pl.BlockSpec((LANES, tm), lambda bb, i: (0, i)),
            pl.BlockSpec((LANES, tm), lambda bb, i: (0, i)),
            full((LANES, 1)), full((LANES, 1)),
        ],
        out_specs=[feat(QK_W), feat(QK_W), feat(V_W),
                   pl.BlockSpec((1, tm, FOUR_W), lambda bb, i: (bb, i, 0))],
        out_shape=[feat_shape(QK_W), feat_shape(QK_W), feat_shape(V_W),
                   jax.ShapeDtypeStruct((b, s_tot, FOUR_W), BF)],
        compiler_params=_params(("parallel", "arbitrary"), 48),
        name="proj0",
    )(x, ctx, nw, sh2, sc2, w_four, w_qkv_t, cos_t, sin_t, qn_col, kn_col)


def _attn_kernel(bound_ref, q_ref, kt_ref, vt_ref, lam_ref, sub_ref, o_ref, k_ref, m_ref, e_ref,
                 *, lam_init, kc):
    @pl.when(pl.program_id(2) == 0)
    def _():
        k_ref[...] = kt_ref[0].astype(F32).T.astype(BF)

    q = q_ref[0]
    feat = lax.broadcasted_iota(I32, q.shape, 0)
    zero = jnp.zeros_like(q)
    lv = lam_ref[...]
    t1 = jnp.sum(lv[0:1] * lv[1:2], axis=-1, keepdims=True)
    t2 = jnp.sum(lv[2:3] * lv[3:4], axis=-1, keepdims=True)
    lam = jnp.exp(t1) - jnp.exp(t2) + lam_init

    qs = (jnp.where(feat < HEAD_DIM, q, zero), jnp.where(feat < HEAD_DIM, zero, q))
    tq = q.shape[1]
    n_chunks = k_ref.shape[0] // kc

    def scores(h, c):
        return jnp.dot(k_ref[c * kc:(c + 1) * kc, :], qs[h], preferred_element_type=F32)

    def fold(x, op):
        return op(x.reshape(kc // 8, 8, tq), axis=0)

    bound = bound_ref[0]
    small = bound <= ATTN_SHIFT_LIMIT

    @pl.when(small)
    def _():
        m_ref[...] = jnp.zeros(m_ref.shape, F32) + bound

    @pl.when(jnp.logical_not(small))
    def _():
        for h in range(2):
            m = jnp.full((8, tq), -jnp.inf, F32)
            for c in range(n_chunks):
                m = jnp.maximum(m, fold(scores(h, c), jnp.max))
            m_ref[h] = jnp.broadcast_to(jnp.max(m, axis=0, keepdims=True), (8, tq))

    ms = [m_ref[h][0:1, :] for h in range(2)]
    ls = [jnp.zeros((8, tq), F32) for _ in range(2)]
    for c in range(n_chunks):
        for h in range(2):
            e = jnp.exp2(scores(h, c) - ms[h])
            ls[h] = ls[h] + fold(e, jnp.sum)
            e_ref[h, c] = e.astype(BF)
    l0, l1 = [jnp.sum(l, axis=0, keepdims=True) for l in ls]

    beta = (lam * l0 / l1).astype(BF)
    acc = jnp.zeros((V_DIM, tq), F32)
    for c in range(n_chunks):
        a = e_ref[0, c] - beta * e_ref[1, c]
        acc = acc + jnp.dot(vt_ref[0, :, c * kc:(c + 1) * kc], a, preferred_element_type=F32)
    o = acc * (1.0 / l0)
    ms = jnp.mean(o * o, axis=0, keepdims=True)
    o = o * lax.rsqrt(ms + EPS) * sub_ref[...] * (1.0 - lam_init)
    o_ref[0] = o.T.astype(BF)


def _attention(bound, q, k, v, lamv, subln, n_lat, lam_init):
    b, _, s_tot = k.shape
    tq = 1024
    kc = 256
    assert s_tot % kc == 0
    kv_spec = pl.BlockSpec((1, V_DIM, s_tot), lambda bb, hh, i: (bb, hh, 0))
    return pl.pallas_call(
        functools.partial(_attn_kernel, lam_init=lam_init, kc=kc),
        grid=(b, N_HEADS, n_lat // tq),
        in_specs=[
            pl.BlockSpec(memory_space=pltpu.SMEM),
            pl.BlockSpec((1, V_DIM, tq), lambda bb, hh, i: (bb, hh, i)),
            kv_spec, kv_spec,
            pl.BlockSpec((8, LANES), lambda bb, hh, i: (0, 0)),
            pl.BlockSpec((V_DIM, 1), lambda bb, hh, i: (0, 0)),
        ],
        out_specs=pl.BlockSpec((1, tq, V_DIM), lambda bb, hh, i: (bb, i, hh)),
        out_shape=jax.ShapeDtypeStruct((b, n_lat, V_W), BF),
        scratch_shapes=[pltpu.VMEM((s_tot, V_DIM), BF),
                        pltpu.VMEM((2, 8, tq), F32),
                        pltpu.VMEM((2, s_tot // kc, kc, tq), BF)],
        compiler_params=_params(("parallel", "parallel", "arbitrary"), 48),
        name="diff_attn",
    )(bound, q, k, v, lamv, subln)


def _fourier_kernel(f_ref, fr_ref, fm_ref, bd_ref, cb_ref, sb_ref, ca_ref, sa_ref, o_ref, g_ref, gm_ref,
                    *, scale):
    nb, half, _ = f_ref.shape

    @pl.when(pl.program_id(0) == 0)
    def _():
        rows = min(1024, half)
        for bb in range(nb):
            cols = slice(bb * FOUR_W, (bb + 1) * FOUR_W)
            for r in range(half // rows):
                sl = slice(r * rows, (r + 1) * rows)
                fwd = f_ref[bb, sl, :].astype(F32)
                rev = fr_ref[bb, sl, :].astype(F32)
                even = fwd + rev
                if r == 0:
                    rowid = lax.broadcasted_iota(I32, even.shape, 0)
                    even = jnp.where(rowid == 0, fwd, even)
                g_ref[sl, cols] = jnp.dot(even.astype(BF), bd_ref[:, :FOUR_W],
                                          preferred_element_type=F32).astype(BF)
                g_ref[half + r * rows:half + (r + 1) * rows, cols] = jnp.dot(
                    (fwd - rev).astype(BF), bd_ref[:, FOUR_W:], preferred_element_type=F32).astype(BF)
            gm_ref[:, cols] = jnp.dot(fm_ref[bb], bd_ref[:, :FOUR_W], preferred_element_type=F32)

    ca = ca_ref[0]
    sa = sa_ref[0]
    cb = cb_ref[...]
    sb = sb_ref[...]
    w_cos = (cb * ca - sb * sa).astype(BF)
    w_sin = (sb * ca + cb * sa).astype(BF)
    y = (jnp.dot(w_cos, g_ref[:half, :], preferred_element_type=F32)
         - jnp.dot(w_sin, g_ref[half:, :], preferred_element_type=F32))
    parity = lax.broadcasted_iota(I32, (y.shape[0], 1), 0) & 1
    y = y + jnp.where(parity == 0, scale, -scale) * gm_ref[0:1, :]
    for bb in range(nb):
        o_ref[bb] = y[:, bb * FOUR_W:(bb + 1) * FOUR_W].astype(BF)


def _fourier(f, bd, tables, n_lat):
    b = f.shape[0]
    half = n_lat // 2
    cos_row, sin_row, cos_tile, sin_tile = tables
    tr = cos_row.shape[0]
    assert tr % 2 == 0 and cos_row.shape[1] == half
    f_rev = jnp.roll(f[:, :n_lat][:, ::-1], 1, axis=1)
    row_tab = pl.BlockSpec((tr, half), lambda i: (0, 0))
    tile_tab = pl.BlockSpec((1, 1, half), lambda i: (i, 0, 0))
    first_half = pl.BlockSpec((b, half, FOUR_W), lambda i: (0, 0, 0))
    return pl.pallas_call(
        functools.partial(_fourier_kernel, scale=n_lat ** -0.5),
        grid=(n_lat // tr,),
        in_specs=[
            first_half, first_half,
            pl.BlockSpec((b, 16, FOUR_W), lambda i: (0, half // 16, 0)),
            pl.BlockSpec((FOUR_W, 2 * FOUR_W), lambda i: (0, 0)),
            row_tab, row_tab, tile_tab, tile_tab,
        ],
        out_specs=pl.BlockSpec((b, tr, FOUR_W), lambda i: (0, i, 0)),
        out_shape=jax.ShapeDtypeStruct((b, n_lat, FOUR_W), BF),
        scratch_shapes=[pltpu.VMEM((n_lat, b * FOUR_W), BF), pltpu.VMEM((16, b * FOUR_W), F32)],
        compiler_params=_params(("arbitrary",), 48),
        name="fourier",
    )(f, f_rev, f, bd, cos_row, sin_row, cos_tile, sin_tile)


SUB = D // LANES


def _store_token_tiles(ref, val):
    rows = val.shape[0]
    for j in range(SUB):
        ref[0, pl.ds(j, rows, stride=SUB), :] = val[:, j * LANES:(j + 1) * LANES]


def _load_token_tiles(ref, rows):
    return jnp.concatenate([ref[0, pl.ds(j, rows, stride=SUB), :] for j in range(SUB)], axis=1)


def _residual_router(x, y, gm, nf, shf, scf, wr_t, x1_ref, hf_ref, aff_ref):
    x1 = x + gm * y
    x1_ref[0] = x1
    hf = _modulate(x1, nf, shf, scf)
    _store_token_tiles(hf_ref, hf)
    logits = _dot3(wr_t, hf, NT)
    m = jnp.max(logits, axis=0, keepdims=True)
    e = jnp.exp(logits - m)
    aff_ref[0] = e / jnp.sum(e, axis=0, keepdims=True)


def _merge0_kernel(att_ref, four_ref, w_ref, x_ref, gm_ref, nf_ref, shf_ref, scf_ref, wr_ref,
                   x1_ref, hf_ref, aff_ref):
    y = jnp.dot(att_ref[0], w_ref[:V_W, :], preferred_element_type=F32)
    y = y + jnp.dot(four_ref[0], w_ref[V_W:, :], preferred_element_type=F32)
    _residual_router(x_ref[0], y, gm_ref[0], nf_ref[...], shf_ref[0], scf_ref[0], wr_ref[...],
                     x1_ref, hf_ref, aff_ref)


def _router_specs(tm):
    vec = pl.BlockSpec((1, 1, D), lambda bb, i: (bb, 0, 0))
    row = pl.BlockSpec((1, tm, D), lambda bb, i: (bb, i, 0))
    in_specs = [vec, pl.BlockSpec((1, D), lambda bb, i: (0, 0)), vec, vec,
                pl.BlockSpec((N_EXP, D), lambda bb, i: (0, 0))]
    tiles = _tile_spec(tm)
    out_specs = [row, tiles, pl.BlockSpec((1, N_EXP, tm), lambda bb, i: (bb, 0, i))]
    return row, in_specs, out_specs


def _tile_spec(tm):
    return pl.BlockSpec((1, tm * SUB, LANES), lambda bb, i: (bb, i, 0))


def _router_out_shape(b, n):
    return [jax.ShapeDtypeStruct((b, n, D), F32), jax.ShapeDtypeStruct((b, n * SUB, LANES), F32),
            jax.ShapeDtypeStruct((b, N_EXP, n), F32)]


def _merge0(att, four, w_bf, x, gm, nf, shf, scf, wr_t):
    b, n, _ = x.shape
    tm = 512
    row, r_in, r_out = _router_specs(tm)
    return pl.pallas_call(
        _merge0_kernel,
        grid=(b, n // tm),
        in_specs=[
            pl.BlockSpec((1, tm, V_W), lambda bb, i: (bb, i, 0)),
            pl.BlockSpec((1, tm, FOUR_W), lambda bb, i: (bb, i, 0)),
            pl.BlockSpec((D, D), lambda bb, i: (0, 0)),
            row,
        ] + r_in,
        out_specs=r_out,
        out_shape=_router_out_shape(b, n),
        compiler_params=_params(("parallel", "arbitrary"), 48),
        name="merge0",
    )(att, four, w_bf, x, gm, nf, shf, scf, wr_t)


HALO = 8


def _mixer1_kernel(x_ref, xp_ref, xn_ref, moe_ref, moep_ref, moen_ref, nw_ref, sh_ref, sc_ref, win_ref,
                   cw_ref, wout_ref, gm_ref, nf_ref, shf_ref, scf_ref, wr_ref, x1_ref, hf_ref, aff_ref):
    i = pl.program_id(1)
    last = pl.num_programs(1) - 1
    tm = x_ref.shape[1]
    x = x_ref[0] + _load_token_tiles(moe_ref, tm)
    x_prev = xp_ref[0] + _load_token_tiles(moep_ref, HALO)
    x_next = xn_ref[0] + _load_token_tiles(moen_ref, HALO)
    x_all = jnp.concatenate([x_prev, x, x_next], axis=0)
    h_all = _modulate(x_all, nw_ref[...], sh_ref[0], sc_ref[0]).astype(BF)
    cg = jnp.dot(h_all, win_ref[:, D:2 * D], preferred_element_type=F32)
    u = jnp.dot(h_all, win_ref[:, 2 * D:], preferred_element_type=F32)
    z_raw = cg * u
    core = slice(HALO, HALO + tm)
    z_all = jnp.concatenate([z_raw[:HALO] * jnp.where(i > 0, 1.0, 0.0), z_raw[core],
                             z_raw[HALO + tm:] * jnp.where(i < last, 1.0, 0.0)], axis=0)
    z_up = pltpu.roll(z_all, 1, axis=0)[core]
    z_dn = pltpu.roll(z_all, tm + 2 * HALO - 1, axis=0)[core]
    cw = cw_ref[...]
    conv = cw[0:1] * z_up + cw[1:2] * z_all[core] + cw[2:3] * z_dn
    bg = jnp.dot(h_all[core], win_ref[:, :D], preferred_element_type=F32)
    y = jnp.dot((bg * conv).astype(BF), wout_ref[...], preferred_element_type=F32)
    _residual_router(x, y, gm_ref[0], nf_ref[...], shf_ref[0], scf_ref[0], wr_ref[...],
                     x1_ref, hf_ref, aff_ref)


def _mixer1(x, moe_tiles, nw, sh, sc, w_in, conv_w8, w_out, gm, nf, shf, scf, wr_t):
    b, n, _ = x.shape
    tm = 512
    per = tm // HALO
    n_halo = n // HALO
    prev_blk = lambda bb, i: (bb, jnp.maximum(i * per - 1, 0), 0)
    next_blk = lambda bb, i: (bb, jnp.minimum((i + 1) * per, n_halo - 1), 0)
    row, r_in, r_out = _router_specs(tm)
    vec = pl.BlockSpec((1, 1, D), lambda bb, i: (bb, 0, 0))
    full = lambda shape: pl.BlockSpec(shape, lambda bb, i: (0, 0))
    return pl.pallas_call(
        _mixer1_kernel,
        grid=(b, n // tm),
        in_specs=[
            row, pl.BlockSpec((1, HALO, D), prev_blk), pl.BlockSpec((1, HALO, D), next_blk),
            _tile_spec(tm), pl.BlockSpec((1, HALO * SUB, LANES), prev_blk),
            pl.BlockSpec((1, HALO * SUB, LANES), next_blk),
            full((1, D)), vec, vec, full((D, 3 * D)), full((8, D)), full((D, D)),
        ] + r_in,
        out_specs=r_out,
        out_shape=_router_out_shape(b, n),
        compiler_params=_params(("parallel", "arbitrary"), 56),
        name="mixer1",
    )(x, x, x, moe_tiles, moe_tiles, moe_tiles, nw, sh, sc, w_in, conv_w8, w_out, gm, nf, shf, scf, wr_t)


def _select_kernel(a_ref, idx_ref, gate_ref, cl_s, off_s, inc_s, hi_s, mid_s, lo_s, *, cap, chunks):
    a = a_ref[...]
    rows = a.shape[0]
    groups = rows // chunks
    per_batch = N_EXP * chunks
    assert chunks & (chunks - 1) == 0
    shift = chunks.bit_length() - 1

    member = jnp.where((lax.broadcasted_iota(I32, (groups, rows), 1) >> shift)
                       == lax.broadcasted_iota(I32, (groups, rows), 0), 1.0, 0.0).astype(BF)
    ri =lax.broadcasted_iota(I32, (per_batch, per_batch), 0)
    ci = lax.broadcasted_iota(I32, (per_batch, per_batch), 1)
    same_f = jnp.where((ri >> shift) == (ci >> shift), 1.0, 0.0)
    same = same_f.astype(BF)
    lower = (same_f * jnp.where(ci < ri, 1.0, 0.0)).astype(BF)
    li = lax.broadcasted_iota(I32, (LANES, LANES), 0)
    lj = lax.broadcasted_iota(I32, (LANES, LANES), 1)
    incl = jnp.where(li <= lj, 1.0, 0.0).astype(BF)

    def bcast(col):
        return jnp.broadcast_to(col, (per_batch, LANES))

    def prefix(maskf):
        cl = jnp.dot(maskf.astype(BF), incl, preferred_element_type=F32)
        tot = bcast(cl[:, LANES - 1:LANES])
        off = jnp.dot(lower, tot.astype(BF), preferred_element_type=F32)
        return cl, off, tot

    min_normal = 0x00800000

    a3 = a.reshape(groups, chunks, LANES)

    def at_least(bits):
        thr_g = lax.bitcast_convert_type(bits, F32)[:, None, :]
        return jnp.where(a3 >= thr_g, 1.0, 0.0).reshape(rows, LANES)

    def search(step, t):
        cand = t | jnp.left_shift(jnp.int32(1), 30 - step)
        part = jnp.dot(member, at_least(cand).astype(BF), preferred_element_type=F32)
        cnt = jnp.sum(part, axis=-1, keepdims=True)
        return jnp.where(cnt >= cap, jnp.where(cand >= min_normal, cand, t), t)

    thr = lax.fori_loop(0, 31, search, jnp.zeros((groups, LANES), I32))
    gtf = at_least(jnp.maximum(thr + 1, min_normal))
    eqf = at_least(thr) - gtf
    for bb in range(rows // per_batch):
        sl = slice(bb * per_batch, (bb + 1) * per_batch)
        gt_b, eq_b = gtf[sl], eqf[sl]
        n_gt = jnp.dot(same, bcast(jnp.sum(gt_b, axis=-1, keepdims=True)).astype(BF),
                       preferred_element_type=F32)
        cl_eq, off_eq, _ = prefix(eq_b)
        sel = gt_b + eq_b * jnp.where(cl_eq + off_eq <= cap - n_gt, 1.0, 0.0)
        cl, off, tot = prefix(sel)
        cl_s[sl, :] = cl.astype(BF)
        off_s[sl, :] = off
        inc_s[sl, :] = off + tot
        a_b = a[sl]
        a_hi = a_b.astype(BF)
        r1 = a_b - a_hi.astype(F32)
        a_mid = r1.astype(BF)
        hi_s[sl, :] = a_hi
        mid_s[sl, :] = a_mid
        lo_s[sl, :] = (r1 - a_mid.astype(F32)).astype(BF)

    slot = lax.broadcasted_iota(I32, (chunks, cap), 1).astype(F32)
    chunk_id = lax.broadcasted_iota(I32, (chunks, cap), 0).astype(F32)
    lane_id = lax.broadcasted_iota(I32, (LANES, cap), 0).astype(F32)
    tn = (((0,), (0,)), ((), ()))
    reps = cap // LANES

    def widen(x):
        return jnp.concatenate([x] * reps, axis=1)

    def per_group(g, carry):
        win = pl.ds(pl.multiple_of(g * chunks, chunks), chunks)
        inc_g = widen(inc_s[win, :])
        off_g = widen(off_s[win, :])
        chunk_of = jnp.sum(jnp.where(inc_g <= slot, 1.0, 0.0), axis=0, keepdims=True)
        pick_f = jnp.where(chunk_id == chunk_of, 1.0, 0.0)
        before = jnp.sum(pick_f * off_g, axis=0, keepdims=True)
        pick = pick_f.astype(BF)
        counts = lax.dot_general(cl_s[win, :], pick, tn, preferred_element_type=F32)
        lane_of = jnp.sum(jnp.where(counts <= slot[0:1] - before, 1.0, 0.0), axis=0, keepdims=True)
        aff = (lax.dot_general(hi_s[win, :], pick, tn, preferred_element_type=F32)
               + lax.dot_general(mid_s[win, :], pick, tn, preferred_element_type=F32)
               + lax.dot_general(lo_s[win, :], pick, tn, preferred_element_type=F32))
        gate = jnp.sum(jnp.where(lane_id == lane_of, aff, 0.0), axis=0, keepdims=True)
        idx_ref[pl.ds(g, 1), :] = ((chunk_of * float(LANES) + lane_of) * float(SUB)).astype(I32)
        gate_ref[pl.ds(g, 1), :] = gate
        return carry

    lax.fori_loop(0, groups, per_group, 0, unroll=2)


def _select(aff_rows, cap, chunks):
    rows = aff_rows.shape[0]
    groups = rows // chunks
    whole = lambda shape: pl.BlockSpec(shape, lambda i: (0, 0))
    return pl.pallas_call(
        functools.partial(_select_kernel, cap=cap, chunks=chunks),
        grid=(1,),
        in_specs=[whole((rows, LANES))],
        out_specs=[whole((groups, cap)), whole((groups, cap))],
        out_shape=[jax.ShapeDtypeStruct((groups, cap), I32), jax.ShapeDtypeStruct((groups, cap), F32)],
        scratch_shapes=[pltpu.VMEM((rows, LANES), BF), pltpu.VMEM((rows, LANES), F32),
                        pltpu.VMEM((rows, LANES), F32), pltpu.VMEM((rows, LANES), BF),
                        pltpu.VMEM((rows, LANES), BF), pltpu.VMEM((rows, LANES), BF)],
        compiler_params=_params(("arbitrary",), 48),
        name="select",
    )(aff_rows)


def _gather_kernel(idx_ref, h_ref, o_ref):
    slots = idx_ref.shape[2]

    def body(c, carry):
        t = pl.multiple_of(idx_ref[0, 0, c], SUB)
        o_ref[0, 0, pl.ds(pl.multiple_of(c * SUB, SUB), SUB), :] = h_ref[0, pl.ds(t, SUB), :]
        return carry

    lax.fori_loop(0, slots, body, 0, unroll=8)


def _gather(idx, hf_tiles, cap):
    b, rows, _ = hf_tiles.shape
    steps = N_EXP // EXPERTS_PER_STEP
    slots = EXPERTS_PER_STEP * cap
    xs = pl.pallas_call(
        _gather_kernel,
        grid=(b, steps),
        in_specs=[
            pl.BlockSpec((1, 1, slots), lambda bb, s: (bb * steps + s, 0, 0), memory_space=pltpu.SMEM),
            pl.BlockSpec((1, rows, LANES), lambda bb, s: (bb, 0, 0)),
        ],
        out_specs=pl.BlockSpec((1, 1, slots * SUB, LANES), lambda bb, s: (bb, s, 0, 0)),
        out_shape=jax.ShapeDtypeStruct((b, steps, slots * SUB, LANES), F32),
        compiler_params=_params(("parallel", "arbitrary"), 56),
        name="gather",
    )(idx, hf_tiles)
    return xs.reshape(b, N_EXP, cap * SUB, LANES)


def _ffn_kernel(x_ref, gate_ref, gf_ref, wg_ref, wu_ref, wd_ref, o_ref, wg_s, wu_s, wd_s):
    @pl.when(pl.program_id(1) == 0)
    def _():
        wg_s[...] = wg_ref[...].astype(BF)
        wu_s[...] = wu_ref[...].astype(BF)
        wd_s[...] = wd_ref[...].astype(BF)

    cap = x_ref.shape[2] // SUB
    x = _load_token_tiles(x_ref.at[0], cap).astype(BF)
    dexp = wg_s.shape[1]
    step = 256
    y = None
    for c0 in range(0, dexp, step):
        g = jnp.dot(x, wg_s[:, c0:c0 + step], preferred_element_type=F32)
        u = jnp.dot(x, wu_s[:, c0:c0 + step], preferred_element_type=F32)
        act = (g * (1.0 / (1.0 + jnp.exp(-g))) * u).astype(BF)
        part = jnp.dot(act, wd_s[c0:c0 + step, :], preferred_element_type=F32)
        y = part if y is None else y + part
    gate_col =jnp.broadcast_to(gate_ref[0], (LANES, cap)).T
    y = y * jnp.concatenate([gate_col] * SUB, axis=1) * gf_ref[0]
    _store_token_tiles(o_ref.at[0], y)


def _ffn(xs, gate, gf, w_gate, w_up, w_down, layer):
    b, _, tile_rows, _ = xs.shape
    cap = tile_rows // SUB
    dexp = w_gate.shape[-1]
    gspec = pl.BlockSpec((1, 1, cap), lambda e, bb: (bb * N_EXP + e, 0, 0))
    gfspec = pl.BlockSpec((1, 1, D), lambda e, bb: (bb, 0, 0))
    yspec = pl.BlockSpec((1, 1, tile_rows, LANES), lambda e, bb: (bb, e, 0, 0))
    xspec = yspec

    def wspec(shape, switch_at):
        def index(e, bb):
            return (layer, jnp.minimum(e + jnp.where(bb >= switch_at, 1, 0), N_EXP - 1), 0, 0)
        return pl.BlockSpec((None, None) + shape, index)

    assert b >= 2
    stagger = [min(j, b - 1) for j in (1, 2, 3)]
    return pl.pallas_call(
        _ffn_kernel,
        grid=(N_EXP, b),
        in_specs=[xspec, gspec, gfspec, wspec((D, dexp), stagger[0]), wspec((D, dexp), stagger[1]),
                  wspec((dexp, D), stagger[2])],
        out_specs=yspec,
        out_shape=jax.ShapeDtypeStruct((b, N_EXP, cap * SUB, LANES), F32),
        scratch_shapes=[pltpu.VMEM((D, dexp), BF), pltpu.VMEM((D, dexp), BF), pltpu.VMEM((dexp, D), BF)],
        compiler_params=_params(("arbitrary", "arbitrary"), 56),
        name="expert_ffn",
    )(xs, gate, gf, w_gate, w_up, w_down)


def _scatter_kernel(idx_ref, y_ref, *rest):
    if len(rest) == 1:
        (o_ref,), x_ref = rest, None
        acc_ref = o_ref
    else:
        x_ref, o_ref, acc_ref = rest
    slots = idx_ref.shape[2]
    expert_steps = N_EXP // EXPERTS_PER_STEP
    step = pl.program_id(1)

    @pl.when(step == 0)
    def _():
        acc_ref[...] = jnp.zeros(acc_ref.shape, F32)

    group = 16

    def body(g, carry):
        c0 = g * group
        ts = [pl.multiple_of(idx_ref[0, 0, c0 + j], SUB) for j in range(group)]
        new = []
        for j in range(group):
            y = y_ref[0, 0, pl.ds(pl.multiple_of((c0 + j) * SUB, SUB), SUB), :]
            new.append(acc_ref[0, pl.ds(ts[j], SUB), :] + y)
        for j in range(group):
            acc_ref[0, pl.ds(ts[j], SUB), :] = new[j]
        return carry

    @pl.when(step < expert_steps)
    def _():
        lax.fori_loop(0, slots // group, body, 0)

    if x_ref is not None:
        @pl.when(step >= expert_steps)
        def _():
            rows = o_ref.shape[1]
            base = (step - expert_steps) * (rows * SUB)
            o_ref[0] = x_ref[0] + jnp.concatenate(
                [acc_ref[0, pl.ds(base + j, rows, stride=SUB), :] for j in range(SUB)], axis=1)


def _scatter(idx, y_tiles, n, x=None):
    b = y_tiles.shape[0]
    slots = idx.shape[2]
    assert (slots // EXPERTS_PER_STEP) % 16 == 0
    rows = n * SUB
    tm = 512
    steps = N_EXP // EXPERTS_PER_STEP
    extra = n // tm if x is not None else 0
    expert_step = lambda s: jnp.minimum(s, steps - 1)
    in_specs = [
        pl.BlockSpec((1, 1, slots), lambda bb, s: (bb * steps + expert_step(s), 0, 0),
                     memory_space=pltpu.SMEM),
        pl.BlockSpec((1, 1, slots * SUB, LANES), lambda bb, s: (bb, expert_step(s), 0, 0)),
    ]
    args = [idx, y_tiles.reshape(b, steps, slots * SUB, LANES)]
    if x is not None:
        piece = pl.BlockSpec((1, tm, D), lambda bb, s: (bb, jnp.maximum(s - steps, 0), 0))
        in_specs.append(piece)
        args.append(x)
        out_spec, out_shape = piece, jax.ShapeDtypeStruct((b, n, D), F32)
        scratch = [pltpu.VMEM((1, rows, LANES), F32)]
    else:
        out_spec = pl.BlockSpec((1, rows, LANES), lambda bb, s: (bb, 0, 0))
        out_shape, scratch = jax.ShapeDtypeStruct((b, rows, LANES), F32), []
    return pl.pallas_call(
        _scatter_kernel,
        grid=(b, steps + extra),
        in_specs=in_specs,
        out_specs=out_spec,
        out_shape=out_shape,
        scratch_shapes=scratch,
        compiler_params=_params(("parallel", "arbitrary"), 56),
        name="scatter_add",
    )(*args)


def _moe(hf_tiles, aff_t, gf, w_gate, w_up, w_down, layer, x=None):
    b, _, n = aff_t.shape
    cap = CAP_FACTOR * n // N_EXP
    chunks = n // LANES
    idx, gate = _select(aff_t.reshape(b * N_EXP * chunks, LANES), cap, chunks)
    idx = idx.reshape(b * N_EXP // EXPERTS_PER_STEP, 1, EXPERTS_PER_STEP * cap)
    gate = gate.reshape(b * N_EXP, 1, cap)
    xs = _gather(idx, hf_tiles, cap)
    y_tiles = _ffn(xs, gate, gf, w_gate, w_up, w_down, layer)
    return _scatter(idx, y_tiles, n, x)


def _rope_tables(n_lat, n_ctx):
    rows = n_lat // GRID_W
    r = jnp.repeat(jnp.arange(rows, dtype=F32), GRID_W)
    col = jnp.tile(jnp.arange(GRID_W, dtype=F32), rows)
    n_freq = HEAD_DIM // 4
    inv = ROPE_BASE ** (-jnp.arange(n_freq, dtype=F32) / n_freq)
    ar = r[:, None] * inv
    ac = col[:, None] * inv
    ang = jnp.concatenate([ar, ar, ac, ac], axis=-1)
    sign = jnp.where((jnp.arange(HEAD_DIM) % 32) < 16, -1.0, 1.0).astype(F32)
    cos = jnp.concatenate([jnp.cos(ang), jnp.ones((n_ctx, HEAD_DIM), F32)], axis=0)
    sin_s = jnp.concatenate([jnp.sin(ang) * sign, jnp.zeros((n_ctx, HEAD_DIM), F32)], axis=0)
    return jnp.tile(cos, (1, 2)).T, jnp.tile(sin_s, (1, 2)).T


def _dft_tables(n_lat):
    c = jnp.arange(FOUR_G, dtype=I32)
    ang_c = (2.0 * math.pi / FOUR_G) * ((c[:, None] * c[None, :]) % FOUR_G).astype(F32)
    eye = jnp.eye(FOUR_W // FOUR_G, dtype=F32)
    bd = jnp.concatenate([jnp.kron(eye, jnp.cos(ang_c)), jnp.kron(eye, jnp.sin(ang_c))], axis=1)
    bd = (bd * FOUR_G ** -0.5).astype(BF)
    tr = 256
    n = jnp.arange(n_lat, dtype=I32)[None, :]
    r = jnp.arange(tr, dtype=I32)[:, None]
    i = jnp.arange(n_lat // tr, dtype=I32)[:, None]
    tile_ang = (2.0 * math.pi / n_lat) * ((i * tr * n) % n_lat).astype(F32)
    side = int(round(math.sqrt(n_lat)))
    assert side * side == n_lat
    s = jnp.arange(side, dtype=I32)[None, :]
    hi_ang = (2.0 * math.pi / side) * ((r * s) % side).astype(F32)
    lo_ang = (2.0 * math.pi / n_lat) * ((r * s) % n_lat).astype(F32)
    ch, sh = jnp.cos(hi_ang)[:, :, None], jnp.sin(hi_ang)[:, :, None]
    cl, sl = jnp.cos(lo_ang)[:, None, :], jnp.sin(lo_ang)[:, None, :]
    scale = n_lat ** -0.5
    half = n_lat // 2
    ch, sh = ch[:, :side // 2], sh[:, :side // 2]
    cos_row = ((ch * cl - sh * sl) * scale).reshape(tr, half)
    sin_row = ((sh * cl + ch * sl) * scale).reshape(tr, half)
    tile_ang = tile_ang[:, :half]
    tables = (cos_row, sin_row, jnp.cos(tile_ang)[:, None, :], jnp.sin(tile_ang)[:, None, :])
    return bd, tables


def kernel(x, c, ctx, c_ctx, ada_w, ada_b, norm_mix, norm_ffn, attn_w_in, attn_q_norm, attn_k_norm,
           lam_q1, lam_k1, lam_q2, lam_k2, attn_subln, attn_w_out, conv_w_in, conv_w, conv_w_out,
           router_w, moe_w_gate, moe_w_up, moe_w_down):
    b, n, _ = x.shape
    n_ctx = ctx.shape[1]
    assert x.shape[2] == D and n % 512 == 0 and n_ctx % 256 == 0

    cond8 = jnp.concatenate([c, c_ctx[None, :], jnp.zeros((8 - b - 1, D), F32)], axis=0)
    ada = _ada(cond8, ada_w, ada_b)

    def mods(layer):
        m = ada[layer].reshape(8, 6, D)
        return [m[:, j] for j in range(6)]

    vec = lambda t: t[:b].reshape(b, 1, D)

    sh_m, sc_m, g_m, sh_f, sc_f, g_f = mods(0)
    both = lambda t: jnp.stack([t[:b], jnp.broadcast_to(t[b], (b, D))], axis=1).reshape(b, 2, 1, D)
    cos_t, sin_t = _rope_tables(n, n_ctx)
    bd, dft_tabs = _dft_tables(n)
    col2 = lambda t: jnp.tile(t.reshape(HEAD_DIM, 1), (2, 1))
    w_in = attn_w_in[0]
    q, k, v, f = _proj0(x, ctx, norm_mix[0].reshape(1, D), both(sh_m), both(sc_m),
                        w_in[:, 2 * QK_W + V_W:].astype(BF), w_in[:, :2 * QK_W + V_W].astype(BF).T,
                        cos_t, sin_t, col2(attn_q_norm[0]), col2(attn_k_norm[0]))
    lam_init = 0.8 - 0.6 * math.exp(-0.3 * 0)
    lamv = jnp.zeros((8, LANES), F32).at[:4, :HEAD_DIM].set(
        jnp.stack([lam_q1[0], lam_k1[0], lam_q2[0], lam_k2[0]]))
    score_bound = (1.01 * HEAD_DIM * Q_SCALE * jnp.max(jnp.abs(attn_q_norm[0]))
                   * jnp.max(jnp.abs(attn_k_norm[0])) + 0.1).reshape(1)
    att = _attention(score_bound, q, k, v, lamv, attn_subln[0].reshape(V_DIM, 1), n, lam_init)
    four = _fourier(f, bd, dft_tabs, n)
    x1, hf, aff_t = _merge0(att, four, attn_w_out[0].astype(BF), x, vec(g_m),
                            norm_ffn[0].reshape(1, D), vec(sh_f), vec(sc_f), router_w[0].T)
    moe0 = _moe(hf, aff_t, vec(g_f), moe_w_gate, moe_w_up, moe_w_down, 0)

    sh_m, sc_m, g_m, sh_f, sc_f, g_f = mods(1)
    conv_w8 = jnp.zeros((8, D), F32).at[:3].set(conv_w[0])
    x3, hf, aff_t = _mixer1(x1, moe0, norm_mix[1].reshape(1, D), vec(sh_m), vec(sc_m),
                            conv_w_in[0].astype(BF), conv_w8, conv_w_out[0].astype(BF), vec(g_m),
                            norm_ffn[1].reshape(1, D), vec(sh_f), vec(sc_f), router_w[1].T)
    return _moe(hf, aff_t, vec(g_f), moe_w_gate, moe_w_up, moe_w_down, 1, x3)
```

```python
import functools
import math

import jax
import jax.numpy as jnp
from jax import lax
from jax.experimental import pallas as pl
from jax.experimental.pallas import tpu as pltpu

BF = jnp.bfloat16
F32 = jnp.float32
I32 = jnp.int32

D = 1024
GRID_W = 64
N_HEADS = 6
HEAD_DIM = 64
V_DIM = 2 * HEAD_DIM
QK_W = N_HEADS * 2 * HEAD_DIM
V_W = N_HEADS * V_DIM
FOUR_W = 256
FOUR_G = 64
N_EXP = 16
EXPERTS_PER_STEP = 4
CAP_FACTOR = 2
ROPE_BASE = 10000.0
EPS = 1e-6
Q_SCALE = HEAD_DIM ** -0.5 * math.log2(math.e)
ATTN_SHIFT_LIMIT = 40.0
LANES = 128
MIB = 1024 * 1024

NT = (((1,), (1,)), ((), ()))


def _params(sem, vmem_mib):
    return pltpu.CompilerParams(dimension_semantics=sem, vmem_limit_bytes=vmem_mib * MIB)


def _split2(x):
    hi = x.astype(BF)
    lo = (x - hi.astype(F32)).astype(BF)
    return hi, lo


def _dot3(a, b, dims=(((1,), (0,)), ((), ()))):
    ah, al = _split2(a)
    bh, bl = _split2(b)
    dg = functools.partial(lax.dot_general, dimension_numbers=dims, preferred_element_type=F32)
    return dg(ah, bh) + dg(ah, bl) + dg(al, bh)


def _modulate(x, nw, shift, scale):
    ms = jnp.mean(x * x, axis=-1, keepdims=True)
    return (x * lax.rsqrt(ms + EPS) * nw) * (1.0 + scale) + shift


def _ada_kernel(c_ref, w_ref, b_ref, o_ref):
    cv = c_ref[...]
    s = cv * (1.0 / (1.0 + jnp.exp(-cv)))
    o_ref[0] = _dot3(s, w_ref[0]) + b_ref[0]


def _ada(cond8, ada_w, ada_b):
    depth = ada_w.shape[0]
    tn = 1536
    return pl.pallas_call(
        _ada_kernel,
        grid=(depth, 6 * D // tn),
        in_specs=[
            pl.BlockSpec((8, D), lambda l, j: (0, 0)),
            pl.BlockSpec((1, D, tn), lambda l, j: (l, 0, j)),
            pl.BlockSpec((1, 1, tn), lambda l, j: (l, 0, j)),
        ],
        out_specs=pl.BlockSpec((1, 8, tn), lambda l, j: (l, 0, j)),
        out_shape=jax.ShapeDtypeStruct((depth, 8, 6 * D), F32),
        compiler_params=_params(("arbitrary", "arbitrary"), 40),
        name="ada",
    )(cond8, ada_w, ada_b.reshape(depth, 1, 6 * D))


def _norm_rope_t(xt, wn_col, cos_t, sin_t, out_scale):
    q4 = HEAD_DIM // 4
    halves = []
    for s in range(2):
        x = xt[s * HEAD_DIM:(s + 1) * HEAD_DIM]
        ms = jnp.sum(x * x, axis=0, keepdims=True) * (1.0 / HEAD_DIM)
        y = x * lax.rsqrt(ms + EPS) * wn_col[s * HEAD_DIM:(s + 1) * HEAD_DIM]
        rot = jnp.concatenate([y[q4:2 * q4], y[:q4], y[3 * q4:], y[2 * q4:3 * q4]], axis=0)
        halves.append(y * cos_t[s * HEAD_DIM:(s + 1) * HEAD_DIM] + rot * sin_t[s * HEAD_DIM:(s + 1) * HEAD_DIM])
    out = jnp.concatenate(halves, axis=0)
    return out if out_scale == 1.0 else out * out_scale


def _proj0_kernel(x_ref, ctx_ref, nw_ref, sh_ref, sc_ref, wf_ref, wt_ref, cost_ref, sint_ref,
                  qn_ref, kn_ref, q_ref, k_ref, v_ref, f_ref, *, lat_tiles):
    x = jnp.where(pl.program_id(1) < lat_tiles, x_ref[0], ctx_ref[0])
    h = _modulate(x, nw_ref[...], sh_ref[0, 0], sc_ref[0, 0]).astype(BF)
    def project_t(r0, width):
        return lax.dot_general(wt_ref[r0:r0 + width, :], h, NT, preferred_element_type=F32)

    cos_t = cost_ref[...]
    sin_t = sint_ref[...]
    for out_ref, wn_ref, r0, scale in ((q_ref, qn_ref, 0, Q_SCALE), (k_ref, kn_ref, QK_W, 1.0)):
        p_t = project_t(r0, QK_W)
        wn_col = wn_ref[...]
        for hh in range(N_HEADS):
            c0 = hh * V_DIM
            out_ref[0, c0:c0 + V_DIM, :] = _norm_rope_t(
                p_t[c0:c0 + V_DIM], wn_col, cos_t, sin_t, scale).astype(BF)
    v_ref[0] = project_t(2 * QK_W, V_W).astype(BF)
    f_ref[0] = jnp.dot(h, wf_ref[...], preferred_element_type=F32).astype(BF)


def _proj0(x, ctx, nw, sh2, sc2, w_four, w_qkv_t, cos_t, sin_t, qn_col, kn_col):
    b, n_lat, _ = x.shape
    n_ctx = ctx.shape[1]
    s_tot = n_lat + n_ctx
    tm = 256
    lat_tiles = n_lat // tm
    mod_spec = pl.BlockSpec((1, 1, 1, D), lambda bb, i: (bb, i // lat_tiles, 0, 0))
    full = lambda shape: pl.BlockSpec(shape, lambda bb, i: tuple(0 for _ in shape))
    feat = lambda w: pl.BlockSpec((1, w, tm), lambda bb, i: (bb, 0, i))
    feat_shape = lambda w: jax.ShapeDtypeStruct((b, w, s_tot), BF)
    return pl.pallas_call(
        functools.partial(_proj0_kernel, lat_tiles=lat_tiles),
        grid=(b, s_tot // tm),
        in_specs=[
            pl.BlockSpec((1, tm, D), lambda bb, i: (bb, jnp.minimum(i, lat_tiles - 1), 0)),
            pl.BlockSpec((1, tm, D), lambda bb, i: (bb, jnp.maximum(i - lat_tiles, 0), 0)),
            full((1, D)), mod_spec, mod_spec, full((D, FOUR_W)), full((2 * QK_W + V_W, D)),
            pl.BlockSpec((LANES, tm), lambda bb, i: (0, i)),
            pl.BlockSpec((LANES, tm), lambda bb, i: (0, i)),
            full((LANES, 1)), full((LANES, 1)),
        ],
        out_specs=[feat(QK_W), feat(QK_W), feat(V_W),
                   pl.BlockSpec((1, tm, FOUR_W), lambda bb, i: (bb, i, 0))],
        out_shape=[feat_shape(QK_W), feat_shape(QK_W), feat_shape(V_W),
                   jax.ShapeDtypeStruct((b, s_tot, FOUR_W), BF)],
        compiler_params=_params(("parallel", "arbitrary"), 48),
        name="proj0",
    )(x, ctx, nw, sh2, sc2, w_four, w_qkv_t, cos_t, sin_t, qn_col, kn_col)


def _attn_kernel(bound_ref, q_ref, kt_ref, vt_ref, lam_ref, sub_ref, o_ref, k_ref, m_ref, e_ref,
                 *, lam_init, kc):
    @pl.when(pl.program_id(2) == 0)
    def _():
        k_ref[...] = kt_ref[0].astype(F32).T.astype(BF)

    q = q_ref[0]
    feat = lax.broadcasted_iota(I32, q.shape, 0)
    zero = jnp.zeros_like(q)
    lv = lam_ref[...]
    t1 = jnp.sum(lv[0:1] * lv[1:2], axis=-1, keepdims=True)
    t2 = jnp.sum(lv[2:3] * lv[3:4], axis=-1, keepdims=True)
    lam = jnp.exp(t1) - jnp.exp(t2) + lam_init

    qs = (jnp.where(feat < HEAD_DIM, q, zero), jnp.where(feat < HEAD_DIM, zero, q))
    tq = q.shape[1]
    n_chunks = k_ref.shape[0] // kc

    def scores(h, c):
        return jnp.dot(k_ref[c * kc:(c + 1) * kc, :], qs[h], preferred_element_type=F32)

    def fold(x, op):
        return op(x.reshape(kc // 8, 8, tq), axis=0)

    bound = bound_ref[0]
    small = bound <= ATTN_SHIFT_LIMIT

    @pl.when(small)
    def _():
        m_ref[...] = jnp.zeros(m_ref.shape, F32) + bound

    @pl.when(jnp.logical_not(small))
    def _():
        for h in range(2):
            m = jnp.full((8, tq), -jnp.inf, F32)
            for c in range(n_chunks):
                m = jnp.maximum(m, fold(scores(h, c), jnp.max))
            m_ref[h] = jnp.broadcast_to(jnp.max(m, axis=0, keepdims=True), (8, tq))

    ms = [m_ref[h][0:1, :] for h in range(2)]
    ls = [jnp.zeros((8, tq), F32) for _ in range(2)]
    for c in range(n_chunks):
        for h in range(2):
            e = jnp.exp2(scores(h, c) - ms[h])
            ls[h] = ls[h] + fold(e, jnp.sum)
            e_ref[h, c] = e.astype(BF)
    l0, l1 = [jnp.sum(l, axis=0, keepdims=True) for l in ls]

    beta = (lam * l0 / l1).astype(BF)
    acc = jnp.zeros((V_DIM, tq), F32)
    for c in range(n_chunks):
        a = e_ref[0, c] - beta * e_ref[1, c]
        acc = acc + jnp.dot(vt_ref[0, :, c * kc:(c + 1) * kc], a, preferred_element_type=F32)
    o = acc * (1.0 / l0)
    ms = jnp.mean(o * o, axis=0, keepdims=True)
    o = o * lax.rsqrt(ms + EPS) * sub_ref[...] * (1.0 - lam_init)
    o_ref[0] = o.T.astype(BF)


def _attention(bound, q, k, v, lamv, subln, n_lat, lam_init):
    b, _, s_tot = k.shape
    tq = 1024
    kc = 256
    assert s_tot % kc == 0
    kv_spec = pl.BlockSpec((1, V_DIM, s_tot), lambda bb, hh, i: (bb, hh, 0))
    return pl.pallas_call(
        functools.partial(_attn_kernel, lam_init=lam_init, kc=kc),
        grid=(b, N_HEADS, n_lat // tq),
        in_specs=[
            pl.BlockSpec(memory_space=pltpu.SMEM),
            pl.BlockSpec((1, V_DIM, tq), lambda bb, hh, i: (bb, hh, i)),
            kv_spec, kv_spec,
            pl.BlockSpec((8, LANES), lambda bb, hh, i: (0, 0)),
            pl.BlockSpec((V_DIM, 1), lambda bb, hh, i: (0, 0)),
        ],
        out_specs=pl.BlockSpec((1, tq, V_DIM), lambda bb, hh, i: (bb, i, hh)),
        out_shape=jax.ShapeDtypeStruct((b, n_lat, V_W), BF),
        scratch_shapes=[pltpu.VMEM((s_tot, V_DIM), BF),
                        pltpu.VMEM((2, 8, tq), F32),
                        pltpu.VMEM((2, s_tot // kc, kc, tq), BF)],
        compiler_params=_params(("parallel", "parallel", "arbitrary"), 48),
        name="diff_attn",
    )(bound, q, k, v, lamv, subln)


def _fourier_kernel(f_ref, bd_ref, cb_ref, sb_ref, ca_ref, sa_ref, o_ref, g_ref, gm_ref, *, scale):
    nb, n_lat, _ = f_ref.shape
    half = n_lat // 2

    @pl.when(pl.program_id(0) == 0)
    def _():
        blk = 256
        nblk = half // blk
        flip = jnp.where(lax.broadcasted_iota(I32, (blk, blk), 0)
                         + lax.broadcasted_iota(I32, (blk, blk), 1) == blk - 1, 1.0, 0.0).astype(BF)
        rowid = lax.broadcasted_iota(I32, (half, FOUR_W), 0)
        for bb in range(nb):
            cols = slice(bb * FOUR_W, (bb + 1) * FOUR_W)
            fwd = f_ref[bb, :half, :].astype(F32)
            upside = jnp.concatenate(
                [jnp.dot(flip, f_ref[bb, half + (nblk - 1 - k) * blk:half + (nblk - k) * blk, :],
                         preferred_element_type=F32) for k in range(nblk)], axis=0)
            rev = jnp.where(rowid == 0, fwd, pltpu.roll(upside, 1, axis=0))
            even = jnp.where(rowid == 0, fwd, fwd + rev)
            g_ref[:half, cols] = jnp.dot(even.astype(BF), bd_ref[:, :FOUR_W],
                                         preferred_element_type=F32).astype(BF)
            g_ref[half:, cols] = jnp.dot((fwd - rev).astype(BF), bd_ref[:, FOUR_W:],
                                         preferred_element_type=F32).astype(BF)
            gm_ref[:, cols] = jnp.dot(f_ref[bb, half:half + 16, :], bd_ref[:, :FOUR_W],
                                      preferred_element_type=F32)

    ca = ca_ref[0]
    sa = sa_ref[0]
    cb = cb_ref[...]
    sb = sb_ref[...]
    w_cos = (cb * ca - sb * sa).astype(BF)
    w_sin = (sb * ca + cb * sa).astype(BF)
    y = (jnp.dot(w_cos, g_ref[:half, :], preferred_element_type=F32)
         - jnp.dot(w_sin, g_ref[half:, :], preferred_element_type=F32))
    parity = lax.broadcasted_iota(I32, (y.shape[0], 1), 0) & 1
    y = y + jnp.where(parity == 0, scale, -scale) * gm_ref[0:1, :]
    for bb in range(nb):
        o_ref[bb] = y[:, bb * FOUR_W:(bb + 1) * FOUR_W].astype(BF)


def _fourier(f, bd, tables, n_lat):
    b = f.shape[0]
    half = n_lat // 2
    cos_row, sin_row, cos_tile, sin_tile = tables
    tr = cos_row.shape[0]
    assert tr % 2 == 0 and cos_row.shape[1] == half
    row_tab = pl.BlockSpec((tr, half), lambda i: (0, 0))
    tile_tab = pl.BlockSpec((1, 1, half), lambda i: (i, 0, 0))
    return pl.pallas_call(
        functools.partial(_fourier_kernel, scale=n_lat ** -0.5),
        grid=(n_lat // tr,),
        in_specs=[
            pl.BlockSpec((b, n_lat, FOUR_W), lambda i: (0, 0, 0)),
            pl.BlockSpec((FOUR_W, 2 * FOUR_W), lambda i: (0, 0)),
            row_tab, row_tab, tile_tab, tile_tab,
        ],
        out_specs=pl.BlockSpec((b, tr, FOUR_W), lambda i: (0, i, 0)),
        out_shape=jax.ShapeDtypeStruct((b, n_lat, FOUR_W), BF),
        scratch_shapes=[pltpu.VMEM((n_lat, b * FOUR_W), BF), pltpu.VMEM((16, b * FOUR_W), F32)],
        compiler_params=_params(("arbitrary",), 48),
        name="fourier",
    )(f, bd, cos_row, sin_row, cos_tile, sin_tile)


SUB = D // LANES


def _store_token_tiles(ref, val):
    rows = val.shape[0]
    for j in range(SUB):
        ref[0, pl.ds(j, rows, stride=SUB), :] = val[:, j * LANES:(j + 1) * LANES]


def _load_token_tiles(ref, rows):
    return jnp.concatenate([ref[0, pl.ds(j, rows, stride=SUB), :] for j in range(SUB)], axis=1)


def _residual_router(x, y, gm, nf, shf, scf, wr_t, x1_ref, hf_ref, aff_ref):
    x1 = x + gm * y
    x1_ref[0] = x1
    hf = _modulate(x1, nf, shf, scf)
    _store_token_tiles(hf_ref, hf)
    logits = _dot3(wr_t, hf, NT)
    m = jnp.max(logits, axis=0, keepdims=True)
    e = jnp.exp(logits - m)
    aff_ref[0] = e / jnp.sum(e, axis=0, keepdims=True)


def _merge0_kernel(att_ref, four_ref, w_ref, x_ref, gm_ref, nf_ref, shf_ref, scf_ref, wr_ref,
                   x1_ref, hf_ref, aff_ref):
    y = jnp.dot(att_ref[0], w_ref[:V_W, :], preferred_element_type=F32)
    y = y + jnp.dot(four_ref[0], w_ref[V_W:, :], preferred_element_type=F32)
    _residual_router(x_ref[0], y, gm_ref[0], nf_ref[...], shf_ref[0], scf_ref[0], wr_ref[...],
                     x1_ref, hf_ref, aff_ref)


def _router_specs(tm):
    vec = pl.BlockSpec((1, 1, D), lambda bb, i: (bb, 0, 0))
    row = pl.BlockSpec((1, tm, D), lambda bb, i: (bb, i, 0))
    in_specs = [vec, pl.BlockSpec((1, D), lambda bb, i: (0, 0)), vec, vec,
                pl.BlockSpec((N_EXP, D), lambda bb, i: (0, 0))]
    tiles = _tile_spec(tm)
    out_specs = [row, tiles, pl.BlockSpec((1, N_EXP, tm), lambda bb, i: (bb, 0, i))]
    return row, in_specs, out_specs


def _tile_spec(tm):
    return pl.BlockSpec((1, tm * SUB, LANES), lambda bb, i: (bb, i, 0))


def _router_out_shape(b, n):
    return [jax.ShapeDtypeStruct((b, n, D), F32), jax.ShapeDtypeStruct((b, n * SUB, LANES), F32),
            jax.ShapeDtypeStruct((b, N_EXP, n), F32)]


def _merge0(att, four, w_bf, x, gm, nf, shf, scf, wr_t):
    b, n, _ = x.shape
    tm = 512
    row, r_in, r_out = _router_specs(tm)
    return pl.pallas_call(
        _merge0_kernel,
        grid=(b, n // tm),
        in_specs=[
            pl.BlockSpec((1, tm, V_W), lambda bb, i: (bb, i, 0)),
            pl.BlockSpec((1, tm, FOUR_W), lambda bb, i: (bb, i, 0)),
            pl.BlockSpec((D, D), lambda bb, i: (0, 0)),
            row,
        ] + r_in,
        out_specs=r_out,
        out_shape=_router_out_shape(b, n),
        compiler_params=_params(("parallel", "arbitrary"), 48),
        name="merge0",
    )(att, four, w_bf, x, gm, nf, shf, scf, wr_t)


HALO = 8


def _mixer1_kernel(x_ref, xp_ref, xn_ref, moe_ref, moep_ref, moen_ref, nw_ref, sh_ref, sc_ref, win_ref,
                   cw_ref, wout_ref, gm_ref, nf_ref, shf_ref, scf_ref, wr_ref, x1_ref, hf_ref, aff_ref):
    i = pl.program_id(1)
    last = pl.num_programs(1) - 1
    tm = x_ref.shape[1]
    x = x_ref[0] + _load_token_tiles(moe_ref, tm)
    x_prev = xp_ref[0] + _load_token_tiles(moep_ref, HALO)
    x_next = xn_ref[0] + _load_token_tiles(moen_ref, HALO)
    x_all = jnp.concatenate([x_prev, x, x_next], axis=0)
    h_all = _modulate(x_all, nw_ref[...], sh_ref[0], sc_ref[0]).astype(BF)
    cg = jnp.dot(h_all, win_ref[:, D:2 * D], preferred_element_type=F32)
    u = jnp.dot(h_all, win_ref[:, 2 * D:], preferred_element_type=F32)
    z_raw = cg * u
    core = slice(HALO, HALO + tm)
    z_all = jnp.concatenate([z_raw[:HALO] * jnp.where(i > 0, 1.0, 0.0), z_raw[core],
                             z_raw[HALO + tm:] * jnp.where(i < last, 1.0, 0.0)], axis=0)
    z_up = pltpu.roll(z_all, 1, axis=0)[core]
    z_dn = pltpu.roll(z_all, tm + 2 * HALO - 1, axis=0)[core]
    cw = cw_ref[...]
    conv = cw[0:1] * z_up + cw[1:2] * z_all[core] + cw[2:3] * z_dn
    bg = jnp.dot(h_all[core], win_ref[:, :D], preferred_element_type=F32)
    y = jnp.dot((bg * conv).astype(BF), wout_ref[...], preferred_element_type=F32)
    _residual_router(x, y, gm_ref[0], nf_ref[...], shf_ref[0], scf_ref[0], wr_ref[...],
                     x1_ref, hf_ref, aff_ref)


def _mixer1(x, moe_tiles, nw, sh, sc, w_in, conv_w8, w_out, gm, nf, shf, scf, wr_t):
    b, n, _ = x.shape
    tm = 512
    per = tm // HALO
    n_halo = n // HALO
    prev_blk = lambda bb, i: (bb, jnp.maximum(i * per - 1, 0), 0)
    next_blk = lambda bb, i: (bb, jnp.minimum((i + 1) * per, n_halo - 1), 0)
    row, r_in, r_out = _router_specs(tm)
    vec = pl.BlockSpec((1, 1, D), lambda bb, i: (bb, 0, 0))
    full = lambda shape: pl.BlockSpec(shape, lambda bb, i: (0, 0))
    return pl.pallas_call(
        _mixer1_kernel,
        grid=(b, n // tm),
        in_specs=[
            row, pl.BlockSpec((1, HALO, D), prev_blk), pl.BlockSpec((1, HALO, D), next_blk),
            _tile_spec(tm), pl.BlockSpec((1, HALO * SUB, LANES), prev_blk),
            pl.BlockSpec((1, HALO * SUB, LANES), next_blk),
            full((1, D)), vec, vec, full((D, 3 * D)), full((8, D)), full((D, D)),
        ] + r_in,
        out_specs=r_out,
        out_shape=_router_out_shape(b, n),
        compiler_params=_params(("parallel", "arbitrary"), 56),
        name="mixer1",
    )(x, x, x, moe_tiles, moe_tiles, moe_tiles, nw, sh, sc, w_in, conv_w8, w_out, gm, nf, shf, scf, wr_t)


def _select_kernel(a_ref, idx_ref, gate_ref, cl_s, off_s, inc_s, hi_s, mid_s, lo_s, *, cap, chunks):
    a = a_ref[...]
    rows = a.shape[0]
    groups = rows // chunks
    per_batch = N_EXP * chunks
    assert chunks & (chunks - 1) == 0
    shift = chunks.bit_length() - 1

    member = jnp.where((lax.broadcasted_iota(I32, (groups, rows), 1) >> shift)
                       == lax.broadcasted_iota(I32, (groups, rows), 0), 1.0, 0.0).astype(BF)
    ri =lax.broadcasted_iota(I32, (per_batch, per_batch), 0)
    ci = lax.broadcasted_iota(I32, (per_batch, per_batch), 1)
    same_f = jnp.where((ri >> shift) == (ci >> shift), 1.0, 0.0)
    same = same_f.astype(BF)
    lower = (same_f * jnp.where(ci < ri, 1.0, 0.0)).astype(BF)
    li = lax.broadcasted_iota(I32, (LANES, LANES), 0)
    lj = lax.broadcasted_iota(I32, (LANES, LANES), 1)
    incl = jnp.where(li <= lj, 1.0, 0.0).astype(BF)

    def bcast(col):
        return jnp.broadcast_to(col, (per_batch, LANES))

    def prefix(maskf):
        cl = jnp.dot(maskf.astype(BF), incl, preferred_element_type=F32)
        tot = bcast(cl[:, LANES - 1:LANES])
        off = jnp.dot(lower, tot.astype(BF), preferred_element_type=F32)
        return cl, off, tot

    min_normal = 0x00800000

    a3 = a.reshape(groups, chunks, LANES)

    def at_least(bits):
        thr_g = lax.bitcast_convert_type(bits, F32)[:, None, :]
        return jnp.where(a3 >= thr_g, 1.0, 0.0).reshape(rows, LANES)

    def search(step, t):
        cand = t | jnp.left_shift(jnp.int32(1), 30 - step)
        part = jnp.dot(member, at_least(cand).astype(BF), preferred_element_type=F32)
        cnt = jnp.sum(part, axis=-1, keepdims=True)
        return jnp.where(cnt >= cap, jnp.where(cand >= min_normal, cand, t), t)

    thr = lax.fori_loop(0, 31, search, jnp.zeros((groups, LANES), I32))
    gtf = at_least(jnp.maximum(thr + 1, min_normal))
    eqf = at_least(thr) - gtf
    for bb in range(rows // per_batch):
        sl = slice(bb * per_batch, (bb + 1) * per_batch)
        gt_b, eq_b = gtf[sl], eqf[sl]
        n_gt = jnp.dot(same, bcast(jnp.sum(gt_b, axis=-1, keepdims=True)).astype(BF),
                       preferred_element_type=F32)
        cl_eq, off_eq, _ = prefix(eq_b)
        sel = gt_b + eq_b * jnp.where(cl_eq + off_eq <= cap - n_gt, 1.0, 0.0)
        cl, off, tot = prefix(sel)
        cl_s[sl, :] = cl.astype(BF)
        off_s[sl, :] = off
        inc_s[sl, :] = off + tot
        a_b = a[sl]
        a_hi = a_b.astype(BF)
        r1 = a_b - a_hi.astype(F32)
        a_mid = r1.astype(BF)
        hi_s[sl, :] = a_hi
        mid_s[sl, :] = a_mid
        lo_s[sl, :] = (r1 - a_mid.astype(F32)).astype(BF)

    slot = lax.broadcasted_iota(I32, (chunks, cap), 1).astype(F32)
    chunk_id = lax.broadcasted_iota(I32, (chunks, cap), 0).astype(F32)
    lane_id = lax.broadcasted_iota(I32, (LANES, cap), 0).astype(F32)
    tn = (((0,), (0,)), ((), ()))
    reps = cap // LANES

    def widen(x):
        return jnp.concatenate([x] * reps, axis=1)

    def per_group(g, carry):
        win = pl.ds(pl.multiple_of(g * chunks, chunks), chunks)
        inc_g = widen(inc_s[win, :])
        off_g = widen(off_s[win, :])
        chunk_of = jnp.sum(jnp.where(inc_g <= slot, 1.0, 0.0), axis=0, keepdims=True)
        pick_f = jnp.where(chunk_id == chunk_of, 1.0, 0.0)
        before = jnp.sum(pick_f * off_g, axis=0, keepdims=True)
        pick = pick_f.astype(BF)
        counts = lax.dot_general(cl_s[win, :], pick, tn, preferred_element_type=F32)
        lane_of = jnp.sum(jnp.where(counts <= slot[0:1] - before, 1.0, 0.0), axis=0, keepdims=True)
        aff = (lax.dot_general(hi_s[win, :], pick, tn, preferred_element_type=F32)
               + lax.dot_general(mid_s[win, :], pick, tn, preferred_element_type=F32)
               + lax.dot_general(lo_s[win, :], pick, tn, preferred_element_type=F32))
        gate = jnp.sum(jnp.where(lane_id == lane_of, aff, 0.0), axis=0, keepdims=True)
        idx_ref[pl.ds(g, 1), :] = ((chunk_of * float(LANES) + lane_of) * float(SUB)).astype(I32)
        gate_ref[pl.ds(g, 1), :] = gate
        return carry

    lax.fori_loop(0, groups, per_group, 0, unroll=2)


def _select(aff_rows, cap, chunks):
    rows = aff_rows.shape[0]
    groups = rows // chunks
    whole = lambda shape: pl.BlockSpec(shape, lambda i: (0, 0))
    return pl.pallas_call(
        functools.partial(_select_kernel, cap=cap, chunks=chunks),
        grid=(1,),
        in_specs=[whole((rows, LANES))],
        out_specs=[whole((groups, cap)), whole((groups, cap))],
        out_shape=[jax.ShapeDtypeStruct((groups, cap), I32), jax.ShapeDtypeStruct((groups, cap), F32)],
        scratch_shapes=[pltpu.VMEM((rows, LANES), BF), pltpu.VMEM((rows, LANES), F32),
                        pltpu.VMEM((rows, LANES), F32), pltpu.VMEM((rows, LANES), BF),
                        pltpu.VMEM((rows, LANES), BF), pltpu.VMEM((rows, LANES), BF)],
        compiler_params=_params(("arbitrary",), 48),
        name="select",
    )(aff_rows)


def _gather_kernel(idx_ref, h_ref, o_ref):
    slots = idx_ref.shape[2]

    def body(c, carry):
        t = pl.multiple_of(idx_ref[0, 0, c], SUB)
        o_ref[0, 0, pl.ds(pl.multiple_of(c * SUB, SUB), SUB), :] = h_ref[0, pl.ds(t, SUB), :]
        return carry

    lax.fori_loop(0, slots, body, 0, unroll=8)


def _gather(idx, hf_tiles, cap):
    b, rows, _ = hf_tiles.shape
    steps = N_EXP // EXPERTS_PER_STEP
    slots = EXPERTS_PER_STEP * cap
    xs = pl.pallas_call(
        _gather_kernel,
        grid=(b, steps),
        in_specs=[
            pl.BlockSpec((1, 1, slots), lambda bb, s: (bb * steps + s, 0, 0), memory_space=pltpu.SMEM),
            pl.BlockSpec((1, rows, LANES), lambda bb, s: (bb, 0, 0)),
        ],
        out_specs=pl.BlockSpec((1, 1, slots * SUB, LANES), lambda bb, s: (bb, s, 0, 0)),
        out_shape=jax.ShapeDtypeStruct((b, steps, slots * SUB, LANES), F32),
        compiler_params=_params(("parallel", "arbitrary"), 56),
        name="gather",
    )(idx, hf_tiles)
    return xs.reshape(b, N_EXP, cap * SUB, LANES)


def _ffn_kernel(x_ref, gate_ref, gf_ref, wg_ref, wu_ref, wd_ref, o_ref, wg_s, wu_s, wd_s):
    @pl.when(pl.program_id(1) == 0)
    def _():
        wg_s[...] = wg_ref[...].astype(BF)
        wu_s[...] = wu_ref[...].astype(BF)
        wd_s[...] = wd_ref[...].astype(BF)

    cap = x_ref.shape[2] // SUB
    x = _load_token_tiles(x_ref.at[0], cap).astype(BF)
    dexp = wg_s.shape[1]
    step = 256
    y = None
    for c0 in range(0, dexp, step):
        g = jnp.dot(x, wg_s[:, c0:c0 + step], preferred_element_type=F32)
        u = jnp.dot(x, wu_s[:, c0:c0 + step], preferred_element_type=F32)
        act = (g * (1.0 / (1.0 + jnp.exp(-g))) * u).astype(BF)
        part = jnp.dot(act, wd_s[c0:c0 + step, :], preferred_element_type=F32)
        y = part if y is None else y + part
    gate_col =jnp.broadcast_to(gate_ref[0], (LANES, cap)).T
    y = y * jnp.concatenate([gate_col] * SUB, axis=1) * gf_ref[0]
    _store_token_tiles(o_ref.at[0], y)


def _ffn(xs, gate, gf, w_gate, w_up, w_down, layer):
    b, _, tile_rows, _ = xs.shape
    cap = tile_rows // SUB
    dexp = w_gate.shape[-1]
    gspec = pl.BlockSpec((1, 1, cap), lambda e, bb: (bb * N_EXP + e, 0, 0))
    gfspec = pl.BlockSpec((1, 1, D), lambda e, bb: (bb, 0, 0))
    yspec = pl.BlockSpec((1, 1, tile_rows, LANES), lambda e, bb: (bb, e, 0, 0))
    xspec = yspec

    def wspec(shape, switch_at):
        def index(e, bb):
            return (layer, jnp.minimum(e + jnp.where(bb >= switch_at, 1, 0), N_EXP - 1), 0, 0)
        return pl.BlockSpec((None, None) + shape, index)

    assert b >= 2
    stagger = [min(j, b - 1) for j in (1, 2, 3)]
    return pl.pallas_call(
        _ffn_kernel,
        grid=(N_EXP, b),
        in_specs=[xspec, gspec, gfspec, wspec((D, dexp), stagger[0]), wspec((D, dexp), stagger[1]),
                  wspec((dexp, D), stagger[2])],
        out_specs=yspec,
        out_shape=jax.ShapeDtypeStruct((b, N_EXP, cap * SUB, LANES), F32),
        scratch_shapes=[pltpu.VMEM((D, dexp), BF), pltpu.VMEM((D, dexp), BF), pltpu.VMEM((dexp, D), BF)],
        compiler_params=_params(("arbitrary", "arbitrary"), 56),
        name="expert_ffn",
    )(xs, gate, gf, w_gate, w_up, w_down)


def _scatter_kernel(idx_ref, y_ref, *rest):
    if len(rest) == 1:
        (o_ref,), x_ref = rest, None
        acc_ref = o_ref
    else:
        x_ref, o_ref, acc_ref = rest
    slots = idx_ref.shape[2]
    expert_steps = N_EXP // EXPERTS_PER_STEP
    step = pl.program_id(1)

    @pl.when(step == 0)
    def _():
        acc_ref[...] = jnp.zeros(acc_ref.shape, F32)

    group = 16

    def body(g, carry):
        c0 = g * group
        ts = [pl.multiple_of(idx_ref[0, 0, c0 + j], SUB) for j in range(group)]
        new = []
        for j in range(group):
            y = y_ref[0, 0, pl.ds(pl.multiple_of((c0 + j) * SUB, SUB), SUB), :]
            new.append(acc_ref[0, pl.ds(ts[j], SUB), :] + y)
        for j in range(group):
            acc_ref[0, pl.ds(ts[j], SUB), :] = new[j]
        return carry

    @pl.when(step < expert_steps)
    def _():
        lax.fori_loop(0, slots // group, body, 0)

    if x_ref is not None:
        @pl.when(step >= expert_steps)
        def _():
            rows = o_ref.shape[1]
            base = (step - expert_steps) * (rows * SUB)
            o_ref[0] = x_ref[0] + jnp.concatenate(
                [acc_ref[0, pl.ds(base + j, rows, stride=SUB), :] for j in range(SUB)], axis=1)


def _scatter(idx, y_tiles, n, x=None):
    b = y_tiles.shape[0]
    slots = idx.shape[2]
    assert (slots // EXPERTS_PER_STEP) % 16 == 0
    rows = n * SUB
    tm = 512
    steps = N_EXP // EXPERTS_PER_STEP
    extra = n // tm if x is not None else 0
    expert_step = lambda s: jnp.minimum(s, steps - 1)
    in_specs = [
        pl.BlockSpec((1, 1, slots), lambda bb, s: (bb * steps + expert_step(s), 0, 0),
                     memory_space=pltpu.SMEM),
        pl.BlockSpec((1, 1, slots * SUB, LANES), lambda bb, s: (bb, expert_step(s), 0, 0)),
    ]
    args = [idx, y_tiles.reshape(b, steps, slots * SUB, LANES)]
    if x is not None:
        piece = pl.BlockSpec((1, tm, D), lambda bb, s: (bb, jnp.maximum(s - steps, 0), 0))
        in_specs.append(piece)
        args.append(x)
        out_spec, out_shape = piece, jax.ShapeDtypeStruct((b, n, D), F32)
        scratch = [pltpu.VMEM((1, rows, LANES), F32)]
    else:
        out_spec = pl.BlockSpec((1, rows, LANES), lambda bb, s: (bb, 0, 0))
        out_shape, scratch = jax.ShapeDtypeStruct((b, rows, LANES), F32), []
    return pl.pallas_call(
        _scatter_kernel,
        grid=(b, steps + extra),
        in_specs=in_specs,
        out_specs=out_spec,
        out_shape=out_shape,
        scratch_shapes=scratch,
        compiler_params=_params(("parallel", "arbitrary"), 56),
        name="scatter_add",
    )(*args)


def _moe(hf_tiles, aff_t, gf, w_gate, w_up, w_down, layer, x=None):
    b, _, n = aff_t.shape
    cap = CAP_FACTOR * n // N_EXP
    chunks = n // LANES
    idx, gate = _select(aff_t.reshape(b * N_EXP * chunks, LANES), cap, chunks)
    idx = idx.reshape(b * N_EXP // EXPERTS_PER_STEP, 1, EXPERTS_PER_STEP * cap)
    gate = gate.reshape(b * N_EXP, 1, cap)
    xs = _gather(idx, hf_tiles, cap)
    y_tiles = _ffn(xs, gate, gf, w_gate, w_up, w_down, layer)
    return _scatter(idx, y_tiles, n, x)


def _rope_tables(n_lat, n_ctx):
    rows = n_lat // GRID_W
    r = jnp.repeat(jnp.arange(rows, dtype=F32), GRID_W)
    col = jnp.tile(jnp.arange(GRID_W, dtype=F32), rows)
    n_freq = HEAD_DIM // 4
    inv = ROPE_BASE ** (-jnp.arange(n_freq, dtype=F32) / n_freq)
    ar = r[:, None] * inv
    ac = col[:, None] * inv
    ang = jnp.concatenate([ar, ar, ac, ac], axis=-1)
    sign = jnp.where((jnp.arange(HEAD_DIM) % 32) < 16, -1.0, 1.0).astype(F32)
    cos = jnp.concatenate([jnp.cos(ang), jnp.ones((n_ctx, HEAD_DIM), F32)], axis=0)
    sin_s = jnp.concatenate([jnp.sin(ang) * sign, jnp.zeros((n_ctx, HEAD_DIM), F32)], axis=0)
    return jnp.tile(cos, (1, 2)).T, jnp.tile(sin_s, (1, 2)).T


def _dft_tables(n_lat):
    c = jnp.arange(FOUR_G, dtype=I32)
    ang_c = (2.0 * math.pi / FOUR_G) * ((c[:, None] * c[None, :]) % FOUR_G).astype(F32)
    eye = jnp.eye(FOUR_W // FOUR_G, dtype=F32)
    bd = jnp.concatenate([jnp.kron(eye, jnp.cos(ang_c)), jnp.kron(eye, jnp.sin(ang_c))], axis=1)
    bd = (bd * FOUR_G ** -0.5).astype(BF)
    tr = 256
    n = jnp.arange(n_lat, dtype=I32)[None, :]
    r = jnp.arange(tr, dtype=I32)[:, None]
    i = jnp.arange(n_lat // tr, dtype=I32)[:, None]
    tile_ang = (2.0 * math.pi / n_lat) * ((i * tr * n) % n_lat).astype(F32)
    side = int(round(math.sqrt(n_lat)))
    assert side * side == n_lat
    s = jnp.arange(side, dtype=I32)[None, :]
    hi_ang = (2.0 * math.pi / side) * ((r * s) % side).astype(F32)
    lo_ang = (2.0 * math.pi / n_lat) * ((r * s) % n_lat).astype(F32)
    ch, sh = jnp.cos(hi_ang)[:, :, None], jnp.sin(hi_ang)[:, :, None]
    cl, sl = jnp.cos(lo_ang)[:, None, :], jnp.sin(lo_ang)[:, None, :]
    scale = n_lat ** -0.5
    half = n_lat // 2
    ch, sh = ch[:, :side // 2], sh[:, :side // 2]
    cos_row = ((ch * cl - sh * sl) * scale).reshape(tr, half)
    sin_row = ((sh * cl + ch * sl) * scale).reshape(tr, half)
    tile_ang = tile_ang[:, :half]
    tables = (cos_row, sin_row, jnp.cos(tile_ang)[:, None, :], jnp.sin(tile_ang)[:, None, :])
    return bd, tables


def kernel(x, c, ctx, c_ctx, ada_w, ada_b, norm_mix, norm_ffn, attn_w_in, attn_q_norm, attn_k_norm,
           lam_q1, lam_k1, lam_q2, lam_k2, attn_subln, attn_w_out, conv_w_in, conv_w, conv_w_out,
           router_w, moe_w_gate, moe_w_up, moe_w_down):
    b, n, _ = x.shape
    n_ctx = ctx.shape[1]
    assert x.shape[2] == D and n % 512 == 0 and n_ctx % 256 == 0

    cond8 = jnp.concatenate([c, c_ctx[None, :], jnp.zeros((8 - b - 1, D), F32)], axis=0)
    ada = _ada(cond8, ada_w, ada_b)

    def mods(layer):
        m = ada[layer].reshape(8, 6, D)
        return [m[:, j] for j in range(6)]

    vec = lambda t: t[:b].reshape(b, 1, D)

    sh_m, sc_m, g_m, sh_f, sc_f, g_f = mods(0)
    both = lambda t: jnp.stack([t[:b], jnp.broadcast_to(t[b], (b, D))], axis=1).reshape(b, 2, 1, D)
    cos_t, sin_t = _rope_tables(n, n_ctx)
    bd, dft_tabs = _dft_tables(n)
    col2 = lambda t: jnp.tile(t.reshape(HEAD_DIM, 1), (2, 1))
    w_in = attn_w_in[0]
    q, k, v, f = _proj0(x, ctx, norm_mix[0].reshape(1, D), both(sh_m), both(sc_m),
                        w_in[:, 2 * QK_W + V_W:].astype(BF), w_in[:, :2 * QK_W + V_W].astype(BF).T,
                        cos_t, sin_t, col2(attn_q_norm[0]), col2(attn_k_norm[0]))
    lam_init = 0.8 - 0.6 * math.exp(-0.3 * 0)
    lamv = jnp.zeros((8, LANES), F32).at[:4, :HEAD_DIM].set(
        jnp.stack([lam_q1[0], lam_k1[0], lam_q2[0], lam_k2[0]]))
    score_bound = (1.01 * HEAD_DIM * Q_SCALE * jnp.max(jnp.abs(attn_q_norm[0]))
                   * jnp.max(jnp.abs(attn_k_norm[0])) + 0.1).reshape(1)
    att = _attention(score_bound, q, k, v, lamv, attn_subln[0].reshape(V_DIM, 1), n, lam_init)
    four = _fourier(f, bd, dft_tabs, n)
    x1, hf, aff_t = _merge0(att, four, attn_w_out[0].astype(BF), x, vec(g_m),
                            norm_ffn[0].reshape(1, D), vec(sh_f), vec(sc_f), router_w[0].T)
    moe0 = _moe(hf, aff_t, vec(g_f), moe_w_gate, moe_w_up, moe_w_down, 0)

    sh_m, sc_m, g_m, sh_f, sc_f, g_f = mods(1)
    conv_w8 = jnp.zeros((8, D), F32).at[:3].set(conv_w[0])
    x3, hf, aff_t = _mixer1(x1, moe0, norm_mix[1].reshape(1, D), vec(sh_m), vec(sc_m),
                            conv_w_in[0].astype(BF), conv_w8, conv_w_out[0].astype(BF), vec(g_m),
                            norm_ffn[1].reshape(1, D), vec(sh_f), vec(sc_f), router_w[1].T)
    return _moe(hf, aff_t, vec(g_f), moe_w_gate, moe_w_up, moe_w_down, 1, x3)
```

```python
import functools
import math

import jax
import jax.numpy as jnp
from jax import lax
from jax.experimental import pallas as pl
from jax.experimental.pallas import tpu as pltpu

BF = jnp.bfloat16
F32 = jnp.float32
I32 = jnp.int32

D = 1024
GRID_W = 64
N_HEADS = 6
HEAD_DIM = 64
V_DIM = 2 * HEAD_DIM
QK_W = N_HEADS * 2 * HEAD_DIM
V_W = N_HEADS * V_DIM
FOUR_W = 256
FOUR_G = 64
N_EXP = 16
EXPERTS_PER_STEP = 4
CAP_FACTOR = 2
ROPE_BASE = 10000.0
EPS = 1e-6
Q_SCALE = HEAD_DIM ** -0.5 * math.log2(math.e)
ATTN_SHIFT_LIMIT = 40.0
LANES = 128
MIB = 1024 * 1024

NT = (((1,), (1,)), ((), ()))


def _params(sem, vmem_mib):
    return pltpu.CompilerParams(dimension_semantics=sem, vmem_limit_bytes=vmem_mib * MIB)


def _split2(x):
    hi = x.astype(BF)
    lo = (x - hi.astype(F32)).astype(BF)
    return hi, lo


def _dot3(a, b, dims=(((1,), (0,)), ((), ()))):
    m = a.shape[0]
    ah, al = _split2(a)
    bh, bl = _split2(b)
    dg = functools.partial(lax.dot_general, dimension_numbers=dims, preferred_element_type=F32)
    both = dg(jnp.concatenate([ah, al], axis=0), bh)
    return both[:m] + both[m:] + dg(ah, bl)


def _modulate(x, nw, shift, scale):
    ms = jnp.mean(x * x, axis=-1, keepdims=True)
    return (x * lax.rsqrt(ms + EPS) * nw) * (1.0 + scale) + shift


def _ada_kernel(c_ref, w_ref, b_ref, o_ref):
    cv = c_ref[...]
    s = cv * (1.0 / (1.0 + jnp.exp(-cv)))
    o_ref[0] = _dot3(s, w_ref[0]) + b_ref[0]


def _ada(cond8, ada_w, ada_b):
    depth = ada_w.shape[0]
    tn = 1536
    return pl.pallas_call(
        _ada_kernel,
        grid=(depth, 6 * D // tn),
        in_specs=[
            pl.BlockSpec((8, D), lambda l, j: (0, 0)),
            pl.BlockSpec((1, D, tn), lambda l, j: (l, 0, j)),
            pl.BlockSpec((1, 1, tn), lambda l, j: (l, 0, j)),
        ],
        out_specs=pl.BlockSpec((1, 8, tn), lambda l, j: (l, 0, j)),
        out_shape=jax.ShapeDtypeStruct((depth, 8, 6 * D), F32),
        compiler_params=_params(("arbitrary", "arbitrary"), 40),
        name="ada",
    )(cond8, ada_w, ada_b.reshape(depth, 1, 6 * D))


def _norm_rope_t(xt, wn_col, cos_t, sin_t, out_scale):
    q4 = HEAD_DIM // 4
    halves = []
    for s in range(2):
        x = xt[s * HEAD_DIM:(s + 1) * HEAD_DIM]
        ms = jnp.sum(x * x, axis=0, keepdims=True) * (1.0 / HEAD_DIM)
        y = x * lax.rsqrt(ms + EPS) * wn_col[s * HEAD_DIM:(s + 1) * HEAD_DIM]
        rot = jnp.concatenate([y[q4:2 * q4], y[:q4], y[3 * q4:], y[2 * q4:3 * q4]], axis=0)
        halves.append(y * cos_t[s * HEAD_DIM:(s + 1) * HEAD_DIM] + rot * sin_t[s * HEAD_DIM:(s + 1) * HEAD_DIM])
    out = jnp.concatenate(halves, axis=0)
    return out if out_scale == 1.0 else out * out_scale


def _proj0_kernel(x_ref, ctx_ref, nw_ref, sh_ref, sc_ref, wf_ref, wt_ref, cost_ref, sint_ref,
                  qn_ref, kn_ref, q_ref, k_ref, v_ref, f_ref, *, lat_tiles):
    x = jnp.where(pl.program_id(1) < lat_tiles, x_ref[0], ctx_ref[0])
    h = _modulate(x, nw_ref[...], sh_ref[0, 0], sc_ref[0, 0]).astype(BF)
    def project_t(r0, width):
        return lax.dot_general(wt_ref[r0:r0 + width, :], h, NT, preferred_element_type=F32)

    cos_t = cost_ref[...]
    sin_t = sint_ref[...]
    for out_ref, wn_ref, r0, scale in ((q_ref, qn_ref, 0, Q_SCALE), (k_ref, kn_ref, QK_W, 1.0)):
        p_t = project_t(r0, QK_W)
        wn_col = wn_ref[...]
        for hh in range(N_HEADS):
            c0 = hh * V_DIM
            out_ref[0, c0:c0 + V_DIM, :] = _norm_rope_t(
                p_t[c0:c0 + V_DIM], wn_col, cos_t, sin_t, scale).astype(BF)
    v_ref[0] = project_t(2 * QK_W, V_W).astype(BF)
    f_ref[0] = jnp.dot(h, wf_ref[...], preferred_element_type=F32).astype(BF)


def _proj0(x, ctx, nw, sh2, sc2, w_four, w_qkv_t, cos_t, sin_t, qn_col, kn_col):
    b, n_lat, _ = x.shape
    n_ctx = ctx.shape[1]
    s_tot = n_lat + n_ctx
    tm = 256
    lat_tiles = n_lat // tm
    mod_spec = pl.BlockSpec((1, 1, 1, D), lambda bb, i: (bb, i // lat_tiles, 0, 0))
    full = lambda shape: pl.BlockSpec(shape, lambda bb, i: tuple(0 for _ in shape))
    feat = lambda w: pl.BlockSpec((1, w, tm), lambda bb, i: (bb, 0, i))
    feat_shape = lambda w: jax.ShapeDtypeStruct((b, w, s_tot), BF)
    return pl.pallas_call(
        functools.partial(_proj0_kernel, lat_tiles=lat_tiles),
        grid=(b, s_tot // tm),
        in_specs=[
            pl.BlockSpec((1, tm, D), lambda bb, i: (bb, jnp.minimum(i, lat_tiles - 1), 0)),
            pl.BlockSpec((1, tm, D), lambda bb, i: (bb, jnp.maximum(i - lat_tiles, 0), 0)),
            full((1, D)), mod_spec, mod_spec, full((D, FOUR_W)), full((2 * QK_W + V_W, D)),
            pl.BlockSpec((LANES, tm), lambda bb, i: (0, i)),
            pl.BlockSpec((LANES, tm), lambda bb, i: (0, i)),
            full((LANES, 1)), full((LANES, 1)),
        ],
        out_specs=[feat(QK_W), feat(QK_W), feat(V_W),
                   pl.BlockSpec((1, tm, FOUR_W), lambda bb, i: (bb, i, 0))],
        out_shape=[feat_shape(QK_W), feat_shape(QK_W), feat_shape(V_W),
                   jax.ShapeDtypeStruct((b, s_tot, FOUR_W), BF)],
        compiler_params=_params(("parallel", "arbitrary"), 48),
        name="proj0",
    )(x, ctx, nw, sh2, sc2, w_four, w_qkv_t, cos_t, sin_t, qn_col, kn_col)


def _attn_kernel(bound_ref, q_ref, kt_ref, vt_ref, lam_ref, sub_ref, o_ref, k_ref, m_ref, e_ref,
                 *, lam_init, kc):
    @pl.when(pl.program_id(2) == 0)
    def _():
        k_ref[...] = kt_ref[0].astype(F32).T.astype(BF)

    q = q_ref[0]
    feat = lax.broadcasted_iota(I32, q.shape, 0)
    zero = jnp.zeros_like(q)
    lv = lam_ref[...]
    t1 = jnp.sum(lv[0:1] * lv[1:2], axis=-1, keepdims=True)
    t2 = jnp.sum(lv[2:3] * lv[3:4], axis=-1, keepdims=True)
    lam = jnp.exp(t1) - jnp.exp(t2) + lam_init

    qs = (jnp.where(feat < HEAD_DIM, q, zero), jnp.where(feat < HEAD_DIM, zero, q))
    tq = q.shape[1]
    n_chunks = k_ref.shape[0] // kc

    def scores(h, c):
        return jnp.dot(k_ref[c * kc:(c + 1) * kc, :], qs[h], preferred_element_type=F32)

    def fold(x, op):
        return op(x.reshape(kc // 8, 8, tq), axis=0)

    bound = bound_ref[0]
    small = bound <= ATTN_SHIFT_LIMIT

    @pl.when(small)
    def _():
        m_ref[...] = jnp.zeros(m_ref.shape, F32) + bound

    @pl.when(jnp.logical_not(small))
    def _():
        for h in range(2):
            m = jnp.full((8, tq), -jnp.inf, F32)
            for c in range(n_chunks):
                m = jnp.maximum(m, fold(scores(h, c), jnp.max))
            m_ref[h] = jnp.broadcast_to(jnp.max(m, axis=0, keepdims=True), (8, tq))

    ms = [m_ref[h][0:1, :] for h in range(2)]
    ls = [jnp.zeros((8, tq), F32) for _ in range(2)]
    for c in range(n_chunks):
        for h in range(2):
            e = jnp.exp2(scores(h, c) - ms[h])
            ls[h] = ls[h] + fold(e, jnp.sum)
            e_ref[h, c] = e.astype(BF)
    l0, l1 = [jnp.sum(l, axis=0, keepdims=True) for l in ls]

    beta = (lam * l0 / l1).astype(BF)
    acc = jnp.zeros((V_DIM, tq), F32)
    for c in range(n_chunks):
        a = e_ref[0, c] - beta * e_ref[1, c]
        acc = acc + jnp.dot(vt_ref[0, :, c * kc:(c + 1) * kc], a, preferred_element_type=F32)
    o = acc * (1.0 / l0)
    ms = jnp.mean(o * o, axis=0, keepdims=True)
    o = o * lax.rsqrt(ms + EPS) * sub_ref[...] * (1.0 - lam_init)
    o_ref[0] = o.T.astype(BF)


def _attention(bound, q, k, v, lamv, subln, n_lat, lam_init):
    b, _, s_tot = k.shape
    tq = 1024
    kc = 256
    assert s_tot % kc == 0
    kv_spec = pl.BlockSpec((1, V_DIM, s_tot), lambda bb, hh, i: (bb, hh, 0))
    return pl.pallas_call(
        functools.partial(_attn_kernel, lam_init=lam_init, kc=kc),
        grid=(b, N_HEADS, n_lat // tq),
        in_specs=[
            pl.BlockSpec(memory_space=pltpu.SMEM),
            pl.BlockSpec((1, V_DIM, tq), lambda bb, hh, i: (bb, hh, i)),
            kv_spec, kv_spec,
            pl.BlockSpec((8, LANES), lambda bb, hh, i: (0, 0)),
            pl.BlockSpec((V_DIM, 1), lambda bb, hh, i: (0, 0)),
        ],
        out_specs=pl.BlockSpec((1, tq, V_DIM), lambda bb, hh, i: (bb, i, hh)),
        out_shape=jax.ShapeDtypeStruct((b, n_lat, V_W), BF),
        scratch_shapes=[pltpu.VMEM((s_tot, V_DIM), BF),
                        pltpu.VMEM((2, 8, tq), F32),
                        pltpu.VMEM((2, s_tot // kc, kc, tq), BF)],
        compiler_params=_params(("parallel", "parallel", "arbitrary"), 48),
        name="diff_attn",
    )(bound, q, k, v, lamv, subln)


def _fourier_kernel(f_ref, bd_ref, cb_ref, sb_ref, ca_ref, sa_ref, o_ref, g_ref, gm_ref, *, scale):
    nb, n_lat, _ = f_ref.shape
    half = n_lat // 2

    @pl.when(pl.program_id(0) == 0)
    def _():
        blk = 256
        nblk = half // blk
        flip = jnp.where(lax.broadcasted_iota(I32, (blk, blk), 0)
                         + lax.broadcasted_iota(I32, (blk, blk), 1) == blk - 1, 1.0, 0.0).astype(BF)
        rowid = lax.broadcasted_iota(I32, (half, FOUR_W), 0)
        for bb in range(nb):
            cols = slice(bb * FOUR_W, (bb + 1) * FOUR_W)
            fwd = f_ref[bb, :half, :].astype(F32)
            upside = jnp.concatenate(
                [jnp.dot(flip, f_ref[bb, half + (nblk - 1 - k) * blk:half + (nblk - k) * blk, :],
                         preferred_element_type=F32) for k in range(nblk)], axis=0)
            rev = jnp.where(rowid == 0, fwd, pltpu.roll(upside, 1, axis=0))
            even = jnp.where(rowid == 0, fwd, fwd + rev)
            g_ref[:half, cols] = jnp.dot(even.astype(BF), bd_ref[:, :FOUR_W],
                                         preferred_element_type=F32).astype(BF)
            g_ref[half:, cols] = jnp.dot((fwd - rev).astype(BF), bd_ref[:, FOUR_W:],
                                         preferred_element_type=F32).astype(BF)
            gm_ref[:, cols] = jnp.dot(f_ref[bb, half:half + 16, :], bd_ref[:, :FOUR_W],
                                      preferred_element_type=F32)

    ca = ca_ref[0]
    sa = sa_ref[0]
    cb = cb_ref[...]
    sb = sb_ref[...]
    w_cos = (cb * ca - sb * sa).astype(BF)
    w_sin = (sb * ca + cb * sa).astype(BF)
    y = (jnp.dot(w_cos, g_ref[:half, :], preferred_element_type=F32)
         - jnp.dot(w_sin, g_ref[half:, :], preferred_element_type=F32))
    parity = lax.broadcasted_iota(I32, (y.shape[0], 1), 0) & 1
    y = y + jnp.where(parity == 0, scale, -scale) * gm_ref[0:1, :]
    for bb in range(nb):
        o_ref[bb] = y[:, bb * FOUR_W:(bb + 1) * FOUR_W].astype(BF)


def _fourier(f, bd, tables, n_lat):
    b = f.shape[0]
    half = n_lat // 2
    cos_row, sin_row, cos_tile, sin_tile = tables
    tr = cos_row.shape[0]
    assert tr % 2 == 0 and cos_row.shape[1] == half
    row_tab = pl.BlockSpec((tr, half), lambda i: (0, 0))
    tile_tab = pl.BlockSpec((1, 1, half), lambda i: (i, 0, 0))
    return pl.pallas_call(
        functools.partial(_fourier_kernel, scale=n_lat ** -0.5),
        grid=(n_lat // tr,),
        in_specs=[
            pl.BlockSpec((b, n_lat, FOUR_W), lambda i: (0, 0, 0)),
            pl.BlockSpec((FOUR_W, 2 * FOUR_W), lambda i: (0, 0)),
            row_tab, row_tab, tile_tab, tile_tab,
        ],
        out_specs=pl.BlockSpec((b, tr, FOUR_W), lambda i: (0, i, 0)),
        out_shape=jax.ShapeDtypeStruct((b, n_lat, FOUR_W), BF),
        scratch_shapes=[pltpu.VMEM((n_lat, b * FOUR_W), BF), pltpu.VMEM((16, b * FOUR_W), F32)],
        compiler_params=_params(("arbitrary",), 48),
        name="fourier",
    )(f, bd, cos_row, sin_row, cos_tile, sin_tile)


SUB = D // LANES


def _store_token_tiles(ref, val):
    rows = val.shape[0]
    for j in range(SUB):
        ref[0, pl.ds(j, rows, stride=SUB), :] = val[:, j * LANES:(j + 1) * LANES]


def _load_token_tiles(ref, rows):
    return jnp.concatenate([ref[0, pl.ds(j, rows, stride=SUB), :] for j in range(SUB)], axis=1)


def _residual_router(x, y, gm, nf, shf, scf, wr_t, x1_ref, hf_ref, aff_ref):
    x1 = x + gm * y
    x1_ref[0] = x1
    hf = _modulate(x1, nf, shf, scf)
    _store_token_tiles(hf_ref, hf)
    logits = _dot3(wr_t, hf, NT)
    m = jnp.max(logits, axis=0, keepdims=True)
    e = jnp.exp(logits - m)
    aff_ref[0] = e / jnp.sum(e, axis=0, keepdims=True)


def _merge0_kernel(att_ref, four_ref, w_ref, x_ref, gm_ref, nf_ref, shf_ref, scf_ref, wr_ref,
                   x1_ref, hf_ref, aff_ref):
    y = jnp.dot(att_ref[0], w_ref[:V_W, :], preferred_element_type=F32)
    y = y + jnp.dot(four_ref[0], w_ref[V_W:, :], preferred_element_type=F32)
    _residual_router(x_ref[0], y, gm_ref[0], nf_ref[...], shf_ref[0], scf_ref[0], wr_ref[...],
                     x1_ref, hf_ref, aff_ref)


def _router_specs(tm):
    vec = pl.BlockSpec((1, 1, D), lambda bb, i: (bb, 0, 0))
    row = pl.BlockSpec((1, tm, D), lambda bb, i: (bb, i, 0))
    in_specs = [vec, pl.BlockSpec((1, D), lambda bb, i: (0, 0)), vec, vec,
                pl.BlockSpec((N_EXP, D), lambda bb, i: (0, 0))]
    tiles = _tile_spec(tm)
    out_specs = [row, tiles, pl.BlockSpec((1, N_EXP, tm), lambda bb, i: (bb, 0, i))]
    return row, in_specs, out_specs


def _tile_spec(tm):
    return pl.BlockSpec((1, tm * SUB, LANES), lambda bb, i: (bb, i, 0))


def _router_out_shape(b, n):
    return [jax.ShapeDtypeStruct((b, n, D), F32), jax.ShapeDtypeStruct((b, n * SUB, LANES), F32),
            jax.ShapeDtypeStruct((b, N_EXP, n), F32)]


def _merge0(att, four, w_bf, x, gm, nf, shf, scf, wr_t):
    b, n, _ = x.shape
    tm = 512
    row, r_in, r_out = _router_specs(tm)
    return pl.pallas_call(
        _merge0_kernel,
        grid=(b, n // tm),
        in_specs=[
            pl.BlockSpec((1, tm, V_W), lambda bb, i: (bb, i, 0)),
            pl.BlockSpec((1, tm, FOUR_W), lambda bb, i: (bb, i, 0)),
            pl.BlockSpec((D, D), lambda bb, i: (0, 0)),
            row,
        ] + r_in,
        out_specs=r_out,
        out_shape=_router_out_shape(b, n),
        compiler_params=_params(("parallel", "arbitrary"), 48),
        name="merge0",
    )(att, four, w_bf, x, gm, nf, shf, scf, wr_t)


HALO = 8


def _mixer1_kernel(x_ref, xp_ref, xn_ref, moe_ref, moep_ref, moen_ref, nw_ref, sh_ref, sc_ref, win_ref,
                   cw_ref, wout_ref, gm_ref, nf_ref, shf_ref, scf_ref, wr_ref, x1_ref, hf_ref, aff_ref):
    i = pl.program_id(1)
    last = pl.num_programs(1) - 1
    tm = x_ref.shape[1]
    x = x_ref[0] + _load_token_tiles(moe_ref, tm)
    x_prev = xp_ref[0] + _load_token_tiles(moep_ref, HALO)
    x_next = xn_ref[0] + _load_token_tiles(moen_ref, HALO)
    x_all = jnp.concatenate([x_prev, x, x_next], axis=0)
    h_all = _modulate(x_all, nw_ref[...], sh_ref[0], sc_ref[0]).astype(BF)
    cg = jnp.dot(h_all, win_ref[:, D:2 * D], preferred_element_type=F32)
    u = jnp.dot(h_all, win_ref[:, 2 * D:], preferred_element_type=F32)
    z_raw = cg * u
    core = slice(HALO, HALO + tm)
    z_all = jnp.concatenate([z_raw[:HALO] * jnp.where(i > 0, 1.0, 0.0), z_raw[core],
                             z_raw[HALO + tm:] * jnp.where(i < last, 1.0, 0.0)], axis=0)
    z_up = pltpu.roll(z_all, 1, axis=0)[core]
    z_dn = pltpu.roll(z_all, tm + 2 * HALO - 1, axis=0)[core]
    cw = cw_ref[...]
    conv = cw[0:1] * z_up + cw[1:2] * z_all[core] + cw[2:3] * z_dn
    bg = jnp.dot(h_all[core], win_ref[:, :D], preferred_element_type=F32)
    y = jnp.dot((bg * conv).astype(BF), wout_ref[...], preferred_element_type=F32)
    _residual_router(x, y, gm_ref[0], nf_ref[...], shf_ref[0], scf_ref[0], wr_ref[...],
                     x1_ref, hf_ref, aff_ref)


def _mixer1(x, moe_tiles, nw, sh, sc, w_in, conv_w8, w_out, gm, nf, shf, scf, wr_t):
    b, n, _ = x.shape
    tm = 512
    per = tm // HALO
    n_halo = n // HALO
    prev_blk = lambda bb, i: (bb, jnp.maximum(i * per - 1, 0), 0)
    next_blk = lambda bb, i: (bb, jnp.minimum((i + 1) * per, n_halo - 1), 0)
    row, r_in, r_out = _router_specs(tm)
    vec = pl.BlockSpec((1, 1, D), lambda bb, i: (bb, 0, 0))
    full = lambda shape: pl.BlockSpec(shape, lambda bb, i: (0, 0))
    return pl.pallas_call(
        _mixer1_kernel,
        grid=(b, n // tm),
        in_specs=[
            row, pl.BlockSpec((1, HALO, D), prev_blk), pl.BlockSpec((1, HALO, D), next_blk),
            _tile_spec(tm), pl.BlockSpec((1, HALO * SUB, LANES), prev_blk),
            pl.BlockSpec((1, HALO * SUB, LANES), next_blk),
            full((1, D)), vec, vec, full((D, 3 * D)), full((8, D)), full((D, D)),
        ] + r_in,
        out_specs=r_out,
        out_shape=_router_out_shape(b, n),
        compiler_params=_params(("parallel", "arbitrary"), 56),
        name="mixer1",
    )(x, x, x, moe_tiles, moe_tiles, moe_tiles, nw, sh, sc, w_in, conv_w8, w_out, gm, nf, shf, scf, wr_t)


def _select_kernel(a_ref, idx_ref, gate_ref, cl_s, off_s, inc_s, hi_s, mid_s, lo_s, *, cap, chunks):
    a = a_ref[...]
    rows = a.shape[0]
    groups = rows // chunks
    per_batch = N_EXP * chunks
    assert chunks & (chunks - 1) == 0
    shift = chunks.bit_length() - 1

    member = jnp.where((lax.broadcasted_iota(I32, (groups, rows), 1) >> shift)
                       == lax.broadcasted_iota(I32, (groups, rows), 0), 1.0, 0.0).astype(BF)
    ri =lax.broadcasted_iota(I32, (per_batch, per_batch), 0)
    ci = lax.broadcasted_iota(I32, (per_batch, per_batch), 1)
    same_f = jnp.where((ri >> shift) == (ci >> shift), 1.0, 0.0)
    same = same_f.astype(BF)
    lower = (same_f * jnp.where(ci < ri, 1.0, 0.0)).astype(BF)
    li = lax.broadcasted_iota(I32, (LANES, LANES), 0)
    lj = lax.broadcasted_iota(I32, (LANES, LANES), 1)
    incl = jnp.where(li <= lj, 1.0, 0.0).astype(BF)

    def bcast(col):
        return jnp.broadcast_to(col, (per_batch, LANES))

    def prefix(maskf):
        cl = jnp.dot(maskf.astype(BF), incl, preferred_element_type=F32)
        tot = bcast(cl[:, LANES - 1:LANES])
        off = jnp.dot(lower, tot.astype(BF), preferred_element_type=F32)
        return cl, off, tot

    min_normal = 0x00800000

    a3 = a.reshape(groups, chunks, LANES)

    def at_least(bits):
        thr_g = lax.bitcast_convert_type(bits, F32)[:, None, :]
        return jnp.where(a3 >= thr_g, 1.0, 0.0).reshape(rows, LANES)

    def search(step, t):
        cand = t | jnp.left_shift(jnp.int32(1), 30 - step)
        part = jnp.dot(member, at_least(cand).astype(BF), preferred_element_type=F32)
        cnt = jnp.sum(part, axis=-1, keepdims=True)
        return jnp.where(cnt >= cap, jnp.where(cand >= min_normal, cand, t), t)

    thr = lax.fori_loop(0, 31, search, jnp.zeros((groups, LANES), I32))
    gtf = at_least(jnp.maximum(thr + 1, min_normal))
    eqf = at_least(thr) - gtf
    for bb in range(rows // per_batch):
        sl = slice(bb * per_batch, (bb + 1) * per_batch)
        gt_b, eq_b = gtf[sl], eqf[sl]
        n_gt = jnp.dot(same, bcast(jnp.sum(gt_b, axis=-1, keepdims=True)).astype(BF),
                       preferred_element_type=F32)
        cl_eq, off_eq, _ = prefix(eq_b)
        sel = gt_b + eq_b * jnp.where(cl_eq + off_eq <= cap - n_gt, 1.0, 0.0)
        cl, off, tot = prefix(sel)
        cl_s[sl, :] = cl.astype(BF)
        off_s[sl, :] = off
        inc_s[sl, :] = off + tot
        a_b = a[sl]
        a_hi = a_b.astype(BF)
        r1 = a_b - a_hi.astype(F32)
        a_mid = r1.astype(BF)
        hi_s[sl, :] = a_hi
        mid_s[sl, :] = a_mid
        lo_s[sl, :] = (r1 - a_mid.astype(F32)).astype(BF)

    slot = lax.broadcasted_iota(I32, (chunks, cap), 1).astype(F32)
    chunk_id = lax.broadcasted_iota(I32, (chunks, cap), 0).astype(F32)
    lane_id = lax.broadcasted_iota(I32, (LANES, cap), 0).astype(F32)
    tn = (((0,), (0,)), ((), ()))
    reps = cap // LANES

    def widen(x):
        return jnp.concatenate([x] * reps, axis=1)

    def per_group(g, carry):
        win = pl.ds(pl.multiple_of(g * chunks, chunks), chunks)
        inc_g = widen(inc_s[win, :])
        off_g = widen(off_s[win, :])
        chunk_of = jnp.sum(jnp.where(inc_g <= slot, 1.0, 0.0), axis=0, keepdims=True)
        pick_f = jnp.where(chunk_id == chunk_of, 1.0, 0.0)
        before = jnp.sum(pick_f * off_g, axis=0, keepdims=True)
        pick = pick_f.astype(BF)
        counts = lax.dot_general(cl_s[win, :], pick, tn, preferred_element_type=F32)
        lane_of = jnp.sum(jnp.where(counts <= slot[0:1] - before, 1.0, 0.0), axis=0, keepdims=True)
        aff = (lax.dot_general(hi_s[win, :], pick, tn, preferred_element_type=F32)
               + lax.dot_general(mid_s[win, :], pick, tn, preferred_element_type=F32)
               + lax.dot_general(lo_s[win, :], pick, tn, preferred_element_type=F32))
        gate = jnp.sum(jnp.where(lane_id == lane_of, aff, 0.0), axis=0, keepdims=True)
        idx_ref[pl.ds(g, 1), :] = ((chunk_of * float(LANES) + lane_of) * float(SUB)).astype(I32)
        gate_ref[pl.ds(g, 1), :] = gate
        return carry

    lax.fori_loop(0, groups, per_group, 0, unroll=2)


def _select(aff_rows, cap, chunks):
    rows = aff_rows.shape[0]
    groups = rows // chunks
    whole = lambda shape: pl.BlockSpec(shape, lambda i: (0, 0))
    return pl.pallas_call(
        functools.partial(_select_kernel, cap=cap, chunks=chunks),
        grid=(1,),
        in_specs=[whole((rows, LANES))],
        out_specs=[whole((groups, cap)), whole((groups, cap))],
        out_shape=[jax.ShapeDtypeStruct((groups, cap), I32), jax.ShapeDtypeStruct((groups, cap), F32)],
        scratch_shapes=[pltpu.VMEM((rows, LANES), BF), pltpu.VMEM((rows, LANES), F32),
                        pltpu.VMEM((rows, LANES), F32), pltpu.VMEM((rows, LANES), BF),
                        pltpu.VMEM((rows, LANES), BF), pltpu.VMEM((rows, LANES), BF)],
        compiler_params=_params(("arbitrary",), 48),
        name="select",
    )(aff_rows)


def _gather_kernel(idx_ref, h_ref, o_ref):
    slots = idx_ref.shape[2]

    def body(c, carry):
        t = pl.multiple_of(idx_ref[0, 0, c], SUB)
        o_ref[0, 0, pl.ds(pl.multiple_of(c * SUB, SUB), SUB), :] = h_ref[0, pl.ds(t, SUB), :]
        return carry

    lax.fori_loop(0, slots, body, 0, unroll=8)


def _gather(idx, hf_tiles, cap):
    b, rows, _ = hf_tiles.shape
    steps = N_EXP // EXPERTS_PER_STEP
    slots = EXPERTS_PER_STEP * cap
    xs = pl.pallas_call(
        _gather_kernel,
        grid=(b, steps),
        in_specs=[
            pl.BlockSpec((1, 1, slots), lambda bb, s: (bb * steps + s, 0, 0), memory_space=pltpu.SMEM),
            pl.BlockSpec((1, rows, LANES), lambda bb, s: (bb, 0, 0)),
        ],
        out_specs=pl.BlockSpec((1, 1, slots * SUB, LANES), lambda bb, s: (bb, s, 0, 0)),
        out_shape=jax.ShapeDtypeStruct((b, steps, slots * SUB, LANES), F32),
        compiler_params=_params(("parallel", "arbitrary"), 56),
        name="gather",
    )(idx, hf_tiles)
    return xs.reshape(b, N_EXP, cap * SUB, LANES)


def _ffn_kernel(x_ref, gate_ref, gf_ref, wg_ref, wu_ref, wd_ref, o_ref, wg_s, wu_s, wd_s):
    @pl.when(pl.program_id(1) == 0)
    def _():
        wg_s[...] = wg_ref[...].astype(BF)
        wu_s[...] = wu_ref[...].astype(BF)
        wd_s[...] = wd_ref[...].astype(BF)

    cap = x_ref.shape[2] // SUB
    x = _load_token_tiles(x_ref.at[0], cap).astype(BF)
    dexp = wg_s.shape[1]
    step = 256
    y = None
    for c0 in range(0, dexp, step):
        g = jnp.dot(x, wg_s[:, c0:c0 + step], preferred_element_type=F32)
        u = jnp.dot(x, wu_s[:, c0:c0 + step], preferred_element_type=F32)
        act = (g * (1.0 / (1.0 + jnp.exp(-g))) * u).astype(BF)
        part = jnp.dot(act, wd_s[c0:c0 + step, :], preferred_element_type=F32)
        y = part if y is None else y + part
    gate_col =jnp.broadcast_to(gate_ref[0], (LANES, cap)).T
    y = y * jnp.concatenate([gate_col] * SUB, axis=1) * gf_ref[0]
    _store_token_tiles(o_ref.at[0], y)


def _ffn(xs, gate, gf, w_gate, w_up, w_down, layer):
    b, _, tile_rows, _ = xs.shape
    cap = tile_rows // SUB
    dexp = w_gate.shape[-1]
    gspec = pl.BlockSpec((1, 1, cap), lambda e, bb: (bb * N_EXP + e, 0, 0))
    gfspec = pl.BlockSpec((1, 1, D), lambda e, bb: (bb, 0, 0))
    yspec = pl.BlockSpec((1, 1, tile_rows, LANES), lambda e, bb: (bb, e, 0, 0))
    xspec = yspec

    def wspec(shape, switch_at):
        def index(e, bb):
            return (layer, jnp.minimum(e + jnp.where(bb >= switch_at, 1, 0), N_EXP - 1), 0, 0)
        return pl.BlockSpec((None, None) + shape, index)

    assert b >= 2
    stagger = [min(j, b - 1) for j in (1, 2, 3)]
    return pl.pallas_call(
        _ffn_kernel,
        grid=(N_EXP, b),
        in_specs=[xspec, gspec, gfspec, wspec((D, dexp), stagger[0]), wspec((D, dexp), stagger[1]),
                  wspec((dexp, D), stagger[2])],
        out_specs=yspec,
        out_shape=jax.ShapeDtypeStruct((b, N_EXP, cap * SUB, LANES), F32),
        scratch_shapes=[pltpu.VMEM((D, dexp), BF), pltpu.VMEM((D, dexp), BF), pltpu.VMEM((dexp, D), BF)],
        compiler_params=_params(("arbitrary", "arbitrary"), 56),
        name="expert_ffn",
    )(xs, gate, gf, w_gate, w_up, w_down)


def _scatter_kernel(idx_ref, y_ref, *rest):
    if len(rest) == 1:
        (o_ref,), x_ref = rest, None
        acc_ref = o_ref
    else:
        x_ref, o_ref, acc_ref = rest
    slots = idx_ref.shape[2]
    expert_steps = N_EXP // EXPERTS_PER_STEP
    step = pl.program_id(1)

    @pl.when(step == 0)
    def _():
        acc_ref[...] = jnp.zeros(acc_ref.shape, F32)

    group = 16

    def body(g, carry):
        c0 = g * group
        ts = [pl.multiple_of(idx_ref[0, 0, c0 + j], SUB) for j in range(group)]
        new = []
        for j in range(group):
            y = y_ref[0, 0, pl.ds(pl.multiple_of((c0 + j) * SUB, SUB), SUB), :]
            new.append(acc_ref[0, pl.ds(ts[j], SUB), :] + y)
        for j in range(group):
            acc_ref[0, pl.ds(ts[j], SUB), :] = new[j]
        return carry

    @pl.when(step < expert_steps)
    def _():
        lax.fori_loop(0, slots // group, body, 0)

    if x_ref is not None:
        @pl.when(step >= expert_steps)
        def _():
            rows = o_ref.shape[1]
            base = (step - expert_steps) * (rows * SUB)
            o_ref[0] = x_ref[0] + jnp.concatenate(
                [acc_ref[0, pl.ds(base + j, rows, stride=SUB), :] for j in range(SUB)], axis=1)


def _scatter(idx, y_tiles, n, x=None):
    b = y_tiles.shape[0]
    slots = idx.shape[2]
    assert (slots // EXPERTS_PER_STEP) % 16 == 0
    rows = n * SUB
    tm = 512
    steps = N_EXP // EXPERTS_PER_STEP
    extra = n // tm if x is not None else 0
    expert_step = lambda s: jnp.minimum(s, steps - 1)
    in_specs = [
        pl.BlockSpec((1, 1, slots), lambda bb, s: (bb * steps + expert_step(s), 0, 0),
                     memory_space=pltpu.SMEM),
        pl.BlockSpec((1, 1, slots * SUB, LANES), lambda bb, s: (bb, expert_step(s), 0, 0)),
    ]
    args = [idx, y_tiles.reshape(b, steps, slots * SUB, LANES)]
    if x is not None:
        piece = pl.BlockSpec((1, tm, D), lambda bb, s: (bb, jnp.maximum(s - steps, 0), 0))
        in_specs.append(piece)
        args.append(x)
        out_spec, out_shape = piece, jax.ShapeDtypeStruct((b, n, D), F32)
        scratch = [pltpu.VMEM((1, rows, LANES), F32)]
    else:
        out_spec = pl.BlockSpec((1, rows, LANES), lambda bb, s: (bb, 0, 0))
        out_shape, scratch = jax.ShapeDtypeStruct((b, rows, LANES), F32), []
    return pl.pallas_call(
        _scatter_kernel,
        grid=(b, steps + extra),
        in_specs=in_specs,
        out_specs=out_spec,
        out_shape=out_shape,
        scratch_shapes=scratch,
        compiler_params=_params(("parallel", "arbitrary"), 56),
        name="scatter_add",
    )(*args)


def _moe(hf_tiles, aff_t, gf, w_gate, w_up, w_down, layer, x=None):
    b, _, n = aff_t.shape
    cap = CAP_FACTOR * n // N_EXP
    chunks = n // LANES
    idx, gate = _select(aff_t.reshape(b * N_EXP * chunks, LANES), cap, chunks)
    idx = idx.reshape(b * N_EXP // EXPERTS_PER_STEP, 1, EXPERTS_PER_STEP * cap)
    gate = gate.reshape(b * N_EXP, 1, cap)
    xs = _gather(idx, hf_tiles, cap)
    y_tiles = _ffn(xs, gate, gf, w_gate, w_up, w_down, layer)
    return _scatter(idx, y_tiles, n, x)


def _rope_tables(n_lat, n_ctx):
    rows = n_lat // GRID_W
    r = jnp.repeat(jnp.arange(rows, dtype=F32), GRID_W)
    col = jnp.tile(jnp.arange(GRID_W, dtype=F32), rows)
    n_freq = HEAD_DIM // 4
    inv = ROPE_BASE ** (-jnp.arange(n_freq, dtype=F32) / n_freq)
    ar = r[:, None] * inv
    ac = col[:, None] * inv
    ang = jnp.concatenate([ar, ar, ac, ac], axis=-1)
    sign = jnp.where((jnp.arange(HEAD_DIM) % 32) < 16, -1.0, 1.0).astype(F32)
    cos = jnp.concatenate([jnp.cos(ang), jnp.ones((n_ctx, HEAD_DIM), F32)], axis=0)
    sin_s = jnp.concatenate([jnp.sin(ang) * sign, jnp.zeros((n_ctx, HEAD_DIM), F32)], axis=0)
    return jnp.tile(cos, (1, 2)).T, jnp.tile(sin_s, (1, 2)).T


def _dft_tables(n_lat):
    c = jnp.arange(FOUR_G, dtype=I32)
    ang_c = (2.0 * math.pi / FOUR_G) * ((c[:, None] * c[None, :]) % FOUR_G).astype(F32)
    eye = jnp.eye(FOUR_W // FOUR_G, dtype=F32)
    bd = jnp.concatenate([jnp.kron(eye, jnp.cos(ang_c)), jnp.kron(eye, jnp.sin(ang_c))], axis=1)
    bd = (bd * FOUR_G ** -0.5).astype(BF)
    tr = 256
    n = jnp.arange(n_lat, dtype=I32)[None, :]
    r = jnp.arange(tr, dtype=I32)[:, None]
    i = jnp.arange(n_lat // tr, dtype=I32)[:, None]
    tile_ang = (2.0 * math.pi / n_lat) * ((i * tr * n) % n_lat).astype(F32)
    side = int(round(math.sqrt(n_lat)))
    assert side * side == n_lat
    s = jnp.arange(side, dtype=I32)[None, :]
    hi_ang = (2.0 * math.pi / side) * ((r * s) % side).astype(F32)
    lo_ang = (2.0 * math.pi / n_lat) * ((r * s) % n_lat).astype(F32)
    ch, sh = jnp.cos(hi_ang)[:, :, None], jnp.sin(hi_ang)[:, :, None]
    cl, sl = jnp.cos(lo_ang)[:, None, :], jnp.sin(lo_ang)[:, None, :]
    scale = n_lat ** -0.5
    half = n_lat // 2
    ch, sh = ch[:, :side // 2], sh[:, :side // 2]
    cos_row = ((ch * cl - sh * sl) * scale).reshape(tr, half)
    sin_row = ((sh * cl + ch * sl) * scale).reshape(tr, half)
    tile_ang = tile_ang[:, :half]
    tables = (cos_row, sin_row, jnp.cos(tile_ang)[:, None, :], jnp.sin(tile_ang)[:, None, :])
    return bd, tables


def kernel(x, c, ctx, c_ctx, ada_w, ada_b, norm_mix, norm_ffn, attn_w_in, attn_q_norm, attn_k_norm,
           lam_q1, lam_k1, lam_q2, lam_k2, attn_subln, attn_w_out, conv_w_in, conv_w, conv_w_out,
           router_w, moe_w_gate, moe_w_up, moe_w_down):
    b, n, _ = x.shape
    n_ctx = ctx.shape[1]
    assert x.shape[2] == D and n % 512 == 0 and n_ctx % 256 == 0

    cond8 = jnp.concatenate([c, c_ctx[None, :], jnp.zeros((8 - b - 1, D), F32)], axis=0)
    ada = _ada(cond8, ada_w, ada_b)

    def mods(layer):
        m = ada[layer].reshape(8, 6, D)
        return [m[:, j] for j in range(6)]

    vec = lambda t: t[:b].reshape(b, 1, D)

    sh_m, sc_m, g_m, sh_f, sc_f, g_f = mods(0)
    both = lambda t: jnp.stack([t[:b], jnp.broadcast_to(t[b], (b, D))], axis=1).reshape(b, 2, 1, D)
    cos_t, sin_t = _rope_tables(n, n_ctx)
    bd, dft_tabs = _dft_tables(n)
    col2 = lambda t: jnp.tile(t.reshape(HEAD_DIM, 1), (2, 1))
    w_in = attn_w_in[0]
    q, k, v, f = _proj0(x, ctx, norm_mix[0].reshape(1, D), both(sh_m), both(sc_m),
                        w_in[:, 2 * QK_W + V_W:].astype(BF), w_in[:, :2 * QK_W + V_W].astype(BF).T,
                        cos_t, sin_t, col2(attn_q_norm[0]), col2(attn_k_norm[0]))
    lam_init = 0.8 - 0.6 * math.exp(-0.3 * 0)
    lamv = jnp.zeros((8, LANES), F32).at[:4, :HEAD_DIM].set(
        jnp.stack([lam_q1[0], lam_k1[0], lam_q2[0], lam_k2[0]]))
    score_bound = (1.01 * HEAD_DIM * Q_SCALE * jnp.max(jnp.abs(attn_q_norm[0]))
                   * jnp.max(jnp.abs(attn_k_norm[0])) + 0.1).reshape(1)
    att = _attention(score_bound, q, k, v, lamv, attn_subln[0].reshape(V_DIM, 1), n, lam_init)
    four = _fourier(f, bd, dft_tabs, n)
    x1, hf, aff_t = _merge0(att, four, attn_w_out[0].astype(BF), x, vec(g_m),
                            norm_ffn[0].reshape(1, D), vec(sh_f), vec(sc_f), router_w[0].T)
    moe0 = _moe(hf, aff_t, vec(g_f), moe_w_gate, moe_w_up, moe_w_down, 0)

    sh_m, sc_m, g_m, sh_f, sc_f, g_f = mods(1)
    conv_w8 = jnp.zeros((8, D), F32).at[:3].set(conv_w[0])
    x3, hf, aff_t = _mixer1(x1, moe0, norm_mix[1].reshape(1, D), vec(sh_m), vec(sc_m),
                            conv_w_in[0].astype(BF), conv_w8, conv_w_out[0].astype(BF), vec(g_m),
                            norm_ffn[1].reshape(1, D), vec(sh_f), vec(sc_f), router_w[1].T)
    return _moe(hf, aff_t, vec(g_f), moe_w_gate, moe_w_up, moe_w_down, 1, x3)
```

```python
import functools
import math

import jax
import jax.numpy as jnp
from jax import lax
from jax.experimental import pallas as pl
from jax.experimental.pallas import tpu as pltpu

BF = jnp.bfloat16
F32 = jnp.float32
I32 = jnp.int32

D = 1024
GRID_W = 64
N_HEADS = 6
HEAD_DIM = 64
V_DIM = 2 * HEAD_DIM
QK_W = N_HEADS * 2 * HEAD_DIM
V_W = N_HEADS * V_DIM
FOUR_W = 256
FOUR_G = 64
N_EXP = 16
EXPERTS_PER_STEP = 4
CAP_FACTOR = 2
ROPE_BASE = 10000.0
EPS = 1e-6
Q_SCALE = HEAD_DIM ** -0.5 * math.log2(math.e)
ATTN_SHIFT_LIMIT = 40.0
LANES = 128
MIB = 1024 * 1024

ADA_COLS = 1536
PROJ0_ROWS = 256
ATTN_QUERIES = 1024
ATTN_KEYS = 256
DFT_ROWS = 256
MIX_ROWS = 512
OUT_ROWS = 1024

NT = (((1,), (1,)), ((), ()))


def _params(sem, vmem_mib):
    return pltpu.CompilerParams(dimension_semantics=sem, vmem_limit_bytes=vmem_mib * MIB)


def _split2(x):
    hi = x.astype(BF)
    lo = (x - hi.astype(F32)).astype(BF)
    return hi, lo


def _dot3(a, b, dims=(((1,), (0,)), ((), ()))):
    m = a.shape[0]
    ah, al = _split2(a)
    bh, bl = _split2(b)
    dg = functools.partial(lax.dot_general, dimension_numbers=dims, preferred_element_type=F32)
    both = dg(jnp.concatenate([ah, al], axis=0), bh)
    return both[:m] + both[m:] + dg(ah, bl)


def _modulate(x, nw, shift, scale):
    ms = jnp.mean(x * x, axis=-1, keepdims=True)
    return (x * lax.rsqrt(ms + EPS) * nw) * (1.0 + scale) + shift


def _ada_kernel(c_ref, w_ref, b_ref, o_ref):
    cv = c_ref[...]
    s = cv * (1.0 / (1.0 + jnp.exp(-cv)))
    o_ref[0] = _dot3(s, w_ref[0]) + b_ref[0]


def _ada(cond8, ada_w, ada_b):
    depth = ada_w.shape[0]
    tn = ADA_COLS
    return pl.pallas_call(
        _ada_kernel,
        grid=(depth, 6 * D // tn),
        in_specs=[
            pl.BlockSpec((8, D), lambda l, j: (0, 0)),
            pl.BlockSpec((1, D, tn), lambda l, j: (l, 0, j)),
            pl.BlockSpec((1, 1, tn), lambda l, j: (l, 0, j)),
        ],
        out_specs=pl.BlockSpec((1, 8, tn), lambda l, j: (l, 0, j)),
        out_shape=jax.ShapeDtypeStruct((depth, 8, 6 * D), F32),
        compiler_params=_params(("arbitrary", "arbitrary"), 40),
        name="ada",
    )(cond8, ada_w, ada_b.reshape(depth, 1, 6 * D))


def _norm_rope_t(xt, wn_col, cos_t, sin_t, out_scale):
    q4 = HEAD_DIM // 4
    halves = []
    for s in range(2):
        x = xt[s * HEAD_DIM:(s + 1) * HEAD_DIM]
        ms = jnp.sum(x * x, axis=0, keepdims=True) * (1.0 / HEAD_DIM)
        y = x * lax.rsqrt(ms + EPS) * wn_col[s * HEAD_DIM:(s + 1) * HEAD_DIM]
        rot = jnp.concatenate([y[q4:2 * q4], y[:q4], y[3 * q4:], y[2 * q4:3 * q4]], axis=0)
        halves.append(y * cos_t[s * HEAD_DIM:(s + 1) * HEAD_DIM] + rot * sin_t[s * HEAD_DIM:(s + 1) * HEAD_DIM])
    out = jnp.concatenate(halves, axis=0)
    return out if out_scale == 1.0 else out * out_scale


def _proj0_kernel(x_ref, ctx_ref, nw_ref, sh_ref, sc_ref, wf_ref, wt_ref, cost_ref, sint_ref,
                  qn_ref, kn_ref, q_ref, k_ref, v_ref, f_ref, *, lat_tiles):
    x = jnp.where(pl.program_id(1) < lat_tiles, x_ref[0], ctx_ref[0])
    h = _modulate(x, nw_ref[...], sh_ref[0, 0], sc_ref[0, 0]).astype(BF)
    def project_t(r0, width):
        return lax.dot_general(wt_ref[r0:r0 + width, :], h, NT, preferred_element_type=F32)

    cos_t = cost_ref[...]
    sin_t = sint_ref[...]
    for out_ref, wn_ref, r0, scale in ((q_ref, qn_ref, 0, Q_SCALE), (k_ref, kn_ref, QK_W, 1.0)):
        p_t = project_t(r0, QK_W)
        wn_col = wn_ref[...]
        for hh in range(N_HEADS):
            c0 = hh * V_DIM
            out_ref[0, c0:c0 + V_DIM, :] = _norm_rope_t(
                p_t[c0:c0 + V_DIM], wn_col, cos_t, sin_t, scale).astype(BF)
    v_ref[0] = project_t(2 * QK_W, V_W).astype(BF)
    f_ref[0] = jnp.dot(h, wf_ref[...], preferred_element_type=F32).astype(BF)


def _proj0(x, ctx, nw, sh2, sc2, w_four, w_qkv_t, cos_t, sin_t, qn_col, kn_col):
    b, n_lat, _ = x.shape
    n_ctx = ctx.shape[1]
    s_tot = n_lat + n_ctx
    tm = PROJ0_ROWS
    lat_tiles = n_lat // tm
    mod_spec = pl.BlockSpec((1, 1, 1, D), lambda bb, i: (bb, i // lat_tiles, 0, 0))
    full = lambda shape: pl.BlockSpec(shape, lambda bb, i: tuple(0 for _ in shape))
    feat = lambda w: pl.BlockSpec((1, w, tm), lambda bb, i: (bb, 0, i))
    feat_shape = lambda w: jax.ShapeDtypeStruct((b, w, s_tot), BF)
    return pl.pallas_call(
        functools.partial(_proj0_kernel, lat_tiles=lat_tiles),
        grid=(b, s_tot // tm),
        in_specs=[
            pl.BlockSpec((1, tm, D), lambda bb, i: (bb, jnp.minimum(i, lat_tiles - 1), 0)),
            pl.BlockSpec((1, tm, D), lambda bb, i: (bb, jnp.maximum(i - lat_tiles, 0), 0)),
            full((1, D)), mod_spec, mod_spec, full((D, FOUR_W)), full((2 * QK_W + V_W, D)),
            pl.BlockSpec((LANES, tm), lambda bb, i: (0, i)),
            pl.BlockSpec((LANES, tm), lambda bb, i: (0, i)),
            full((LANES, 1)), full((LANES, 1)),
        ],
        out_specs=[feat(QK_W), feat(QK_W), feat(V_W),
                   pl.BlockSpec((1, tm, FOUR_W), lambda bb, i: (bb, i, 0))],
        out_shape=[feat_shape(QK_W), feat_shape(QK_W), feat_shape(V_W),
                   jax.ShapeDtypeStruct((b, s_tot, FOUR_W), BF)],
        compiler_params=_params(("parallel", "arbitrary"), 48),
        name="proj0",
    )(x, ctx, nw, sh2, sc2, w_four, w_qkv_t, cos_t, sin_t, qn_col, kn_col)


def _attn_kernel(bound_ref, q_ref, kt_ref, vt_ref, lam_ref, sub_ref, o_ref, k_ref, m_ref, e_ref,
                 *, lam_init, kc):
    @pl.when(pl.program_id(2) == 0)
    def _():
        k_ref[...] = kt_ref[0].astype(F32).T.astype(BF)

    q = q_ref[0]
    feat = lax.broadcasted_iota(I32, q.shape, 0)
    zero = jnp.zeros_like(q)
    lv = lam_ref[...]
    t1 = jnp.sum(lv[0:1] * lv[1:2], axis=-1, keepdims=True)
    t2 = jnp.sum(lv[2:3] * lv[3:4], axis=-1, keepdims=True)
    lam = jnp.exp(t1) - jnp.exp(t2) + lam_init

    qs = (jnp.where(feat < HEAD_DIM, q, zero), jnp.where(feat < HEAD_DIM, zero, q))
    tq = q.shape[1]
    n_chunks = k_ref.shape[0] // kc

    def scores(h, c):
        return jnp.dot(k_ref[c * kc:(c + 1) * kc, :], qs[h], preferred_element_type=F32)

    def fold(x, op):
        return op(x.reshape(kc // 8, 8, tq), axis=0)

    bound = bound_ref[0]
    small = bound <= ATTN_SHIFT_LIMIT

    @pl.when(small)
    def _():
        m_ref[...] = jnp.zeros(m_ref.shape, F32) + bound

    @pl.when(jnp.logical_not(small))
    def _():
        for h in range(2):
            m = jnp.full((8, tq), -jnp.inf, F32)
            for c in range(n_chunks):
                m = jnp.maximum(m, fold(scores(h, c), jnp.max))
            m_ref[h] = jnp.broadcast_to(jnp.max(m, axis=0, keepdims=True), (8, tq))

    ms = [m_ref[h][0:1, :] for h in range(2)]
    ls = [jnp.zeros((8, tq), F32) for _ in range(2)]
    for c in range(n_chunks):
        for h in range(2):
            e = jnp.exp2(scores(h, c) - ms[h])
            ls[h] = ls[h] + fold(e, jnp.sum)
            e_ref[h, c] = e.astype(BF)
    l0, l1 = [jnp.sum(l, axis=0, keepdims=True) for l in ls]

    beta = (lam * l0 / l1).astype(BF)
    acc = jnp.zeros((V_DIM, tq), F32)
    for c in range(n_chunks):
        a = e_ref[0, c] - beta * e_ref[1, c]
        acc = acc + jnp.dot(vt_ref[0, :, c * kc:(c + 1) * kc], a, preferred_element_type=F32)
    o = acc * (1.0 / l0)
    ms = jnp.mean(o * o, axis=0, keepdims=True)
    o = o * lax.rsqrt(ms + EPS) * sub_ref[...] * (1.0 - lam_init)
    o_ref[0] = o.T.astype(BF)


def _attention(bound, q, k, v, lamv, subln, n_lat, lam_init):
    b, _, s_tot = k.shape
    tq = ATTN_QUERIES
    kc = ATTN_KEYS
    assert s_tot % kc == 0
    kv_spec = pl.BlockSpec((1, V_DIM, s_tot), lambda bb, hh, i: (bb, hh, 0))
    return pl.pallas_call(
        functools.partial(_attn_kernel, lam_init=lam_init, kc=kc),
        grid=(b, N_HEADS, n_lat // tq),
        in_specs=[
            pl.BlockSpec(memory_space=pltpu.SMEM),
            pl.BlockSpec((1, V_DIM, tq), lambda bb, hh, i: (bb, hh, i)),
            kv_spec, kv_spec,
            pl.BlockSpec((8, LANES), lambda bb, hh, i: (0, 0)),
            pl.BlockSpec((V_DIM, 1), lambda bb, hh, i: (0, 0)),
        ],
        out_specs=pl.BlockSpec((1, tq, V_DIM), lambda bb, hh, i: (bb, i, hh)),
        out_shape=jax.ShapeDtypeStruct((b, n_lat, V_W), BF),
        scratch_shapes=[pltpu.VMEM((s_tot, V_DIM), BF),
                        pltpu.VMEM((2, 8, tq), F32),
                        pltpu.VMEM((2, s_tot // kc, kc, tq), BF)],
        compiler_params=_params(("parallel", "parallel", "arbitrary"), 48),
        name="diff_attn",
    )(bound, q, k, v, lamv, subln)


def _fourier_kernel(f_ref, bd_ref, cb_ref, sb_ref, ca_ref, sa_ref, o_ref, g_ref, gm_ref, *, scale):
    nb, n_lat, _ = f_ref.shape
    half = n_lat // 2

    @pl.when(pl.program_id(0) == 0)
    def _():
        blk = 256
        nblk = half // blk
        flip = jnp.where(lax.broadcasted_iota(I32, (blk, blk), 0)
                         + lax.broadcasted_iota(I32, (blk, blk), 1) == blk - 1, 1.0, 0.0).astype(BF)
        rowid = lax.broadcasted_iota(I32, (half, FOUR_W), 0)
        for bb in range(nb):
            cols = slice(bb * FOUR_W, (bb + 1) * FOUR_W)
            fwd = f_ref[bb, :half, :].astype(F32)
            upside = jnp.concatenate(
                [jnp.dot(flip, f_ref[bb, half + (nblk - 1 - k) * blk:half + (nblk - k) * blk, :],
                         preferred_element_type=F32) for k in range(nblk)], axis=0)
            rev = jnp.where(rowid == 0, fwd, pltpu.roll(upside, 1, axis=0))
            even = jnp.where(rowid == 0, fwd, fwd + rev)
            g_ref[:half, cols] = jnp.dot(even.astype(BF), bd_ref[:, :FOUR_W],
                                         preferred_element_type=F32).astype(BF)
            g_ref[half:, cols] = jnp.dot((fwd - rev).astype(BF), bd_ref[:, FOUR_W:],
                                         preferred_element_type=F32).astype(BF)
            gm_ref[:, cols] = jnp.dot(f_ref[bb, half:half + 16, :], bd_ref[:, :FOUR_W],
                                      preferred_element_type=F32)

    ca = ca_ref[0]
    sa = sa_ref[0]
    cb = cb_ref[...]
    sb = sb_ref[...]
    w_cos = (cb * ca - sb * sa).astype(BF)
    w_sin = (sb * ca + cb * sa).astype(BF)
    y = (jnp.dot(w_cos, g_ref[:half, :], preferred_element_type=F32)
         - jnp.dot(w_sin, g_ref[half:, :], preferred_element_type=F32))
    parity = lax.broadcasted_iota(I32, (y.shape[0], 1), 0) & 1
    y = y + jnp.where(parity == 0, scale, -scale) * gm_ref[0:1, :]
    for bb in range(nb):
        o_ref[bb] = y[:, bb * FOUR_W:(bb + 1) * FOUR_W].astype(BF)


def _fourier(f, bd, tables, n_lat):
    b = f.shape[0]
    half = n_lat // 2
    cos_row, sin_row, cos_tile, sin_tile = tables
    tr = cos_row.shape[0]
    assert tr % 2 == 0 and cos_row.shape[1] == half
    row_tab = pl.BlockSpec((tr, half), lambda i: (0, 0))
    tile_tab = pl.BlockSpec((1, 1, half), lambda i: (i, 0, 0))
    return pl.pallas_call(
        functools.partial(_fourier_kernel, scale=n_lat ** -0.5),
        grid=(n_lat // tr,),
        in_specs=[
            pl.BlockSpec((b, n_lat, FOUR_W), lambda i: (0, 0, 0)),
            pl.BlockSpec((FOUR_W, 2 * FOUR_W), lambda i: (0, 0)),
            row_tab, row_tab, tile_tab, tile_tab,
        ],
        out_specs=pl.BlockSpec((b, tr, FOUR_W), lambda i: (0, i, 0)),
        out_shape=jax.ShapeDtypeStruct((b, n_lat, FOUR_W), BF),
        scratch_shapes=[pltpu.VMEM((n_lat, b * FOUR_W), BF), pltpu.VMEM((16, b * FOUR_W), F32)],
        compiler_params=_params(("arbitrary",), 48),
        name="fourier",
    )(f, bd, cos_row, sin_row, cos_tile, sin_tile)


SUB = D // LANES


def _store_token_tiles(ref, val):
    rows = val.shape[0]
    for j in range(SUB):
        ref[0, pl.ds(j, rows, stride=SUB), :] = val[:, j * LANES:(j + 1) * LANES]


def _load_token_tiles(ref, rows):
    return jnp.concatenate([ref[0, pl.ds(j, rows, stride=SUB), :] for j in range(SUB)], axis=1)


def _residual_router(x, y, gm, nf, shf, scf, wr_t, x1_ref, hf_ref, aff_ref):
    x1 = x + gm * y
    x1_ref[0] = x1
    hf = _modulate(x1, nf, shf, scf)
    _store_token_tiles(hf_ref, hf)
    logits = _dot3(wr_t, hf, NT)
    m = jnp.max(logits, axis=0, keepdims=True)
    e = jnp.exp(logits - m)
    aff_ref[0] = e / jnp.sum(e, axis=0, keepdims=True)


def _merge0_kernel(att_ref, four_ref, w_ref, x_ref, gm_ref, nf_ref, shf_ref, scf_ref, wr_ref,
                   x1_ref, hf_ref, aff_ref):
    y = jnp.dot(att_ref[0], w_ref[:V_W, :], preferred_element_type=F32)
    y = y + jnp.dot(four_ref[0], w_ref[V_W:, :], preferred_element_type=F32)
    _residual_router(x_ref[0], y, gm_ref[0], nf_ref[...], shf_ref[0], scf_ref[0], wr_ref[...],
                     x1_ref, hf_ref, aff_ref)


def _router_specs(tm):
    vec = pl.BlockSpec((1, 1, D), lambda bb, i: (bb, 0, 0))
    row = pl.BlockSpec((1, tm, D), lambda bb, i: (bb, i, 0))
    in_specs = [vec, pl.BlockSpec((1, D), lambda bb, i: (0, 0)), vec, vec,
                pl.BlockSpec((N_EXP, D), lambda bb, i: (0, 0))]
    tiles = _tile_spec(tm)
    out_specs = [row, tiles, pl.BlockSpec((1, N_EXP, tm), lambda bb, i: (bb, 0, i))]
    return row, in_specs, out_specs


def _tile_spec(tm):
    return pl.BlockSpec((1, tm * SUB, LANES), lambda bb, i: (bb, i, 0))


def _router_out_shape(b, n):
    return [jax.ShapeDtypeStruct((b, n, D), F32), jax.ShapeDtypeStruct((b, n * SUB, LANES), F32),
            jax.ShapeDtypeStruct((b, N_EXP, n), F32)]


def _merge0(att, four, w_bf, x, gm, nf, shf, scf, wr_t):
    b, n, _ = x.shape
    tm = MIX_ROWS
    row, r_in, r_out = _router_specs(tm)
    return pl.pallas_call(
        _merge0_kernel,
        grid=(b, n // tm),
        in_specs=[
            pl.BlockSpec((1, tm, V_W), lambda bb, i: (bb, i, 0)),
            pl.BlockSpec((1, tm, FOUR_W), lambda bb, i: (bb, i, 0)),
            pl.BlockSpec((D, D), lambda bb, i: (0, 0)),
            row,
        ] + r_in,
        out_specs=r_out,
        out_shape=_router_out_shape(b, n),
        compiler_params=_params(("parallel", "arbitrary"), 48),
        name="merge0",
    )(att, four, w_bf, x, gm, nf, shf, scf, wr_t)


HALO = 8


def _mixer1_kernel(x_ref, xp_ref, xn_ref, moe_ref, moep_ref, moen_ref, nw_ref, sh_ref, sc_ref, win_ref,
                   cw_ref, wout_ref, gm_ref, nf_ref, shf_ref, scf_ref, wr_ref, x1_ref, hf_ref, aff_ref):
    i = pl.program_id(1)
    last = pl.num_programs(1) - 1
    tm = x_ref.shape[1]
    x = x_ref[0] + _load_token_tiles(moe_ref, tm)
    x_prev = xp_ref[0] + _load_token_tiles(moep_ref, HALO)
    x_next = xn_ref[0] + _load_token_tiles(moen_ref, HALO)
    x_all = jnp.concatenate([x_prev, x, x_next], axis=0)
    h_all = _modulate(x_all, nw_ref[...], sh_ref[0], sc_ref[0]).astype(BF)
    cg = jnp.dot(h_all, win_ref[:, D:2 * D], preferred_element_type=F32)
    u = jnp.dot(h_all, win_ref[:, 2 * D:], preferred_element_type=F32)
    z_raw = cg * u
    core = slice(HALO, HALO + tm)
    z_all = jnp.concatenate([z_raw[:HALO] * jnp.where(i > 0, 1.0, 0.0), z_raw[core],
                             z_raw[HALO + tm:] * jnp.where(i < last, 1.0, 0.0)], axis=0)
    z_up = pltpu.roll(z_all, 1, axis=0)[core]
    z_dn = pltpu.roll(z_all, tm + 2 * HALO - 1, axis=0)[core]
    cw = cw_ref[...]
    conv = cw[0:1] * z_up + cw[1:2] * z_all[core] + cw[2:3] * z_dn
    bg = jnp.dot(h_all[core], win_ref[:, :D], preferred_element_type=F32)
    y = jnp.dot((bg * conv).astype(BF), wout_ref[...], preferred_element_type=F32)
    _residual_router(x, y, gm_ref[0], nf_ref[...], shf_ref[0], scf_ref[0], wr_ref[...],
                     x1_ref, hf_ref, aff_ref)


def _mixer1(x, moe_tiles, nw, sh, sc, w_in, conv_w8, w_out, gm, nf, shf, scf, wr_t):
    b, n, _ = x.shape
    tm = MIX_ROWS
    per = tm // HALO
    n_halo = n // HALO
    prev_blk = lambda bb, i: (bb, jnp.maximum(i * per - 1, 0), 0)
    next_blk = lambda bb, i: (bb, jnp.minimum((i + 1) * per, n_halo - 1), 0)
    row, r_in, r_out = _router_specs(tm)
    vec = pl.BlockSpec((1, 1, D), lambda bb, i: (bb, 0, 0))
    full = lambda shape: pl.BlockSpec(shape, lambda bb, i: (0, 0))
    return pl.pallas_call(
        _mixer1_kernel,
        grid=(b, n // tm),
        in_specs=[
            row, pl.BlockSpec((1, HALO, D), prev_blk), pl.BlockSpec((1, HALO, D), next_blk),
            _tile_spec(tm), pl.BlockSpec((1, HALO * SUB, LANES), prev_blk),
            pl.BlockSpec((1, HALO * SUB, LANES), next_blk),
            full((1, D)), vec, vec, full((D, 3 * D)), full((8, D)), full((D, D)),
        ] + r_in,
        out_specs=r_out,
        out_shape=_router_out_shape(b, n),
        compiler_params=_params(("parallel", "arbitrary"), 56),
        name="mixer1",
    )(x, x, x, moe_tiles, moe_tiles, moe_tiles, nw, sh, sc, w_in, conv_w8, w_out, gm, nf, shf, scf, wr_t)


def _select_kernel(a_ref, idx_ref, gate_ref, cl_s, off_s, inc_s, hi_s, mid_s, lo_s, *, cap, chunks):
    a = a_ref[...]
    rows = a.shape[0]
    groups = rows // chunks
    per_batch = N_EXP * chunks
    assert chunks & (chunks - 1) == 0
    shift = chunks.bit_length() - 1

    member = jnp.where((lax.broadcasted_iota(I32, (groups, rows), 1) >> shift)
                       == lax.broadcasted_iota(I32, (groups, rows), 0), 1.0, 0.0).astype(BF)
    ri =lax.broadcasted_iota(I32, (per_batch, per_batch), 0)
    ci = lax.broadcasted_iota(I32, (per_batch, per_batch), 1)
    same_f = jnp.where((ri >> shift) == (ci >> shift), 1.0, 0.0)
    same = same_f.astype(BF)
    lower = (same_f * jnp.where(ci < ri, 1.0, 0.0)).astype(BF)
    li = lax.broadcasted_iota(I32, (LANES, LANES), 0)
    lj = lax.broadcasted_iota(I32, (LANES, LANES), 1)
    incl = jnp.where(li <= lj, 1.0, 0.0).astype(BF)

    def bcast(col):
        return jnp.broadcast_to(col, (per_batch, LANES))

    def prefix(maskf):
        cl = jnp.dot(maskf.astype(BF), incl, preferred_element_type=F32)
        tot = bcast(cl[:, LANES - 1:LANES])
        off = jnp.dot(lower, tot.astype(BF), preferred_element_type=F32)
        return cl, off, tot

    min_normal = 0x00800000

    a3 = a.reshape(groups, chunks, LANES)

    def at_least(bits):
        thr_g = lax.bitcast_convert_type(bits, F32)[:, None, :]
        return jnp.where(a3 >= thr_g, 1.0, 0.0).reshape(rows, LANES)

    def search(step, t):
        cand = t | jnp.left_shift(jnp.int32(1), 30 - step)
        part = jnp.dot(member, at_least(cand).astype(BF), preferred_element_type=F32)
        cnt = jnp.sum(part, axis=-1, keepdims=True)
        return jnp.where(cnt >= cap, jnp.where(cand >= min_normal, cand, t), t)

    thr = lax.fori_loop(0, 31, search, jnp.zeros((groups, LANES), I32))
    gtf = at_least(jnp.maximum(thr + 1, min_normal))
    eqf = at_least(thr) - gtf
    for bb in range(rows // per_batch):
        sl = slice(bb * per_batch, (bb + 1) * per_batch)
        gt_b, eq_b = gtf[sl], eqf[sl]
        n_gt = jnp.dot(same, bcast(jnp.sum(gt_b, axis=-1, keepdims=True)).astype(BF),
                       preferred_element_type=F32)
        cl_eq, off_eq, _ = prefix(eq_b)
        sel = gt_b + eq_b * jnp.where(cl_eq + off_eq <= cap - n_gt, 1.0, 0.0)
        cl, off, tot = prefix(sel)
        cl_s[sl, :] = cl.astype(BF)
        off_s[sl, :] = off
        inc_s[sl, :] = off + tot
        a_b = a[sl]
        a_hi = a_b.astype(BF)
        r1 = a_b - a_hi.astype(F32)
        a_mid = r1.astype(BF)
        hi_s[sl, :] = a_hi
        mid_s[sl, :] = a_mid
        lo_s[sl, :] = (r1 - a_mid.astype(F32)).astype(BF)

    slot = lax.broadcasted_iota(I32, (chunks, cap), 1).astype(F32)
    chunk_id = lax.broadcasted_iota(I32, (chunks, cap), 0).astype(F32)
    lane_id = lax.broadcasted_iota(I32, (LANES, cap), 0).astype(F32)
    tn = (((0,), (0,)), ((), ()))
    reps = cap // LANES

    def widen(x):
        return jnp.concatenate([x] * reps, axis=1)

    def per_group(g, carry):
        win = pl.ds(pl.multiple_of(g * chunks, chunks), chunks)
        inc_g = widen(inc_s[win, :])
        off_g = widen(off_s[win, :])
        chunk_of = jnp.sum(jnp.where(inc_g <= slot, 1.0, 0.0), axis=0, keepdims=True)
        pick_f = jnp.where(chunk_id == chunk_of, 1.0, 0.0)
        before = jnp.sum(pick_f * off_g, axis=0, keepdims=True)
        pick = pick_f.astype(BF)
        counts = lax.dot_general(cl_s[win, :], pick, tn, preferred_element_type=F32)
        lane_of = jnp.sum(jnp.where(counts <= slot[0:1] - before, 1.0, 0.0), axis=0, keepdims=True)
        aff = (lax.dot_general(hi_s[win, :], pick, tn, preferred_element_type=F32)
               + lax.dot_general(mid_s[win, :], pick, tn, preferred_element_type=F32)
               + lax.dot_general(lo_s[win, :], pick, tn, preferred_element_type=F32))
        gate = jnp.sum(jnp.where(lane_id == lane_of, aff, 0.0), axis=0, keepdims=True)
        idx_ref[pl.ds(g, 1), :] = ((chunk_of * float(LANES) + lane_of) * float(SUB)).astype(I32)
        gate_ref[pl.ds(g, 1), :] = gate
        return carry

    lax.fori_loop(0, groups, per_group, 0, unroll=2)


def _select(aff_rows, cap, chunks):
    rows = aff_rows.shape[0]
    groups = rows // chunks
    whole = lambda shape: pl.BlockSpec(shape, lambda i: (0, 0))
    return pl.pallas_call(
        functools.partial(_select_kernel, cap=cap, chunks=chunks),
        grid=(1,),
        in_specs=[whole((rows, LANES))],
        out_specs=[whole((groups, cap)), whole((groups, cap))],
        out_shape=[jax.ShapeDtypeStruct((groups, cap), I32), jax.ShapeDtypeStruct((groups, cap), F32)],
        scratch_shapes=[pltpu.VMEM((rows, LANES), BF), pltpu.VMEM((rows, LANES), F32),
                        pltpu.VMEM((rows, LANES), F32), pltpu.VMEM((rows, LANES), BF),
                        pltpu.VMEM((rows, LANES), BF), pltpu.VMEM((rows, LANES), BF)],
        compiler_params=_params(("arbitrary",), 48),
        name="select",
    )(aff_rows)


def _gather_kernel(idx_ref, h_ref, o_ref):
    slots = idx_ref.shape[2]

    def body(c, carry):
        t = pl.multiple_of(idx_ref[0, 0, c], SUB)
        o_ref[0, 0, pl.ds(pl.multiple_of(c * SUB, SUB), SUB), :] = h_ref[0, pl.ds(t, SUB), :]
        return carry

    lax.fori_loop(0, slots, body, 0, unroll=8)


def _gather(idx, hf_tiles, cap):
    b, rows, _ = hf_tiles.shape
    steps = N_EXP // EXPERTS_PER_STEP
    slots = EXPERTS_PER_STEP * cap
    xs = pl.pallas_call(
        _gather_kernel,
        grid=(b, steps),
        in_specs=[
            pl.BlockSpec((1, 1, slots), lambda bb, s: (bb * steps + s, 0, 0), memory_space=pltpu.SMEM),
            pl.BlockSpec((1, rows, LANES), lambda bb, s: (bb, 0, 0)),
        ],
        out_specs=pl.BlockSpec((1, 1, slots * SUB, LANES), lambda bb, s: (bb, s, 0, 0)),
        out_shape=jax.ShapeDtypeStruct((b, steps, slots * SUB, LANES), F32),
        compiler_params=_params(("parallel", "arbitrary"), 56),
        name="gather",
    )(idx, hf_tiles)
    return xs.reshape(b, N_EXP, cap * SUB, LANES)


def _ffn_kernel(x_ref, gate_ref, gf_ref, wg_ref, wu_ref, wd_ref, o_ref, wg_s, wu_s, wd_s):
    @pl.when(pl.program_id(1) == 0)
    def _():
        wg_s[...] = wg_ref[...].astype(BF)
        wu_s[...] = wu_ref[...].astype(BF)
        wd_s[...] = wd_ref[...].astype(BF)

    cap = x_ref.shape[2] // SUB
    x = _load_token_tiles(x_ref.at[0], cap).astype(BF)
    dexp = wg_s.shape[1]
    step = 256
    y = None
    for c0 in range(0, dexp, step):
        g = jnp.dot(x, wg_s[:, c0:c0 + step], preferred_element_type=F32)
        u = jnp.dot(x, wu_s[:, c0:c0 + step], preferred_element_type=F32)
        act = (g * (1.0 / (1.0 + jnp.exp(-g))) * u).astype(BF)
        part = jnp.dot(act, wd_s[c0:c0 + step, :], preferred_element_type=F32)
        y = part if y is None else y + part
    gate_col = jnp.broadcast_to(gate_ref[0], (LANES, cap)).T
    y = y * jnp.concatenate([gate_col] * SUB, axis=1) * gf_ref[0]
    _store_token_tiles(o_ref.at[0], y)


def _ffn(xs, gate, gf, w_gate, w_up, w_down, layer):
    b, _, tile_rows, _ = xs.shape
    cap = tile_rows // SUB
    dexp = w_gate.shape[-1]
    gspec = pl.BlockSpec((1, 1, cap), lambda e, bb: (bb * N_EXP + e, 0, 0))
    gfspec = pl.BlockSpec((1, 1, D), lambda e, bb: (bb, 0, 0))
    yspec = pl.BlockSpec((1, 1, tile_rows, LANES), lambda e, bb: (bb, e, 0, 0))
    xspec = yspec

    def wspec(shape, switch_at):
        def index(e, bb):
            return (layer, jnp.minimum(e + jnp.where(bb >= switch_at, 1, 0), N_EXP - 1), 0, 0)
        return pl.BlockSpec((None, None) + shape, index)

    assert b >= 2
    stagger = [min(j, b - 1) for j in (1, 2, 3)]
    return pl.pallas_call(
        _ffn_kernel,
        grid=(N_EXP, b),
        in_specs=[xspec, gspec, gfspec, wspec((D, dexp), stagger[0]), wspec((D, dexp), stagger[1]),
                  wspec((dexp, D), stagger[2])],
        out_specs=yspec,
        out_shape=jax.ShapeDtypeStruct((b, N_EXP, cap * SUB, LANES), F32),
        scratch_shapes=[pltpu.VMEM((D, dexp), BF), pltpu.VMEM((D, dexp), BF), pltpu.VMEM((dexp, D), BF)],
        compiler_params=_params(("arbitrary", "arbitrary"), 56),
        name="expert_ffn",
    )(xs, gate, gf, w_gate, w_up, w_down)


def _scatter_kernel(idx_ref, y_ref, *rest):
    if len(rest) == 1:
        (o_ref,), x_ref = rest, None
        acc_ref = o_ref
    else:
        x_ref, o_ref, acc_ref = rest
    slots = idx_ref.shape[2]
    expert_steps = N_EXP // EXPERTS_PER_STEP
    step = pl.program_id(1)

    @pl.when(step == 0)
    def _():
        acc_ref[...] = jnp.zeros(acc_ref.shape, F32)

    group = 16

    def body(g, carry):
        c0 = g * group
        ts = [pl.multiple_of(idx_ref[0, 0, c0 + j], SUB) for j in range(group)]
        new = []
        for j in range(group):
            y = y_ref[0, 0, pl.ds(pl.multiple_of((c0 + j) * SUB, SUB), SUB), :]
            new.append(acc_ref[0, pl.ds(ts[j], SUB), :] + y)
        for j in range(group):
            acc_ref[0, pl.ds(ts[j], SUB), :] = new[j]
        return carry

    @pl.when(step < expert_steps)
    def _():
        lax.fori_loop(0, slots // group, body, 0)

    if x_ref is not None:
        @pl.when(step >= expert_steps)
        def _():
            rows = o_ref.shape[1]
            base = (step - expert_steps) * (rows * SUB)
            o_ref[0] = x_ref[0] + jnp.concatenate(
                [acc_ref[0, pl.ds(base + j, rows, stride=SUB), :] for j in range(SUB)], axis=1)


def _scatter(idx, y_tiles, n, x=None):
    b = y_tiles.shape[0]
    slots = idx.shape[2]
    assert (slots // EXPERTS_PER_STEP) % 16 == 0
    rows = n * SUB
    tm = min(OUT_ROWS, n)
    steps = N_EXP // EXPERTS_PER_STEP
    extra = n // tm if x is not None else 0
    expert_step = lambda s: jnp.minimum(s, steps - 1)
    in_specs = [
        pl.BlockSpec((1, 1, slots), lambda bb, s: (bb * steps + expert_step(s), 0, 0),
                     memory_space=pltpu.SMEM),
        pl.BlockSpec((1, 1, slots * SUB, LANES), lambda bb, s: (bb, expert_step(s), 0, 0)),
    ]
    args = [idx, y_tiles.reshape(b, steps, slots * SUB, LANES)]
    if x is not None:
        piece = pl.BlockSpec((1, tm, D), lambda bb, s: (bb, jnp.maximum(s - steps, 0), 0))
        in_specs.append(piece)
        args.append(x)
        out_spec, out_shape = piece, jax.ShapeDtypeStruct((b, n, D), F32)
        scratch = [pltpu.VMEM((1, rows, LANES), F32)]
    else:
        out_spec = pl.BlockSpec((1, rows, LANES), lambda bb, s: (bb, 0, 0))
        out_shape, scratch = jax.ShapeDtypeStruct((b, rows, LANES), F32), []
    return pl.pallas_call(
        _scatter_kernel,
        grid=(b, steps + extra),
        in_specs=in_specs,
        out_specs=out_spec,
        out_shape=out_shape,
        scratch_shapes=scratch,
        compiler_params=_params(("parallel", "arbitrary"), 56),
        name="scatter_add",
    )(*args)


def _moe(hf_tiles, aff_t, gf, w_gate, w_up, w_down, layer, x=None):
    b, _, n = aff_t.shape
    cap = CAP_FACTOR * n // N_EXP
    chunks = n // LANES
    idx, gate = _select(aff_t.reshape(b * N_EXP * chunks, LANES), cap, chunks)
    idx = idx.reshape(b * N_EXP // EXPERTS_PER_STEP, 1, EXPERTS_PER_STEP * cap)
    gate = gate.reshape(b * N_EXP, 1, cap)
    xs = _gather(idx, hf_tiles, cap)
    y_tiles = _ffn(xs, gate, gf, w_gate, w_up, w_down, layer)
    return _scatter(idx, y_tiles, n, x)


def _rope_tables(n_lat, n_ctx):
    rows = n_lat // GRID_W
    r = jnp.repeat(jnp.arange(rows, dtype=F32), GRID_W)
    col = jnp.tile(jnp.arange(GRID_W, dtype=F32), rows)
    n_freq = HEAD_DIM // 4
    inv = ROPE_BASE ** (-jnp.arange(n_freq, dtype=F32) / n_freq)
    ar = r[:, None] * inv
    ac = col[:, None] * inv
    ang = jnp.concatenate([ar, ar, ac, ac], axis=-1)
    sign = jnp.where((jnp.arange(HEAD_DIM) % 32) < 16, -1.0, 1.0).astype(F32)
    cos = jnp.concatenate([jnp.cos(ang), jnp.ones((n_ctx, HEAD_DIM), F32)], axis=0)
    sin_s = jnp.concatenate([jnp.sin(ang) * sign, jnp.zeros((n_ctx, HEAD_DIM), F32)], axis=0)
    return jnp.tile(cos, (1, 2)).T, jnp.tile(sin_s, (1, 2)).T


def _dft_tables(n_lat):
    c = jnp.arange(FOUR_G, dtype=I32)
    ang_c = (2.0 * math.pi / FOUR_G) * ((c[:, None] * c[None, :]) % FOUR_G).astype(F32)
    eye = jnp.eye(FOUR_W // FOUR_G, dtype=F32)
    bd = jnp.concatenate([jnp.kron(eye, jnp.cos(ang_c)), jnp.kron(eye, jnp.sin(ang_c))], axis=1)
    bd = (bd * FOUR_G ** -0.5).astype(BF)
    tr = DFT_ROWS
    n = jnp.arange(n_lat, dtype=I32)[None, :]
    r = jnp.arange(tr, dtype=I32)[:, None]
    i = jnp.arange(n_lat // tr, dtype=I32)[:, None]
    tile_ang = (2.0 * math.pi / n_lat) * ((i * tr * n) % n_lat).astype(F32)
    side = int(round(math.sqrt(n_lat)))
    assert side * side == n_lat
    s = jnp.arange(side, dtype=I32)[None, :]
    hi_ang = (2.0 * math.pi / side) * ((r * s) % side).astype(F32)
    lo_ang = (2.0 * math.pi / n_lat) * ((r * s) % n_lat).astype(F32)
    ch, sh = jnp.cos(hi_ang)[:, :, None], jnp.sin(hi_ang)[:, :, None]
    cl, sl = jnp.cos(lo_ang)[:, None, :], jnp.sin(lo_ang)[:, None, :]
    scale = n_lat ** -0.5
    half = n_lat // 2
    ch, sh = ch[:, :side // 2], sh[:, :side // 2]
    cos_row = ((ch * cl - sh * sl) * scale).reshape(tr, half)
    sin_row = ((sh * cl + ch * sl) * scale).reshape(tr, half)
    tile_ang = tile_ang[:, :half]
    tables = (cos_row, sin_row, jnp.cos(tile_ang)[:, None, :], jnp.sin(tile_ang)[:, None, :])
    return bd, tables


def kernel(x, c, ctx, c_ctx, ada_w, ada_b, norm_mix, norm_ffn, attn_w_in, attn_q_norm, attn_k_norm,
           lam_q1, lam_k1, lam_q2, lam_k2, attn_subln, attn_w_out, conv_w_in, conv_w, conv_w_out,
           router_w, moe_w_gate, moe_w_up, moe_w_down):
    b, n, _ = x.shape
    n_ctx = ctx.shape[1]
    assert x.shape[2] == D and n_ctx % PROJ0_ROWS == 0
    assert all(n % t == 0 for t in (PROJ0_ROWS, ATTN_QUERIES, DFT_ROWS, MIX_ROWS, min(OUT_ROWS, n)))

    cond8 = jnp.concatenate([c, c_ctx[None, :], jnp.zeros((8 - b - 1, D), F32)], axis=0)
    ada = _ada(cond8, ada_w, ada_b)

    def mods(layer):
        m = ada[layer].reshape(8, 6, D)
        return [m[:, j] for j in range(6)]

    vec = lambda t: t[:b].reshape(b, 1, D)

    sh_m, sc_m, g_m, sh_f, sc_f, g_f = mods(0)
    both = lambda t: jnp.stack([t[:b], jnp.broadcast_to(t[b], (b, D))], axis=1).reshape(b, 2, 1, D)
    cos_t, sin_t = _rope_tables(n, n_ctx)
    bd, dft_tabs = _dft_tables(n)
    col2 = lambda t: jnp.tile(t.reshape(HEAD_DIM, 1), (2, 1))
    w_in = attn_w_in[0]
    q, k, v, f = _proj0(x, ctx, norm_mix[0].reshape(1, D), both(sh_m), both(sc_m),
                        w_in[:, 2 * QK_W + V_W:].astype(BF), w_in[:, :2 * QK_W + V_W].astype(BF).T,
                        cos_t, sin_t, col2(attn_q_norm[0]), col2(attn_k_norm[0]))
    lam_init = 0.8 - 0.6 * math.exp(-0.3 * 0)
    lamv = jnp.zeros((8, LANES), F32).at[:4, :HEAD_DIM].set(
        jnp.stack([lam_q1[0], lam_k1[0], lam_q2[0], lam_k2[0]]))
    score_bound = (1.01 * HEAD_DIM * Q_SCALE * jnp.max(jnp.abs(attn_q_norm[0]))
                   * jnp.max(jnp.abs(attn_k_norm[0])) + 0.1).reshape(1)
    att = _attention(score_bound, q, k, v, lamv, attn_subln[0].reshape(V_DIM, 1), n, lam_init)
    four = _fourier(f, bd, dft_tabs, n)
    x1, hf, aff_t = _merge0(att, four, attn_w_out[0].astype(BF), x, vec(g_m),
                            norm_ffn[0].reshape(1, D), vec(sh_f), vec(sc_f), router_w[0].T)
    moe0 = _moe(hf, aff_t, vec(g_f), moe_w_gate, moe_w_up, moe_w_down, 0)

    sh_m, sc_m, g_m, sh_f, sc_f, g_f = mods(1)
    conv_w8 = jnp.zeros((8, D), F32).at[:3].set(conv_w[0])
    x3, hf, aff_t = _mixer1(x1, moe0, norm_mix[1].reshape(1, D), vec(sh_m), vec(sc_m),
                            conv_w_in[0].astype(BF), conv_w8, conv_w_out[0].astype(BF), vec(g_m),
                            norm_ffn[1].reshape(1, D), vec(sh_f), vec(sc_f), router_w[1].T)
    return _moe(hf, aff_t, vec(g_f), moe_w_gate, moe_w_up, moe_w_down, 1, x3)
```

```python
import functools
import math

import jax
import jax.numpy as jnp
from jax import lax
from jax.experimental import pallas as pl
from jax.experimental.pallas import tpu as pltpu

BF = jnp.bfloat16
F32 = jnp.float32
I32 = jnp.int32

D = 1024
GRID_W = 64
N_HEADS = 6
HEAD_DIM = 64
V_DIM = 2 * HEAD_DIM
QK_W = N_HEADS * 2 * HEAD_DIM
V_W = N_HEADS * V_DIM
FOUR_W = 256
FOUR_G = 64
N_EXP = 16
EXPERTS_PER_STEP = 4
CAP_FACTOR = 2
ROPE_BASE = 10000.0
EPS = 1e-6
Q_SCALE = HEAD_DIM ** -0.5 * math.log2(math.e)
ATTN_SHIFT_LIMIT = 40.0
LANES = 128
MIB = 1024 * 1024

ADA_COLS = 1536
PROJ0_ROWS = 256
ATTN_QUERIES = 1024
ATTN_KEYS = 256
DFT_ROWS = 256
MIX_ROWS = 512
OUT_ROWS = 1024

NT = (((1,), (1,)), ((), ()))


def _params(sem, vmem_mib):
    return pltpu.CompilerParams(dimension_semantics=sem, vmem_limit_bytes=vmem_mib * MIB)


def _split2(x):
    hi = x.astype(BF)
    lo = (x - hi.astype(F32)).astype(BF)
    return hi, lo


def _dot3(a, b, dims=(((1,), (0,)), ((), ()))):
    m = a.shape[0]
    ah, al = _split2(a)
    bh, bl = _split2(b)
    dg = functools.partial(lax.dot_general, dimension_numbers=dims, preferred_element_type=F32)
    both = dg(jnp.concatenate([ah, al], axis=0), bh)
    return both[:m] + both[m:] + dg(ah, bl)


def _modulate(x, nw, shift, scale):
    ms = jnp.mean(x * x, axis=-1, keepdims=True)
    return (x * lax.rsqrt(ms + EPS) * nw) * (1.0 + scale) + shift


def _ada_kernel(c_ref, w_ref, b_ref, o_ref):
    cv = c_ref[...]
    s = cv * (1.0 / (1.0 + jnp.exp(-cv)))
    o_ref[0] = _dot3(s, w_ref[0]) + b_ref[0]


def _ada(cond8, ada_w, ada_b):
    depth = ada_w.shape[0]
    tn = ADA_COLS
    return pl.pallas_call(
        _ada_kernel,
        grid=(depth, 6 * D // tn),
        in_specs=[
            pl.BlockSpec((8, D), lambda l, j: (0, 0)),
            pl.BlockSpec((1, D, tn), lambda l, j: (l, 0, j)),
            pl.BlockSpec((1, 1, tn), lambda l, j: (l, 0, j)),
        ],
        out_specs=pl.BlockSpec((1, 8, tn), lambda l, j: (l, 0, j)),
        out_shape=jax.ShapeDtypeStruct((depth, 8, 6 * D), F32),
        compiler_params=_params(("arbitrary", "arbitrary"), 40),
        name="ada",
    )(cond8, ada_w, ada_b.reshape(depth, 1, 6 * D))


def _norm_rope_t(xt, wn_col, cos_t, sin_t, out_scale):
    q4 = HEAD_DIM // 4
    halves = []
    for s in range(2):
        x = xt[s * HEAD_DIM:(s + 1) * HEAD_DIM]
        ms = jnp.sum(x * x, axis=0, keepdims=True) * (1.0 / HEAD_DIM)
        y = x * lax.rsqrt(ms + EPS) * wn_col[s * HEAD_DIM:(s + 1) * HEAD_DIM]
        rot = jnp.concatenate([y[q4:2 * q4], y[:q4], y[3 * q4:], y[2 * q4:3 * q4]], axis=0)
        halves.append(y * cos_t[s * HEAD_DIM:(s + 1) * HEAD_DIM] + rot * sin_t[s * HEAD_DIM:(s + 1) * HEAD_DIM])
    out = jnp.concatenate(halves, axis=0)
    return out if out_scale == 1.0 else out * out_scale


def _proj0_kernel(x_ref, ctx_ref, nw_ref, sh_ref, sc_ref, w_ref, cost_ref, sint_ref,
                  qn_ref, kn_ref, q_ref, k_ref, v_ref, f_ref, wt_ref, wf_ref, *, lat_tiles):
    @pl.when(pl.program_id(1) == 0)
    def _():
        for r0 in range(0, 2 * QK_W + V_W, QK_W):
            wt_ref[r0:r0 + QK_W, :] = w_ref[:, r0:r0 + QK_W].T.astype(BF)
        wf_ref[...] = w_ref[:, 2 * QK_W + V_W:].astype(BF)

    x = jnp.where(pl.program_id(1) < lat_tiles, x_ref[0], ctx_ref[0])
    h = _modulate(x, nw_ref[...], sh_ref[0, 0], sc_ref[0, 0]).astype(BF)
    def project_t(r0, width):
        return lax.dot_general(wt_ref[r0:r0 + width, :], h, NT, preferred_element_type=F32)

    cos_t = cost_ref[...]
    sin_t = sint_ref[...]
    for out_ref, wn_ref, r0, scale in ((q_ref, qn_ref, 0, Q_SCALE), (k_ref, kn_ref, QK_W, 1.0)):
        p_t = project_t(r0, QK_W)
        wn_col = wn_ref[...]
        for hh in range(N_HEADS):
            c0 = hh * V_DIM
            out_ref[0, c0:c0 + V_DIM, :] = _norm_rope_t(
                p_t[c0:c0 + V_DIM], wn_col, cos_t, sin_t, scale).astype(BF)
    v_ref[0] = project_t(2 * QK_W, V_W).astype(BF)
    f_ref[0] = jnp.dot(h, wf_ref[...], preferred_element_type=F32).astype(BF)


def _proj0(x, ctx, nw, sh2, sc2, w_in, cos_t, sin_t, qn_col, kn_col):
    b, n_lat, _ = x.shape
    n_ctx = ctx.shape[1]
    s_tot = n_lat + n_ctx
    tm = PROJ0_ROWS
    lat_tiles = n_lat // tm
    mod_spec = pl.BlockSpec((1, 1, 1, D), lambda bb, i: (bb, i // lat_tiles, 0, 0))
    full = lambda shape: pl.BlockSpec(shape, lambda bb, i: tuple(0 for _ in shape))
    feat = lambda w: pl.BlockSpec((1, w, tm), lambda bb, i: (bb, 0, i))
    feat_shape = lambda w: jax.ShapeDtypeStruct((b, w, s_tot), BF)
    return pl.pallas_call(
        functools.partial(_proj0_kernel, lat_tiles=lat_tiles),
        grid=(b, s_tot // tm),
        in_specs=[
            pl.BlockSpec((1, tm, D), lambda bb, i: (bb, jnp.minimum(i, lat_tiles - 1), 0)),
            pl.BlockSpec((1, tm, D), lambda bb, i: (bb, jnp.maximum(i - lat_tiles, 0), 0)),
            full((1, D)), mod_spec, mod_spec,
            pl.BlockSpec(w_in.shape, lambda bb, i: (0, 0), pipeline_mode=pl.Buffered(1)),
            pl.BlockSpec((LANES, tm), lambda bb, i: (0, i)),
            pl.BlockSpec((LANES, tm), lambda bb, i: (0, i)),
            full((LANES, 1)), full((LANES, 1)),
        ],
        out_specs=[feat(QK_W), feat(QK_W), feat(V_W),
                   pl.BlockSpec((1, tm, FOUR_W), lambda bb, i: (bb, i, 0))],
        out_shape=[feat_shape(QK_W), feat_shape(QK_W), feat_shape(V_W),
                   jax.ShapeDtypeStruct((b, s_tot, FOUR_W), BF)],
        scratch_shapes=[pltpu.VMEM((2 * QK_W + V_W, D), BF), pltpu.VMEM((D, FOUR_W), BF)],
        compiler_params=_params(("parallel", "arbitrary"), 48),
        name="proj0",
    )(x, ctx, nw, sh2, sc2, w_in, cos_t, sin_t, qn_col, kn_col)


def _attn_kernel(bound_ref, q_ref, kt_ref, vt_ref, lam_ref, sub_ref, o_ref, k_ref, m_ref, e_ref,
                 *, lam_init, kc):
    @pl.when(pl.program_id(2) == 0)
    def _():
        k_ref[...] = kt_ref[0].astype(F32).T.astype(BF)

    q = q_ref[0]
    feat = lax.broadcasted_iota(I32, q.shape, 0)
    zero = jnp.zeros_like(q)
    lv = lam_ref[...]
    t1 = jnp.sum(lv[0:1] * lv[1:2], axis=-1, keepdims=True)
    t2 = jnp.sum(lv[2:3] * lv[3:4], axis=-1, keepdims=True)
    lam = jnp.exp(t1) - jnp.exp(t2) + lam_init

    qs = (jnp.where(feat < HEAD_DIM, q, zero), jnp.where(feat < HEAD_DIM, zero, q))
    tq = q.shape[1]
    n_chunks = k_ref.shape[0] // kc

    def scores(h, c):
        return jnp.dot(k_ref[c * kc:(c + 1) * kc, :], qs[h], preferred_element_type=F32)

    def fold(x, op):
        return op(x.reshape(kc // 8, 8, tq), axis=0)

    bound = bound_ref[0]
    small = bound <= ATTN_SHIFT_LIMIT

    @pl.when(small)
    def _():
        m_ref[...] = jnp.zeros(m_ref.shape, F32) + bound

    @pl.when(jnp.logical_not(small))
    def _():
        for h in range(2):
            m = jnp.full((8, tq), -jnp.inf, F32)
            for c in range(n_chunks):
                m = jnp.maximum(m, fold(scores(h, c), jnp.max))
            m_ref[h] = jnp.broadcast_to(jnp.max(m, axis=0, keepdims=True), (8, tq))

    ms = [m_ref[h][0:1, :] for h in range(2)]
    ls = [jnp.zeros((8, tq), F32) for _ in range(2)]
    for c in range(n_chunks):
        for h in range(2):
            e = jnp.exp2(scores(h, c) - ms[h])
            ls[h] = ls[h] + fold(e, jnp.sum)
            e_ref[h, c] = e.astype(BF)
    l0, l1 = [jnp.sum(l, axis=0, keepdims=True) for l in ls]

    beta = (lam * l0 / l1).astype(BF)
    acc = jnp.zeros((V_DIM, tq), F32)
    for c in range(n_chunks):
        a = e_ref[0, c] - beta * e_ref[1, c]
        acc = acc + jnp.dot(vt_ref[0, :, c * kc:(c + 1) * kc], a, preferred_element_type=F32)
    o = acc * (1.0 / l0)
    ms = jnp.mean(o * o, axis=0, keepdims=True)
    o = o * lax.rsqrt(ms + EPS) * sub_ref[...] * (1.0 - lam_init)
    o_ref[0] = o.T.astype(BF)


def _attention(bound, q, k, v, lamv, subln, n_lat, lam_init):
    b, _, s_tot = k.shape
    tq = ATTN_QUERIES
    kc = ATTN_KEYS
    assert s_tot % kc == 0
    kv_spec = pl.BlockSpec((1, V_DIM, s_tot), lambda bb, hh, i: (bb, hh, 0))
    return pl.pallas_call(
        functools.partial(_attn_kernel, lam_init=lam_init, kc=kc),
        grid=(b, N_HEADS, n_lat // tq),
        in_specs=[
            pl.BlockSpec(memory_space=pltpu.SMEM),
            pl.BlockSpec((1, V_DIM, tq), lambda bb, hh, i: (bb, hh, i)),
            kv_spec, kv_spec,
            pl.BlockSpec((8, LANES), lambda bb, hh, i: (0, 0)),
            pl.BlockSpec((V_DIM, 1), lambda bb, hh, i: (0, 0)),
        ],
        out_specs=pl.BlockSpec((1, tq, V_DIM), lambda bb, hh, i: (bb, i, hh)),
        out_shape=jax.ShapeDtypeStruct((b, n_lat, V_W), BF),
        scratch_shapes=[pltpu.VMEM((s_tot, V_DIM), BF),
                        pltpu.VMEM((2, 8, tq), F32),
                        pltpu.VMEM((2, s_tot // kc, kc, tq), BF)],
        compiler_params=_params(("parallel", "parallel", "arbitrary"), 48),
        name="diff_attn",
    )(bound, q, k, v, lamv, subln)


def _fourier_kernel(f_ref, bd_ref, cb_ref, sb_ref, ca_ref, sa_ref, o_ref, g_ref, gm_ref, *, scale):
    nb, n_lat, _ = f_ref.shape
    half = n_lat // 2

    @pl.when(pl.program_id(0) == 0)
    def _():
        blk = 256
        nblk = half // blk
        flip = jnp.where(lax.broadcasted_iota(I32, (blk, blk), 0)
                         + lax.broadcasted_iota(I32, (blk, blk), 1) == blk - 1, 1.0, 0.0).astype(BF)
        rowid = lax.broadcasted_iota(I32, (half, FOUR_W), 0)
        for bb in range(nb):
            cols = slice(bb * FOUR_W, (bb + 1) * FOUR_W)
            fwd = f_ref[bb, :half, :].astype(F32)
            upside = jnp.concatenate(
                [jnp.dot(flip, f_ref[bb, half + (nblk - 1 - k) * blk:half + (nblk - k) * blk, :],
                         preferred_element_type=F32) for k in range(nblk)], axis=0)
            rev = jnp.where(rowid == 0, fwd, pltpu.roll(upside, 1, axis=0))
            even = jnp.where(rowid == 0, fwd, fwd + rev)
            g_ref[:half, cols] = jnp.dot(even.astype(BF), bd_ref[:, :FOUR_W],
                                         preferred_element_type=F32).astype(BF)
            g_ref[half:, cols] = jnp.dot((fwd - rev).astype(BF), bd_ref[:, FOUR_W:],
                                         preferred_element_type=F32).astype(BF)
            gm_ref[:, cols] = jnp.dot(f_ref[bb, half:half + 16, :], bd_ref[:, :FOUR_W],
                                      preferred_element_type=F32)

    ca = ca_ref[0]
    sa = sa_ref[0]
    cb = cb_ref[...]
    sb = sb_ref[...]
    w_cos = (cb * ca - sb * sa).astype(BF)
    w_sin = (sb * ca + cb * sa).astype(BF)
    y = (jnp.dot(w_cos, g_ref[:half, :], preferred_element_type=F32)
         - jnp.dot(w_sin, g_ref[half:, :], preferred_element_type=F32))
    parity = lax.broadcasted_iota(I32, (y.shape[0], 1), 0) & 1
    y = y + jnp.where(parity == 0, scale, -scale) * gm_ref[0:1, :]
    for bb in range(nb):
        o_ref[bb] = y[:, bb * FOUR_W:(bb + 1) * FOUR_W].astype(BF)


def _fourier(f, bd, tables, n_lat):
    b = f.shape[0]
    half = n_lat // 2
    cos_row, sin_row, cos_tile, sin_tile = tables
    tr = cos_row.shape[0]
    assert tr % 2 == 0 and cos_row.shape[1] == half
    row_tab = pl.BlockSpec((tr, half), lambda i: (0, 0))
    tile_tab = pl.BlockSpec((1, 1, half), lambda i: (i, 0, 0))
    return pl.pallas_call(
        functools.partial(_fourier_kernel, scale=n_lat ** -0.5),
        grid=(n_lat // tr,),
        in_specs=[
            pl.BlockSpec((b, n_lat, FOUR_W), lambda i: (0, 0, 0)),
            pl.BlockSpec((FOUR_W, 2 * FOUR_W), lambda i: (0, 0)),
            row_tab, row_tab, tile_tab, tile_tab,
        ],
        out_specs=pl.BlockSpec((b, tr, FOUR_W), lambda i: (0, i, 0)),
        out_shape=jax.ShapeDtypeStruct((b, n_lat, FOUR_W), BF),
        scratch_shapes=[pltpu.VMEM((n_lat, b * FOUR_W), BF), pltpu.VMEM((16, b * FOUR_W), F32)],
        compiler_params=_params(("arbitrary",), 48),
        name="fourier",
    )(f, bd, cos_row, sin_row, cos_tile, sin_tile)


SUB = D // LANES


def _store_token_tiles(ref, val):
    rows = val.shape[0]
    for j in range(SUB):
        ref[0, pl.ds(j, rows, stride=SUB), :] = val[:, j * LANES:(j + 1) * LANES]


def _load_token_tiles(ref, rows):
    return jnp.concatenate([ref[0, pl.ds(j, rows, stride=SUB), :] for j in range(SUB)], axis=1)


def _residual_router(x, y, gm, nf, shf, scf, wr_t, x1_ref, hf_ref, aff_ref):
    x1 = x + gm * y
    x1_ref[0] = x1
    hf = _modulate(x1, nf, shf, scf)
    _store_token_tiles(hf_ref, hf)
    logits = _dot3(wr_t, hf, NT)
    m = jnp.max(logits, axis=0, keepdims=True)
    e = jnp.exp(logits - m)
    aff_ref[0] = e / jnp.sum(e, axis=0, keepdims=True)


def _merge0_kernel(att_ref, four_ref, w32_ref, x_ref, gm_ref, nf_ref, shf_ref, scf_ref, wr_ref,
                   x1_ref, hf_ref, aff_ref, w_ref):
    @pl.when(pl.program_id(1) == 0)
    def _():
        w_ref[...] = w32_ref[...].astype(BF)

    y = jnp.dot(att_ref[0], w_ref[:V_W, :], preferred_element_type=F32)
    y = y + jnp.dot(four_ref[0], w_ref[V_W:, :], preferred_element_type=F32)
    _residual_router(x_ref[0], y, gm_ref[0], nf_ref[...], shf_ref[0], scf_ref[0], wr_ref[...],
                     x1_ref, hf_ref, aff_ref)


def _router_specs(tm):
    vec = pl.BlockSpec((1, 1, D), lambda bb, i: (bb, 0, 0))
    row = pl.BlockSpec((1, tm, D), lambda bb, i: (bb, i, 0))
    in_specs = [vec, pl.BlockSpec((1, D), lambda bb, i: (0, 0)), vec, vec,
                pl.BlockSpec((N_EXP, D), lambda bb, i: (0, 0))]
    tiles = _tile_spec(tm)
    out_specs = [row, tiles, pl.BlockSpec((1, N_EXP, tm), lambda bb, i: (bb, 0, i))]
    return row, in_specs, out_specs


def _tile_spec(tm):
    return pl.BlockSpec((1, tm * SUB, LANES), lambda bb, i: (bb, i, 0))


def _router_out_shape(b, n):
    return [jax.ShapeDtypeStruct((b, n, D), F32), jax.ShapeDtypeStruct((b, n * SUB, LANES), F32),
            jax.ShapeDtypeStruct((b, N_EXP, n), F32)]


def _merge0(att, four, w_out, x, gm, nf, shf, scf, wr_t):
    b, n, _ = x.shape
    tm = MIX_ROWS
    row, r_in, r_out = _router_specs(tm)
    return pl.pallas_call(
        _merge0_kernel,
        grid=(b, n // tm),
        in_specs=[
            pl.BlockSpec((1, tm, V_W), lambda bb, i: (bb, i, 0)),
            pl.BlockSpec((1, tm, FOUR_W), lambda bb, i: (bb, i, 0)),
            pl.BlockSpec((D, D), lambda bb, i: (0, 0), pipeline_mode=pl.Buffered(1)),
            row,
        ] + r_in,
        out_specs=r_out,
        out_shape=_router_out_shape(b, n),
        scratch_shapes=[pltpu.VMEM((D, D), BF)],
        compiler_params=_params(("parallel", "arbitrary"), 48),
        name="merge0",
    )(att, four, w_out, x, gm, nf, shf, scf, wr_t)


HALO = 8


def _mixer1_kernel(x_ref, xp_ref, xn_ref, moe_ref, moep_ref, moen_ref, nw_ref, sh_ref, sc_ref, win32_ref,
                   cw_ref, wout32_ref, gm_ref, nf_ref, shf_ref, scf_ref, wr_ref, x1_ref, hf_ref, aff_ref,
                   win_ref, wout_ref):
    i = pl.program_id(1)
    last = pl.num_programs(1) - 1

    @pl.when(i == 0)
    def _():
        win_ref[...] = win32_ref[...].astype(BF)
        wout_ref[...] = wout32_ref[...].astype(BF)

    tm = x_ref.shape[1]
    x = x_ref[0] + _load_token_tiles(moe_ref, tm)
    x_prev = xp_ref[0] + _load_token_tiles(moep_ref, HALO)
    x_next = xn_ref[0] + _load_token_tiles(moen_ref, HALO)
    x_all = jnp.concatenate([x_prev, x, x_next], axis=0)
    h_all = _modulate(x_all, nw_ref[...], sh_ref[0], sc_ref[0]).astype(BF)
    cg = jnp.dot(h_all, win_ref[:, D:2 * D], preferred_element_type=F32)
    u = jnp.dot(h_all, win_ref[:, 2 * D:], preferred_element_type=F32)
    z_raw = cg * u
    core = slice(HALO, HALO + tm)
    z_all = jnp.concatenate([z_raw[:HALO] * jnp.where(i > 0, 1.0, 0.0), z_raw[core],
                             z_raw[HALO + tm:] * jnp.where(i < last, 1.0, 0.0)], axis=0)
    z_up = pltpu.roll(z_all, 1, axis=0)[core]
    z_dn = pltpu.roll(z_all, tm + 2 * HALO - 1, axis=0)[core]
    cw = cw_ref[...]
    conv = cw[0:1] * z_up + cw[1:2] * z_all[core] + cw[2:3] * z_dn
    bg = jnp.dot(h_all[core], win_ref[:, :D], preferred_element_type=F32)
    y = jnp.dot((bg * conv).astype(BF), wout_ref[...], preferred_element_type=F32)
    _residual_router(x, y, gm_ref[0], nf_ref[...], shf_ref[0], scf_ref[0], wr_ref[...],
                     x1_ref, hf_ref, aff_ref)


def _mixer1(x, moe_tiles, nw, sh, sc, w_in, conv_w8, w_out, gm, nf, shf, scf, wr_t):
    b, n, _ = x.shape
    tm = MIX_ROWS
    per = tm // HALO
    n_halo = n // HALO
    prev_blk = lambda bb, i: (bb, jnp.maximum(i * per - 1, 0), 0)
    next_blk = lambda bb, i: (bb, jnp.minimum((i + 1) * per, n_halo - 1), 0)
    row, r_in, r_out = _router_specs(tm)
    vec = pl.BlockSpec((1, 1, D), lambda bb, i: (bb, 0, 0))
    full = lambda shape: pl.BlockSpec(shape, lambda bb, i: (0, 0))
    once = lambda shape: pl.BlockSpec(shape, lambda bb, i: (0, 0), pipeline_mode=pl.Buffered(1))
    return pl.pallas_call(
        _mixer1_kernel,
        grid=(b, n // tm),
        in_specs=[
            row, pl.BlockSpec((1, HALO, D), prev_blk), pl.BlockSpec((1, HALO, D), next_blk),
            _tile_spec(tm), pl.BlockSpec((1, HALO * SUB, LANES), prev_blk),
            pl.BlockSpec((1, HALO * SUB, LANES), next_blk),
            full((1, D)), vec, vec, once((D, 3 * D)), full((8, D)), once((D, D)),
        ] + r_in,
        out_specs=r_out,
        out_shape=_router_out_shape(b, n),
        scratch_shapes=[pltpu.VMEM((D, 3 * D), BF), pltpu.VMEM((D, D), BF)],
        compiler_params=_params(("parallel", "arbitrary"), 56),
        name="mixer1",
    )(x, x, x, moe_tiles, moe_tiles, moe_tiles, nw, sh, sc, w_in, conv_w8, w_out, gm, nf, shf, scf, wr_t)


def _select_kernel(a_ref, idx_ref, gate_ref, cl_s, off_s, inc_s, hi_s, mid_s, lo_s, *, cap, chunks):
    a = a_ref[...]
    rows = a.shape[0]
    groups = rows // chunks
    per_batch = N_EXP * chunks
    assert chunks & (chunks - 1) == 0
    shift = chunks.bit_length() - 1

    member = jnp.where((lax.broadcasted_iota(I32, (groups, rows), 1) >> shift)
                       == lax.broadcasted_iota(I32, (groups, rows), 0), 1.0, 0.0).astype(BF)
    ri =lax.broadcasted_iota(I32, (per_batch, per_batch), 0)
    ci = lax.broadcasted_iota(I32, (per_batch, per_batch), 1)
    same_f = jnp.where((ri >> shift) == (ci >> shift), 1.0, 0.0)
    same = same_f.astype(BF)
    lower = (same_f * jnp.where(ci < ri, 1.0, 0.0)).astype(BF)
    li = lax.broadcasted_iota(I32, (LANES, LANES), 0)
    lj = lax.broadcasted_iota(I32, (LANES, LANES), 1)
    incl = jnp.where(li <= lj, 1.0, 0.0).astype(BF)

    def bcast(col):
        return jnp.broadcast_to(col, (per_batch, LANES))

    def prefix(maskf):
        cl = jnp.dot(maskf.astype(BF), incl, preferred_element_type=F32)
        tot = bcast(cl[:, LANES - 1:LANES])
        off = jnp.dot(lower, tot.astype(BF), preferred_element_type=F32)
        return cl, off, tot

    min_normal = 0x00800000

    a3 = a.reshape(groups, chunks, LANES)

    def at_least(bits):
        thr_g = lax.bitcast_convert_type(bits, F32)[:, None, :]
        return jnp.where(a3 >= thr_g, 1.0, 0.0).reshape(rows, LANES)

    def search(step, t):
        cand = t | jnp.left_shift(jnp.int32(1), 30 - step)
        part = jnp.dot(member, at_least(cand).astype(BF), preferred_element_type=F32)
        cnt = jnp.sum(part, axis=-1, keepdims=True)
        return jnp.where(cnt >= cap, jnp.where(cand >= min_normal, cand, t), t)

    thr = lax.fori_loop(0, 31, search, jnp.zeros((groups, LANES), I32))
    gtf = at_least(jnp.maximum(thr + 1, min_normal))
    eqf = at_least(thr) - gtf
    for bb in range(rows // per_batch):
        sl = slice(bb * per_batch, (bb + 1) * per_batch)
        gt_b, eq_b = gtf[sl], eqf[sl]
        n_gt = jnp.dot(same, bcast(jnp.sum(gt_b, axis=-1, keepdims=True)).astype(BF),
                       preferred_element_type=F32)
        cl_eq, off_eq, _ = prefix(eq_b)
        sel = gt_b + eq_b * jnp.where(cl_eq + off_eq <= cap - n_gt, 1.0, 0.0)
        cl, off, tot = prefix(sel)
        cl_s[sl, :] = cl.astype(BF)
        off_s[sl, :] = off
        inc_s[sl, :] = off + tot
        a_b = a[sl]
        a_hi = a_b.astype(BF)
        r1 = a_b - a_hi.astype(F32)
        a_mid = r1.astype(BF)
        hi_s[sl, :] = a_hi
        mid_s[sl, :] = a_mid
        lo_s[sl, :] = (r1 - a_mid.astype(F32)).astype(BF)

    slot = lax.broadcasted_iota(I32, (chunks, cap), 1).astype(F32)
    chunk_id = lax.broadcasted_iota(I32, (chunks, cap), 0).astype(F32)
    lane_id = lax.broadcasted_iota(I32, (LANES, cap), 0).astype(F32)
    tn = (((0,), (0,)), ((), ()))
    reps = cap // LANES

    def widen(x):
        return jnp.concatenate([x] * reps, axis=1)

    def per_group(g, carry):
        win = pl.ds(pl.multiple_of(g * chunks, chunks), chunks)
        inc_g = widen(inc_s[win, :])
        off_g = widen(off_s[win, :])
        chunk_of = jnp.sum(jnp.where(inc_g <= slot, 1.0, 0.0), axis=0, keepdims=True)
        pick_f = jnp.where(chunk_id == chunk_of, 1.0, 0.0)
        before = jnp.sum(pick_f * off_g, axis=0, keepdims=True)
        pick = pick_f.astype(BF)
        counts = lax.dot_general(cl_s[win, :], pick, tn, preferred_element_type=F32)
        lane_of = jnp.sum(jnp.where(counts <= slot[0:1] - before, 1.0, 0.0), axis=0, keepdims=True)
        aff = (lax.dot_general(hi_s[win, :], pick, tn, preferred_element_type=F32)
               + lax.dot_general(mid_s[win, :], pick, tn, preferred_element_type=F32)
               + lax.dot_general(lo_s[win, :], pick, tn, preferred_element_type=F32))
        gate = jnp.sum(jnp.where(lane_id == lane_of, aff, 0.0), axis=0, keepdims=True)
        idx_ref[pl.ds(g, 1), :] = ((chunk_of * float(LANES) + lane_of) * float(SUB)).astype(I32)
        gate_ref[pl.ds(g, 1), :] = gate
        return carry

    lax.fori_loop(0, groups, per_group, 0, unroll=2)


def _select(aff_rows, cap, chunks):
    rows = aff_rows.shape[0]
    groups = rows // chunks
    whole = lambda shape: pl.BlockSpec(shape, lambda i: (0, 0))
    return pl.pallas_call(
        functools.partial(_select_kernel, cap=cap, chunks=chunks),
        grid=(1,),
        in_specs=[whole((rows, LANES))],
        out_specs=[whole((groups, cap)), whole((groups, cap))],
        out_shape=[jax.ShapeDtypeStruct((groups, cap), I32), jax.ShapeDtypeStruct((groups, cap), F32)],
        scratch_shapes=[pltpu.VMEM((rows, LANES), BF), pltpu.VMEM((rows, LANES), F32),
                        pltpu.VMEM((rows, LANES), F32), pltpu.VMEM((rows, LANES), BF),
                        pltpu.VMEM((rows, LANES), BF), pltpu.VMEM((rows, LANES), BF)],
        compiler_params=_params(("arbitrary",), 48),
        name="select",
    )(aff_rows)


def _gather_kernel(idx_ref, h_ref, o_ref):
    slots = idx_ref.shape[2]

    def body(c, carry):
        t = pl.multiple_of(idx_ref[0, 0, c], SUB)
        o_ref[0, 0, pl.ds(pl.multiple_of(c * SUB, SUB), SUB), :] = h_ref[0, pl.ds(t, SUB), :]
        return carry

    lax.fori_loop(0, slots, body, 0, unroll=8)


def _gather(idx, hf_tiles, cap):
    b, rows, _ = hf_tiles.shape
    steps = N_EXP // EXPERTS_PER_STEP
    slots = EXPERTS_PER_STEP * cap
    xs = pl.pallas_call(
        _gather_kernel,
        grid=(b, steps),
        in_specs=[
            pl.BlockSpec((1, 1, slots), lambda bb, s: (bb * steps + s, 0, 0), memory_space=pltpu.SMEM),
            pl.BlockSpec((1, rows, LANES), lambda bb, s: (bb, 0, 0)),
        ],
        out_specs=pl.BlockSpec((1, 1, slots * SUB, LANES), lambda bb, s: (bb, s, 0, 0)),
        out_shape=jax.ShapeDtypeStruct((b, steps, slots * SUB, LANES), F32),
        compiler_params=_params(("parallel", "arbitrary"), 56),
        name="gather",
    )(idx, hf_tiles)
    return xs.reshape(b, N_EXP, cap * SUB, LANES)


def _ffn_kernel(x_ref, gate_ref, gf_ref, wg_ref, wu_ref, wd_ref, o_ref, wg_s, wu_s, wd_s):
    @pl.when(pl.program_id(1) == 0)
    def _():
        wg_s[...] = wg_ref[...].astype(BF)
        wu_s[...] = wu_ref[...].astype(BF)
        wd_s[...] = wd_ref[...].astype(BF)

    cap = x_ref.shape[2] // SUB
    x = _load_token_tiles(x_ref.at[0], cap).astype(BF)
    dexp = wg_s.shape[1]
    step = 256
    y = None
    for c0 in range(0, dexp, step):
        g = jnp.dot(x, wg_s[:, c0:c0 + step], preferred_element_type=F32)
        u = jnp.dot(x, wu_s[:, c0:c0 + step], preferred_element_type=F32)
        act = (g * (1.0 / (1.0 + jnp.exp(-g))) * u).astype(BF)
        part = jnp.dot(act, wd_s[c0:c0 + step, :], preferred_element_type=F32)
        y = part if y is None else y + part
    gate_col = jnp.broadcast_to(gate_ref[0], (LANES, cap)).T
    y = y * jnp.concatenate([gate_col] * SUB, axis=1) * gf_ref[0]
    _store_token_tiles(o_ref.at[0], y)


def _ffn(xs, gate, gf, w_gate, w_up, w_down, layer):
    b, _, tile_rows, _ = xs.shape
    cap = tile_rows // SUB
    dexp = w_gate.shape[-1]
    gspec = pl.BlockSpec((1, 1, cap), lambda e, bb: (bb * N_EXP + e, 0, 0))
    gfspec = pl.BlockSpec((1, 1, D), lambda e, bb: (bb, 0, 0))
    yspec = pl.BlockSpec((1, 1, tile_rows, LANES), lambda e, bb: (bb, e, 0, 0))
    xspec = yspec

    def wspec(shape, switch_at):
        def index(e, bb):
            return (layer, jnp.minimum(e + jnp.where(bb >= switch_at, 1, 0), N_EXP - 1), 0, 0)
        return pl.BlockSpec((None, None) + shape, index)

    assert b >= 2
    stagger = [min(j, b - 1) for j in (1, 2, 3)]
    return pl.pallas_call(
        _ffn_kernel,
        grid=(N_EXP, b),
        in_specs=[xspec, gspec, gfspec, wspec((D, dexp), stagger[0]), wspec((D, dexp), stagger[1]),
                  wspec((dexp, D), stagger[2])],
        out_specs=yspec,
        out_shape=jax.ShapeDtypeStruct((b, N_EXP, cap * SUB, LANES), F32),
        scratch_shapes=[pltpu.VMEM((D, dexp), BF), pltpu.VMEM((D, dexp), BF), pltpu.VMEM((dexp, D), BF)],
        compiler_params=_params(("arbitrary", "arbitrary"), 56),
        name="expert_ffn",
    )(xs, gate, gf, w_gate, w_up, w_down)


def _scatter_kernel(idx_ref, y_ref, *rest):
    if len(rest) == 1:
        (o_ref,), x_ref = rest, None
        acc_ref = o_ref
    else:
        x_ref, o_ref, acc_ref = rest
    slots = idx_ref.shape[2]
    expert_steps = N_EXP // EXPERTS_PER_STEP
    step = pl.program_id(1)

    @pl.when(step == 0)
    def _():
        acc_ref[...] = jnp.zeros(acc_ref.shape, F32)

    group = 16

    def body(g, carry):
        c0 = g * group
        ts = [pl.multiple_of(idx_ref[0, 0, c0 + j], SUB) for j in range(group)]
        new = []
        for j in range(group):
            y = y_ref[0, 0, pl.ds(pl.multiple_of((c0 + j) * SUB, SUB), SUB), :]
            new.append(acc_ref[0, pl.ds(ts[j], SUB), :] + y)
        for j in range(group):
            acc_ref[0, pl.ds(ts[j], SUB), :] = new[j]
        return carry

    @pl.when(step < expert_steps)
    def _():
        lax.fori_loop(0, slots // group, body, 0)

    if x_ref is not None:
        @pl.when(step >= expert_steps)
        def _():
            rows = o_ref.shape[1]
            base = (step - expert_steps) * (rows * SUB)
            o_ref[0] = x_ref[0] + jnp.concatenate(
                [acc_ref[0, pl.ds(base + j, rows, stride=SUB), :] for j in range(SUB)], axis=1)


def _scatter(idx, y_tiles, n, x=None):
    b = y_tiles.shape[0]
    slots = idx.shape[2]
    assert (slots // EXPERTS_PER_STEP) % 16 == 0
    rows = n * SUB
    tm = min(OUT_ROWS, n)
    steps = N_EXP // EXPERTS_PER_STEP
    extra = n // tm if x is not None else 0
    expert_step = lambda s: jnp.minimum(s, steps - 1)
    in_specs = [
        pl.BlockSpec((1, 1, slots), lambda bb, s: (bb * steps + expert_step(s), 0, 0),
                     memory_space=pltpu.SMEM),
        pl.BlockSpec((1, 1, slots * SUB, LANES), lambda bb, s: (bb, expert_step(s), 0, 0)),
    ]
    args = [idx, y_tiles.reshape(b, steps, slots * SUB, LANES)]
    if x is not None:
        piece = pl.BlockSpec((1, tm, D), lambda bb, s: (bb, jnp.maximum(s - steps, 0), 0))
        in_specs.append(piece)
        args.append(x)
        out_spec, out_shape = piece, jax.ShapeDtypeStruct((b, n, D), F32)
        scratch = [pltpu.VMEM((1, rows, LANES), F32)]
    else:
        out_spec = pl.BlockSpec((1, rows, LANES), lambda bb, s: (bb, 0, 0))
        out_shape, scratch = jax.ShapeDtypeStruct((b, rows, LANES), F32), []
    return pl.pallas_call(
        _scatter_kernel,
        grid=(b, steps + extra),
        in_specs=in_specs,
        out_specs=out_spec,
        out_shape=out_shape,
        scratch_shapes=scratch,
        compiler_params=_params(("parallel", "arbitrary"), 56),
        name="scatter_add",
    )(*args)


def _moe(hf_tiles, aff_t, gf, w_gate, w_up, w_down, layer, x=None):
    b, _, n = aff_t.shape
    cap = CAP_FACTOR * n // N_EXP
    chunks = n // LANES
    idx, gate = _select(aff_t.reshape(b * N_EXP * chunks, LANES), cap, chunks)
    idx = idx.reshape(b * N_EXP // EXPERTS_PER_STEP, 1, EXPERTS_PER_STEP * cap)
    gate = gate.reshape(b * N_EXP, 1, cap)
    xs = _gather(idx, hf_tiles, cap)
    y_tiles = _ffn(xs, gate, gf, w_gate, w_up, w_down, layer)
    return _scatter(idx, y_tiles, n, x)


def _rope_tables(n_lat, n_ctx):
    rows = n_lat // GRID_W
    r = jnp.repeat(jnp.arange(rows, dtype=F32), GRID_W)
    col = jnp.tile(jnp.arange(GRID_W, dtype=F32), rows)
    n_freq = HEAD_DIM // 4
    inv = ROPE_BASE ** (-jnp.arange(n_freq, dtype=F32) / n_freq)
    ar = r[:, None] * inv
    ac = col[:, None] * inv
    ang = jnp.concatenate([ar, ar, ac, ac], axis=-1)
    sign = jnp.where((jnp.arange(HEAD_DIM) % 32) < 16, -1.0, 1.0).astype(F32)
    cos = jnp.concatenate([jnp.cos(ang), jnp.ones((n_ctx, HEAD_DIM), F32)], axis=0)
    sin_s = jnp.concatenate([jnp.sin(ang) * sign, jnp.zeros((n_ctx, HEAD_DIM), F32)], axis=0)
    return jnp.tile(cos, (1, 2)).T, jnp.tile(sin_s, (1, 2)).T


def _dft_tables(n_lat):
    c = jnp.arange(FOUR_G, dtype=I32)
    ang_c = (2.0 * math.pi / FOUR_G) * ((c[:, None] * c[None, :]) % FOUR_G).astype(F32)
    eye = jnp.eye(FOUR_W // FOUR_G, dtype=F32)
    bd = jnp.concatenate([jnp.kron(eye, jnp.cos(ang_c)), jnp.kron(eye, jnp.sin(ang_c))], axis=1)
    bd = (bd * FOUR_G ** -0.5).astype(BF)
    tr = DFT_ROWS
    n = jnp.arange(n_lat, dtype=I32)[None, :]
    r = jnp.arange(tr, dtype=I32)[:, None]
    i = jnp.arange(n_lat // tr, dtype=I32)[:, None]
    tile_ang = (2.0 * math.pi / n_lat) * ((i * tr * n) % n_lat).astype(F32)
    side = int(round(math.sqrt(n_lat)))
    assert side * side == n_lat
    s = jnp.arange(side, dtype=I32)[None, :]
    hi_ang = (2.0 * math.pi / side) * ((r * s) % side).astype(F32)
    lo_ang = (2.0 * math.pi / n_lat) * ((r * s) % n_lat).astype(F32)
    ch, sh = jnp.cos(hi_ang)[:, :, None], jnp.sin(hi_ang)[:, :, None]
    cl, sl = jnp.cos(lo_ang)[:, None, :], jnp.sin(lo_ang)[:, None, :]
    scale = n_lat ** -0.5
    half = n_lat // 2
    ch, sh = ch[:, :side // 2], sh[:, :side // 2]
    cos_row = ((ch * cl - sh * sl) * scale).reshape(tr, half)
    sin_row = ((sh * cl + ch * sl) * scale).reshape(tr, half)
    tile_ang = tile_ang[:, :half]
    tables = (cos_row, sin_row, jnp.cos(tile_ang)[:, None, :], jnp.sin(tile_ang)[:, None, :])
    return bd, tables


def kernel(x, c, ctx, c_ctx, ada_w, ada_b, norm_mix, norm_ffn, attn_w_in, attn_q_norm, attn_k_norm,
           lam_q1, lam_k1, lam_q2, lam_k2, attn_subln, attn_w_out, conv_w_in, conv_w, conv_w_out,
           router_w, moe_w_gate, moe_w_up, moe_w_down):
    b, n, _ = x.shape
    n_ctx = ctx.shape[1]
    assert x.shape[2] == D and n_ctx % PROJ0_ROWS == 0
    assert all(n % t == 0 for t in (PROJ0_ROWS, ATTN_QUERIES, DFT_ROWS, MIX_ROWS, min(OUT_ROWS, n)))

    cond8 = jnp.concatenate([c, c_ctx[None, :], jnp.zeros((8 - b - 1, D), F32)], axis=0)
    ada = _ada(cond8, ada_w, ada_b)

    def mods(layer):
        m = ada[layer].reshape(8, 6, D)
        return [m[:, j] for j in range(6)]

    vec = lambda t: t[:b].reshape(b, 1, D)

    sh_m, sc_m, g_m, sh_f, sc_f, g_f = mods(0)
    both = lambda t: jnp.stack([t[:b], jnp.broadcast_to(t[b], (b, D))], axis=1).reshape(b, 2, 1, D)
    cos_t, sin_t = _rope_tables(n, n_ctx)
    bd, dft_tabs = _dft_tables(n)
    col2 = lambda t: jnp.tile(t.reshape(HEAD_DIM, 1), (2, 1))
    q, k, v, f = _proj0(x, ctx, norm_mix[0].reshape(1, D), both(sh_m), both(sc_m), attn_w_in[0],
                        cos_t, sin_t, col2(attn_q_norm[0]), col2(attn_k_norm[0]))
    lam_init = 0.8 - 0.6 * math.exp(-0.3 * 0)
    lamv = jnp.zeros((8, LANES), F32).at[:4, :HEAD_DIM].set(
        jnp.stack([lam_q1[0], lam_k1[0], lam_q2[0], lam_k2[0]]))
    score_bound = (1.01 * HEAD_DIM * Q_SCALE * jnp.max(jnp.abs(attn_q_norm[0]))
                   * jnp.max(jnp.abs(attn_k_norm[0])) + 0.1).reshape(1)
    att = _attention(score_bound, q, k, v, lamv, attn_subln[0].reshape(V_DIM, 1), n, lam_init)
    four = _fourier(f, bd, dft_tabs, n)
    x1, hf, aff_t = _merge0(att, four, attn_w_out[0], x, vec(g_m),
                            norm_ffn[0].reshape(1, D), vec(sh_f), vec(sc_f), router_w[0].T)
    moe0 = _moe(hf, aff_t, vec(g_f), moe_w_gate, moe_w_up, moe_w_down, 0)

    sh_m, sc_m, g_m, sh_f, sc_f, g_f = mods(1)
    conv_w8 = jnp.zeros((8, D), F32).at[:3].set(conv_w[0])
    x3, hf, aff_t = _mixer1(x1, moe0, norm_mix[1].reshape(1, D), vec(sh_m), vec(sc_m),
                            conv_w_in[0], conv_w8, conv_w_out[0], vec(g_m),
                            norm_ffn[1].reshape(1, D), vec(sh_f), vec(sc_f), router_w[1].T)
    return _moe(hf, aff_t, vec(g_f), moe_w_gate, moe_w_up, moe_w_down, 1, x3)
```

```python
import functools
import math

import jax
import jax.numpy as jnp
from jax import lax
from jax.experimental import pallas as pl
from jax.experimental.pallas import tpu as pltpu

BF = jnp.bfloat16
F32 = jnp.float32
I32 = jnp.int32

D = 1024
GRID_W = 64
N_HEADS = 6
HEAD_DIM = 64
V_DIM = 2 * HEAD_DIM
QK_W = N_HEADS * 2 * HEAD_DIM
V_W = N_HEADS * V_DIM
FOUR_W = 256
FOUR_G = 64
N_EXP = 16
EXPERTS_PER_STEP = 4
CAP_FACTOR = 2
ROPE_BASE = 10000.0
EPS = 1e-6
Q_SCALE = HEAD_DIM ** -0.5 * math.log2(math.e)
ATTN_SHIFT_LIMIT = 40.0
LANES = 128
MIB = 1024 * 1024

ADA_COLS = 1536
PROJ0_ROWS = 256
ATTN_QUERIES = 1024
ATTN_KEYS = 256
DFT_ROWS = 256
MIX_ROWS = 512
OUT_ROWS = 1024

NT = (((1,), (1,)), ((), ()))


def _params(sem, vmem_mib):
    return pltpu.CompilerParams(dimension_semantics=sem, vmem_limit_bytes=vmem_mib * MIB)


def _split2(x):
    hi = x.astype(BF)
    lo = (x - hi.astype(F32)).astype(BF)
    return hi, lo


def _dot3(a, b, dims=(((1,), (0,)), ((), ()))):
    m = a.shape[0]
    ah, al = _split2(a)
    bh, bl = _split2(b)
    dg = functools.partial(lax.dot_general, dimension_numbers=dims, preferred_element_type=F32)
    both = dg(jnp.concatenate([ah, al], axis=0), bh)
    return both[:m] + both[m:] + dg(ah, bl)


def _modulate(x, nw, shift, scale):
    ms = jnp.mean(x * x, axis=-1, keepdims=True)
    return (x * lax.rsqrt(ms + EPS) * nw) * (1.0 + scale) + shift


def _ada_kernel(c_ref, w_ref, b_ref, o_ref):
    cv = c_ref[...]
    s = cv * (1.0 / (1.0 + jnp.exp(-cv)))
    o_ref[0] = _dot3(s, w_ref[0]) + b_ref[0]


def _ada(cond8, ada_w, ada_b):
    depth = ada_w.shape[0]
    tn = ADA_COLS
    return pl.pallas_call(
        _ada_kernel,
        grid=(depth, 6 * D // tn),
        in_specs=[
            pl.BlockSpec((8, D), lambda l, j: (0, 0)),
            pl.BlockSpec((1, D, tn), lambda l, j: (l, 0, j)),
            pl.BlockSpec((1, 1, tn), lambda l, j: (l, 0, j)),
        ],
        out_specs=pl.BlockSpec((1, 8, tn), lambda l, j: (l, 0, j)),
        out_shape=jax.ShapeDtypeStruct((depth, 8, 6 * D), F32),
        compiler_params=_params(("arbitrary", "arbitrary"), 40),
        name="ada",
    )(cond8, ada_w, ada_b.reshape(depth, 1, 6 * D))


def _norm_rope_t(xt, wn_col, cos_t, sin_t, out_scale):
    q4 = HEAD_DIM // 4
    halves = []
    for s in range(2):
        x = xt[s * HEAD_DIM:(s + 1) * HEAD_DIM]
        ms = jnp.sum(x * x, axis=0, keepdims=True) * (1.0 / HEAD_DIM)
        y = x * lax.rsqrt(ms + EPS) * wn_col[s * HEAD_DIM:(s + 1) * HEAD_DIM]
        rot = jnp.concatenate([y[q4:2 * q4], y[:q4], y[3 * q4:], y[2 * q4:3 * q4]], axis=0)
        halves.append(y * cos_t[s * HEAD_DIM:(s + 1) * HEAD_DIM] + rot * sin_t[s * HEAD_DIM:(s + 1) * HEAD_DIM])
    out = jnp.concatenate(halves, axis=0)
    return out if out_scale == 1.0 else out * out_scale


def _proj0_kernel(x_ref, ctx_ref, nw_ref, sh_ref, sc_ref, wf_ref, wt_ref, cost_ref, sint_ref,
                  qn_ref, kn_ref, q_ref, k_ref, v_ref, f_ref, *, lat_tiles):
    x = jnp.where(pl.program_id(1) < lat_tiles, x_ref[0], ctx_ref[0])
    h = _modulate(x, nw_ref[...], sh_ref[0, 0], sc_ref[0, 0]).astype(BF)
    def project_t(r0, width):
        return lax.dot_general(wt_ref[r0:r0 + width, :], h, NT, preferred_element_type=F32)

    cos_t = cost_ref[...]
    sin_t = sint_ref[...]
    for out_ref, wn_ref, r0, scale in ((q_ref, qn_ref, 0, Q_SCALE), (k_ref, kn_ref, QK_W, 1.0)):
        p_t = project_t(r0, QK_W)
        wn_col = wn_ref[...]
        for hh in range(N_HEADS):
            c0 = hh * V_DIM
            out_ref[0, c0:c0 + V_DIM, :] = _norm_rope_t(
                p_t[c0:c0 + V_DIM], wn_col, cos_t, sin_t, scale).astype(BF)
    v_ref[0] = project_t(2 * QK_W, V_W).astype(BF)
    f_ref[0] = jnp.dot(h, wf_ref[...], preferred_element_type=F32).astype(BF)


def _proj0(x, ctx, nw, sh2, sc2, w_four, w_qkv_t, cos_t, sin_t, qn_col, kn_col):
    b, n_lat, _ = x.shape
    n_ctx = ctx.shape[1]
    s_tot = n_lat + n_ctx
    tm = PROJ0_ROWS
    lat_tiles = n_lat // tm
    mod_spec = pl.BlockSpec((1, 1, 1, D), lambda bb, i: (bb, i // lat_tiles, 0, 0))
    full = lambda shape: pl.BlockSpec(shape, lambda bb, i: tuple(0 for _ in shape))
    feat = lambda w: pl.BlockSpec((1, w, tm), lambda bb, i: (bb, 0, i))
    feat_shape = lambda w: jax.ShapeDtypeStruct((b, w, s_tot), BF)
    return pl.pallas_call(
        functools.partial(_proj0_kernel, lat_tiles=lat_tiles),
        grid=(b, s_tot // tm),
        in_specs=[
            pl.BlockSpec((1, tm, D), lambda bb, i: (bb, jnp.minimum(i, lat_tiles - 1), 0)),
            pl.BlockSpec((1, tm, D), lambda bb, i: (bb, jnp.maximum(i - lat_tiles, 0), 0)),
            full((1, D)), mod_spec, mod_spec, full((D, FOUR_W)), full((2 * QK_W + V_W, D)),
            pl.BlockSpec((LANES, tm), lambda bb, i: (0, i)),
            pl.BlockSpec((LANES, tm), lambda bb, i: (0, i)),
            full((LANES, 1)), full((LANES, 1)),
        ],
        out_specs=[feat(QK_W), feat(QK_W), feat(V_W),
                   pl.BlockSpec((1, tm, FOUR_W), lambda bb, i: (bb, i, 0))],
        out_shape=[feat_shape(QK_W), feat_shape(QK_W), feat_shape(V_W),
                   jax.ShapeDtypeStruct((b, s_tot, FOUR_W), BF)],
        compiler_params=_params(("parallel", "arbitrary"), 48),
        name="proj0",
    )(x, ctx, nw, sh2, sc2, w_four, w_qkv_t, cos_t, sin_t, qn_col, kn_col)


def _attn_kernel(bound_ref, q_ref, kt_ref, vt_ref, lam_ref, sub_ref, o_ref, k_ref, m_ref, e_ref,
                 *, lam_init, kc):
    @pl.when(pl.program_id(2) == 0)
    def _():
        k_ref[...] = kt_ref[0].astype(F32).T.astype(BF)

    q = q_ref[0]
    feat = lax.broadcasted_iota(I32, q.shape, 0)
    zero = jnp.zeros_like(q)
    lv = lam_ref[...]
    t1 = jnp.sum(lv[0:1] * lv[1:2], axis=-1, keepdims=True)
    t2 = jnp.sum(lv[2:3] * lv[3:4], axis=-1, keepdims=True)
    lam = jnp.exp(t1) - jnp.exp(t2) + lam_init

    qs = (jnp.where(feat < HEAD_DIM, q, zero), jnp.where(feat < HEAD_DIM, zero, q))
    tq = q.shape[1]
    n_chunks = k_ref.shape[0] // kc

    def scores(h, c):
        return jnp.dot(k_ref[c * kc:(c + 1) * kc, :], qs[h], preferred_element_type=F32)

    def fold(x, op):
        return op(x.reshape(kc // 8, 8, tq), axis=0)

    bound = bound_ref[0]
    small = bound <= ATTN_SHIFT_LIMIT

    @pl.when(small)
    def _():
        m_ref[...] = jnp.zeros(m_ref.shape, F32) + bound

    @pl.when(jnp.logical_not(small))
    def _():
        for h in range(2):
            m = jnp.full((8, tq), -jnp.inf, F32)
            for c in range(n_chunks):
                m = jnp.maximum(m, fold(scores(h, c), jnp.max))
            m_ref[h] = jnp.broadcast_to(jnp.max(m, axis=0, keepdims=True), (8, tq))

    ms = [m_ref[h][0:1, :] for h in range(2)]
    ls = [jnp.zeros((8, tq), F32) for _ in range(2)]
    for h in range(2):
        for c in range(n_chunks):
            e = jnp.exp2(scores(h, c) - ms[h])
            ls[h] = ls[h] + fold(e, jnp.sum)
            e_ref[h, c] = e.astype(BF)
    l0, l1 = [jnp.sum(l, axis=0, keepdims=True) for l in ls]

    beta = (lam * l0 / l1).astype(BF)
    acc = jnp.zeros((V_DIM, tq), F32)
    for c in range(n_chunks):
        a = e_ref[0, c] - beta * e_ref[1, c]
        acc = acc + jnp.dot(vt_ref[0, :, c * kc:(c + 1) * kc], a, preferred_element_type=F32)
    o = acc * (1.0 / l0)
    ms = jnp.mean(o * o, axis=0, keepdims=True)
    o = o * lax.rsqrt(ms + EPS) * sub_ref[...] * (1.0 - lam_init)
    o_ref[0] = o.T.astype(BF)


def _attention(bound, q, k, v, lamv, subln, n_lat, lam_init):
    b, _, s_tot = k.shape
    tq = ATTN_QUERIES
    kc = ATTN_KEYS
    assert s_tot % kc == 0
    kv_spec = pl.BlockSpec((1, V_DIM, s_tot), lambda bb, hh, i: (bb, hh, 0))
    return pl.pallas_call(
        functools.partial(_attn_kernel, lam_init=lam_init, kc=kc),
        grid=(b, N_HEADS, n_lat // tq),
        in_specs=[
            pl.BlockSpec(memory_space=pltpu.SMEM),
            pl.BlockSpec((1, V_DIM, tq), lambda bb, hh, i: (bb, hh, i)),
            kv_spec, kv_spec,
            pl.BlockSpec((8, LANES), lambda bb, hh, i: (0, 0)),
            pl.BlockSpec((V_DIM, 1), lambda bb, hh, i: (0, 0)),
        ],
        out_specs=pl.BlockSpec((1, tq, V_DIM), lambda bb, hh, i: (bb, i, hh)),
        out_shape=jax.ShapeDtypeStruct((b, n_lat, V_W), BF),
        scratch_shapes=[pltpu.VMEM((s_tot, V_DIM), BF),
                        pltpu.VMEM((2, 8, tq), F32),
                        pltpu.VMEM((2, s_tot // kc, kc, tq), BF)],
        compiler_params=_params(("parallel", "parallel", "arbitrary"), 48),
        name="diff_attn",
    )(bound, q, k, v, lamv, subln)


def _fourier_kernel(f_ref, bd_ref, cb_ref, sb_ref, ca_ref, sa_ref, o_ref, g_ref, gm_ref, *, scale):
    nb, n_lat, _ = f_ref.shape
    half = n_lat // 2

    @pl.when(pl.program_id(0) == 0)
    def _():
        blk = 256
        nblk = half // blk
        flip = jnp.where(lax.broadcasted_iota(I32, (blk, blk), 0)
                         + lax.broadcasted_iota(I32, (blk, blk), 1) == blk - 1, 1.0, 0.0).astype(BF)
        rowid = lax.broadcasted_iota(I32, (half, FOUR_W), 0)
        for bb in range(nb):
            cols = slice(bb * FOUR_W, (bb + 1) * FOUR_W)
            fwd = f_ref[bb, :half, :].astype(F32)
            upside = jnp.concatenate(
                [jnp.dot(flip, f_ref[bb, half + (nblk - 1 - k) * blk:half + (nblk - k) * blk, :],
                         preferred_element_type=F32) for k in range(nblk)], axis=0)
            rev = jnp.where(rowid == 0, fwd, pltpu.roll(upside, 1, axis=0))
            even = jnp.where(rowid == 0, fwd, fwd + rev)
            g_ref[:half, cols] = jnp.dot(even.astype(BF), bd_ref[:, :FOUR_W],
                                         preferred_element_type=F32).astype(BF)
            g_ref[half:, cols] = jnp.dot((fwd - rev).astype(BF), bd_ref[:, FOUR_W:],
                                         preferred_element_type=F32).astype(BF)
            gm_ref[:, cols] = jnp.dot(f_ref[bb, half:half + 16, :], bd_ref[:, :FOUR_W],
                                      preferred_element_type=F32)

    ca = ca_ref[0]
    sa = sa_ref[0]
    cb = cb_ref[...]
    sb = sb_ref[...]
    w_cos = (cb * ca - sb * sa).astype(BF)
    w_sin = (sb * ca + cb * sa).astype(BF)
    y = (jnp.dot(w_cos, g_ref[:half, :], preferred_element_type=F32)
         - jnp.dot(w_sin, g_ref[half:, :], preferred_element_type=F32))
    parity = lax.broadcasted_iota(I32, (y.shape[0], 1), 0) & 1
    y = y + jnp.where(parity == 0, scale, -scale) * gm_ref[0:1, :]
    for bb in range(nb):
        o_ref[bb] = y[:, bb * FOUR_W:(bb + 1) * FOUR_W].astype(BF)


def _fourier(f, bd, tables, n_lat):
    b = f.shape[0]
    half = n_lat // 2
    cos_row, sin_row, cos_tile, sin_tile = tables
    tr = cos_row.shape[0]
    assert tr % 2 == 0 and cos_row.shape[1] == half
    row_tab = pl.BlockSpec((tr, half), lambda i: (0, 0))
    tile_tab = pl.BlockSpec((1, 1, half), lambda i: (i, 0, 0))
    return pl.pallas_call(
        functools.partial(_fourier_kernel, scale=n_lat ** -0.5),
        grid=(n_lat // tr,),
        in_specs=[
            pl.BlockSpec((b, n_lat, FOUR_W), lambda i: (0, 0, 0)),
            pl.BlockSpec((FOUR_W, 2 * FOUR_W), lambda i: (0, 0)),
            row_tab, row_tab, tile_tab, tile_tab,
        ],
        out_specs=pl.BlockSpec((b, tr, FOUR_W), lambda i: (0, i, 0)),
        out_shape=jax.ShapeDtypeStruct((b, n_lat, FOUR_W), BF),
        scratch_shapes=[pltpu.VMEM((n_lat, b * FOUR_W), BF), pltpu.VMEM((16, b * FOUR_W), F32)],
        compiler_params=_params(("arbitrary",), 48),
        name="fourier",
    )(f, bd, cos_row, sin_row, cos_tile, sin_tile)


SUB = D // LANES


def _store_token_tiles(ref, val):
    rows = val.shape[0]
    for j in range(SUB):
        ref[0, pl.ds(j, rows, stride=SUB), :] = val[:, j * LANES:(j + 1) * LANES]


def _load_token_tiles(ref, rows):
    return jnp.concatenate([ref[0, pl.ds(j, rows, stride=SUB), :] for j in range(SUB)], axis=1)


def _residual_router(x, y, gm, nf, shf, scf, wr_t, x1_ref, hf_ref, aff_ref):
    x1 = x + gm * y
    x1_ref[0] = x1
    hf = _modulate(x1, nf, shf, scf)
    _store_token_tiles(hf_ref, hf)
    logits = _dot3(wr_t, hf, NT)
    m = jnp.max(logits, axis=0, keepdims=True)
    e = jnp.exp(logits - m)
    aff_ref[0] = e / jnp.sum(e, axis=0, keepdims=True)


def _merge0_kernel(att_ref, four_ref, w_ref, x_ref, gm_ref, nf_ref, shf_ref, scf_ref, wr_ref,
                   x1_ref, hf_ref, aff_ref):
    y = jnp.dot(att_ref[0], w_ref[:V_W, :], preferred_element_type=F32)
    y = y + jnp.dot(four_ref[0], w_ref[V_W:, :], preferred_element_type=F32)
    _residual_router(x_ref[0], y, gm_ref[0], nf_ref[...], shf_ref[0], scf_ref[0], wr_ref[...],
                     x1_ref, hf_ref, aff_ref)


def _router_specs(tm):
    vec = pl.BlockSpec((1, 1, D), lambda bb, i: (bb, 0, 0))
    row = pl.BlockSpec((1, tm, D), lambda bb, i: (bb, i, 0))
    in_specs = [vec, pl.BlockSpec((1, D), lambda bb, i: (0, 0)), vec, vec,
                pl.BlockSpec((N_EXP, D), lambda bb, i: (0, 0))]
    tiles = _tile_spec(tm)
    out_specs = [row, tiles, pl.BlockSpec((1, N_EXP, tm), lambda bb, i: (bb, 0, i))]
    return row, in_specs, out_specs


def _tile_spec(tm):
    return pl.BlockSpec((1, tm * SUB, LANES), lambda bb, i: (bb, i, 0))


def _router_out_shape(b, n):
    return [jax.ShapeDtypeStruct((b, n, D), F32), jax.ShapeDtypeStruct((b, n * SUB, LANES), F32),
            jax.ShapeDtypeStruct((b, N_EXP, n), F32)]


def _merge0(att, four, w_bf, x, gm, nf, shf, scf, wr_t):
    b, n, _ = x.shape
    tm = MIX_ROWS
    row, r_in, r_out = _router_specs(tm)
    return pl.pallas_call(
        _merge0_kernel,
        grid=(b, n // tm),
        in_specs=[
            pl.BlockSpec((1, tm, V_W), lambda bb, i: (bb, i, 0)),
            pl.BlockSpec((1, tm, FOUR_W), lambda bb, i: (bb, i, 0)),
            pl.BlockSpec((D, D), lambda bb, i: (0, 0)),
            row,
        ] + r_in,
        out_specs=r_out,
        out_shape=_router_out_shape(b, n),
        compiler_params=_params(("parallel", "arbitrary"), 48),
        name="merge0",
    )(att, four, w_bf, x, gm, nf, shf, scf, wr_t)


HALO = 8


def _mixer1_kernel(x_ref, xp_ref, xn_ref, moe_ref, moep_ref, moen_ref, nw_ref, sh_ref, sc_ref, win_ref,
                   cw_ref, wout_ref, gm_ref, nf_ref, shf_ref, scf_ref, wr_ref, x1_ref, hf_ref, aff_ref):
    i = pl.program_id(1)
    last = pl.num_programs(1) - 1
    tm = x_ref.shape[1]
    x = x_ref[0] + _load_token_tiles(moe_ref, tm)
    x_prev = xp_ref[0] + _load_token_tiles(moep_ref, HALO)
    x_next = xn_ref[0] + _load_token_tiles(moen_ref, HALO)
    x_all = jnp.concatenate([x_prev, x, x_next], axis=0)
    h_all = _modulate(x_all, nw_ref[...], sh_ref[0], sc_ref[0]).astype(BF)
    cg = jnp.dot(h_all, win_ref[:, D:2 * D], preferred_element_type=F32)
    u = jnp.dot(h_all, win_ref[:, 2 * D:], preferred_element_type=F32)
    z_raw = cg * u
    core = slice(HALO, HALO + tm)
    z_all = jnp.concatenate([z_raw[:HALO] * jnp.where(i > 0, 1.0, 0.0), z_raw[core],
                             z_raw[HALO + tm:] * jnp.where(i < last, 1.0, 0.0)], axis=0)
    z_up = pltpu.roll(z_all, 1, axis=0)[core]
    z_dn = pltpu.roll(z_all, tm + 2 * HALO - 1, axis=0)[core]
    cw = cw_ref[...]
    conv = cw[0:1] * z_up + cw[1:2] * z_all[core] + cw[2:3] * z_dn
    bg = jnp.dot(h_all[core], win_ref[:, :D], preferred_element_type=F32)
    y = jnp.dot((bg * conv).astype(BF), wout_ref[...], preferred_element_type=F32)
    _residual_router(x, y, gm_ref[0], nf_ref[...], shf_ref[0], scf_ref[0], wr_ref[...],
                     x1_ref, hf_ref, aff_ref)


def _mixer1(x, moe_tiles, nw, sh, sc, w_in, conv_w8, w_out, gm, nf, shf, scf, wr_t):
    b, n, _ = x.shape
    tm = MIX_ROWS
    per = tm // HALO
    n_halo = n // HALO
    prev_blk = lambda bb, i: (bb, jnp.maximum(i * per - 1, 0), 0)
    next_blk = lambda bb, i: (bb, jnp.minimum((i + 1) * per, n_halo - 1), 0)
    row, r_in, r_out = _router_specs(tm)
    vec = pl.BlockSpec((1, 1, D), lambda bb, i: (bb, 0, 0))
    full = lambda shape: pl.BlockSpec(shape, lambda bb, i: (0, 0))
    return pl.pallas_call(
        _mixer1_kernel,
        grid=(b, n // tm),
        in_specs=[
            row, pl.BlockSpec((1, HALO, D), prev_blk), pl.BlockSpec((1, HALO, D), next_blk),
            _tile_spec(tm), pl.BlockSpec((1, HALO * SUB, LANES), prev_blk),
            pl.BlockSpec((1, HALO * SUB, LANES), next_blk),
            full((1, D)), vec, vec, full((D, 3 * D)), full((8, D)), full((D, D)),
        ] + r_in,
        out_specs=r_out,
        out_shape=_router_out_shape(b, n),
        compiler_params=_params(("parallel", "arbitrary"), 56),
        name="mixer1",
    )(x, x, x, moe_tiles, moe_tiles, moe_tiles, nw, sh, sc, w_in, conv_w8, w_out, gm, nf, shf, scf, wr_t)


def _select_kernel(a_ref, idx_ref, gate_ref, cl_s, off_s, inc_s, hi_s, mid_s, lo_s, *, cap, chunks):
    a = a_ref[...]
    rows = a.shape[0]
    groups = rows // chunks
    per_batch = N_EXP * chunks
    assert chunks & (chunks - 1) == 0
    shift = chunks.bit_length() - 1

    member = jnp.where((lax.broadcasted_iota(I32, (groups, rows), 1) >> shift)
                       == lax.broadcasted_iota(I32, (groups, rows), 0), 1.0, 0.0).astype(BF)
    ri =lax.broadcasted_iota(I32, (per_batch, per_batch), 0)
    ci = lax.broadcasted_iota(I32, (per_batch, per_batch), 1)
    same_f = jnp.where((ri >> shift) == (ci >> shift), 1.0, 0.0)
    same = same_f.astype(BF)
    lower = (same_f * jnp.where(ci < ri, 1.0, 0.0)).astype(BF)
    li = lax.broadcasted_iota(I32, (LANES, LANES), 0)
    lj = lax.broadcasted_iota(I32, (LANES, LANES), 1)
    incl = jnp.where(li <= lj, 1.0, 0.0).astype(BF)

    def bcast(col):
        return jnp.broadcast_to(col, (per_batch, LANES))

    def prefix(maskf):
        cl = jnp.dot(maskf.astype(BF), incl, preferred_element_type=F32)
        tot = bcast(cl[:, LANES - 1:LANES])
        off = jnp.dot(lower, tot.astype(BF), preferred_element_type=F32)
        return cl, off, tot

    min_normal = 0x00800000

    a3 = a.reshape(groups, chunks, LANES)

    def at_least(bits):
        thr_g = lax.bitcast_convert_type(bits, F32)[:, None, :]
        return jnp.where(a3 >= thr_g, 1.0, 0.0).reshape(rows, LANES)

    def search(step, t):
        cand = t | jnp.left_shift(jnp.int32(1), 30 - step)
        part = jnp.dot(member, at_least(cand).astype(BF), preferred_element_type=F32)
        cnt = jnp.sum(part, axis=-1, keepdims=True)
        return jnp.where(cnt >= cap, jnp.where(cand >= min_normal, cand, t), t)

    thr = lax.fori_loop(0, 31, search, jnp.zeros((groups, LANES), I32))
    gtf = at_least(jnp.maximum(thr + 1, min_normal))
    eqf = at_least(thr) - gtf
    for bb in range(rows // per_batch):
        sl = slice(bb * per_batch, (bb + 1) * per_batch)
        gt_b, eq_b = gtf[sl], eqf[sl]
        n_gt = jnp.dot(same, bcast(jnp.sum(gt_b, axis=-1, keepdims=True)).astype(BF),
                       preferred_element_type=F32)
        cl_eq, off_eq, _ = prefix(eq_b)
        sel = gt_b + eq_b * jnp.where(cl_eq + off_eq <= cap - n_gt, 1.0, 0.0)
        cl, off, tot = prefix(sel)
        cl_s[sl, :] = cl.astype(BF)
        off_s[sl, :] = off
        inc_s[sl, :] = off + tot
        a_b = a[sl]
        a_hi = a_b.astype(BF)
        r1 = a_b - a_hi.astype(F32)
        a_mid = r1.astype(BF)
        hi_s[sl, :] = a_hi
        mid_s[sl, :] = a_mid
        lo_s[sl, :] = (r1 - a_mid.astype(F32)).astype(BF)

    slot = lax.broadcasted_iota(I32, (chunks, cap), 1).astype(F32)
    chunk_id = lax.broadcasted_iota(I32, (chunks, cap), 0).astype(F32)
    lane_id = lax.broadcasted_iota(I32, (LANES, cap), 0).astype(F32)
    tn = (((0,), (0,)), ((), ()))
    reps = cap // LANES

    def widen(x):
        return jnp.concatenate([x] * reps, axis=1)

    def per_group(g, carry):
        win = pl.ds(pl.multiple_of(g * chunks, chunks), chunks)
        inc_g = widen(inc_s[win, :])
        off_g = widen(off_s[win, :])
        chunk_of = jnp.sum(jnp.where(inc_g <= slot, 1.0, 0.0), axis=0, keepdims=True)
        pick_f = jnp.where(chunk_id == chunk_of, 1.0, 0.0)
        before = jnp.sum(pick_f * off_g, axis=0, keepdims=True)
        pick = pick_f.astype(BF)
        counts = lax.dot_general(cl_s[win, :], pick, tn, preferred_element_type=F32)
        lane_of = jnp.sum(jnp.where(counts <= slot[0:1] - before, 1.0, 0.0), axis=0, keepdims=True)
        aff = (lax.dot_general(hi_s[win, :], pick, tn, preferred_element_type=F32)
               + lax.dot_general(mid_s[win, :], pick, tn, preferred_element_type=F32)
               + lax.dot_general(lo_s[win, :], pick, tn, preferred_element_type=F32))
        gate = jnp.sum(jnp.where(lane_id == lane_of, aff, 0.0), axis=0, keepdims=True)
        idx_ref[pl.ds(g, 1), :] = ((chunk_of * float(LANES) + lane_of) * float(SUB)).astype(I32)
        gate_ref[pl.ds(g, 1), :] = gate
        return carry

    lax.fori_loop(0, groups, per_group, 0, unroll=2)


def _select(aff_rows, cap, chunks):
    rows = aff_rows.shape[0]
    groups = rows // chunks
    whole = lambda shape: pl.BlockSpec(shape, lambda i: (0, 0))
    return pl.pallas_call(
        functools.partial(_select_kernel, cap=cap, chunks=chunks),
        grid=(1,),
        in_specs=[whole((rows, LANES))],
        out_specs=[whole((groups, cap)), whole((groups, cap))],
        out_shape=[jax.ShapeDtypeStruct((groups, cap), I32), jax.ShapeDtypeStruct((groups, cap), F32)],
        scratch_shapes=[pltpu.VMEM((rows, LANES), BF), pltpu.VMEM((rows, LANES), F32),
                        pltpu.VMEM((rows, LANES), F32), pltpu.VMEM((rows, LANES), BF),
                        pltpu.VMEM((rows, LANES), BF), pltpu.VMEM((rows, LANES), BF)],
        compiler_params=_params(("arbitrary",), 48),
        name="select",
    )(aff_rows)


def _gather_kernel(idx_ref, h_ref, o_ref):
    slots = idx_ref.shape[2]

    def body(c, carry):
        t = pl.multiple_of(idx_ref[0, 0, c], SUB)
        o_ref[0, 0, pl.ds(pl.multiple_of(c * SUB, SUB), SUB), :] = h_ref[0, pl.ds(t, SUB), :]
        return carry

    lax.fori_loop(0, slots, body, 0, unroll=8)


def _gather(idx, hf_tiles, cap):
    b, rows, _ = hf_tiles.shape
    steps = N_EXP // EXPERTS_PER_STEP
    slots = EXPERTS_PER_STEP * cap
    xs = pl.pallas_call(
        _gather_kernel,
        grid=(b, steps),
        in_specs=[
            pl.BlockSpec((1, 1, slots), lambda bb, s: (bb * steps + s, 0, 0), memory_space=pltpu.SMEM),
            pl.BlockSpec((1, rows, LANES), lambda bb, s: (bb, 0, 0)),
        ],
        out_specs=pl.BlockSpec((1, 1, slots * SUB, LANES), lambda bb, s: (bb, s, 0, 0)),
        out_shape=jax.ShapeDtypeStruct((b, steps, slots * SUB, LANES), F32),
        compiler_params=_params(("parallel", "arbitrary"), 56),
        name="gather",
    )(idx, hf_tiles)
    return xs.reshape(b, N_EXP, cap * SUB, LANES)


def _ffn_kernel(x_ref, gate_ref, gf_ref, wg_ref, wu_ref, wd_ref, o_ref, wg_s, wu_s, wd_s):
    @pl.when(pl.program_id(1) == 0)
    def _():
        wg_s[...] = wg_ref[...].astype(BF)
        wu_s[...] = wu_ref[...].astype(BF)
        wd_s[...] = wd_ref[...].astype(BF)

    cap = x_ref.shape[2] // SUB
    x = _load_token_tiles(x_ref.at[0], cap).astype(BF)
    dexp = wg_s.shape[1]
    step = 256
    y = None
    for c0 in range(0, dexp, step):
        g = jnp.dot(x, wg_s[:, c0:c0 + step], preferred_element_type=F32)
        u = jnp.dot(x, wu_s[:, c0:c0 + step], preferred_element_type=F32)
        act = (g * (1.0 / (1.0 + jnp.exp(-g))) * u).astype(BF)
        part = jnp.dot(act, wd_s[c0:c0 + step, :], preferred_element_type=F32)
        y = part if y is None else y + part
    gate_col = jnp.broadcast_to(gate_ref[0], (LANES, cap)).T
    y = y * jnp.concatenate([gate_col] * SUB, axis=1) * gf_ref[0]
    _store_token_tiles(o_ref.at[0], y)


def _ffn(xs, gate, gf, w_gate, w_up, w_down, layer):
    b, _, tile_rows, _ = xs.shape
    cap = tile_rows // SUB
    dexp = w_gate.shape[-1]
    gspec = pl.BlockSpec((1, 1, cap), lambda e, bb: (bb * N_EXP + e, 0, 0))
    gfspec = pl.BlockSpec((1, 1, D), lambda e, bb: (bb, 0, 0))
    yspec = pl.BlockSpec((1, 1, tile_rows, LANES), lambda e, bb: (bb, e, 0, 0))
    xspec = yspec

    def wspec(shape, switch_at):
        def index(e, bb):
            return (layer, jnp.minimum(e + jnp.where(bb >= switch_at, 1, 0), N_EXP - 1), 0, 0)
        return pl.BlockSpec((None, None) + shape, index)

    assert b >= 2
    stagger = [min(j, b - 1) for j in (1, 2, 3)]
    return pl.pallas_call(
        _ffn_kernel,
        grid=(N_EXP, b),
        in_specs=[xspec, gspec, gfspec, wspec((D, dexp), stagger[0]), wspec((D, dexp), stagger[1]),
                  wspec((dexp, D), stagger[2])],
        out_specs=yspec,
        out_shape=jax.ShapeDtypeStruct((b, N_EXP, cap * SUB, LANES), F32),
        scratch_shapes=[pltpu.VMEM((D, dexp), BF), pltpu.VMEM((D, dexp), BF), pltpu.VMEM((dexp, D), BF)],
        compiler_params=_params(("arbitrary", "arbitrary"), 56),
        name="expert_ffn",
    )(xs, gate, gf, w_gate, w_up, w_down)


def _scatter_kernel(idx_ref, y_ref, *rest):
    if len(rest) == 1:
        (o_ref,), x_ref = rest, None
        acc_ref = o_ref
    else:
        x_ref, o_ref, acc_ref = rest
    slots = idx_ref.shape[2]
    expert_steps = N_EXP // EXPERTS_PER_STEP
    step = pl.program_id(1)

    @pl.when(step == 0)
    def _():
        acc_ref[...] = jnp.zeros(acc_ref.shape, F32)

    group = 16

    def body(g, carry):
        c0 = g * group
        ts = [pl.multiple_of(idx_ref[0, 0, c0 + j], SUB) for j in range(group)]
        new = []
        for j in range(group):
            y = y_ref[0, 0, pl.ds(pl.multiple_of((c0 + j) * SUB, SUB), SUB), :]
            new.append(acc_ref[0, pl.ds(ts[j], SUB), :] + y)
        for j in range(group):
            acc_ref[0, pl.ds(ts[j], SUB), :] = new[j]
        return carry

    @pl.when(step < expert_steps)
    def _():
        lax.fori_loop(0, slots // group, body, 0)

    if x_ref is not None:
        @pl.when(step >= expert_steps)
        def _():
            rows = o_ref.shape[1]
            base = (step - expert_steps) * (rows * SUB)
            o_ref[0] = x_ref[0] + jnp.concatenate(
                [acc_ref[0, pl.ds(base + j, rows, stride=SUB), :] for j in range(SUB)], axis=1)


def _scatter(idx, y_tiles, n, x=None):
    b = y_tiles.shape[0]
    slots = idx.shape[2]
    assert (slots // EXPERTS_PER_STEP) % 16 == 0
    rows = n * SUB
    tm = min(OUT_ROWS, n)
    steps = N_EXP // EXPERTS_PER_STEP
    extra = n // tm if x is not None else 0
    expert_step = lambda s: jnp.minimum(s, steps - 1)
    in_specs = [
        pl.BlockSpec((1, 1, slots), lambda bb, s: (bb * steps + expert_step(s), 0, 0),
                     memory_space=pltpu.SMEM),
        pl.BlockSpec((1, 1, slots * SUB, LANES), lambda bb, s: (bb, expert_step(s), 0, 0)),
    ]
    args = [idx, y_tiles.reshape(b, steps, slots * SUB, LANES)]
    if x is not None:
        piece = pl.BlockSpec((1, tm, D), lambda bb, s: (bb, jnp.maximum(s - steps, 0), 0))
        in_specs.append(piece)
        args.append(x)
        out_spec, out_shape = piece, jax.ShapeDtypeStruct((b, n, D), F32)
        scratch = [pltpu.VMEM((1, rows, LANES), F32)]
    else:
        out_spec = pl.BlockSpec((1, rows, LANES), lambda bb, s: (bb, 0, 0))
        out_shape, scratch = jax.ShapeDtypeStruct((b, rows, LANES), F32), []
    return pl.pallas_call(
        _scatter_kernel,
        grid=(b, steps + extra),
        in_specs=in_specs,
        out_specs=out_spec,
        out_shape=out_shape,
        scratch_shapes=scratch,
        compiler_params=_params(("parallel", "arbitrary"), 56),
        name="scatter_add",
    )(*args)


def _moe(hf_tiles, aff_t, gf, w_gate, w_up, w_down, layer, x=None):
    b, _, n = aff_t.shape
    cap = CAP_FACTOR * n // N_EXP
    chunks = n // LANES
    idx, gate = _select(aff_t.reshape(b * N_EXP * chunks, LANES), cap, chunks)
    idx = idx.reshape(b * N_EXP // EXPERTS_PER_STEP, 1, EXPERTS_PER_STEP * cap)
    gate = gate.reshape(b * N_EXP, 1, cap)
    xs = _gather(idx, hf_tiles, cap)
    y_tiles = _ffn(xs, gate, gf, w_gate, w_up, w_down, layer)
    return _scatter(idx, y_tiles, n, x)


def _rope_tables(n_lat, n_ctx):
    rows = n_lat // GRID_W
    r = jnp.repeat(jnp.arange(rows, dtype=F32), GRID_W)
    col = jnp.tile(jnp.arange(GRID_W, dtype=F32), rows)
    n_freq = HEAD_DIM // 4
    inv = ROPE_BASE ** (-jnp.arange(n_freq, dtype=F32) / n_freq)
    ar = r[:, None] * inv
    ac = col[:, None] * inv
    ang = jnp.concatenate([ar, ar, ac, ac], axis=-1)
    sign = jnp.where((jnp.arange(HEAD_DIM) % 32) < 16, -1.0, 1.0).astype(F32)
    cos = jnp.concatenate([jnp.cos(ang), jnp.ones((n_ctx, HEAD_DIM), F32)], axis=0)
    sin_s = jnp.concatenate([jnp.sin(ang) * sign, jnp.zeros((n_ctx, HEAD_DIM), F32)], axis=0)
    return jnp.tile(cos, (1, 2)).T, jnp.tile(sin_s, (1, 2)).T


def _dft_tables(n_lat):
    c = jnp.arange(FOUR_G, dtype=I32)
    ang_c = (2.0 * math.pi / FOUR_G) * ((c[:, None] * c[None, :]) % FOUR_G).astype(F32)
    eye = jnp.eye(FOUR_W // FOUR_G, dtype=F32)
    bd = jnp.concatenate([jnp.kron(eye, jnp.cos(ang_c)), jnp.kron(eye, jnp.sin(ang_c))], axis=1)
    bd = (bd * FOUR_G ** -0.5).astype(BF)
    tr = DFT_ROWS
    n = jnp.arange(n_lat, dtype=I32)[None, :]
    r = jnp.arange(tr, dtype=I32)[:, None]
    i = jnp.arange(n_lat // tr, dtype=I32)[:, None]
    tile_ang = (2.0 * math.pi / n_lat) * ((i * tr * n) % n_lat).astype(F32)
    side = int(round(math.sqrt(n_lat)))
    assert side * side == n_lat
    s = jnp.arange(side, dtype=I32)[None, :]
    hi_ang = (2.0 * math.pi / side) * ((r * s) % side).astype(F32)
    lo_ang = (2.0 * math.pi / n_lat) * ((r * s) % n_lat).astype(F32)
    ch, sh = jnp.cos(hi_ang)[:, :, None], jnp.sin(hi_ang)[:, :, None]
    cl, sl = jnp.cos(lo_ang)[:, None, :], jnp.sin(lo_ang)[:, None, :]
    scale = n_lat ** -0.5
    half = n_lat // 2
    ch, sh = ch[:, :side // 2], sh[:, :side // 2]
    cos_row = ((ch * cl - sh * sl) * scale).reshape(tr, half)
    sin_row = ((sh * cl + ch * sl) * scale).reshape(tr, half)
    tile_ang = tile_ang[:, :half]
    tables = (cos_row, sin_row, jnp.cos(tile_ang)[:, None, :], jnp.sin(tile_ang)[:, None, :])
    return bd, tables


def kernel(x, c, ctx, c_ctx, ada_w, ada_b, norm_mix, norm_ffn, attn_w_in, attn_q_norm, attn_k_norm,
           lam_q1, lam_k1, lam_q2, lam_k2, attn_subln, attn_w_out, conv_w_in, conv_w, conv_w_out,
           router_w, moe_w_gate, moe_w_up, moe_w_down):
    b, n, _ = x.shape
    n_ctx = ctx.shape[1]
    assert x.shape[2] == D and n_ctx % PROJ0_ROWS == 0
    assert all(n % t == 0 for t in (PROJ0_ROWS, ATTN_QUERIES, DFT_ROWS, MIX_ROWS, min(OUT_ROWS, n)))

    cond8 = jnp.concatenate([c, c_ctx[None, :], jnp.zeros((8 - b - 1, D), F32)], axis=0)
    ada = _ada(cond8, ada_w, ada_b)

    def mods(layer):
        m = ada[layer].reshape(8, 6, D)
        return [m[:, j] for j in range(6)]

    vec = lambda t: t[:b].reshape(b, 1, D)

    sh_m, sc_m, g_m, sh_f, sc_f, g_f = mods(0)
    both = lambda t: jnp.stack([t[:b], jnp.broadcast_to(t[b], (b, D))], axis=1).reshape(b, 2, 1, D)
    cos_t, sin_t = _rope_tables(n, n_ctx)
    bd, dft_tabs = _dft_tables(n)
    col2 = lambda t: jnp.tile(t.reshape(HEAD_DIM, 1), (2, 1))
    w_in = attn_w_in[0]
    q, k, v, f = _proj0(x, ctx, norm_mix[0].reshape(1, D), both(sh_m), both(sc_m),
                        w_in[:, 2 * QK_W + V_W:].astype(BF), w_in[:, :2 * QK_W + V_W].astype(BF).T,
                        cos_t, sin_t, col2(attn_q_norm[0]), col2(attn_k_norm[0]))
    lam_init = 0.8 - 0.6 * math.exp(-0.3 * 0)
    lamv = jnp.zeros((8, LANES), F32).at[:4, :HEAD_DIM].set(
        jnp.stack([lam_q1[0], lam_k1[0], lam_q2[0], lam_k2[0]]))
    score_bound = (1.01 * HEAD_DIM * Q_SCALE * jnp.max(jnp.abs(attn_q_norm[0]))
                   * jnp.max(jnp.abs(attn_k_norm[0])) + 0.1).reshape(1)
    att = _attention(score_bound, q, k, v, lamv, attn_subln[0].reshape(V_DIM, 1), n, lam_init)
    four = _fourier(f, bd, dft_tabs, n)
    x1, hf, aff_t = _merge0(att, four, attn_w_out[0].astype(BF), x, vec(g_m),
                            norm_ffn[0].reshape(1, D), vec(sh_f), vec(sc_f), router_w[0].T)
    moe0 = _moe(hf, aff_t, vec(g_f), moe_w_gate, moe_w_up, moe_w_down, 0)

    sh_m, sc_m, g_m, sh_f, sc_f, g_f = mods(1)
    conv_w8 = jnp.zeros((8, D), F32).at[:3].set(conv_w[0])
    x3, hf, aff_t = _mixer1(x1, moe0, norm_mix[1].reshape(1, D), vec(sh_m), vec(sc_m),
                            conv_w_in[0].astype(BF), conv_w8, conv_w_out[0].astype(BF), vec(g_m),
                            norm_ffn[1].reshape(1, D), vec(sh_f), vec(sc_f), router_w[1].T)
    return _moe(hf, aff_t, vec(g_f), moe_w_gate, moe_w_up, moe_w_down, 1, x3)
```
